```python
import math
import jax, jax.numpy as jnp
from jax import lax
import numpy as np

D_MODEL = 1024
BATCH = 4
SEQ = 8192
DEPTH = 1

NSA_HEADS = 8
NSA_KV_HEADS = 2
HEAD_DIM = 64
CMP_BLOCK = 32
CMP_STRIDE = 16
CMP_HIDDEN = 256
SEL_BLOCK = 64
N_SEL = 16
WINDOW = 512
Q_BLOCK = 64
FORCE_BONUS = 1e3
NEG_INF = -1e30
ROPE_THETA = 500000.0
ROPE_DIM = HEAD_DIM // 4
SSM_WIDTH = D_MODEL // 2
SSM_GROUP = 16
SSM_GROUPS = SSM_WIDTH // SSM_GROUP
SSM_STATE = 64
D_FF = 4 * D_MODEL
EPS = 1e-6

NSA_WIDTH = NSA_HEADS * HEAD_DIM
KV_WIDTH = NSA_KV_HEADS * HEAD_DIM
SPLITS = (NSA_WIDTH, 6 * KV_WIDTH, 3 * NSA_HEADS, SSM_WIDTH, 2 * D_MODEL)
IN_WIDTH = sum(SPLITS)

kernel_name = "hybrid_nsa_s5_gated_block"


def rmsnorm(x, g):
    x32 = x.astype(jnp.float32)
    y = x32 * lax.rsqrt(jnp.mean(x32 * x32, axis=-1, keepdims=True) + EPS)
    return (y * g.astype(jnp.float32)).astype(x.dtype)


def rope_partial(x, pos):
    half = ROPE_DIM // 2
    inv = ROPE_THETA ** (-(jnp.arange(half, dtype=jnp.float32) * 2.0) / ROPE_DIM)
    ang = pos.astype(jnp.float32)[:, None] * inv[None, :]
    cos = jnp.cos(ang)[None, :, None, :]
    sin = jnp.sin(ang)[None, :, None, :]
    x32 = x.astype(jnp.float32)
    x1, x2 = x32[..., :half], x32[..., half:ROPE_DIM]
    out = jnp.concatenate([x1 * cos - x2 * sin, x2 * cos + x1 * sin, x32[..., ROPE_DIM:]], axis=-1)
    return out.astype(x.dtype)


def compress(kv, pe, w1, w2):
    b, s, g, d = kv.shape
    r = kv.reshape(b, s // CMP_STRIDE, CMP_STRIDE, g, d)
    blocks = jnp.concatenate([r[:, :-1], r[:, 1:]], axis=2) + pe[None, None, :, None, :]
    nc = blocks.shape[1]
    flat = blocks.transpose(0, 1, 3, 2, 4).reshape(b, nc, g, CMP_BLOCK * d)
    return jax.nn.gelu(flat @ w1) @ w2


def nsa(q_raw, q_rot, kc, vc, ks, vs, kw, vw, gates):
    b, s, h, d = q_raw.shape
    g = kc.shape[2]
    hg = h // g
    nc = kc.shape[1]
    nb = s // SEL_BLOCK
    n_sel = min(N_SEL, nb)
    scale = d ** -0.5
    q_raw = q_raw.reshape(b, s, g, hg, d)
    q_rot = q_rot.reshape(b, s, g, hg, d)
    gates = gates.reshape(b, s, g, hg, 3)
    cmp_end = jnp.arange(nc) * CMP_STRIDE + CMP_BLOCK - 1
    n_np = np.arange(nc)[:, None] * CMP_STRIDE
    j_np = np.arange(nb)[None, :] * SEL_BLOCK
    overlap = jnp.asarray(((n_np < j_np + SEL_BLOCK) & (n_np + CMP_BLOCK > j_np)).astype(np.float32))
    ks_blk = ks.reshape(b, nb, SEL_BLOCK, g, d).transpose(0, 3, 1, 2, 4)
    vs_blk = vs.reshape(b, nb, SEL_BLOCK, g, d).transpose(0, 3, 1, 2, 4)
    kw_pad = jnp.pad(kw, ((0, 0), (WINDOW, 0), (0, 0), (0, 0)))
    vw_pad = jnp.pad(vw, ((0, 0), (WINDOW, 0), (0, 0), (0, 0)))
    gather = jax.vmap(jax.vmap(lambda blk, idx: blk[idx]))
    jb = jnp.arange(nb)

    def block(i):
        t0 = i * Q_BLOCK
        tq = t0 + jnp.arange(Q_BLOCK)
        sl = lambda a: lax.dynamic_slice_in_dim(a, t0, Q_BLOCK, axis=1)
        qr, qp, gt = sl(q_raw), sl(q_rot), sl(gates).astype(jnp.float32)
        mask_c = cmp_end[None, :] <= tq[:, None]
        s_c = jnp.einsum('bqghd,bngd->bghqn', qr, kc).astype(jnp.float32) * scale
        p_c = jax.nn.softmax(jnp.where(mask_c, s_c, NEG_INF), axis=-1)
        p_c = p_c * jnp.any(mask_c, axis=-1)[:, None].astype(jnp.float32)
        o_c = jnp.einsum('bghqn,bngd->bqghd', p_c.astype(vc.dtype), vc)
        imp = jnp.einsum('bghqn,nj->bgqj', p_c, overlap)
        cur = tq // SEL_BLOCK
        valid_j = jb[None, :] * SEL_BLOCK <= tq[:, None]
        forced = (jb[None, :] == 0) | (jb[None, :] == cur[:, None]) | (jb[None, :] == cur[:, None] - 1)
        score = jnp.where(valid_j, imp + FORCE_BONUS * forced.astype(jnp.float32), NEG_INF)
        _, idx = lax.top_k(score, n_sel)
        k_sel = gather(ks_blk, idx).reshape(b, g, Q_BLOCK, n_sel * SEL_BLOCK, d)
        v_sel = gather(vs_blk, idx).reshape(b, g, Q_BLOCK, n_sel * SEL_BLOCK, d)
        kpos = (idx[..., None] * SEL_BLOCK + jnp.arange(SEL_BLOCK)).reshape(b, g, Q_BLOCK, n_sel * SEL_BLOCK)
        mask_s = kpos <= tq[:, None]
        s_s = jnp.einsum('bqghd,bgqkd->bghqk', qp, k_sel).astype(jnp.float32) * scale
        p_s = jax.nn.softmax(jnp.where(mask_s[:, :, None], s_s, NEG_INF), axis=-1)
        o_s = jnp.einsum('bghqk,bgqkd->bqghd', p_s.astype(v_sel.dtype), v_sel)
        kwb = lax.dynamic_slice_in_dim(kw_pad, t0, WINDOW + Q_BLOCK, axis=1)
        vwb = lax.dynamic_slice_in_dim(vw_pad, t0, WINDOW + Q_BLOCK, axis=1)
        kp = t0 - WINDOW + jnp.arange(WINDOW + Q_BLOCK)
        mask_w = (kp[None, :] <= tq[:, None]) & (kp[None, :] > tq[:, None] - WINDOW) & (kp[None, :] >= 0)
        s_w = jnp.einsum('bqghd,bkgd->bghqk', qp, kwb).astype(jnp.float32) * scale
        p_w = jax.nn.softmax(jnp.where(mask_w, s_w, NEG_INF), axis=-1)
        o_w = jnp.einsum('bghqk,bkgd->bqghd', p_w.astype(vwb.dtype), vwb)
        out = gt[..., 0:1] * o_c + gt[..., 1:2] * o_s + gt[..., 2:3] * o_w
        return out.reshape(b, Q_BLOCK, h * d).astype(q_raw.dtype)

    out = lax.map(block, jnp.arange(s // Q_BLOCK))
    return out.transpose(1, 0, 2, 3).reshape(b, s, h * d)


def _scan_op(e1, e2):
    a1, b1 = e1
    a2, b2 = e2
    return a1 * a2, a2 * b1 + b2


def s5(u, lam_re, lam_im, log_step, b_re, b_im, c_re, c_im, d_skip):
    b, s, _ = u.shape
    f32 = jnp.float32
    u32 = u.astype(f32).reshape(b, s, SSM_GROUPS, SSM_GROUP)
    lam = lax.complex(lam_re.astype(f32), lam_im.astype(f32))
    step = jnp.exp(log_step.astype(f32))[:, None]
    lam_bar = jnp.exp(lam * step)
    b_bar = ((lam_bar - 1.0) / lam)[..., None] * lax.complex(b_re.astype(f32), b_im.astype(f32))
    bu = jnp.einsum('bsgc,gnc->bsgn', u32.astype(jnp.complex64), b_bar)
    a = jnp.broadcast_to(lam_bar, bu.shape)
    _, hs = lax.associative_scan(_scan_op, (a, bu), axis=1)
    cc = lax.complex(c_re.astype(f32), c_im.astype(f32))
    y = jnp.real(jnp.einsum('bsgn,gcn->bsgc', hs, cc)) + d_skip.astype(f32).reshape(SSM_GROUPS, SSM_GROUP) * u32
    return y.reshape(b, s, SSM_WIDTH).astype(u.dtype)


def setup_inputs(seed: int = 0) -> dict:
    key = jax.random.key(seed)
    ks = jax.random.split(key, 24)
    nrm = lambda k, shape, fan: jax.random.normal(k, shape, jnp.float32) * fan ** -0.5
    L = DEPTH
    n_idx = jnp.arange(SSM_STATE, dtype=jnp.float32)
    return {
        "x": jax.random.normal(ks[0], (BATCH, SEQ, D_MODEL), jnp.float32),
        "norm_mix_g": 1.0 + 0.02 * jax.random.normal(ks[1], (L, D_MODEL), jnp.float32),
        "w_in": nrm(ks[2], (L, D_MODEL, IN_WIDTH), D_MODEL),
        "cmp_pe": 0.02 * jax.random.normal(ks[3], (L, CMP_BLOCK, HEAD_DIM), jnp.float32),
        "cmp_k_w1": nrm(ks[4], (L, CMP_BLOCK * HEAD_DIM, CMP_HIDDEN), CMP_BLOCK * HEAD_DIM),
        "cmp_k_w2": nrm(ks[5], (L, CMP_HIDDEN, HEAD_DIM), CMP_HIDDEN),
        "cmp_v_w1": nrm(ks[6], (L, CMP_BLOCK * HEAD_DIM, CMP_HIDDEN), CMP_BLOCK * HEAD_DIM),
        "cmp_v_w2": nrm(ks[7], (L, CMP_HIDDEN, HEAD_DIM), CMP_HIDDEN),
        "ssm_lam_re": -0.5 * jnp.exp(0.05 * jax.random.normal(ks[8], (L, SSM_GROUPS, SSM_STATE), jnp.float32)),
        "ssm_lam_im": jnp.broadcast_to(math.pi * n_idx, (L, SSM_GROUPS, SSM_STATE)),
        "ssm_log_step": jax.random.uniform(ks[9], (L, SSM_GROUPS), jnp.float32, math.log(1e-3), math.log(1e-1)),
        "ssm_b_re": nrm(ks[10], (L, SSM_GROUPS, SSM_STATE, SSM_GROUP), 2 * SSM_GROUP),
        "ssm_b_im": nrm(ks[11], (L, SSM_GROUPS, SSM_STATE, SSM_GROUP), 2 * SSM_GROUP),
        "ssm_c_re": nrm(ks[12], (L, SSM_GROUPS, SSM_GROUP, SSM_STATE), SSM_STATE),
        "ssm_c_im": nrm(ks[13], (L, SSM_GROUPS, SSM_GROUP, SSM_STATE), SSM_STATE),
        "ssm_d": jax.random.normal(ks[14], (L, SSM_WIDTH), jnp.float32),
        "w_attn_branch": nrm(ks[15], (L, NSA_WIDTH, D_MODEL), NSA_WIDTH),
        "w_ssm_val": nrm(ks[16], (L, SSM_WIDTH, D_MODEL), SSM_WIDTH),
        "w_ssm_gate": nrm(ks[17], (L, SSM_WIDTH, D_MODEL), SSM_WIDTH),
        "w_out": nrm(ks[18], (L, D_MODEL, D_MODEL), D_MODEL),
        "norm_mlp_g": 1.0 + 0.02 * jax.random.normal(ks[19], (L, D_MODEL), jnp.float32),
        "w_up": nrm(ks[20], (L, D_MODEL, D_FF), D_MODEL),
        "w_down": nrm(ks[21], (L, D_FF, D_MODEL), D_FF),
        "norm_final_g": 1.0 + 0.02 * jax.random.normal(ks[22], (D_MODEL,), jnp.float32),
    }


def reference(x, norm_mix_g, w_in, cmp_pe, cmp_k_w1, cmp_k_w2, cmp_v_w1, cmp_v_w2,
              ssm_lam_re, ssm_lam_im, ssm_log_step, ssm_b_re, ssm_b_im, ssm_c_re, ssm_c_im, ssm_d,
              w_attn_branch, w_ssm_val, w_ssm_gate, w_out, norm_mlp_g, w_up, w_down, norm_final_g):
    b, s, _ = x.shape
    pos = jnp.arange(s)
    offs = [int(v) for v in np.cumsum(SPLITS)[:-1]]
    for l in range(DEPTH):
        h = rmsnorm(x, norm_mix_g[l])
        proj = h @ w_in[l]
        q, kv, nsa_g, u, merge_g = jnp.split(proj, offs, axis=-1)
        q = q.reshape(b, s, NSA_HEADS, HEAD_DIM)
        kv = kv.reshape(b, s, 6, NSA_KV_HEADS, HEAD_DIM)
        k_c, v_c, k_s, v_s, k_w, v_w = (kv[:, :, i] for i in range(6))
        q_rot = rope_partial(q, pos)
        k_s = rope_partial(k_s, pos)
        k_w = rope_partial(k_w, pos)
        kc = compress(k_c, cmp_pe[l], cmp_k_w1[l], cmp_k_w2[l])
        vc = compress(v_c, cmp_pe[l], cmp_v_w1[l], cmp_v_w2[l])
        gates = jax.nn.sigmoid(nsa_g.astype(jnp.float32)).reshape(b, s, NSA_HEADS, 3)
        y_a = nsa(q, q_rot, kc, vc, k_s, v_s, k_w, v_w, gates) @ w_attn_branch[l]
        y_ssm = jax.nn.gelu(s5(u, ssm_lam_re[l], ssm_lam_im[l], ssm_log_step[l], ssm_b_re[l], ssm_b_im[l],
                               ssm_c_re[l], ssm_c_im[l], ssm_d[l]))
        y_b = (y_ssm @ w_ssm_val[l]) * jax.nn.sigmoid(y_ssm @ w_ssm_gate[l])
        g_a, g_b = jnp.split(merge_g, 2, axis=-1)
        merged = jax.nn.sigmoid(g_a) * y_a + jax.nn.sigmoid(g_b) * y_b
        x = x + merged @ w_out[l]
        h2 = rmsnorm(x, norm_mlp_g[l])
        x = x + jnp.square(jax.nn.relu(h2 @ w_up[l])) @ w_down[l]
    return rmsnorm(x, norm_final_g)
```

```python
import functools
import math

import jax
import jax.numpy as jnp
import numpy as np
from jax import lax
from jax.experimental import pallas as pl
from jax.experimental.pallas import tpu as pltpu

NSA_HEADS = 8
NSA_KV_HEADS = 2
HEAD_DIM = 64
CMP_BLOCK = 32
CMP_STRIDE = 16
CMP_HIDDEN = 256
SEL_BLOCK = 64
N_SEL = 16
WINDOW = 512
FORCE_BONUS = 1e3
NEG_INF = -1e30
ROPE_THETA = 500000.0
ROPE_DIM = HEAD_DIM // 4
SSM_GROUP = 16
SSM_STATE = 64
EPS = 1e-6

HEADS_PER_GROUP = NSA_HEADS // NSA_KV_HEADS
KV_LANES = NSA_KV_HEADS * HEAD_DIM
Q_LANES = NSA_HEADS * KV_LANES

V7X_LANES = 128
V7X_VMEM_BYTES = 64 * 1024 * 1024
VMEM_LIMIT = V7X_VMEM_BYTES - 8 * 1024 * 1024

ROW_TILE = 512
SEL_TILE = 512
WIN_KEYS = WINDOW + 2 * SEL_BLOCK
SSM_CHUNK = 16
SSM_SUPER = V7X_LANES // SSM_GROUP
FF_CHUNK = 1024

BF16 = jnp.bfloat16
F32 = jnp.float32
_NT = (((1,), (1,)), ((), ()))


def _cparams(semantics):
    return pltpu.CompilerParams(dimension_semantics=semantics, vmem_limit_bytes=VMEM_LIMIT)


def _rmsnorm(x, g):
    return x * lax.rsqrt(jnp.mean(x * x, axis=-1, keepdims=True) + EPS) * g


def _gelu(x):
    return jax.nn.gelu(x)


def _rope_cols(x, cos, sin_lo, sin_hi):
    cols = []
    for c in range(x.shape[1] // V7X_LANES):
        xc = x[:, c * V7X_LANES:(c + 1) * V7X_LANES]
        up = pltpu.roll(xc, V7X_LANES - ROPE_DIM // 2, axis=1)
        dn = pltpu.roll(xc, ROPE_DIM // 2, axis=1)
        cols.append(xc * cos + up * sin_lo + dn * sin_hi)
    return jnp.concatenate(cols, axis=1) if len(cols) > 1 else cols[0]


def _proj_kernel(x_ref, g_ref, wq_ref, wkv_ref, wg_ref, wu_ref, cos_ref, slo_ref, shi_ref,
                 qraw_ref, qrot_ref, kc_ref, vc_ref, ks_ref, vs_ref, kw_ref, vw_ref, gate_ref, u_ref):
    hb = _rmsnorm(x_ref[...], g_ref[...]).astype(BF16)
    cos, slo, shi = cos_ref[...], slo_ref[...], shi_ref[...]
    q = jnp.dot(hb, wq_ref[...], preferred_element_type=F32) * (HEAD_DIM ** -0.5)
    qraw_ref[...] = q.astype(BF16)
    qrot_ref[...] = _rope_cols(q, cos, slo, shi).astype(BF16)
    kv = jnp.dot(hb, wkv_ref[...], preferred_element_type=F32)
    w = KV_LANES
    kc_ref[...] = kv[:, 0 * w:1 * w]
    vc_ref[...] = kv[:, 1 * w:2 * w]
    ks_ref[...] = _rope_cols(kv[:, 2 * w:3 * w], cos, slo, shi).astype(BF16)
    vs_ref[...] = kv[:, 3 * w:4 * w].astype(BF16)
    kw_ref[...] = _rope_cols(kv[:, 4 * w:5 * w], cos, slo, shi).astype(BF16)
    vw_ref[...] = kv[:, 5 * w:6 * w].astype(BF16)
    gate_ref[...] = jax.nn.sigmoid(jnp.dot(hb, wg_ref[...], preferred_element_type=F32))
    u_ref[...] = jnp.dot(hb, wu_ref[...], preferred_element_type=F32)


def _proj_call(x2, g, wq, wkv, wg, wu, cos, slo, shi, seq):
    t, d = x2.shape
    tm = ROW_TILE
    s_tiles = seq // tm
    row = lambda i: (i, 0)
    pos = lambda i: (i % s_tiles, 0)
    full = lambda i: (0, 0)
    wspec = lambda a: pl.BlockSpec(a.shape, full)
    out_shape = [
        jax.ShapeDtypeStruct((t, Q_LANES), BF16), jax.ShapeDtypeStruct((t, Q_LANES), BF16),
        jax.ShapeDtypeStruct((t, KV_LANES), F32), jax.ShapeDtypeStruct((t, KV_LANES), F32),
        jax.ShapeDtypeStruct((t, KV_LANES), BF16), jax.ShapeDtypeStruct((t, KV_LANES), BF16),
        jax.ShapeDtypeStruct((t, KV_LANES), BF16), jax.ShapeDtypeStruct((t, KV_LANES), BF16),
        jax.ShapeDtypeStruct((t, V7X_LANES), F32), jax.ShapeDtypeStruct((t, wu.shape[1]), F32),
    ]
    return pl.pallas_call(
        _proj_kernel,
        grid=(t // tm,),
        in_specs=[pl.BlockSpec((tm, d), row), wspec(g), wspec(wq), wspec(wkv), wspec(wg), wspec(wu),
                  pl.BlockSpec((tm, V7X_LANES), pos), pl.BlockSpec((tm, V7X_LANES), pos),
                  pl.BlockSpec((tm, V7X_LANES), pos)],
        out_specs=[pl.BlockSpec((tm, o.shape[1]), row) for o in out_shape],
        out_shape=out_shape,
        compiler_params=_cparams(("arbitrary",)),
        name="proj",
    )(x2, g, wq, wkv, wg, wu, cos, slo, shi)


def _compress_kernel(k_ref, v_ref, pea_ref, peb_ref, kw1a_ref, kw1b_ref, kw2_ref, vw1a_ref, vw1b_ref, vw2_ref,
                     kc_ref, vc_ref, ca_ref, cb_ref):
    nch = ca_ref.shape[0]

    def one(src_ref, w1a_ref, w1b_ref, w2_ref, dst_ref):
        for t in range(CMP_STRIDE):
            rows = src_ref[pl.ds(t, nch, stride=CMP_STRIDE), :]
            sl = slice(t * KV_LANES, (t + 1) * KV_LANES)
            ca_ref[:, sl] = (rows + pea_ref[:, sl]).astype(BF16)
            cb_ref[:, sl] = (rows + peb_ref[:, sl]).astype(BF16)
        ha = jnp.dot(ca_ref[...], w1a_ref[...], preferred_element_type=F32)
        hb = jnp.dot(cb_ref[...], w1b_ref[...], preferred_element_type=F32)
        hid = _gelu(ha + pltpu.roll(hb, nch - 1, axis=0))
        dst_ref[0] = jnp.dot(hid.astype(BF16), w2_ref[...], preferred_element_type=F32).astype(BF16)

    one(k_ref, kw1a_ref, kw1b_ref, kw2_ref, kc_ref)
    one(v_ref, vw1a_ref, vw1b_ref, vw2_ref, vc_ref)


def _compress_call(kc_raw, vc_raw, pea, peb, kw1a, kw1b, kw2, vw1a, vw1b, vw2, batch, seq):
    nch = seq // CMP_STRIDE
    full = lambda b: (0, 0)
    wspec = lambda a: pl.BlockSpec(a.shape, full)
    out = jax.ShapeDtypeStruct((batch, nch, KV_LANES), BF16)
    return pl.pallas_call(
        _compress_kernel,
        grid=(batch,),
        in_specs=[pl.BlockSpec((seq, KV_LANES), lambda b: (b, 0)), pl.BlockSpec((seq, KV_LANES), lambda b: (b, 0)),
                  wspec(pea), wspec(peb), wspec(kw1a), wspec(kw1b), wspec(kw2), wspec(vw1a), wspec(vw1b), wspec(vw2)],
        out_specs=[pl.BlockSpec((1, nch, KV_LANES), lambda b: (b, 0, 0))] * 2,
        out_shape=[out, out],
        scratch_shapes=[pltpu.VMEM((nch, CMP_STRIDE * KV_LANES), BF16), pltpu.VMEM((nch, CMP_STRIDE * KV_LANES), BF16)],
        compiler_params=_cparams(("arbitrary",)),
        name="compress",
    )(kc_raw, vc_raw, pea, peb, kw1a, kw1b, kw2, vw1a, vw1b, vw2)


def _nsa_kernel(qraw_ref, qrot_ref, gate_ref, kc_ref, vc_ref, ks_ref, vs_ref, kw_ref, vw_ref, ovt_ref, exp_ref,
                o_ref, score_ref):
    i = pl.program_id(1)
    qb = SEL_BLOCK
    hg = HEADS_PER_GROUP
    rows = hg * qb
    nb = score_ref.shape[0]
    nc = kc_ref.shape[1]
    t0 = i * qb
    tq = t0 + lax.broadcasted_iota(jnp.int32, (qb, 1), 0)
    tq_rows = t0 + lax.broadcasted_iota(jnp.int32, (rows, 1), 0) % qb

    def group_q(ref, g):
        return jnp.concatenate(
            [ref[0, :, (g * hg + h) * KV_LANES:(g * hg + h + 1) * KV_LANES] for h in range(hg)], axis=0)

    cmp_end = lax.broadcasted_iota(jnp.int32, (1, nc), 1) * CMP_STRIDE + (CMP_BLOCK - 1)
    mask_c = cmp_end <= tq_rows
    o_cmp, p_sum = [], []
    for g in range(NSA_KV_HEADS):
        s = lax.dot_general(group_q(qraw_ref, g), kc_ref[0], _NT, preferred_element_type=F32)
        s = jnp.where(mask_c, s, NEG_INF)
        e = jnp.where(mask_c, jnp.exp(s - jnp.max(s, axis=-1, keepdims=True)), 0.0)
        l = jnp.sum(e, axis=-1, keepdims=True)
        p = e / jnp.where(l > 0.0, l, 1.0)
        o_cmp.append(jnp.dot(p.astype(BF16), vc_ref[0], preferred_element_type=F32))
        p_sum.append(p[0:qb] + p[qb:2 * qb] + p[2 * qb:3 * qb] + p[3 * qb:4 * qb])
    p2 = jnp.concatenate(p_sum, axis=0)
    p_hi = p2.astype(BF16)
    p_lo = (p2 - p_hi.astype(F32)).astype(BF16)
    imp_t = (lax.dot_general(ovt_ref[...], p_hi, _NT, preferred_element_type=F32)
             + lax.dot_general(ovt_ref[...], p_lo, _NT, preferred_element_type=F32))

    jcol = lax.broadcasted_iota(jnp.int32, (nb, 1), 0)
    valid = jcol <= i
    forced = (jcol == 0) | (jcol == i) | (jcol == i - 1)
    score = jnp.where(valid, imp_t + jnp.where(forced, FORCE_BONUS, 0.0), NEG_INF)
    score_ref[...] = score
    jfull = lax.broadcasted_iota(jnp.int32, score.shape, 0)

    def rank_body(jp, cnt):
        row = jnp.broadcast_to(score_ref[pl.ds(jp, 1), :], score.shape)
        tie = jnp.where(jp < jfull, 1.0, 0.0)
        return cnt + jnp.where(row > score, 1.0, jnp.where(row == score, tie, 0.0))

    cnt = lax.fori_loop(0, i + 1, rank_body, jnp.zeros(score.shape, F32))
    sel_t = jnp.where(valid, jnp.where(cnt < float(min(N_SEL, nb)), 1.0, 0.0), 0.0)
    sel = sel_t.T.astype(BF16)

    n_tiles = i // (SEL_TILE // qb) + 1
    lane_t = lax.broadcasted_iota(jnp.int32, (1, SEL_TILE), 1)
    o_sel = []
    for g in range(NSA_KV_HEADS):
        qp = group_q(qrot_ref, g)
        sel_g = sel[g * qb:(g + 1) * qb, :]

        def tile_body(c, carry, qp=qp, sel_g=sel_g):
            m, l, acc = carry
            k0 = pl.multiple_of(c * SEL_TILE, SEL_TILE)
            s = lax.dot_general(qp, ks_ref[0, pl.ds(k0, SEL_TILE), :], _NT, preferred_element_type=F32)
            picked = jnp.dot(sel_g, exp_ref[:, pl.ds(k0, SEL_TILE)], preferred_element_type=F32)
            ok = jnp.where((k0 + lane_t) <= tq, picked, 0.0) > 0.5
            s3 = jnp.where(ok[None], s.reshape(hg, qb, SEL_TILE), NEG_INF)
            m_new = jnp.maximum(m, jnp.max(s3, axis=-1, keepdims=True))
            p = jnp.exp(s3 - m_new)
            alpha = jnp.exp(m - m_new)
            l = alpha * l + jnp.sum(p, axis=-1, keepdims=True)
            pv = jnp.dot(p.reshape(rows, SEL_TILE).astype(BF16), vs_ref[0, pl.ds(k0, SEL_TILE), :],
                         preferred_element_type=F32)
            acc = alpha * acc + pv.reshape(hg, qb, KV_LANES)
            return m_new, l, acc

        init = (jnp.full((hg, qb, 1), NEG_INF, F32), jnp.zeros((hg, qb, 1), F32), jnp.zeros((hg, qb, KV_LANES), F32))
        _, l, acc = lax.fori_loop(0, n_tiles, tile_body, init)
        o_sel.append((acc / l).reshape(rows, KV_LANES))

    w0 = pl.multiple_of(jnp.maximum(i - (WIN_KEYS // qb - 1), 0) * qb, qb)
    kp = w0 + lax.broadcasted_iota(jnp.int32, (1, WIN_KEYS), 1)
    ok_w = (kp <= tq_rows) & (kp > tq_rows - WINDOW)
    o_win = []
    for g in range(NSA_KV_HEADS):
        s = lax.dot_general(group_q(qrot_ref, g), kw_ref[0, pl.ds(w0, WIN_KEYS), :], _NT, preferred_element_type=F32)
        s = jnp.where(ok_w, s, NEG_INF)
        e = jnp.exp(s - jnp.max(s, axis=-1, keepdims=True))
        l = jnp.sum(e, axis=-1, keepdims=True)
        o_win.append(jnp.dot(e.astype(BF16), vw_ref[0, pl.ds(w0, WIN_KEYS), :], preferred_element_type=F32) / l)

    gt = gate_ref[0]
    lane = lax.broadcasted_iota(jnp.int32, (qb, KV_LANES), 1)
    for h in range(hg):
        r = slice(h * qb, (h + 1) * qb)
        parts = []
        for g in range(NSA_KV_HEADS):
            c = (g * hg + h) * 3
            parts.append(gt[:, c:c + 1] * o_cmp[g][r] + gt[:, c + 1:c + 2] * o_sel[g][r] + gt[:, c + 2:c + 3] * o_win[g][r])
        out = parts[0]
        for g in range(1, NSA_KV_HEADS):
            out = jnp.where(lane >= g * HEAD_DIM, parts[g], out)
        o_ref[0, :, h * KV_LANES:(h + 1) * KV_LANES] = out.astype(BF16)


def _nsa_call(qraw, qrot, gates, kc, vc, ks, vs, kw, vw, ovt, expand):
    batch, seq, _ = qraw.shape
    nq = seq // SEL_BLOCK
    nb = seq // SEL_BLOCK
    nch = kc.shape[1]
    qspec = pl.BlockSpec((1, SEL_BLOCK, Q_LANES), lambda b, i: (b, i, 0))
    whole = lambda a: pl.BlockSpec((1,) + a.shape[1:], lambda b, i: (b, 0, 0))
    const = lambda a: pl.BlockSpec(a.shape, lambda b, i: (0, 0))
    return pl.pallas_call(
        _nsa_kernel,
        grid=(batch, nq),
        in_specs=[qspec, qspec, pl.BlockSpec((1, SEL_BLOCK, V7X_LANES), lambda b, i: (b, i, 0)),
                  whole(kc), whole(vc), whole(ks), whole(vs), whole(kw), whole(vw), const(ovt), const(expand)],
        out_specs=pl.BlockSpec((1, SEL_BLOCK, HEADS_PER_GROUP * KV_LANES), lambda b, i: (b, i, 0)),
        out_shape=jax.ShapeDtypeStruct((batch, seq, HEADS_PER_GROUP * KV_LANES), BF16),
        scratch_shapes=[pltpu.VMEM((nb, NSA_KV_HEADS * SEL_BLOCK), F32)],
        compiler_params=_cparams(("arbitrary", "arbitrary")),
        name="nsa",
    )(qraw, qrot, gates, kc, vc, ks, vs, kw, vw, ovt, expand)


def _s5_kernel(u_ref, mp_ref, q_ref, lr_ref, li_ref, d_ref, o_ref, x_ref, y_ref, z_ref, sp_ref):
    nch = x_ref.shape[0]
    lw = SSM_CHUNK * V7X_LANES
    half = z_ref.shape[1] // 2
    for t in range(SSM_CHUNK):
        x_ref[:, t * V7X_LANES:(t + 1) * V7X_LANES] = u_ref[pl.ds(t, nch, stride=SSM_CHUNK), :].astype(BF16)
    y_ref[...] = jnp.dot(x_ref[...], mp_ref[0, :, :lw], preferred_element_type=F32)
    z_ref[...] = jnp.dot(x_ref[...], mp_ref[0, :, lw:], preferred_element_type=F32)
    lr, li = lr_ref[0], li_ref[0]

    def scan_body(k, carry):
        sr, si = carry
        sp_ref[pl.ds(k, 1), 0:half] = sr
        sp_ref[pl.ds(k, 1), half:2 * half] = si
        zr = z_ref[pl.ds(k, 1), 0:half]
        zi = z_ref[pl.ds(k, 1), half:2 * half]
        return lr * sr - li * si + zr, lr * si + li * sr + zi

    zero = jnp.zeros((1, half), F32)
    lax.fori_loop(0, nch, scan_body, (zero, zero))
    y = y_ref[...] + jnp.dot(sp_ref[...].astype(BF16), q_ref[0], preferred_element_type=F32)
    d = d_ref[...]
    for t in range(SSM_CHUNK):
        yt = y[:, t * V7X_LANES:(t + 1) * V7X_LANES] + d * u_ref[pl.ds(t, nch, stride=SSM_CHUNK), :]
        o_ref[pl.ds(t, nch, stride=SSM_CHUNK), :] = _gelu(yt)


def _s5_call(u, mp, q, lr, li, dskip, batch, seq):
    nsg = mp.shape[0]
    nch = seq // SSM_CHUNK
    nstate = q.shape[1]
    lw = SSM_CHUNK * V7X_LANES
    return pl.pallas_call(
        _s5_kernel,
        grid=(nsg, batch),
        in_specs=[pl.BlockSpec((seq, V7X_LANES), lambda g, b: (b, g)),
                  pl.BlockSpec((1,) + mp.shape[1:], lambda g, b: (g, 0, 0), pipeline_mode=pl.Buffered(1)),
                  pl.BlockSpec((1,) + q.shape[1:], lambda g, b: (g, 0, 0), pipeline_mode=pl.Buffered(1)),
                  pl.BlockSpec((1, 1, nstate // 2), lambda g, b: (g, 0, 0)),
                  pl.BlockSpec((1, 1, nstate // 2), lambda g, b: (g, 0, 0)),
                  pl.BlockSpec((1, V7X_LANES), lambda g, b: (0, g))],
        out_specs=pl.BlockSpec((seq, V7X_LANES), lambda g, b: (b, g)),
        out_shape=jax.ShapeDtypeStruct(u.shape, F32),
        scratch_shapes=[pltpu.VMEM((nch, lw), BF16), pltpu.VMEM((nch, lw), F32),
                        pltpu.VMEM((nch, nstate), F32), pltpu.VMEM((nch, nstate), F32)],
        compiler_params=_cparams(("arbitrary", "arbitrary")),
        name="s5",
    )(u, mp, q, lr, li, dskip)


def _merge_kernel(x_ref, g_ref, attn_ref, ssm_ref, wga_ref, wgb_ref, wattn_ref, wval_ref, wgate_ref, wout_ref, o_ref):
    x = x_ref[...]
    hb = _rmsnorm(x, g_ref[...]).astype(BF16)
    dot = functools.partial(jnp.dot, preferred_element_type=F32)
    y_a = dot(attn_ref[...], wattn_ref[...])
    ys = ssm_ref[...].astype(BF16)
    y_b = dot(ys, wval_ref[...]) * jax.nn.sigmoid(dot(ys, wgate_ref[...]))
    merged = jax.nn.sigmoid(dot(hb, wga_ref[...])) * y_a + jax.nn.sigmoid(dot(hb, wgb_ref[...])) * y_b
    o_ref[...] = x + dot(merged.astype(BF16), wout_ref[...])


def _merge_call(x2, g, attn, ssm, wga, wgb, wattn, wval, wgate, wout):
    t, d = x2.shape
    tm = ROW_TILE
    row = lambda i: (i, 0)
    wspec = lambda a: pl.BlockSpec(a.shape, lambda i: (0, 0))
    return pl.pallas_call(
        _merge_kernel,
        grid=(t // tm,),
        in_specs=[pl.BlockSpec((tm, d), row), wspec(g), pl.BlockSpec((tm, attn.shape[1]), row),
                  pl.BlockSpec((tm, ssm.shape[1]), row), wspec(wga), wspec(wgb), wspec(wattn), wspec(wval),
                  wspec(wgate), wspec(wout)],
        out_specs=pl.BlockSpec((tm, d), row),
        out_shape=jax.ShapeDtypeStruct((t, d), F32),
        compiler_params=_cparams(("arbitrary",)),
        name="merge",
    )(x2, g, attn, ssm, wga, wgb, wattn, wval, wgate, wout)


def _mlp_kernel(x_ref, g_ref, wup_ref, wdown_ref, gf_ref, o_ref):
    x = x_ref[...]
    hb = _rmsnorm(x, g_ref[...]).astype(BF16)
    acc = x
    for c in range(wup_ref.shape[1] // FF_CHUNK):
        sl = slice(c * FF_CHUNK, (c + 1) * FF_CHUNK)
        up = jnp.maximum(jnp.dot(hb, wup_ref[:, sl], preferred_element_type=F32), 0.0)
        acc = acc + jnp.dot((up * up).astype(BF16), wdown_ref[sl, :], preferred_element_type=F32)
    o_ref[...] = _rmsnorm(acc, gf_ref[...])


def _mlp_call(x1, g, wup, wdown, gf):
    t, d = x1.shape
    tm = ROW_TILE
    row = lambda i: (i, 0)
    wspec = lambda a: pl.BlockSpec(a.shape, lambda i: (0, 0))
    return pl.pallas_call(
        _mlp_kernel,
        grid=(t // tm,),
        in_specs=[pl.BlockSpec((tm, d), row), wspec(g), wspec(wup), wspec(wdown), wspec(gf)],
        out_specs=pl.BlockSpec((tm, d), row),
        out_shape=jax.ShapeDtypeStruct((t, d), F32),
        compiler_params=_cparams(("arbitrary",)),
        name="mlp",
    )(x1, g, wup, wdown, gf)


def _pad_heads_to_group_lanes(wq):
    d = wq.shape[0]
    w = wq.reshape(d, NSA_KV_HEADS, HEADS_PER_GROUP, HEAD_DIM)
    eye = jnp.eye(NSA_KV_HEADS, dtype=wq.dtype)
    return jnp.einsum('dghe,gk->dghke', w, eye).reshape(d, Q_LANES)


def _rope_tables(seq):
    half = ROPE_DIM // 2
    inv = ROPE_THETA ** (-(jnp.arange(half, dtype=F32) * 2.0) / ROPE_DIM)
    ang = jnp.arange(seq, dtype=F32)[:, None] * inv[None, :]
    cos, sin = jnp.cos(ang), jnp.sin(ang)
    rest = HEAD_DIM - ROPE_DIM
    cos_h = jnp.concatenate([cos, cos, jnp.ones((seq, rest), F32)], axis=1)
    slo_h = jnp.concatenate([-sin, jnp.zeros((seq, half + rest), F32)], axis=1)
    shi_h = jnp.concatenate([jnp.zeros((seq, half), F32), sin, jnp.zeros((seq, rest), F32)], axis=1)
    reps = V7X_LANES // HEAD_DIM
    return jnp.tile(cos_h, (1, reps)), jnp.tile(slo_h, (1, reps)), jnp.tile(shi_h, (1, reps))


def _compress_weights(pe, w1, w2):
    eye = jnp.eye(NSA_KV_HEADS, dtype=F32)
    w1e = jnp.einsum('tdj,gk->tgdkj', w1.reshape(CMP_BLOCK, HEAD_DIM, CMP_HIDDEN), eye)
    w1e = w1e.reshape(CMP_BLOCK * KV_LANES, NSA_KV_HEADS * CMP_HIDDEN).astype(BF16)
    w2e = jnp.einsum('jd,gk->gjkd', w2, eye).reshape(NSA_KV_HEADS * CMP_HIDDEN, KV_LANES).astype(BF16)
    pee = jnp.tile(pe, (1, NSA_KV_HEADS)).reshape(1, CMP_BLOCK * KV_LANES)
    halfw = CMP_STRIDE * KV_LANES
    return pee[:, :halfw], pee[:, halfw:], w1e[:halfw], w1e[halfw:], w2e


def _selection_constants(seq):
    nc = seq // CMP_STRIDE - 1
    nb = seq // SEL_BLOCK
    n_np = np.arange(nc)[:, None] * CMP_STRIDE
    j_np = np.arange(nb)[None, :] * SEL_BLOCK
    overlap = ((n_np < j_np + SEL_BLOCK) & (n_np + CMP_BLOCK > j_np)).astype(np.float32)
    ovt = np.zeros((nb, nc + 1), np.float32)
    ovt[:, :nc] = overlap.T
    expand = (np.arange(seq)[None, :] // SEL_BLOCK == np.arange(nb)[:, None]).astype(np.float32)
    return jnp.asarray(ovt, BF16), jnp.asarray(expand, BF16)


def _s5_matrices(lam_re, lam_im, log_step, b_re, b_im, c_re, c_im):
    hp = lax.Precision.HIGHEST
    ng, ns = lam_re.shape
    gc = b_re.shape[-1]
    L = SSM_CHUNK
    step = jnp.exp(log_step)[:, None]
    a, b = lam_re * step, lam_im * step
    k = jnp.arange(L + 1, dtype=F32)[:, None, None]
    mag = jnp.exp(a[None] * k)
    pr, pi = mag * jnp.cos(b[None] * k), mag * jnp.sin(b[None] * k)
    nr, ni = pr[1] - 1.0, pi[1]
    den = lam_re * lam_re + lam_im * lam_im
    cr, ci = (nr * lam_re + ni * lam_im) / den, (ni * lam_re - nr * lam_im) / den
    bbr = cr[..., None] * b_re - ci[..., None] * b_im
    bbi = cr[..., None] * b_im + ci[..., None] * b_re
    cpr = c_re[None] * pr[:, :, None, :] - c_im[None] * pi[:, :, None, :]
    cpi = c_re[None] * pi[:, :, None, :] + c_im[None] * pr[:, :, None, :]
    kk = (jnp.einsum('kgcn,gnd->kgcd', cpr[:L], bbr, precision=hp)
          - jnp.einsum('kgcn,gnd->kgcd', cpi[:L], bbi, precision=hp))
    lag = np.arange(L)[None, :] - np.arange(L)[:, None]
    m = kk[np.clip(lag, 0, L - 1)] * jnp.asarray(lag >= 0, F32)[:, :, None, None, None]
    m = m.transpose(2, 0, 4, 1, 3)
    pw_r, pw_i = pr[L - 1 - np.arange(L)], pi[L - 1 - np.arange(L)]
    p_r = pw_r[:, :, :, None] * bbr[None] - pw_i[:, :, :, None] * bbi[None]
    p_i = pw_r[:, :, :, None] * bbi[None] + pw_i[:, :, :, None] * bbr[None]
    p_r, p_i = p_r.transpose(1, 0, 3, 2), p_i.transpose(1, 0, 3, 2)
    q_r = cpr[1:].transpose(1, 3, 0, 2)
    q_i = -cpi[1:].transpose(1, 3, 0, 2)
    sup = SSM_SUPER
    nsg = ng // sup
    eye = jnp.eye(sup, dtype=F32)
    lw = L * sup * gc
    m_sg = jnp.einsum('xasdtc,ab->xsadtbc', m.reshape(nsg, sup, L, gc, L, gc), eye).reshape(nsg, lw, lw)
    pr_sg = jnp.einsum('xasdn,ab->xsadbn', p_r.reshape(nsg, sup, L, gc, ns), eye).reshape(nsg, lw, sup * ns)
    pi_sg = jnp.einsum('xasdn,ab->xsadbn', p_i.reshape(nsg, sup, L, gc, ns), eye).reshape(nsg, lw, sup * ns)
    qr_sg = jnp.einsum('xantc,ab->xantbc', q_r.reshape(nsg, sup, ns, L, gc), eye).reshape(nsg, sup * ns, lw)
    qi_sg = jnp.einsum('xantc,ab->xantbc', q_i.reshape(nsg, sup, ns, L, gc), eye).reshape(nsg, sup * ns, lw)
    mp = jnp.concatenate([m_sg, pr_sg, pi_sg], axis=2).astype(BF16)
    q = jnp.concatenate([qr_sg, qi_sg], axis=1).astype(BF16)
    return mp, q, pr[L].reshape(nsg, 1, sup * ns), pi[L].reshape(nsg, 1, sup * ns)


def kernel(x, norm_mix_g, w_in, cmp_pe, cmp_k_w1, cmp_k_w2, cmp_v_w1, cmp_v_w2, ssm_lam_re, ssm_lam_im, ssm_log_step, ssm_b_re, ssm_b_im, ssm_c_re, ssm_c_im, ssm_d, w_attn_branch, w_ssm_val, w_ssm_gate, w_out, norm_mlp_g, w_up, w_down, norm_final_g):
    batch, seq, d = x.shape
    depth = w_in.shape[0]
    assert depth == 1, "the final rmsnorm is fused into the single layer's mlp kernel"
    nsa_w = NSA_HEADS * HEAD_DIM
    ssm_w = ssm_d.shape[1]
    o_q, o_kv, o_g, o_u = nsa_w, nsa_w + 6 * KV_LANES, nsa_w + 6 * KV_LANES + 3 * NSA_HEADS, 0
    o_u = o_g + ssm_w
    cos, slo, shi = _rope_tables(seq)
    ovt, expand = _selection_constants(seq)
    head_order = np.array([g * HEADS_PER_GROUP + h for h in range(HEADS_PER_GROUP) for g in range(NSA_KV_HEADS)])
    x2 = x.reshape(batch * seq, d)
    for l in range(depth):
        wl = w_in[l]
        wq = _pad_heads_to_group_lanes(wl[:, :o_q]).astype(BF16)
        wkv = wl[:, o_q:o_kv].astype(BF16)
        wg = jnp.pad(wl[:, o_kv:o_g], ((0, 0), (0, V7X_LANES - 3 * NSA_HEADS))).astype(BF16)
        wu = wl[:, o_g:o_u].astype(BF16)
        wga = wl[:, o_u:o_u + d].astype(BF16)
        wgb = wl[:, o_u + d:].astype(BF16)
        g_mix = norm_mix_g[l].reshape(1, d)
        qraw, qrot, kc_raw, vc_raw, ks, vs, kw, vw, gates, u = _proj_call(x2, g_mix, wq, wkv, wg, wu, cos, slo, shi, seq)

        pea, peb, kw1a, kw1b, kw2 = _compress_weights(cmp_pe[l], cmp_k_w1[l], cmp_k_w2[l])
        _, _, vw1a, vw1b, vw2 = _compress_weights(cmp_pe[l], cmp_v_w1[l], cmp_v_w2[l])
        kc, vc = _compress_call(kc_raw, vc_raw, pea, peb, kw1a, kw1b, kw2, vw1a, vw1b, vw2, batch, seq)

        b3 = lambda a: a.reshape(batch, seq, a.shape[-1])
        attn = _nsa_call(b3(qraw), b3(qrot), b3(gates), kc, vc, b3(ks), b3(vs), b3(kw), b3(vw), ovt, expand)
        attn = attn.reshape(batch * seq, nsa_w)

        mp, qm, lr, li = _s5_matrices(ssm_lam_re[l], ssm_lam_im[l], ssm_log_step[l], ssm_b_re[l], ssm_b_im[l],
                                      ssm_c_re[l], ssm_c_im[l])
        y_ssm = _s5_call(u, mp, qm, lr, li, ssm_d[l].reshape(1, ssm_w), batch, seq)

        wattn = w_attn_branch[l].reshape(NSA_HEADS, HEAD_DIM, d)[head_order].reshape(nsa_w, d).astype(BF16)
        x1 = _merge_call(x2, g_mix, attn, y_ssm, wga, wgb, wattn, w_ssm_val[l].astype(BF16),
                         w_ssm_gate[l].astype(BF16), w_out[l].astype(BF16))
        x2 = _mlp_call(x1, norm_mlp_g[l].reshape(1, d), w_up[l].astype(BF16), w_down[l].astype(BF16),
                       norm_final_g.reshape(1, d))
    return x2.reshape(batch, seq, d)
```

```python
import functools
import math

import jax
import jax.numpy as jnp
import numpy as np
from jax import lax
from jax.experimental import pallas as pl
from jax.experimental.pallas import tpu as pltpu

NSA_HEADS = 8
NSA_KV_HEADS = 2
HEAD_DIM = 64
CMP_BLOCK = 32
CMP_STRIDE = 16
CMP_HIDDEN = 256
SEL_BLOCK = 64
N_SEL = 16
WINDOW = 512
FORCE_BONUS = 1e3
NEG_INF = -1e30
ROPE_THETA = 500000.0
ROPE_DIM = HEAD_DIM // 4
SSM_GROUP = 16
SSM_STATE = 64
EPS = 1e-6

HEADS_PER_GROUP = NSA_HEADS // NSA_KV_HEADS
KV_LANES = NSA_KV_HEADS * HEAD_DIM
Q_LANES = NSA_HEADS * KV_LANES

V7X_LANES = 128
V7X_VMEM_BYTES = 64 * 1024 * 1024
VMEM_LIMIT = V7X_VMEM_BYTES - 8 * 1024 * 1024

ROW_TILE = 512
SEL_TILE = 512
WIN_KEYS = WINDOW + 2 * SEL_BLOCK
SSM_CHUNK = 16
SSM_SUPER = V7X_LANES // SSM_GROUP
FF_CHUNK = 1024

LOG2E = math.log2(math.e)
MASK_BIAS = 1e30

BF16 = jnp.bfloat16
F32 = jnp.float32
_NT = (((1,), (1,)), ((), ()))


def _cparams(semantics):
    return pltpu.CompilerParams(dimension_semantics=semantics, vmem_limit_bytes=VMEM_LIMIT)


def _rmsnorm(x, g):
    return x * lax.rsqrt(jnp.mean(x * x, axis=-1, keepdims=True) + EPS) * g


def _gelu(x):
    return jax.nn.gelu(x)


def _rope_cols(x, cos, sin_lo, sin_hi):
    cols = []
    for c in range(x.shape[1] // V7X_LANES):
        xc = x[:, c * V7X_LANES:(c + 1) * V7X_LANES]
        up = pltpu.roll(xc, V7X_LANES - ROPE_DIM // 2, axis=1)
        dn = pltpu.roll(xc, ROPE_DIM // 2, axis=1)
        cols.append(xc * cos + up * sin_lo + dn * sin_hi)
    return jnp.concatenate(cols, axis=1) if len(cols) > 1 else cols[0]


def _proj_kernel(x_ref, g_ref, wq_ref, wkv_ref, wg_ref, wu_ref, cos_ref, slo_ref, shi_ref,
                 qraw_ref, qrot_ref, kc_ref, vc_ref, ksa_ref, vs_ref, kw_ref, vw_ref, gate_ref, u_ref, *, seq_tiles):
    hb = _rmsnorm(x_ref[...], g_ref[...]).astype(BF16)
    cos, slo, shi = cos_ref[...], slo_ref[...], shi_ref[...]
    q = jnp.dot(hb, wq_ref[...], preferred_element_type=F32) * (HEAD_DIM ** -0.5 * LOG2E)
    qraw_ref[...] = q.astype(BF16)
    qrot_ref[...] = _rope_cols(q, cos, slo, shi).astype(BF16)
    kv = jnp.dot(hb, wkv_ref[...], preferred_element_type=F32)
    w = KV_LANES
    kc_ref[...] = kv[:, 0 * w:1 * w]
    vc_ref[...] = kv[:, 1 * w:2 * w]
    tm = x_ref.shape[0]
    pos = (pl.program_id(0) % seq_tiles) * tm + lax.broadcasted_iota(jnp.int32, (tm, V7X_LANES), 0)
    onehot = jnp.where(lax.broadcasted_iota(jnp.int32, (tm, V7X_LANES), 1) == pos // SEL_BLOCK, 1.0, 0.0)
    ksa_ref[:, :w] = _rope_cols(kv[:, 2 * w:3 * w], cos, slo, shi).astype(BF16)
    ksa_ref[:, w:] = onehot.astype(BF16)
    vs_ref[...] = kv[:, 3 * w:4 * w].astype(BF16)
    kw_ref[...] = _rope_cols(kv[:, 4 * w:5 * w], cos, slo, shi).astype(BF16)
    vw_ref[...] = kv[:, 5 * w:6 * w].astype(BF16)
    gate_ref[...] = jax.nn.sigmoid(jnp.dot(hb, wg_ref[...], preferred_element_type=F32))
    u_ref[...] = jnp.dot(hb, wu_ref[...], preferred_element_type=F32)


def _proj_call(x2, g, wq, wkv, wg, wu, cos, slo, shi, seq):
    t, d = x2.shape
    tm = ROW_TILE
    s_tiles = seq // tm
    row = lambda i: (i, 0)
    pos = lambda i: (i % s_tiles, 0)
    full = lambda i: (0, 0)
    wspec = lambda a: pl.BlockSpec(a.shape, full)
    out_shape = [
        jax.ShapeDtypeStruct((t, Q_LANES), BF16), jax.ShapeDtypeStruct((t, Q_LANES), BF16),
        jax.ShapeDtypeStruct((t, KV_LANES), F32), jax.ShapeDtypeStruct((t, KV_LANES), F32),
        jax.ShapeDtypeStruct((t, KV_LANES + V7X_LANES), BF16), jax.ShapeDtypeStruct((t, KV_LANES), BF16),
        jax.ShapeDtypeStruct((t, KV_LANES), BF16), jax.ShapeDtypeStruct((t, KV_LANES), BF16),
        jax.ShapeDtypeStruct((t, V7X_LANES), F32), jax.ShapeDtypeStruct((t, wu.shape[1]), F32),
    ]
    assert seq // SEL_BLOCK <= V7X_LANES, "the selection-block one-hot must fit one lane tile"
    return pl.pallas_call(
        functools.partial(_proj_kernel, seq_tiles=s_tiles),
        grid=(t // tm,),
        in_specs=[pl.BlockSpec((tm, d), row), wspec(g), wspec(wq), wspec(wkv), wspec(wg), wspec(wu),
                  pl.BlockSpec((tm, V7X_LANES), pos), pl.BlockSpec((tm, V7X_LANES), pos),
                  pl.BlockSpec((tm, V7X_LANES), pos)],
        out_specs=[pl.BlockSpec((tm, o.shape[1]), row) for o in out_shape],
        out_shape=out_shape,
        compiler_params=_cparams(("arbitrary",)),
        name="proj",
    )(x2, g, wq, wkv, wg, wu, cos, slo, shi)


def _compress_kernel(k_ref, v_ref, pea_ref, peb_ref, kw1a_ref, kw1b_ref, kw2_ref, vw1a_ref, vw1b_ref, vw2_ref,
                     kc_ref, vc_ref, ca_ref, cb_ref):
    nch = ca_ref.shape[0]

    def one(src_ref, w1a_ref, w1b_ref, w2_ref, dst_ref):
        for t in range(CMP_STRIDE):
            rows = src_ref[pl.ds(t, nch, stride=CMP_STRIDE), :]
            sl = slice(t * KV_LANES, (t + 1) * KV_LANES)
            ca_ref[:, sl] = (rows + pea_ref[:, sl]).astype(BF16)
            cb_ref[:, sl] = (rows + peb_ref[:, sl]).astype(BF16)
        ha = jnp.dot(ca_ref[...], w1a_ref[...], preferred_element_type=F32)
        hb = jnp.dot(cb_ref[...], w1b_ref[...], preferred_element_type=F32)
        hid = _gelu(ha + pltpu.roll(hb, nch - 1, axis=0))
        dst_ref[0] = jnp.dot(hid.astype(BF16), w2_ref[...], preferred_element_type=F32).astype(BF16)

    one(k_ref, kw1a_ref, kw1b_ref, kw2_ref, kc_ref)
    one(v_ref, vw1a_ref, vw1b_ref, vw2_ref, vc_ref)


def _compress_call(kc_raw, vc_raw, pea, peb, kw1a, kw1b, kw2, vw1a, vw1b, vw2, batch, seq):
    nch = seq // CMP_STRIDE
    full = lambda b: (0, 0)
    wspec = lambda a: pl.BlockSpec(a.shape, full)
    out = jax.ShapeDtypeStruct((batch, nch, KV_LANES), BF16)
    return pl.pallas_call(
        _compress_kernel,
        grid=(batch,),
        in_specs=[pl.BlockSpec((seq, KV_LANES), lambda b: (b, 0)), pl.BlockSpec((seq, KV_LANES), lambda b: (b, 0)),
                  wspec(pea), wspec(peb), wspec(kw1a), wspec(kw1b), wspec(kw2), wspec(vw1a), wspec(vw1b), wspec(vw2)],
        out_specs=[pl.BlockSpec((1, nch, KV_LANES), lambda b: (b, 0, 0))] * 2,
        out_shape=[out, out],
        scratch_shapes=[pltpu.VMEM((nch, CMP_STRIDE * KV_LANES), BF16), pltpu.VMEM((nch, CMP_STRIDE * KV_LANES), BF16)],
        compiler_params=_cparams(("arbitrary",)),
        name="compress",
    )(kc_raw, vc_raw, pea, peb, kw1a, kw1b, kw2, vw1a, vw1b, vw2)


def _nsa_kernel(qraw_ref, qrot_ref, gate_ref, kc_ref, vc_ref, ksa_ref, vs_ref, kw_ref, vw_ref, ovt_ref,
                o_ref, score_ref, *, n_sel):
    i = pl.program_id(1)
    qb = SEL_BLOCK
    hg = HEADS_PER_GROUP
    rows = hg * qb
    nb = score_ref.shape[0]
    nc = kc_ref.shape[1]
    t0 = i * qb
    tq_rows = t0 + lax.broadcasted_iota(jnp.int32, (rows, 1), 0) % qb

    def group_q(ref, g):
        return jnp.concatenate(
            [ref[0, :, (g * hg + h) * KV_LANES:(g * hg + h + 1) * KV_LANES] for h in range(hg)], axis=0)

    cmp_end = lax.broadcasted_iota(jnp.int32, (1, nc), 1) * CMP_STRIDE + (CMP_BLOCK - 1)
    mask_c = cmp_end <= tq_rows
    o_cmp, p_sum = [], []
    for g in range(NSA_KV_HEADS):
        s = lax.dot_general(group_q(qraw_ref, g), kc_ref[0], _NT, preferred_element_type=F32)
        s = jnp.where(mask_c, s, NEG_INF)
        e = jnp.where(mask_c, jnp.exp2(s - jnp.max(s, axis=-1, keepdims=True)), 0.0)
        l = jnp.sum(e, axis=-1, keepdims=True)
        p = e / jnp.where(l > 0.0, l, 1.0)
        o_cmp.append(jnp.dot(p.astype(BF16), vc_ref[0], preferred_element_type=F32))
        p_sum.append(p[0:qb] + p[qb:2 * qb] + p[2 * qb:3 * qb] + p[3 * qb:4 * qb])
    p2 = jnp.concatenate(p_sum, axis=0)
    p_hi = p2.astype(BF16)
    p_lo = (p2 - p_hi.astype(F32)).astype(BF16)
    imp_t = (lax.dot_general(ovt_ref[...], p_hi, _NT, preferred_element_type=F32)
             + lax.dot_general(ovt_ref[...], p_lo, _NT, preferred_element_type=F32))

    jcol = lax.broadcasted_iota(jnp.int32, (nb, 1), 0)
    valid = jcol <= i
    forced = (jcol == 0) | (jcol == i) | (jcol == i - 1)
    score = jnp.where(valid, imp_t + jnp.where(forced, FORCE_BONUS, 0.0), NEG_INF)
    score_ref[...] = score
    jfull = lax.broadcasted_iota(jnp.int32, score.shape, 0)

    def count_above(jp, cnt):
        row = jnp.broadcast_to(score_ref[pl.ds(jp, 1), :], score.shape)
        return cnt + jnp.where(row > score, 1.0, 0.0)

    def count_above_or_tied_earlier(jp, cnt):
        row = jnp.broadcast_to(score_ref[pl.ds(jp, 1), :], score.shape)
        tie = jnp.where(jp < jfull, 1.0, 0.0)
        return cnt + jnp.where(row > score, 1.0, jnp.where(row == score, tie, 0.0))

    def members(count_fn):
        cnt = lax.fori_loop(0, i + 1, count_fn, jnp.zeros(score.shape, F32))
        return jnp.where(valid, jnp.where(cnt < float(n_sel), 1.0, 0.0), 0.0)

    sel_t = members(count_above)
    picked = jnp.sum(sel_t, axis=0, keepdims=True)
    wrong = jnp.sum(jnp.where(picked == jnp.minimum(i + 1, n_sel).astype(F32), 0.0, 1.0))
    sel_t = lax.cond(wrong == 0.0, lambda: sel_t, lambda: members(count_above_or_tied_earlier))
    bias = ((sel_t.T - 1.0) * MASK_BIAS).astype(BF16)

    n_tiles = i // (SEL_TILE // qb) + 1
    lane_t = lax.broadcasted_iota(jnp.int32, (1, SEL_TILE), 1)
    qa = [jnp.concatenate([group_q(qrot_ref, g), jnp.concatenate([bias[g * qb:(g + 1) * qb]] * hg, axis=0)], axis=1)
          for g in range(NSA_KV_HEADS)]

    def tile_body(c, carry, causal):
        k0 = pl.multiple_of(c * SEL_TILE, SEL_TILE)
        ka = ksa_ref[0, pl.ds(k0, SEL_TILE), :]
        v = vs_ref[0, pl.ds(k0, SEL_TILE), :]
        new = []
        for g in range(NSA_KV_HEADS):
            m, l, acc = carry[g]
            s = lax.dot_general(qa[g], ka, _NT, preferred_element_type=F32)
            if causal:
                s = jnp.where((k0 + lane_t) <= tq_rows, s, NEG_INF)
            m_new = jnp.maximum(m, jnp.max(s, axis=-1, keepdims=True))
            p = jnp.exp2(s - m_new)
            alpha = jnp.exp2(m - m_new)
            l = alpha * l + jnp.sum(p, axis=-1, keepdims=True)
            acc = alpha * acc + jnp.dot(p.astype(BF16), v, preferred_element_type=F32)
            new.append((m_new, l, acc))
        return tuple(new)

    init = tuple((jnp.full((rows, 1), NEG_INF, F32), jnp.zeros((rows, 1), F32), jnp.zeros((rows, KV_LANES), F32))
                 for _ in range(NSA_KV_HEADS))
    carry = lax.fori_loop(0, n_tiles - 1, functools.partial(tile_body, causal=False), init)
    carry = tile_body(n_tiles - 1, carry, causal=True)
    o_sel = [acc / l for _, l, acc in carry]

    w0 = pl.multiple_of(jnp.maximum(i - (WIN_KEYS // qb - 1), 0) * qb, qb)
    kp = w0 + lax.broadcasted_iota(jnp.int32, (1, WIN_KEYS), 1)
    ok_w = (kp <= tq_rows) & (kp > tq_rows - WINDOW)
    o_win = []
    for g in range(NSA_KV_HEADS):
        s = lax.dot_general(group_q(qrot_ref, g), kw_ref[0, pl.ds(w0, WIN_KEYS), :], _NT, preferred_element_type=F32)
        s = jnp.where(ok_w, s, NEG_INF)
        e = jnp.exp2(s - jnp.max(s, axis=-1, keepdims=True))
        l = jnp.sum(e, axis=-1, keepdims=True)
        o_win.append(jnp.dot(e.astype(BF16), vw_ref[0, pl.ds(w0, WIN_KEYS), :], preferred_element_type=F32) / l)

    gt = gate_ref[0]
    lane = lax.broadcasted_iota(jnp.int32, (qb, KV_LANES), 1)
    for h in range(hg):
        r = slice(h * qb, (h + 1) * qb)
        parts = []
        for g in range(NSA_KV_HEADS):
            c = (g * hg + h) * 3
            parts.append(gt[:, c:c + 1] * o_cmp[g][r] + gt[:, c + 1:c + 2] * o_sel[g][r] + gt[:, c + 2:c + 3] * o_win[g][r])
        out = parts[0]
        for g in range(1, NSA_KV_HEADS):
            out = jnp.where(lane >= g * HEAD_DIM, parts[g], out)
        o_ref[0, :, h * KV_LANES:(h + 1) * KV_LANES] = out.astype(BF16)


def _nsa_call(qraw, qrot, gates, kc, vc, ksa, vs, kw, vw, ovt):
    batch, seq, _ = qraw.shape
    nq = seq // SEL_BLOCK
    qspec = pl.BlockSpec((1, SEL_BLOCK, Q_LANES), lambda b, i: (b, i, 0))
    whole = lambda a: pl.BlockSpec((1,) + a.shape[1:], lambda b, i: (b, 0, 0))
    const = lambda a: pl.BlockSpec(a.shape, lambda b, i: (0, 0))
    return pl.pallas_call(
        functools.partial(_nsa_kernel, n_sel=min(N_SEL, seq // SEL_BLOCK)),
        grid=(batch, nq),
        in_specs=[qspec, qspec, pl.BlockSpec((1, SEL_BLOCK, V7X_LANES), lambda b, i: (b, i, 0)),
                  whole(kc), whole(vc), whole(ksa), whole(vs), whole(kw), whole(vw), const(ovt)],
        out_specs=pl.BlockSpec((1, SEL_BLOCK, HEADS_PER_GROUP * KV_LANES), lambda b, i: (b, i, 0)),
        out_shape=jax.ShapeDtypeStruct((batch, seq, HEADS_PER_GROUP * KV_LANES), BF16),
        scratch_shapes=[pltpu.VMEM((V7X_LANES, NSA_KV_HEADS * SEL_BLOCK), F32)],
        compiler_params=_cparams(("arbitrary", "arbitrary")),
        name="nsa",
    )(qraw, qrot, gates, kc, vc, ksa, vs, kw, vw, ovt)


def _s5_kernel(u_ref, mp_ref, q_ref, lr_ref, li_ref, d_ref, o_ref, x_ref, y_ref, z_ref, sp_ref):
    nch = x_ref.shape[0]
    lw = SSM_CHUNK * V7X_LANES
    half = z_ref.shape[1] // 2
    for t in range(SSM_CHUNK):
        x_ref[:, t * V7X_LANES:(t + 1) * V7X_LANES] = u_ref[pl.ds(t, nch, stride=SSM_CHUNK), :].astype(BF16)
    y_ref[...] = jnp.dot(x_ref[...], mp_ref[0, :, :lw], preferred_element_type=F32)
    z_ref[...] = jnp.dot(x_ref[...], mp_ref[0, :, lw:], preferred_element_type=F32)
    lr, li = lr_ref[0], li_ref[0]

    def scan_body(k, carry):
        sr, si = carry
        sp_ref[pl.ds(k, 1), 0:half] = sr
        sp_ref[pl.ds(k, 1), half:2 * half] = si
        zr = z_ref[pl.ds(k, 1), 0:half]
        zi = z_ref[pl.ds(k, 1), half:2 * half]
        return lr * sr - li * si + zr, lr * si + li * sr + zi

    zero = jnp.zeros((1, half), F32)
    lax.fori_loop(0, nch, scan_body, (zero, zero))
    y = y_ref[...] + jnp.dot(sp_ref[...].astype(BF16), q_ref[0], preferred_element_type=F32)
    d = d_ref[...]
    for t in range(SSM_CHUNK):
        yt = y[:, t * V7X_LANES:(t + 1) * V7X_LANES] + d * u_ref[pl.ds(t, nch, stride=SSM_CHUNK), :]
        o_ref[pl.ds(t, nch, stride=SSM_CHUNK), :] = _gelu(yt)


def _s5_call(u, mp, q, lr, li, dskip, batch, seq):
    nsg = mp.shape[0]
    nch = seq // SSM_CHUNK
    nstate = q.shape[1]
    lw = SSM_CHUNK * V7X_LANES
    return pl.pallas_call(
        _s5_kernel,
        grid=(nsg, batch),
        in_specs=[pl.BlockSpec((seq, V7X_LANES), lambda g, b: (b, g)),
                  pl.BlockSpec((1,) + mp.shape[1:], lambda g, b: (g, 0, 0), pipeline_mode=pl.Buffered(1)),
                  pl.BlockSpec((1,) + q.shape[1:], lambda g, b: (g, 0, 0), pipeline_mode=pl.Buffered(1)),
                  pl.BlockSpec((1, 1, nstate // 2), lambda g, b: (g, 0, 0)),
                  pl.BlockSpec((1, 1, nstate // 2), lambda g, b: (g, 0, 0)),
                  pl.BlockSpec((1, V7X_LANES), lambda g, b: (0, g))],
        out_specs=pl.BlockSpec((seq, V7X_LANES), lambda g, b: (b, g)),
        out_shape=jax.ShapeDtypeStruct(u.shape, F32),
        scratch_shapes=[pltpu.VMEM((nch, lw), BF16), pltpu.VMEM((nch, lw), F32),
                        pltpu.VMEM((nch, nstate), F32), pltpu.VMEM((nch, nstate), F32)],
        compiler_params=_cparams(("arbitrary", "arbitrary")),
        name="s5",
    )(u, mp, q, lr, li, dskip)


def _merge_kernel(x_ref, g_ref, attn_ref, ssm_ref, wga_ref, wgb_ref, wattn_ref, wval_ref, wgate_ref, wout_ref, o_ref):
    x = x_ref[...]
    hb = _rmsnorm(x, g_ref[...]).astype(BF16)
    dot = functools.partial(jnp.dot, preferred_element_type=F32)
    y_a = dot(attn_ref[...], wattn_ref[...])
    ys = ssm_ref[...].astype(BF16)
    y_b = dot(ys, wval_ref[...]) * jax.nn.sigmoid(dot(ys, wgate_ref[...]))
    merged = jax.nn.sigmoid(dot(hb, wga_ref[...])) * y_a + jax.nn.sigmoid(dot(hb, wgb_ref[...])) * y_b
    o_ref[...] = x + dot(merged.astype(BF16), wout_ref[...])


def _merge_call(x2, g, attn, ssm, wga, wgb, wattn, wval, wgate, wout):
    t, d = x2.shape
    tm = ROW_TILE
    row = lambda i: (i, 0)
    wspec = lambda a: pl.BlockSpec(a.shape, lambda i: (0, 0))
    return pl.pallas_call(
        _merge_kernel,
        grid=(t // tm,),
        in_specs=[pl.BlockSpec((tm, d), row), wspec(g), pl.BlockSpec((tm, attn.shape[1]), row),
                  pl.BlockSpec((tm, ssm.shape[1]), row), wspec(wga), wspec(wgb), wspec(wattn), wspec(wval),
                  wspec(wgate), wspec(wout)],
        out_specs=pl.BlockSpec((tm, d), row),
        out_shape=jax.ShapeDtypeStruct((t, d), F32),
        compiler_params=_cparams(("arbitrary",)),
        name="merge",
    )(x2, g, attn, ssm, wga, wgb, wattn, wval, wgate, wout)


def _mlp_kernel(x_ref, g_ref, wup_ref, wdown_ref, gf_ref, o_ref):
    x = x_ref[...]
    hb = _rmsnorm(x, g_ref[...]).astype(BF16)
    acc = x
    for c in range(wup_ref.shape[1] // FF_CHUNK):
        sl = slice(c * FF_CHUNK, (c + 1) * FF_CHUNK)
        up = jnp.maximum(jnp.dot(hb, wup_ref[:, sl], preferred_element_type=F32), 0.0)
        acc = acc + jnp.dot((up * up).astype(BF16), wdown_ref[sl, :], preferred_element_type=F32)
    o_ref[...] = _rmsnorm(acc, gf_ref[...])


def _mlp_call(x1, g, wup, wdown, gf):
    t, d = x1.shape
    tm = ROW_TILE
    row = lambda i: (i, 0)
    wspec = lambda a: pl.BlockSpec(a.shape, lambda i: (0, 0))
    return pl.pallas_call(
        _mlp_kernel,
        grid=(t // tm,),
        in_specs=[pl.BlockSpec((tm, d), row), wspec(g), wspec(wup), wspec(wdown), wspec(gf)],
        out_specs=pl.BlockSpec((tm, d), row),
        out_shape=jax.ShapeDtypeStruct((t, d), F32),
        compiler_params=_cparams(("arbitrary",)),
        name="mlp",
    )(x1, g, wup, wdown, gf)


def _pad_heads_to_group_lanes(wq):
    d = wq.shape[0]
    w = wq.reshape(d, NSA_KV_HEADS, HEADS_PER_GROUP, HEAD_DIM)
    eye = jnp.eye(NSA_KV_HEADS, dtype=wq.dtype)
    return jnp.einsum('dghe,gk->dghke', w, eye).reshape(d, Q_LANES)


def _rope_tables(seq):
    half = ROPE_DIM // 2
    inv = ROPE_THETA ** (-(jnp.arange(half, dtype=F32) * 2.0) / ROPE_DIM)
    ang = jnp.arange(seq, dtype=F32)[:, None] * inv[None, :]
    cos, sin = jnp.cos(ang), jnp.sin(ang)
    rest = HEAD_DIM - ROPE_DIM
    cos_h = jnp.concatenate([cos, cos, jnp.ones((seq, rest), F32)], axis=1)
    slo_h = jnp.concatenate([-sin, jnp.zeros((seq, half + rest), F32)], axis=1)
    shi_h = jnp.concatenate([jnp.zeros((seq, half), F32), sin, jnp.zeros((seq, rest), F32)], axis=1)
    reps = V7X_LANES // HEAD_DIM
    return jnp.tile(cos_h, (1, reps)), jnp.tile(slo_h, (1, reps)), jnp.tile(shi_h, (1, reps))


def _compress_weights(pe, w1, w2):
    eye = jnp.eye(NSA_KV_HEADS, dtype=F32)
    w1e = jnp.einsum('tdj,gk->tgdkj', w1.reshape(CMP_BLOCK, HEAD_DIM, CMP_HIDDEN), eye)
    w1e = w1e.reshape(CMP_BLOCK * KV_LANES, NSA_KV_HEADS * CMP_HIDDEN).astype(BF16)
    w2e = jnp.einsum('jd,gk->gjkd', w2, eye).reshape(NSA_KV_HEADS * CMP_HIDDEN, KV_LANES).astype(BF16)
    pee = jnp.tile(pe, (1, NSA_KV_HEADS)).reshape(1, CMP_BLOCK * KV_LANES)
    halfw = CMP_STRIDE * KV_LANES
    return pee[:, :halfw], pee[:, halfw:], w1e[:halfw], w1e[halfw:], w2e


def _selection_constants(seq):
    nc = seq // CMP_STRIDE - 1
    nb = seq // SEL_BLOCK
    n_np = np.arange(nc)[:, None] * CMP_STRIDE
    j_np = np.arange(nb)[None, :] * SEL_BLOCK
    overlap = ((n_np < j_np + SEL_BLOCK) & (n_np + CMP_BLOCK > j_np)).astype(np.float32)
    ovt = np.zeros((V7X_LANES, nc + 1), np.float32)
    ovt[:nb, :nc] = overlap.T
    return jnp.asarray(ovt, BF16)


def _s5_matrices(lam_re, lam_im, log_step, b_re, b_im, c_re, c_im):
    hp = lax.Precision.HIGHEST
    ng, ns = lam_re.shape
    gc = b_re.shape[-1]
    L = SSM_CHUNK
    step = jnp.exp(log_step)[:, None]
    a, b = lam_re * step, lam_im * step
    k = jnp.arange(L + 1, dtype=F32)[:, None, None]
    mag = jnp.exp(a[None] * k)
    pr, pi = mag * jnp.cos(b[None] * k), mag * jnp.sin(b[None] * k)
    nr, ni = pr[1] - 1.0, pi[1]
    den = lam_re * lam_re + lam_im * lam_im
    cr, ci = (nr * lam_re + ni * lam_im) / den, (ni * lam_re - nr * lam_im) / den
    bbr = cr[..., None] * b_re - ci[..., None] * b_im
    bbi = cr[..., None] * b_im + ci[..., None] * b_re
    cpr = c_re[None] * pr[:, :, None, :] - c_im[None] * pi[:, :, None, :]
    cpi = c_re[None] * pi[:, :, None, :] + c_im[None] * pr[:, :, None, :]
    kk = (jnp.einsum('kgcn,gnd->kgcd', cpr[:L], bbr, precision=hp)
          - jnp.einsum('kgcn,gnd->kgcd', cpi[:L], bbi, precision=hp))
    lag = np.arange(L)[None, :] - np.arange(L)[:, None]
    m = kk[np.clip(lag, 0, L - 1)] * jnp.asarray(lag >= 0, F32)[:, :, None, None, None]
    m = m.transpose(2, 0, 4, 1, 3)
    pw_r, pw_i = pr[L - 1 - np.arange(L)], pi[L - 1 - np.arange(L)]
    p_r = pw_r[:, :, :, None] * bbr[None] - pw_i[:, :, :, None] * bbi[None]
    p_i = pw_r[:, :, :, None] * bbi[None] + pw_i[:, :, :, None] * bbr[None]
    p_r, p_i = p_r.transpose(1, 0, 3, 2), p_i.transpose(1, 0, 3, 2)
    q_r = cpr[1:].transpose(1, 3, 0, 2)
    q_i = -cpi[1:].transpose(1, 3, 0, 2)
    sup = SSM_SUPER
    nsg = ng // sup
    eye = jnp.eye(sup, dtype=F32)
    lw = L * sup * gc
    m_sg = jnp.einsum('xasdtc,ab->xsadtbc', m.reshape(nsg, sup, L, gc, L, gc), eye).reshape(nsg, lw, lw)
    pr_sg = jnp.einsum('xasdn,ab->xsadbn', p_r.reshape(nsg, sup, L, gc, ns), eye).reshape(nsg, lw, sup * ns)
    pi_sg = jnp.einsum('xasdn,ab->xsadbn', p_i.reshape(nsg, sup, L, gc, ns), eye).reshape(nsg, lw, sup * ns)
    qr_sg = jnp.einsum('xantc,ab->xantbc', q_r.reshape(nsg, sup, ns, L, gc), eye).reshape(nsg, sup * ns, lw)
    qi_sg = jnp.einsum('xantc,ab->xantbc', q_i.reshape(nsg, sup, ns, L, gc), eye).reshape(nsg, sup * ns, lw)
    mp = jnp.concatenate([m_sg, pr_sg, pi_sg], axis=2).astype(BF16)
    q = jnp.concatenate([qr_sg, qi_sg], axis=1).astype(BF16)
    return mp, q, pr[L].reshape(nsg, 1, sup * ns), pi[L].reshape(nsg, 1, sup * ns)


def kernel(x, norm_mix_g, w_in, cmp_pe, cmp_k_w1, cmp_k_w2, cmp_v_w1, cmp_v_w2, ssm_lam_re, ssm_lam_im, ssm_log_step, ssm_b_re, ssm_b_im, ssm_c_re, ssm_c_im, ssm_d, w_attn_branch, w_ssm_val, w_ssm_gate, w_out, norm_mlp_g, w_up, w_down, norm_final_g):
    batch, seq, d = x.shape
    depth = w_in.shape[0]
    assert depth == 1, "the final rmsnorm is fused into the single layer's mlp kernel"
    nsa_w = NSA_HEADS * HEAD_DIM
    ssm_w = ssm_d.shape[1]
    o_q, o_kv, o_g, o_u = nsa_w, nsa_w + 6 * KV_LANES, nsa_w + 6 * KV_LANES + 3 * NSA_HEADS, 0
    o_u = o_g + ssm_w
    cos, slo, shi = _rope_tables(seq)
    ovt = _selection_constants(seq)
    head_order = np.array([g * HEADS_PER_GROUP + h for h in range(HEADS_PER_GROUP) for g in range(NSA_KV_HEADS)])
    x2 = x.reshape(batch * seq, d)
    for l in range(depth):
        wl = w_in[l]
        wq = _pad_heads_to_group_lanes(wl[:, :o_q]).astype(BF16)
        wkv = wl[:, o_q:o_kv].astype(BF16)
        wg = jnp.pad(wl[:, o_kv:o_g], ((0, 0), (0, V7X_LANES - 3 * NSA_HEADS))).astype(BF16)
        wu = wl[:, o_g:o_u].astype(BF16)
        wga = wl[:, o_u:o_u + d].astype(BF16)
        wgb = wl[:, o_u + d:].astype(BF16)
        g_mix = norm_mix_g[l].reshape(1, d)
        qraw, qrot, kc_raw, vc_raw, ksa, vs, kw, vw, gates, u = _proj_call(x2, g_mix, wq, wkv, wg, wu, cos, slo, shi, seq)

        pea, peb, kw1a, kw1b, kw2 = _compress_weights(cmp_pe[l], cmp_k_w1[l], cmp_k_w2[l])
        _, _, vw1a, vw1b, vw2 = _compress_weights(cmp_pe[l], cmp_v_w1[l], cmp_v_w2[l])
        kc, vc = _compress_call(kc_raw, vc_raw, pea, peb, kw1a, kw1b, kw2, vw1a, vw1b, vw2, batch, seq)

        b3 = lambda a: a.reshape(batch, seq, a.shape[-1])
        attn = _nsa_call(b3(qraw), b3(qrot), b3(gates), kc, vc, b3(ksa), b3(vs), b3(kw), b3(vw), ovt)
        attn = attn.reshape(batch * seq, nsa_w)

        mp, qm, lr, li = _s5_matrices(ssm_lam_re[l], ssm_lam_im[l], ssm_log_step[l], ssm_b_re[l], ssm_b_im[l],
                                      ssm_c_re[l], ssm_c_im[l])
        y_ssm = _s5_call(u, mp, qm, lr, li, ssm_d[l].reshape(1, ssm_w), batch, seq)

        wattn = w_attn_branch[l].reshape(NSA_HEADS, HEAD_DIM, d)[head_order].reshape(nsa_w, d).astype(BF16)
        x1 = _merge_call(x2, g_mix, attn, y_ssm, wga, wgb, wattn, w_ssm_val[l].astype(BF16),
                         w_ssm_gate[l].astype(BF16), w_out[l].astype(BF16))
        x2 = _mlp_call(x1, norm_mlp_g[l].reshape(1, d), w_up[l].astype(BF16), w_down[l].astype(BF16),
                       norm_final_g.reshape(1, d))
    return x2.reshape(batch, seq, d)
```

```python
import functools
import math

import jax
import jax.numpy as jnp
import numpy as np
from jax import lax
from jax.experimental import pallas as pl
from jax.experimental.pallas import tpu as pltpu

NSA_HEADS = 8
NSA_KV_HEADS = 2
HEAD_DIM = 64
CMP_BLOCK = 32
CMP_STRIDE = 16
CMP_HIDDEN = 256
SEL_BLOCK = 64
N_SEL = 16
WINDOW = 512
FORCE_BONUS = 1e3
NEG_INF = -1e30
ROPE_THETA = 500000.0
ROPE_DIM = HEAD_DIM // 4
SSM_GROUP = 16
SSM_STATE = 64
EPS = 1e-6

HEADS_PER_GROUP = NSA_HEADS // NSA_KV_HEADS
KV_LANES = NSA_KV_HEADS * HEAD_DIM
Q_LANES = NSA_HEADS * KV_LANES

V7X_LANES = 128
V7X_VMEM_BYTES = 64 * 1024 * 1024
VMEM_LIMIT = V7X_VMEM_BYTES - 8 * 1024 * 1024

ROW_TILE = 512
SEL_TILE = 512
NSA_QUERIES = 256
SSM_CHUNK = 16
SSM_SUPER = V7X_LANES // SSM_GROUP
FF_CHUNK = 1024

LOG2E = math.log2(math.e)
MASK_BIAS = 1e30

BF16 = jnp.bfloat16
F32 = jnp.float32
_NT = (((1,), (1,)), ((), ()))


def _cparams(semantics):
    return pltpu.CompilerParams(dimension_semantics=semantics, vmem_limit_bytes=VMEM_LIMIT)


def _rmsnorm(x, g):
    return x * lax.rsqrt(jnp.mean(x * x, axis=-1, keepdims=True) + EPS) * g


def _gelu(x):
    return jax.nn.gelu(x)


def _rope_cols(x, cos, sin_lo, sin_hi):
    cols = []
    for c in range(x.shape[1] // V7X_LANES):
        xc = x[:, c * V7X_LANES:(c + 1) * V7X_LANES]
        up = pltpu.roll(xc, V7X_LANES - ROPE_DIM // 2, axis=1)
        dn = pltpu.roll(xc, ROPE_DIM // 2, axis=1)
        cols.append(xc * cos + up * sin_lo + dn * sin_hi)
    return jnp.concatenate(cols, axis=1) if len(cols) > 1 else cols[0]


def _proj_kernel(x_ref, g_ref, wq_ref, wkv_ref, wg_ref, wu_ref, cos_ref, slo_ref, shi_ref,
                 qraw_ref, qrot_ref, kc_ref, vc_ref, ksat_ref, vs_ref, kwt_ref, vw_ref, gate_ref, u_ref, *, seq_tiles):
    hb = _rmsnorm(x_ref[...], g_ref[...]).astype(BF16)
    cos, slo, shi = cos_ref[...], slo_ref[...], shi_ref[...]
    q = jnp.dot(hb, wq_ref[...], preferred_element_type=F32) * (HEAD_DIM ** -0.5 * LOG2E)
    qraw_ref[...] = q.astype(BF16)
    qrot_ref[...] = _rope_cols(q, cos, slo, shi).astype(BF16)
    kv = jnp.dot(hb, wkv_ref[...], preferred_element_type=F32)
    w = KV_LANES
    kc_ref[...] = kv[:, 0 * w:1 * w]
    vc_ref[...] = kv[:, 1 * w:2 * w]
    tm = x_ref.shape[0]
    pos = (pl.program_id(0) % seq_tiles) * tm + lax.broadcasted_iota(jnp.int32, (V7X_LANES, tm), 1)
    onehot = jnp.where(lax.broadcasted_iota(jnp.int32, (V7X_LANES, tm), 0) == pos // SEL_BLOCK, 1.0, 0.0)
    ksat_ref[0, :w, :] = _rope_cols(kv[:, 2 * w:3 * w], cos, slo, shi).T.astype(BF16)
    ksat_ref[0, w:, :] = onehot.astype(BF16)
    vs_ref[...] = kv[:, 3 * w:4 * w].astype(BF16)
    kwt_ref[0] = _rope_cols(kv[:, 4 * w:5 * w], cos, slo, shi).T.astype(BF16)
    vw_ref[...] = kv[:, 5 * w:6 * w].astype(BF16)
    gate_ref[...] = jax.nn.sigmoid(jnp.dot(hb, wg_ref[...], preferred_element_type=F32))
    u_ref[...] = jnp.dot(hb, wu_ref[...], preferred_element_type=F32)


def _proj_call(x2, g, wq, wkv, wg, wu, cos, slo, shi, batch, seq):
    t, d = x2.shape
    tm = ROW_TILE
    s_tiles = seq // tm
    row = lambda i: (i, 0)
    pos = lambda i: (i % s_tiles, 0)
    trans = lambda i: (i // s_tiles, 0, i % s_tiles)
    wspec = lambda a: pl.BlockSpec(a.shape, lambda i: (0, 0))
    rows_out = lambda n, dt: (jax.ShapeDtypeStruct((t, n), dt), pl.BlockSpec((tm, n), row))
    trans_out = lambda n: (jax.ShapeDtypeStruct((batch, n, seq), BF16), pl.BlockSpec((1, n, tm), trans))
    outs = [rows_out(Q_LANES, BF16), rows_out(Q_LANES, BF16), rows_out(KV_LANES, F32), rows_out(KV_LANES, F32),
            trans_out(KV_LANES + V7X_LANES), rows_out(KV_LANES, BF16), trans_out(KV_LANES), rows_out(KV_LANES, BF16),
            rows_out(V7X_LANES, F32), rows_out(wu.shape[1], F32)]
    assert seq // SEL_BLOCK <= V7X_LANES, "the selection-block one-hot must fit one lane tile"
    return pl.pallas_call(
        functools.partial(_proj_kernel, seq_tiles=s_tiles),
        grid=(t // tm,),
        in_specs=[pl.BlockSpec((tm, d), row), wspec(g), wspec(wq), wspec(wkv), wspec(wg), wspec(wu),
                  pl.BlockSpec((tm, V7X_LANES), pos), pl.BlockSpec((tm, V7X_LANES), pos),
                  pl.BlockSpec((tm, V7X_LANES), pos)],
        out_specs=[o[1] for o in outs],
        out_shape=[o[0] for o in outs],
        compiler_params=_cparams(("arbitrary",)),
        name="proj",
    )(x2, g, wq, wkv, wg, wu, cos, slo, shi)


def _compress_kernel(k_ref, v_ref, pea_ref, peb_ref, kw1a_ref, kw1b_ref, kw2t_ref, vw1a_ref, vw1b_ref, vw2_ref,
                     kct_ref, vc_ref, ca_ref, cb_ref):
    nch = ca_ref.shape[0]

    def hidden(src_ref, w1a_ref, w1b_ref):
        for t in range(CMP_STRIDE):
            rows = src_ref[pl.ds(t, nch, stride=CMP_STRIDE), :]
            sl = slice(t * KV_LANES, (t + 1) * KV_LANES)
            ca_ref[:, sl] = (rows + pea_ref[:, sl]).astype(BF16)
            cb_ref[:, sl] = (rows + peb_ref[:, sl]).astype(BF16)
        ha = jnp.dot(ca_ref[...], w1a_ref[...], preferred_element_type=F32)
        hb = jnp.dot(cb_ref[...], w1b_ref[...], preferred_element_type=F32)
        return _gelu(ha + pltpu.roll(hb, nch - 1, axis=0)).astype(BF16)

    kct_ref[0] = lax.dot_general(kw2t_ref[...], hidden(k_ref, kw1a_ref, kw1b_ref), _NT,
                                 preferred_element_type=F32).astype(BF16)
    vc_ref[0] = jnp.dot(hidden(v_ref, vw1a_ref, vw1b_ref), vw2_ref[...], preferred_element_type=F32).astype(BF16)


def _compress_call(kc_raw, vc_raw, pea, peb, kw1a, kw1b, kw2, vw1a, vw1b, vw2, batch, seq):
    nch = seq // CMP_STRIDE
    wspec = lambda a: pl.BlockSpec(a.shape, lambda b: (0, 0))
    return pl.pallas_call(
        _compress_kernel,
        grid=(batch,),
        in_specs=[pl.BlockSpec((seq, KV_LANES), lambda b: (b, 0)), pl.BlockSpec((seq, KV_LANES), lambda b: (b, 0)),
                  wspec(pea), wspec(peb), wspec(kw1a), wspec(kw1b), wspec(kw2), wspec(vw1a), wspec(vw1b), wspec(vw2)],
        out_specs=[pl.BlockSpec((1, KV_LANES, nch), lambda b: (b, 0, 0)),
                   pl.BlockSpec((1, nch, KV_LANES), lambda b: (b, 0, 0))],
        out_shape=[jax.ShapeDtypeStruct((batch, KV_LANES, nch), BF16), jax.ShapeDtypeStruct((batch, nch, KV_LANES), BF16)],
        scratch_shapes=[pltpu.VMEM((nch, CMP_STRIDE * KV_LANES), BF16), pltpu.VMEM((nch, CMP_STRIDE * KV_LANES), BF16)],
        compiler_params=_cparams(("arbitrary",)),
        name="compress",
    )(kc_raw, vc_raw, pea, peb, kw1a, kw1b, kw2, vw1a, vw1b, vw2)


def _window_keys(nq):
    return WINDOW + nq


def _nsa_kernel(qraw_ref, qrot_ref, gate_ref, kct_ref, vc_ref, ksat_ref, vs_ref, kwt_ref, vw_ref, ovt_ref,
                o_ref, score_ref, *, n_sel):
    i = pl.program_id(1)
    nq = qraw_ref.shape[1]
    hg = HEADS_PER_GROUP
    rows = hg * nq
    lanes_gq = NSA_KV_HEADS * nq
    nb = score_ref.shape[0]
    nc = kct_ref.shape[2]
    t0 = i * nq
    blk_first = t0 // SEL_BLOCK
    blk_last = blk_first + nq // SEL_BLOCK - 1
    tq_rows = t0 + lax.broadcasted_iota(jnp.int32, (rows, 1), 0) % nq

    def group_q(ref, g):
        return jnp.concatenate(
            [ref[0, :, (g * hg + h) * KV_LANES:(g * hg + h + 1) * KV_LANES] for h in range(hg)], axis=0)

    def compressed_branch(width):
        cmp_end = lax.broadcasted_iota(jnp.int32, (1, width), 1) * CMP_STRIDE + (CMP_BLOCK - 1)
        bias_c = jnp.where(cmp_end <= tq_rows, 0.0, NEG_INF)
        any_valid = tq_rows >= CMP_BLOCK - 1
        outs, p_sum = [], []
        for g in range(NSA_KV_HEADS):
            s = jnp.dot(group_q(qraw_ref, g), kct_ref[0, :, :width], preferred_element_type=F32) + bias_c
            e = jnp.exp2(s - jnp.max(s, axis=-1, keepdims=True))
            p = e * jnp.where(any_valid, 1.0 / jnp.sum(e, axis=-1, keepdims=True), 0.0)
            outs.append(jnp.dot(p.astype(BF16), vc_ref[0, :width, :], preferred_element_type=F32))
            ph = p[0:nq]
            for h in range(1, hg):
                ph = ph + p[h * nq:(h + 1) * nq]
            p_sum.append(ph)
        p2 = jnp.concatenate(p_sum, axis=0)
        p_hi = p2.astype(BF16)
        p_lo = (p2 - p_hi.astype(F32)).astype(BF16)
        ovt = ovt_ref[:, :width]
        imp = (lax.dot_general(ovt, p_hi, _NT, preferred_element_type=F32)
               + lax.dot_general(ovt, p_lo, _NT, preferred_element_type=F32))
        return tuple(outs) + (imp,)

    *o_cmp, imp_t = compressed_branch(nc)

    blk_q = (t0 + lax.broadcasted_iota(jnp.int32, (1, lanes_gq), 1) % nq) // SEL_BLOCK
    jfull = lax.broadcasted_iota(jnp.int32, (nb, lanes_gq), 0)
    valid = jfull <= blk_q
    forced = (jfull == 0) | (jfull == blk_q) | (jfull == blk_q - 1)
    score = jnp.where(valid, imp_t + jnp.where(forced, FORCE_BONUS, 0.0), NEG_INF)

    rest = score
    for _ in range(n_sel):
        thr = jnp.max(rest, axis=0, keepdims=True)
        rest = jnp.where(rest >= thr, NEG_INF, rest)
    sel_t = jnp.where(valid, jnp.where(score >= thr, 1.0, 0.0), 0.0)

    def ranked_members():
        def count_above_or_tied_earlier(jp, cnt):
            row = jnp.broadcast_to(score_ref[pl.ds(jp, 1), :], score.shape)
            tie = jnp.where(jp < jfull, 1.0, 0.0)
            return cnt + jnp.where(row > score, 1.0, jnp.where(row == score, tie, 0.0))

        score_ref[...] = score
        cnt = lax.fori_loop(0, blk_last + 1, count_above_or_tied_earlier, jnp.zeros(score.shape, F32))
        return jnp.where(valid, jnp.where(cnt < float(n_sel), 1.0, 0.0), 0.0)

    picked = jnp.sum(sel_t, axis=0, keepdims=True)
    wrong = jnp.sum(jnp.where(picked == jnp.minimum(blk_q + 1, n_sel).astype(F32), 0.0, 1.0))
    sel_t = lax.cond(wrong == 0.0, lambda: sel_t, ranked_members)
    bias = ((sel_t.T - 1.0) * MASK_BIAS).astype(BF16)

    n_tiles = blk_last // (SEL_TILE // SEL_BLOCK) + 1
    lane_t = lax.broadcasted_iota(jnp.int32, (1, SEL_TILE), 1)
    qa = [jnp.concatenate([group_q(qrot_ref, g), jnp.concatenate([bias[g * nq:(g + 1) * nq]] * hg, axis=0)], axis=1)
          for g in range(NSA_KV_HEADS)]

    def tile_body(c, carry, causal):
        k0 = pl.multiple_of(c * SEL_TILE, SEL_TILE)
        ka = ksat_ref[0, :, pl.ds(k0, SEL_TILE)]
        v = vs_ref[0, pl.ds(k0, SEL_TILE), :]
        new = []
        for g in range(NSA_KV_HEADS):
            m, l, acc = carry[g]
            s = jnp.dot(qa[g], ka, preferred_element_type=F32)
            if causal:
                s = jnp.where((k0 + lane_t) <= tq_rows, s, NEG_INF)
            m_new = jnp.maximum(m, jnp.max(s, axis=-1, keepdims=True))
            p = jnp.exp2(s - m_new)
            alpha = jnp.exp2(m - m_new)
            l = alpha * l + jnp.sum(p, axis=-1, keepdims=True)
            acc = alpha * acc + jnp.dot(p.astype(BF16), v, preferred_element_type=F32)
            new.append((m_new, l, acc))
        return tuple(new)

    init = tuple((jnp.full((rows, 1), NEG_INF, F32), jnp.zeros((rows, 1), F32), jnp.zeros((rows, KV_LANES), F32))
                 for _ in range(NSA_KV_HEADS))
    carry = lax.fori_loop(0, n_tiles - 1, functools.partial(tile_body, causal=False), init)
    carry = tile_body(n_tiles - 1, carry, causal=True)
    o_sel = [acc / l for _, l, acc in carry]

    nwin = _window_keys(nq)
    w0 = pl.multiple_of(jnp.maximum(blk_first - WINDOW // SEL_BLOCK, 0) * SEL_BLOCK, V7X_LANES)
    kp = w0 + lax.broadcasted_iota(jnp.int32, (1, nwin), 1)
    bias_w = jnp.where(kp <= tq_rows, jnp.where(kp > tq_rows - WINDOW, 0.0, NEG_INF), NEG_INF)
    o_win = []
    for g in range(NSA_KV_HEADS):
        s = jnp.dot(group_q(qrot_ref, g), kwt_ref[0, :, pl.ds(w0, nwin)], preferred_element_type=F32) + bias_w
        e = jnp.exp2(s - jnp.max(s, axis=-1, keepdims=True))
        l = jnp.sum(e, axis=-1, keepdims=True)
        o_win.append(jnp.dot(e.astype(BF16), vw_ref[0, pl.ds(w0, nwin), :], preferred_element_type=F32) / l)

    gt = gate_ref[0]
    lane = lax.broadcasted_iota(jnp.int32, (nq, KV_LANES), 1)
    for h in range(hg):
        r = slice(h * nq, (h + 1) * nq)
        parts = []
        for g in range(NSA_KV_HEADS):
            c = (g * hg + h) * 3
            parts.append(gt[:, c:c + 1] * o_cmp[g][r] + gt[:, c + 1:c + 2] * o_sel[g][r] + gt[:, c + 2:c + 3] * o_win[g][r])
        out = parts[0]
        for g in range(1, NSA_KV_HEADS):
            out = jnp.where(lane >= g * HEAD_DIM, parts[g], out)
        o_ref[0, :, h * KV_LANES:(h + 1) * KV_LANES] = out.astype(BF16)


def _nsa_call(qraw, qrot, gates, kct, vc, ksat, vs, kwt, vw, ovt):
    batch, seq, _ = qraw.shape
    nq = NSA_QUERIES
    assert nq % V7X_LANES == 0 and seq % nq == 0 and seq >= _window_keys(nq) and seq % SEL_TILE == 0
    step = lambda n: pl.BlockSpec((1, nq, n), lambda b, i: (b, i, 0))
    whole = lambda a: pl.BlockSpec((1,) + a.shape[1:], lambda b, i: (b, 0, 0))
    return pl.pallas_call(
        functools.partial(_nsa_kernel, n_sel=min(N_SEL, seq // SEL_BLOCK)),
        grid=(batch, seq // nq),
        in_specs=[step(Q_LANES), step(Q_LANES), step(V7X_LANES),
                  whole(kct), whole(vc), whole(ksat), whole(vs), whole(kwt), whole(vw),
                  pl.BlockSpec(ovt.shape, lambda b, i: (0, 0))],
        out_specs=step(HEADS_PER_GROUP * KV_LANES),
        out_shape=jax.ShapeDtypeStruct((batch, seq, HEADS_PER_GROUP * KV_LANES), BF16),
        scratch_shapes=[pltpu.VMEM((V7X_LANES, NSA_KV_HEADS * nq), F32)],
        compiler_params=_cparams(("arbitrary", "arbitrary")),
        name="nsa",
    )(qraw, qrot, gates, kct, vc, ksat, vs, kwt, vw, ovt)


def _s5_kernel(u_ref, lag_ref, p_ref, q_ref, lr_ref, li_ref, d_ref, o_ref, m_ref, x_ref, y_ref, z_ref, sp_ref):
    nch = x_ref.shape[0]
    half = z_ref.shape[1] // 2
    lanes = V7X_LANES

    @pl.when(pl.program_id(1) == 0)
    def _():
        for s in range(SSM_CHUNK):
            for t in range(SSM_CHUNK):
                tile = lag_ref[0, t - s] if t >= s else jnp.zeros((lanes, lanes), BF16)
                m_ref[s * lanes:(s + 1) * lanes, t * lanes:(t + 1) * lanes] = tile

    for t in range(SSM_CHUNK):
        x_ref[:, t * lanes:(t + 1) * lanes] = u_ref[pl.ds(t, nch, stride=SSM_CHUNK), :].astype(BF16)
    y_ref[...] = jnp.dot(x_ref[...], m_ref[...], preferred_element_type=F32)
    z_ref[...] = jnp.dot(x_ref[...], p_ref[0], preferred_element_type=F32)
    lr, li = lr_ref[0], li_ref[0]

    def scan_body(k, carry):
        sr, si = carry
        sp_ref[pl.ds(k, 1), 0:half] = sr
        sp_ref[pl.ds(k, 1), half:2 * half] = si
        zr = z_ref[pl.ds(k, 1), 0:half]
        zi = z_ref[pl.ds(k, 1), half:2 * half]
        return lr * sr - li * si + zr, lr * si + li * sr + zi

    zero = jnp.zeros((1, half), F32)
    lax.fori_loop(0, nch, scan_body, (zero, zero))
    y = y_ref[...] + jnp.dot(sp_ref[...].astype(BF16), q_ref[0], preferred_element_type=F32)
    d = d_ref[...]
    for t in range(SSM_CHUNK):
        yt = y[:, t * lanes:(t + 1) * lanes] + d * u_ref[pl.ds(t, nch, stride=SSM_CHUNK), :]
        o_ref[pl.ds(t, nch, stride=SSM_CHUNK), :] = _gelu(yt)


def _s5_call(u, lag, p, q, lr, li, dskip, batch, seq):
    nsg = lag.shape[0]
    nch = seq // SSM_CHUNK
    nstate = q.shape[1]
    lw = SSM_CHUNK * V7X_LANES
    slab = lambda a: pl.BlockSpec((1,) + a.shape[1:], lambda g, b: (g,) + (0,) * (a.ndim - 1),
                                  pipeline_mode=pl.Buffered(1))
    return pl.pallas_call(
        _s5_kernel,
        grid=(nsg, batch),
        in_specs=[pl.BlockSpec((seq, V7X_LANES), lambda g, b: (b, g)), slab(lag), slab(p), slab(q),
                  pl.BlockSpec((1, 1, nstate // 2), lambda g, b: (g, 0, 0)),
                  pl.BlockSpec((1, 1, nstate // 2), lambda g, b: (g, 0, 0)),
                  pl.BlockSpec((1, V7X_LANES), lambda g, b: (0, g))],
        out_specs=pl.BlockSpec((seq, V7X_LANES), lambda g, b: (b, g)),
        out_shape=jax.ShapeDtypeStruct(u.shape, F32),
        scratch_shapes=[pltpu.VMEM((lw, lw), BF16), pltpu.VMEM((nch, lw), BF16), pltpu.VMEM((nch, lw), F32),
                        pltpu.VMEM((nch, nstate), F32), pltpu.VMEM((nch, nstate), F32)],
        compiler_params=_cparams(("arbitrary", "arbitrary")),
        name="s5",
    )(u, lag, p, q, lr, li, dskip)


def _merge_kernel(x_ref, g_ref, attn_ref, ssm_ref, wga_ref, wgb_ref, wattn_ref, wval_ref, wgate_ref, wout_ref, o_ref):
    x = x_ref[...]
    hb = _rmsnorm(x, g_ref[...]).astype(BF16)
    dot = functools.partial(jnp.dot, preferred_element_type=F32)
    y_a = dot(attn_ref[...], wattn_ref[...])
    ys = ssm_ref[...].astype(BF16)
    y_b = dot(ys, wval_ref[...]) * jax.nn.sigmoid(dot(ys, wgate_ref[...]))
    merged = jax.nn.sigmoid(dot(hb, wga_ref[...])) * y_a + jax.nn.sigmoid(dot(hb, wgb_ref[...])) * y_b
    o_ref[...] = x + dot(merged.astype(BF16), wout_ref[...])


def _merge_call(x2, g, attn, ssm, wga, wgb, wattn, wval, wgate, wout):
    t, d = x2.shape
    tm = ROW_TILE
    row = lambda i: (i, 0)
    wspec = lambda a: pl.BlockSpec(a.shape, lambda i: (0, 0))
    return pl.pallas_call(
        _merge_kernel,
        grid=(t // tm,),
        in_specs=[pl.BlockSpec((tm, d), row), wspec(g), pl.BlockSpec((tm, attn.shape[1]), row),
                  pl.BlockSpec((tm, ssm.shape[1]), row), wspec(wga), wspec(wgb), wspec(wattn), wspec(wval),
                  wspec(wgate), wspec(wout)],
        out_specs=pl.BlockSpec((tm, d), row),
        out_shape=jax.ShapeDtypeStruct((t, d), F32),
        compiler_params=_cparams(("arbitrary",)),
        name="merge",
    )(x2, g, attn, ssm, wga, wgb, wattn, wval, wgate, wout)


def _mlp_kernel(x_ref, g_ref, wup_ref, wdown_ref, gf_ref, o_ref):
    x = x_ref[...]
    hb = _rmsnorm(x, g_ref[...]).astype(BF16)
    acc = x
    for c in range(wup_ref.shape[1] // FF_CHUNK):
        sl = slice(c * FF_CHUNK, (c + 1) * FF_CHUNK)
        up = jnp.maximum(jnp.dot(hb, wup_ref[:, sl], preferred_element_type=F32), 0.0)
        acc = acc + jnp.dot((up * up).astype(BF16), wdown_ref[sl, :], preferred_element_type=F32)
    o_ref[...] = _rmsnorm(acc, gf_ref[...])


def _mlp_call(x1, g, wup, wdown, gf):
    t, d = x1.shape
    tm = ROW_TILE
    row = lambda i: (i, 0)
    wspec = lambda a: pl.BlockSpec(a.shape, lambda i: (0, 0))
    return pl.pallas_call(
        _mlp_kernel,
        grid=(t // tm,),
        in_specs=[pl.BlockSpec((tm, d), row), wspec(g), wspec(wup), wspec(wdown), wspec(gf)],
        out_specs=pl.BlockSpec((tm, d), row),
        out_shape=jax.ShapeDtypeStruct((t, d), F32),
        compiler_params=_cparams(("arbitrary",)),
        name="mlp",
    )(x1, g, wup, wdown, gf)


def _pad_heads_to_group_lanes(wq):
    d = wq.shape[0]
    w = wq.reshape(d, NSA_KV_HEADS, HEADS_PER_GROUP, HEAD_DIM)
    eye = jnp.eye(NSA_KV_HEADS, dtype=wq.dtype)
    return jnp.einsum('dghe,gk->dghke', w, eye).reshape(d, Q_LANES)


def _rope_tables(seq):
    half = ROPE_DIM // 2
    inv = ROPE_THETA ** (-(jnp.arange(half, dtype=F32) * 2.0) / ROPE_DIM)
    ang = jnp.arange(seq, dtype=F32)[:, None] * inv[None, :]
    cos, sin = jnp.cos(ang), jnp.sin(ang)
    rest = HEAD_DIM - ROPE_DIM
    cos_h = jnp.concatenate([cos, cos, jnp.ones((seq, rest), F32)], axis=1)
    slo_h = jnp.concatenate([-sin, jnp.zeros((seq, half + rest), F32)], axis=1)
    shi_h = jnp.concatenate([jnp.zeros((seq, half), F32), sin, jnp.zeros((seq, rest), F32)], axis=1)
    reps = V7X_LANES // HEAD_DIM
    return jnp.tile(cos_h, (1, reps)), jnp.tile(slo_h, (1, reps)), jnp.tile(shi_h, (1, reps))


def _compress_weights(pe, w1, w2):
    eye = jnp.eye(NSA_KV_HEADS, dtype=F32)
    w1e = jnp.einsum('tdj,gk->tgdkj', w1.reshape(CMP_BLOCK, HEAD_DIM, CMP_HIDDEN), eye)
    w1e = w1e.reshape(CMP_BLOCK * KV_LANES, NSA_KV_HEADS * CMP_HIDDEN).astype(BF16)
    w2e = jnp.einsum('jd,gk->gjkd', w2, eye).reshape(NSA_KV_HEADS * CMP_HIDDEN, KV_LANES).astype(BF16)
    pee = jnp.tile(pe, (1, NSA_KV_HEADS)).reshape(1, CMP_BLOCK * KV_LANES)
    halfw = CMP_STRIDE * KV_LANES
    return pee[:, :halfw], pee[:, halfw:], w1e[:halfw], w1e[halfw:], w2e


def _selection_constants(seq):
    nc = seq // CMP_STRIDE - 1
    nb = seq // SEL_BLOCK
    n_np = np.arange(nc)[:, None] * CMP_STRIDE
    j_np = np.arange(nb)[None, :] * SEL_BLOCK
    overlap = ((n_np < j_np + SEL_BLOCK) & (n_np + CMP_BLOCK > j_np)).astype(np.float32)
    ovt = np.zeros((V7X_LANES, nc + 1), np.float32)
    ovt[:nb, :nc] = overlap.T
    return jnp.asarray(ovt, BF16)


def _s5_matrices(lam_re, lam_im, log_step, b_re, b_im, c_re, c_im):
    hp = lax.Precision.HIGHEST
    ng, ns = lam_re.shape
    gc = b_re.shape[-1]
    L = SSM_CHUNK
    step = jnp.exp(log_step)[:, None]
    a, b = lam_re * step, lam_im * step
    k = jnp.arange(L + 1, dtype=F32)[:, None, None]
    mag = jnp.exp(a[None] * k)
    pr, pi = mag * jnp.cos(b[None] * k), mag * jnp.sin(b[None] * k)
    nr, ni = pr[1] - 1.0, pi[1]
    den = lam_re * lam_re + lam_im * lam_im
    cr, ci = (nr * lam_re + ni * lam_im) / den, (ni * lam_re - nr * lam_im) / den
    bbr = cr[..., None] * b_re - ci[..., None] * b_im
    bbi = cr[..., None] * b_im + ci[..., None] * b_re
    cpr = c_re[None] * pr[:, :, None, :] - c_im[None] * pi[:, :, None, :]
    cpi = c_re[None] * pi[:, :, None, :] + c_im[None] * pr[:, :, None, :]
    kk = (jnp.einsum('kgcn,gnd->kgcd', cpr[:L], bbr, precision=hp)
          - jnp.einsum('kgcn,gnd->kgcd', cpi[:L], bbi, precision=hp))
    pw_r, pw_i = pr[L - 1 - np.arange(L)], pi[L - 1 - np.arange(L)]
    p_r = pw_r[:, :, :, None] * bbr[None] - pw_i[:, :, :, None] * bbi[None]
    p_i = pw_r[:, :, :, None] * bbi[None] + pw_i[:, :, :, None] * bbr[None]
    p_r, p_i = p_r.transpose(1, 0, 3, 2), p_i.transpose(1, 0, 3, 2)
    q_r = cpr[1:].transpose(1, 3, 0, 2)
    q_i = -cpi[1:].transpose(1, 3, 0, 2)
    sup = SSM_SUPER
    nsg = ng // sup
    eye = jnp.eye(sup, dtype=BF16)
    lw = L * sup * gc
    bf = lambda a: a.astype(BF16)
    lag_sg = jnp.einsum('kxacd,ab->xkadbc', bf(kk).reshape(L, nsg, sup, gc, gc), eye).reshape(nsg, L, sup * gc, sup * gc)
    pr_sg = jnp.einsum('xasdn,ab->xsadbn', bf(p_r).reshape(nsg, sup, L, gc, ns), eye).reshape(nsg, lw, sup * ns)
    pi_sg = jnp.einsum('xasdn,ab->xsadbn', bf(p_i).reshape(nsg, sup, L, gc, ns), eye).reshape(nsg, lw, sup * ns)
    qr_sg = jnp.einsum('xantc,ab->xantbc', bf(q_r).reshape(nsg, sup, ns, L, gc), eye).reshape(nsg, sup * ns, lw)
    qi_sg = jnp.einsum('xantc,ab->xantbc', bf(q_i).reshape(nsg, sup, ns, L, gc), eye).reshape(nsg, sup * ns, lw)
    p = jnp.concatenate([pr_sg, pi_sg], axis=2)
    q = jnp.concatenate([qr_sg, qi_sg], axis=1)
    return lag_sg, p, q, pr[L].reshape(nsg, 1, sup * ns), pi[L].reshape(nsg, 1, sup * ns)


def kernel(x, norm_mix_g, w_in, cmp_pe, cmp_k_w1, cmp_k_w2, cmp_v_w1, cmp_v_w2, ssm_lam_re, ssm_lam_im, ssm_log_step, ssm_b_re, ssm_b_im, ssm_c_re, ssm_c_im, ssm_d, w_attn_branch, w_ssm_val, w_ssm_gate, w_out, norm_mlp_g, w_up, w_down, norm_final_g):
    batch, seq, d = x.shape
    depth = w_in.shape[0]
    assert depth == 1, "the final rmsnorm is fused into the single layer's mlp kernel"
    nsa_w = NSA_HEADS * HEAD_DIM
    ssm_w = ssm_d.shape[1]
    o_q, o_kv, o_g, o_u = nsa_w, nsa_w + 6 * KV_LANES, nsa_w + 6 * KV_LANES + 3 * NSA_HEADS, 0
    o_u = o_g + ssm_w
    cos, slo, shi = _rope_tables(seq)
    ovt = _selection_constants(seq)
    head_order = np.array([g * HEADS_PER_GROUP + h for h in range(HEADS_PER_GROUP) for g in range(NSA_KV_HEADS)])
    x2 = x.reshape(batch * seq, d)
    for l in range(depth):
        wl = w_in[l]
        wq = _pad_heads_to_group_lanes(wl[:, :o_q]).astype(BF16)
        wkv = wl[:, o_q:o_kv].astype(BF16)
        wg = jnp.pad(wl[:, o_kv:o_g], ((0, 0), (0, V7X_LANES - 3 * NSA_HEADS))).astype(BF16)
        wu = wl[:, o_g:o_u].astype(BF16)
        wga = wl[:, o_u:o_u + d].astype(BF16)
        wgb = wl[:, o_u + d:].astype(BF16)
        g_mix = norm_mix_g[l].reshape(1, d)
        qraw, qrot, kc_raw, vc_raw, ksat, vs, kwt, vw, gates, u = _proj_call(
            x2, g_mix, wq, wkv, wg, wu, cos, slo, shi, batch, seq)

        pea, peb, kw1a, kw1b, kw2 = _compress_weights(cmp_pe[l], cmp_k_w1[l], cmp_k_w2[l])
        _, _, vw1a, vw1b, vw2 = _compress_weights(cmp_pe[l], cmp_v_w1[l], cmp_v_w2[l])
        kct, vc = _compress_call(kc_raw, vc_raw, pea, peb, kw1a, kw1b, kw2.T, vw1a, vw1b, vw2, batch, seq)

        b3 = lambda a: a.reshape(batch, seq, a.shape[-1])
        attn = _nsa_call(b3(qraw), b3(qrot), b3(gates), kct, vc, ksat, b3(vs), kwt, b3(vw), ovt)
        attn = attn.reshape(batch * seq, nsa_w)

        lag, pm, qm, lr, li = _s5_matrices(ssm_lam_re[l], ssm_lam_im[l], ssm_log_step[l], ssm_b_re[l], ssm_b_im[l],
                                      ssm_c_re[l], ssm_c_im[l])
        y_ssm = _s5_call(u, lag, pm, qm, lr, li, ssm_d[l].reshape(1, ssm_w), batch, seq)

        wattn = w_attn_branch[l].reshape(NSA_HEADS, HEAD_DIM, d)[head_order].reshape(nsa_w, d).astype(BF16)
        x1 = _merge_call(x2, g_mix, attn, y_ssm, wga, wgb, wattn, w_ssm_val[l].astype(BF16),
                         w_ssm_gate[l].astype(BF16), w_out[l].astype(BF16))
        x2 = _mlp_call(x1, norm_mlp_g[l].reshape(1, d), w_up[l].astype(BF16), w_down[l].astype(BF16),
                       norm_final_g.reshape(1, d))
    return x2.reshape(batch, seq, d)
```

```python
import functools
import math

import jax
import jax.numpy as jnp
import numpy as np
from jax import lax
from jax.experimental import pallas as pl
from jax.experimental.pallas import tpu as pltpu

NSA_HEADS = 8
NSA_KV_HEADS = 2
HEAD_DIM = 64
CMP_BLOCK = 32
CMP_STRIDE = 16
CMP_HIDDEN = 256
SEL_BLOCK = 64
N_SEL = 16
WINDOW = 512
FORCE_BONUS = 1e3
NEG_INF = -1e30
ROPE_THETA = 500000.0
ROPE_DIM = HEAD_DIM // 4
SSM_GROUP = 16
SSM_STATE = 64
EPS = 1e-6

HEADS_PER_GROUP = NSA_HEADS // NSA_KV_HEADS
KV_LANES = NSA_KV_HEADS * HEAD_DIM
Q_LANES = NSA_HEADS * KV_LANES

V7X_LANES = 128
V7X_VMEM_BYTES = 64 * 1024 * 1024
VMEM_LIMIT = V7X_VMEM_BYTES - 8 * 1024 * 1024

ROW_TILE = 512
SEL_TILE = 512
NSA_QUERIES = 256
SSM_CHUNK = 16
SSM_SUPER = V7X_LANES // SSM_GROUP
FF_CHUNK = 1024

LOG2E = math.log2(math.e)
MASK_BIAS = 1e30

BF16 = jnp.bfloat16
F32 = jnp.float32
_NT = (((1,), (1,)), ((), ()))


def _cparams(semantics):
    return pltpu.CompilerParams(dimension_semantics=semantics, vmem_limit_bytes=VMEM_LIMIT)


def _rmsnorm(x, g):
    return x * lax.rsqrt(jnp.mean(x * x, axis=-1, keepdims=True) + EPS) * g


def _gelu(x):
    return jax.nn.gelu(x)


def _rope_cols(x, cos, sin_lo, sin_hi):
    cols = []
    for c in range(x.shape[1] // V7X_LANES):
        xc = x[:, c * V7X_LANES:(c + 1) * V7X_LANES]
        up = pltpu.roll(xc, V7X_LANES - ROPE_DIM // 2, axis=1)
        dn = pltpu.roll(xc, ROPE_DIM // 2, axis=1)
        cols.append(xc * cos + up * sin_lo + dn * sin_hi)
    return jnp.concatenate(cols, axis=1) if len(cols) > 1 else cols[0]


def _values_with_ones(v):
    lane_head = lax.broadcasted_iota(jnp.int32, v.shape, 1) // HEAD_DIM
    return jnp.concatenate([jnp.where(lane_head == g, v, 1.0) for g in range(NSA_KV_HEADS)], axis=1).astype(BF16)


def _proj_kernel(x_ref, g_ref, wq_ref, wkv_ref, wg_ref, wu_ref, cos_ref, slo_ref, shi_ref,
                 qraw_ref, qrot_ref, kc_ref, vc_ref, ksat_ref, vs_ref, kwt_ref, vw_ref, gate_ref, u_ref, *, seq_tiles):
    hb = _rmsnorm(x_ref[...], g_ref[...]).astype(BF16)
    cos, slo, shi = cos_ref[...], slo_ref[...], shi_ref[...]
    q = jnp.dot(hb, wq_ref[...], preferred_element_type=F32) * (HEAD_DIM ** -0.5 * LOG2E)
    qraw_ref[...] = q.astype(BF16)
    qrot_ref[...] = _rope_cols(q, cos, slo, shi).astype(BF16)
    kv = jnp.dot(hb, wkv_ref[...], preferred_element_type=F32)
    w = KV_LANES
    kc_ref[...] = kv[:, 0 * w:1 * w]
    vc_ref[...] = kv[:, 1 * w:2 * w]
    tm = x_ref.shape[0]
    pos = (pl.program_id(0) % seq_tiles) * tm + lax.broadcasted_iota(jnp.int32, (V7X_LANES, tm), 1)
    onehot = jnp.where(lax.broadcasted_iota(jnp.int32, (V7X_LANES, tm), 0) == pos // SEL_BLOCK, 1.0, 0.0)
    ksat_ref[0, :w, :] = _rope_cols(kv[:, 2 * w:3 * w], cos, slo, shi).T.astype(BF16)
    ksat_ref[0, w:, :] = onehot.astype(BF16)
    vs_ref[...] = _values_with_ones(kv[:, 3 * w:4 * w])
    kwt_ref[0] = _rope_cols(kv[:, 4 * w:5 * w], cos, slo, shi).T.astype(BF16)
    vw_ref[...] = _values_with_ones(kv[:, 5 * w:6 * w])
    gate_ref[...] = jax.nn.sigmoid(jnp.dot(hb, wg_ref[...], preferred_element_type=F32))
    u_ref[...] = jnp.dot(hb, wu_ref[...], preferred_element_type=F32)


def _proj_call(x2, g, wq, wkv, wg, wu, cos, slo, shi, batch, seq):
    t, d = x2.shape
    tm = ROW_TILE
    s_tiles = seq // tm
    row = lambda i: (i, 0)
    pos = lambda i: (i % s_tiles, 0)
    trans = lambda i: (i // s_tiles, 0, i % s_tiles)
    wspec = lambda a: pl.BlockSpec(a.shape, lambda i: (0, 0))
    rows_out = lambda n, dt: (jax.ShapeDtypeStruct((t, n), dt), pl.BlockSpec((tm, n), row))
    trans_out = lambda n: (jax.ShapeDtypeStruct((batch, n, seq), BF16), pl.BlockSpec((1, n, tm), trans))
    outs = [rows_out(Q_LANES, BF16), rows_out(Q_LANES, BF16), rows_out(KV_LANES, F32), rows_out(KV_LANES, F32),
            trans_out(KV_LANES + V7X_LANES), rows_out(NSA_KV_HEADS * KV_LANES, BF16),
            trans_out(KV_LANES), rows_out(NSA_KV_HEADS * KV_LANES, BF16),
            rows_out(V7X_LANES, F32), rows_out(wu.shape[1], F32)]
    assert seq // SEL_BLOCK <= V7X_LANES, "the selection-block one-hot must fit one lane tile"
    return pl.pallas_call(
        functools.partial(_proj_kernel, seq_tiles=s_tiles),
        grid=(t // tm,),
        in_specs=[pl.BlockSpec((tm, d), row), wspec(g), wspec(wq), wspec(wkv), wspec(wg), wspec(wu),
                  pl.BlockSpec((tm, V7X_LANES), pos), pl.BlockSpec((tm, V7X_LANES), pos),
                  pl.BlockSpec((tm, V7X_LANES), pos)],
        out_specs=[o[1] for o in outs],
        out_shape=[o[0] for o in outs],
        compiler_params=_cparams(("arbitrary",)),
        name="proj",
    )(x2, g, wq, wkv, wg, wu, cos, slo, shi)


def _compress_kernel(k_ref, v_ref, pea_ref, peb_ref, kw1a_ref, kw1b_ref, kw2t_ref, vw1a_ref, vw1b_ref, vw2_ref,
                     kct_ref, vc_ref, ca_ref, cb_ref):
    nch = ca_ref.shape[0]

    def hidden(src_ref, w1a_ref, w1b_ref):
        for t in range(CMP_STRIDE):
            rows = src_ref[pl.ds(t, nch, stride=CMP_STRIDE), :]
            sl = slice(t * KV_LANES, (t + 1) * KV_LANES)
            ca_ref[:, sl] = (rows + pea_ref[:, sl]).astype(BF16)
            cb_ref[:, sl] = (rows + peb_ref[:, sl]).astype(BF16)
        ha = jnp.dot(ca_ref[...], w1a_ref[...], preferred_element_type=F32)
        hb = jnp.dot(cb_ref[...], w1b_ref[...], preferred_element_type=F32)
        return _gelu(ha + pltpu.roll(hb, nch - 1, axis=0)).astype(BF16)

    kct_ref[0] = lax.dot_general(kw2t_ref[...], hidden(k_ref, kw1a_ref, kw1b_ref), _NT,
                                 preferred_element_type=F32).astype(BF16)
    vc_ref[0] = jnp.dot(hidden(v_ref, vw1a_ref, vw1b_ref), vw2_ref[...], preferred_element_type=F32).astype(BF16)


def _compress_call(kc_raw, vc_raw, pea, peb, kw1a, kw1b, kw2, vw1a, vw1b, vw2, batch, seq):
    nch = seq // CMP_STRIDE
    wspec = lambda a: pl.BlockSpec(a.shape, lambda b: (0, 0))
    return pl.pallas_call(
        _compress_kernel,
        grid=(batch,),
        in_specs=[pl.BlockSpec((seq, KV_LANES), lambda b: (b, 0)), pl.BlockSpec((seq, KV_LANES), lambda b: (b, 0)),
                  wspec(pea), wspec(peb), wspec(kw1a), wspec(kw1b), wspec(kw2), wspec(vw1a), wspec(vw1b), wspec(vw2)],
        out_specs=[pl.BlockSpec((1, KV_LANES, nch), lambda b: (b, 0, 0)),
                   pl.BlockSpec((1, nch, KV_LANES), lambda b: (b, 0, 0))],
        out_shape=[jax.ShapeDtypeStruct((batch, KV_LANES, nch), BF16), jax.ShapeDtypeStruct((batch, nch, KV_LANES), BF16)],
        scratch_shapes=[pltpu.VMEM((nch, CMP_STRIDE * KV_LANES), BF16), pltpu.VMEM((nch, CMP_STRIDE * KV_LANES), BF16)],
        compiler_params=_cparams(("arbitrary",)),
        name="compress",
    )(kc_raw, vc_raw, pea, peb, kw1a, kw1b, kw2, vw1a, vw1b, vw2)


def _window_keys(nq):
    return WINDOW + nq


def _nsa_kernel(qraw_ref, qrot_ref, gate_ref, kct_ref, vc_ref, ksat_ref, vs_ref, kwt_ref, vw_ref, ovt_ref,
                o_ref, score_ref, qa_ref, *, n_sel):
    i = pl.program_id(1)
    nq = qraw_ref.shape[1]
    hg = HEADS_PER_GROUP
    rows = hg * nq
    lanes_gq = NSA_KV_HEADS * nq
    nb = score_ref.shape[0]
    nc = kct_ref.shape[2]
    t0 = i * nq
    blk_first = t0 // SEL_BLOCK
    blk_last = blk_first + nq // SEL_BLOCK - 1
    tq_rows = t0 + lax.broadcasted_iota(jnp.int32, (rows, 1), 0) % nq

    def group_q(ref, g):
        return jnp.concatenate(
            [ref[0, :, (g * hg + h) * KV_LANES:(g * hg + h + 1) * KV_LANES] for h in range(hg)], axis=0)

    def compressed_branch(width):
        cmp_end = lax.broadcasted_iota(jnp.int32, (1, width), 1) * CMP_STRIDE + (CMP_BLOCK - 1)
        bias_c = jnp.where(cmp_end <= tq_rows, 0.0, NEG_INF)
        any_valid = tq_rows >= CMP_BLOCK - 1
        outs, p_sum = [], []
        for g in range(NSA_KV_HEADS):
            s = jnp.dot(group_q(qraw_ref, g), kct_ref[0, :, :width], preferred_element_type=F32) + bias_c
            e = jnp.exp2(s - jnp.max(s, axis=-1, keepdims=True))
            p = e * jnp.where(any_valid, 1.0 / jnp.sum(e, axis=-1, keepdims=True), 0.0)
            outs.append(jnp.dot(p.astype(BF16), vc_ref[0, :width, :], preferred_element_type=F32))
            ph = p[0:nq]
            for h in range(1, hg):
                ph = ph + p[h * nq:(h + 1) * nq]
            p_sum.append(ph)
        p2 = jnp.concatenate(p_sum, axis=0)
        p_hi = p2.astype(BF16)
        p_lo = (p2 - p_hi.astype(F32)).astype(BF16)
        ovt = ovt_ref[:, :width]
        imp = (lax.dot_general(ovt, p_hi, _NT, preferred_element_type=F32)
               + lax.dot_general(ovt, p_lo, _NT, preferred_element_type=F32))
        return tuple(outs) + (imp,)

    *o_cmp, imp_t = compressed_branch(nc)

    blk_q = (t0 + lax.broadcasted_iota(jnp.int32, (1, lanes_gq), 1) % nq) // SEL_BLOCK
    jfull = lax.broadcasted_iota(jnp.int32, (nb, lanes_gq), 0)
    valid = jfull <= blk_q
    forced = (jfull == 0) | (jfull == blk_q) | (jfull == blk_q - 1)
    score = jnp.where(valid, imp_t + jnp.where(forced, FORCE_BONUS, 0.0), NEG_INF)

    rest = score
    for _ in range(n_sel):
        thr = jnp.max(rest, axis=0, keepdims=True)
        rest = jnp.where(rest >= thr, NEG_INF, rest)
    sel_t = jnp.where(valid, jnp.where(score >= thr, 1.0, 0.0), 0.0)

    def ranked_members():
        def count_above_or_tied_earlier(jp, cnt):
            row = jnp.broadcast_to(score_ref[pl.ds(jp, 1), :], score.shape)
            tie = jnp.where(jp < jfull, 1.0, 0.0)
            return cnt + jnp.where(row > score, 1.0, jnp.where(row == score, tie, 0.0))

        score_ref[...] = score
        cnt = lax.fori_loop(0, blk_last + 1, count_above_or_tied_earlier, jnp.zeros(score.shape, F32))
        return jnp.where(valid, jnp.where(cnt < float(n_sel), 1.0, 0.0), 0.0)

    picked = jnp.sum(sel_t, axis=0, keepdims=True)
    wrong = jnp.sum(jnp.where(picked == jnp.minimum(blk_q + 1, n_sel).astype(F32), 0.0, 1.0))
    sel_t = lax.cond(wrong == 0.0, lambda: sel_t, ranked_members)
    bias = ((sel_t.T - 1.0) * MASK_BIAS).astype(BF16)

    n_tiles = blk_last // (SEL_TILE // SEL_BLOCK) + 1
    lane_t = lax.broadcasted_iota(jnp.int32, (1, SEL_TILE), 1)
    for g in range(NSA_KV_HEADS):
        qa_ref[g, :, :KV_LANES] = group_q(qrot_ref, g)
        qa_ref[g, :, KV_LANES:] = jnp.concatenate([bias[g * nq:(g + 1) * nq]] * hg, axis=0)

    def normalized(pv):
        return pv * (1.0 / pltpu.roll(pv, HEAD_DIM, axis=1))

    def tile_body(c, carry, causal):
        k0 = pl.multiple_of(c * SEL_TILE, SEL_TILE)
        ka = ksat_ref[0, :, pl.ds(k0, SEL_TILE)]
        new = []
        for g in range(NSA_KV_HEADS):
            m, acc = carry[g]
            s = jnp.dot(qa_ref[g], ka, preferred_element_type=F32)
            if causal:
                s = jnp.where((k0 + lane_t) <= tq_rows, s, NEG_INF)
            m_new = jnp.maximum(m, jnp.max(s, axis=-1, keepdims=True))
            p = jnp.exp2(s - m_new)
            v = vs_ref[0, pl.ds(k0, SEL_TILE), g * KV_LANES:(g + 1) * KV_LANES]
            acc = jnp.exp2(m - m_new) * acc + jnp.dot(p.astype(BF16), v, preferred_element_type=F32)
            new.append((m_new, acc))
        return tuple(new)

    init = tuple((jnp.full((rows, 1), NEG_INF, F32), jnp.zeros((rows, KV_LANES), F32)) for _ in range(NSA_KV_HEADS))
    carry = lax.fori_loop(0, n_tiles - 1, functools.partial(tile_body, causal=False), init)
    carry = tile_body(n_tiles - 1, carry, causal=True)
    o_sel = [normalized(acc) for _, acc in carry]

    nwin = _window_keys(nq)
    w0 = pl.multiple_of(jnp.maximum(blk_first - WINDOW // SEL_BLOCK, 0) * SEL_BLOCK, V7X_LANES)
    kp = w0 + lax.broadcasted_iota(jnp.int32, (1, nwin), 1)
    bias_w = jnp.where(kp <= tq_rows, jnp.where(kp > tq_rows - WINDOW, 0.0, NEG_INF), NEG_INF)
    o_win = []
    for g in range(NSA_KV_HEADS):
        s = jnp.dot(group_q(qrot_ref, g), kwt_ref[0, :, pl.ds(w0, nwin)], preferred_element_type=F32) + bias_w
        e = jnp.exp2(s - jnp.max(s, axis=-1, keepdims=True))
        v = vw_ref[0, pl.ds(w0, nwin), g * KV_LANES:(g + 1) * KV_LANES]
        o_win.append(normalized(jnp.dot(e.astype(BF16), v, preferred_element_type=F32)))

    gt = gate_ref[0]
    lane = lax.broadcasted_iota(jnp.int32, (nq, KV_LANES), 1)
    for h in range(hg):
        r = slice(h * nq, (h + 1) * nq)
        parts = []
        for g in range(NSA_KV_HEADS):
            c = (g * hg + h) * 3
            parts.append(gt[:, c:c + 1] * o_cmp[g][r] + gt[:, c + 1:c + 2] * o_sel[g][r] + gt[:, c + 2:c + 3] * o_win[g][r])
        out = parts[0]
        for g in range(1, NSA_KV_HEADS):
            out = jnp.where(lane >= g * HEAD_DIM, parts[g], out)
        o_ref[0, :, h * KV_LANES:(h + 1) * KV_LANES] = out.astype(BF16)


def _nsa_call(qraw, qrot, gates, kct, vc, ksat, vs, kwt, vw, ovt):
    batch, seq, _ = qraw.shape
    nq = NSA_QUERIES
    assert nq % V7X_LANES == 0 and seq % nq == 0 and seq >= _window_keys(nq) and seq % SEL_TILE == 0
    step = lambda n: pl.BlockSpec((1, nq, n), lambda b, i: (b, i, 0))
    whole = lambda a: pl.BlockSpec((1,) + a.shape[1:], lambda b, i: (b, 0, 0))
    return pl.pallas_call(
        functools.partial(_nsa_kernel, n_sel=min(N_SEL, seq // SEL_BLOCK)),
        grid=(batch, seq // nq),
        in_specs=[step(Q_LANES), step(Q_LANES), step(V7X_LANES),
                  whole(kct), whole(vc), whole(ksat), whole(vs), whole(kwt), whole(vw),
                  pl.BlockSpec(ovt.shape, lambda b, i: (0, 0))],
        out_specs=step(HEADS_PER_GROUP * KV_LANES),
        out_shape=jax.ShapeDtypeStruct((batch, seq, HEADS_PER_GROUP * KV_LANES), BF16),
        scratch_shapes=[pltpu.VMEM((V7X_LANES, NSA_KV_HEADS * nq), F32),
                        pltpu.VMEM((NSA_KV_HEADS, HEADS_PER_GROUP * nq, KV_LANES + V7X_LANES), BF16)],
        compiler_params=_cparams(("arbitrary", "arbitrary")),
        name="nsa",
    )(qraw, qrot, gates, kct, vc, ksat, vs, kwt, vw, ovt)


def _s5_kernel(u_ref, lag_ref, p_ref, q_ref, lr_ref, li_ref, d_ref, o_ref, m_ref, x_ref, y_ref, z_ref, sp_ref):
    nch = x_ref.shape[0]
    half = z_ref.shape[1] // 2
    lanes = V7X_LANES

    @pl.when(pl.program_id(1) == 0)
    def _():
        for s in range(SSM_CHUNK):
            for t in range(SSM_CHUNK):
                tile = lag_ref[0, t - s] if t >= s else jnp.zeros((lanes, lanes), BF16)
                m_ref[s * lanes:(s + 1) * lanes, t * lanes:(t + 1) * lanes] = tile

    for t in range(SSM_CHUNK):
        x_ref[:, t * lanes:(t + 1) * lanes] = u_ref[pl.ds(t, nch, stride=SSM_CHUNK), :].astype(BF16)
    y_ref[...] = jnp.dot(x_ref[...], m_ref[...], preferred_element_type=F32)
    z_ref[...] = jnp.dot(x_ref[...], p_ref[0], preferred_element_type=F32)
    lr, li = lr_ref[0], li_ref[0]

    def scan_body(k, carry):
        sr, si = carry
        sp_ref[pl.ds(k, 1), 0:half] = sr
        sp_ref[pl.ds(k, 1), half:2 * half] = si
        zr = z_ref[pl.ds(k, 1), 0:half]
        zi = z_ref[pl.ds(k, 1), half:2 * half]
        return lr * sr - li * si + zr, lr * si + li * sr + zi

    zero = jnp.zeros((1, half), F32)
    lax.fori_loop(0, nch, scan_body, (zero, zero))
    y = y_ref[...] + jnp.dot(sp_ref[...].astype(BF16), q_ref[0], preferred_element_type=F32)
    d = d_ref[...]
    for t in range(SSM_CHUNK):
        yt = y[:, t * lanes:(t + 1) * lanes] + d * u_ref[pl.ds(t, nch, stride=SSM_CHUNK), :]
        o_ref[pl.ds(t, nch, stride=SSM_CHUNK), :] = _gelu(yt)


def _s5_call(u, lag, p, q, lr, li, dskip, batch, seq):
    nsg = lag.shape[0]
    nch = seq // SSM_CHUNK
    nstate = q.shape[1]
    lw = SSM_CHUNK * V7X_LANES
    slab = lambda a: pl.BlockSpec((1,) + a.shape[1:], lambda g, b: (g,) + (0,) * (a.ndim - 1),
                                  pipeline_mode=pl.Buffered(1))
    return pl.pallas_call(
        _s5_kernel,
        grid=(nsg, batch),
        in_specs=[pl.BlockSpec((seq, V7X_LANES), lambda g, b: (b, g)), slab(lag), slab(p), slab(q),
                  pl.BlockSpec((1, 1, nstate // 2), lambda g, b: (g, 0, 0)),
                  pl.BlockSpec((1, 1, nstate // 2), lambda g, b: (g, 0, 0)),
                  pl.BlockSpec((1, V7X_LANES), lambda g, b: (0, g))],
        out_specs=pl.BlockSpec((seq, V7X_LANES), lambda g, b: (b, g)),
        out_shape=jax.ShapeDtypeStruct(u.shape, F32),
        scratch_shapes=[pltpu.VMEM((lw, lw), BF16), pltpu.VMEM((nch, lw), BF16), pltpu.VMEM((nch, lw), F32),
                        pltpu.VMEM((nch, nstate), F32), pltpu.VMEM((nch, nstate), F32)],
        compiler_params=_cparams(("arbitrary", "arbitrary")),
        name="s5",
    )(u, lag, p, q, lr, li, dskip)


def _merge_kernel(x_ref, g_ref, attn_ref, ssm_ref, wga_ref, wgb_ref, wattn_ref, wval_ref, wgate_ref, wout_ref, o_ref):
    x = x_ref[...]
    hb = _rmsnorm(x, g_ref[...]).astype(BF16)
    dot = functools.partial(jnp.dot, preferred_element_type=F32)
    y_a = dot(attn_ref[...], wattn_ref[...])
    ys = ssm_ref[...].astype(BF16)
    y_b = dot(ys, wval_ref[...]) * jax.nn.sigmoid(dot(ys, wgate_ref[...]))
    merged = jax.nn.sigmoid(dot(hb, wga_ref[...])) * y_a + jax.nn.sigmoid(dot(hb, wgb_ref[...])) * y_b
    o_ref[...] = x + dot(merged.astype(BF16), wout_ref[...])


def _merge_call(x2, g, attn, ssm, wga, wgb, wattn, wval, wgate, wout):
    t, d = x2.shape
    tm = ROW_TILE
    row = lambda i: (i, 0)
    wspec = lambda a: pl.BlockSpec(a.shape, lambda i: (0, 0))
    return pl.pallas_call(
        _merge_kernel,
        grid=(t // tm,),
        in_specs=[pl.BlockSpec((tm, d), row), wspec(g), pl.BlockSpec((tm, attn.shape[1]), row),
                  pl.BlockSpec((tm, ssm.shape[1]), row), wspec(wga), wspec(wgb), wspec(wattn), wspec(wval),
                  wspec(wgate), wspec(wout)],
        out_specs=pl.BlockSpec((tm, d), row),
        out_shape=jax.ShapeDtypeStruct((t, d), F32),
        compiler_params=_cparams(("arbitrary",)),
        name="merge",
    )(x2, g, attn, ssm, wga, wgb, wattn, wval, wgate, wout)


def _mlp_kernel(x_ref, g_ref, wup_ref, wdown_ref, gf_ref, o_ref):
    x = x_ref[...]
    hb = _rmsnorm(x, g_ref[...]).astype(BF16)
    acc = x
    for c in range(wup_ref.shape[1] // FF_CHUNK):
        sl = slice(c * FF_CHUNK, (c + 1) * FF_CHUNK)
        up = jnp.maximum(jnp.dot(hb, wup_ref[:, sl], preferred_element_type=F32), 0.0)
        acc = acc + jnp.dot((up * up).astype(BF16), wdown_ref[sl, :], preferred_element_type=F32)
    o_ref[...] = _rmsnorm(acc, gf_ref[...])


def _mlp_call(x1, g, wup, wdown, gf):
    t, d = x1.shape
    tm = ROW_TILE
    row = lambda i: (i, 0)
    wspec = lambda a: pl.BlockSpec(a.shape, lambda i: (0, 0))
    return pl.pallas_call(
        _mlp_kernel,
        grid=(t // tm,),
        in_specs=[pl.BlockSpec((tm, d), row), wspec(g), wspec(wup), wspec(wdown), wspec(gf)],
        out_specs=pl.BlockSpec((tm, d), row),
        out_shape=jax.ShapeDtypeStruct((t, d), F32),
        compiler_params=_cparams(("arbitrary",)),
        name="mlp",
    )(x1, g, wup, wdown, gf)


def _pad_heads_to_group_lanes(wq):
    d = wq.shape[0]
    w = wq.reshape(d, NSA_KV_HEADS, HEADS_PER_GROUP, HEAD_DIM)
    eye = jnp.eye(NSA_KV_HEADS, dtype=wq.dtype)
    return jnp.einsum('dghe,gk->dghke', w, eye).reshape(d, Q_LANES)


def _rope_tables(seq):
    half = ROPE_DIM // 2
    inv = ROPE_THETA ** (-(jnp.arange(half, dtype=F32) * 2.0) / ROPE_DIM)
    ang = jnp.arange(seq, dtype=F32)[:, None] * inv[None, :]
    cos, sin = jnp.cos(ang), jnp.sin(ang)
    rest = HEAD_DIM - ROPE_DIM
    cos_h = jnp.concatenate([cos, cos, jnp.ones((seq, rest), F32)], axis=1)
    slo_h = jnp.concatenate([-sin, jnp.zeros((seq, half + rest), F32)], axis=1)
    shi_h = jnp.concatenate([jnp.zeros((seq, half), F32), sin, jnp.zeros((seq, rest), F32)], axis=1)
    reps = V7X_LANES // HEAD_DIM
    return jnp.tile(cos_h, (1, reps)), jnp.tile(slo_h, (1, reps)), jnp.tile(shi_h, (1, reps))


def _compress_weights(pe, w1, w2):
    eye = jnp.eye(NSA_KV_HEADS, dtype=F32)
    w1e = jnp.einsum('tdj,gk->tgdkj', w1.reshape(CMP_BLOCK, HEAD_DIM, CMP_HIDDEN), eye)
    w1e = w1e.reshape(CMP_BLOCK * KV_LANES, NSA_KV_HEADS * CMP_HIDDEN).astype(BF16)
    w2e = jnp.einsum('jd,gk->gjkd', w2, eye).reshape(NSA_KV_HEADS * CMP_HIDDEN, KV_LANES).astype(BF16)
    pee = jnp.tile(pe, (1, NSA_KV_HEADS)).reshape(1, CMP_BLOCK * KV_LANES)
    halfw = CMP_STRIDE * KV_LANES
    return pee[:, :halfw], pee[:, halfw:], w1e[:halfw], w1e[halfw:], w2e


def _selection_constants(seq):
    nc = seq // CMP_STRIDE - 1
    nb = seq // SEL_BLOCK
    n_np = np.arange(nc)[:, None] * CMP_STRIDE
    j_np = np.arange(nb)[None, :] * SEL_BLOCK
    overlap = ((n_np < j_np + SEL_BLOCK) & (n_np + CMP_BLOCK > j_np)).astype(np.float32)
    ovt = np.zeros((V7X_LANES, nc + 1), np.float32)
    ovt[:nb, :nc] = overlap.T
    return jnp.asarray(ovt, BF16)


def _s5_matrices(lam_re, lam_im, log_step, b_re, b_im, c_re, c_im):
    hp = lax.Precision.HIGHEST
    ng, ns = lam_re.shape
    gc = b_re.shape[-1]
    L = SSM_CHUNK
    step = jnp.exp(log_step)[:, None]
    a, b = lam_re * step, lam_im * step
    k = jnp.arange(L + 1, dtype=F32)[:, None, None]
    mag = jnp.exp(a[None] * k)
    pr, pi = mag * jnp.cos(b[None] * k), mag * jnp.sin(b[None] * k)
    nr, ni = pr[1] - 1.0, pi[1]
    den = lam_re * lam_re + lam_im * lam_im
    cr, ci = (nr * lam_re + ni * lam_im) / den, (ni * lam_re - nr * lam_im) / den
    bbr = cr[..., None] * b_re - ci[..., None] * b_im
    bbi = cr[..., None] * b_im + ci[..., None] * b_re
    cpr = c_re[None] * pr[:, :, None, :] - c_im[None] * pi[:, :, None, :]
    cpi = c_re[None] * pi[:, :, None, :] + c_im[None] * pr[:, :, None, :]
    kk = (jnp.einsum('kgcn,gnd->kgcd', cpr[:L], bbr, precision=hp)
          - jnp.einsum('kgcn,gnd->kgcd', cpi[:L], bbi, precision=hp))
    pw_r, pw_i = pr[L - 1 - np.arange(L)], pi[L - 1 - np.arange(L)]
    p_r = pw_r[:, :, :, None] * bbr[None] - pw_i[:, :, :, None] * bbi[None]
    p_i = pw_r[:, :, :, None] * bbi[None] + pw_i[:, :, :, None] * bbr[None]
    p_r, p_i = p_r.transpose(1, 0, 3, 2), p_i.transpose(1, 0, 3, 2)
    q_r = cpr[1:].transpose(1, 3, 0, 2)
    q_i = -cpi[1:].transpose(1, 3, 0, 2)
    sup = SSM_SUPER
    nsg = ng // sup
    eye = jnp.eye(sup, dtype=BF16)
    lw = L * sup * gc
    bf = lambda a: a.astype(BF16)
    lag_sg = jnp.einsum('kxacd,ab->xkadbc', bf(kk).reshape(L, nsg, sup, gc, gc), eye).reshape(nsg, L, sup * gc, sup * gc)
    pr_sg = jnp.einsum('xasdn,ab->xsadbn', bf(p_r).reshape(nsg, sup, L, gc, ns), eye).reshape(nsg, lw, sup * ns)
    pi_sg = jnp.einsum('xasdn,ab->xsadbn', bf(p_i).reshape(nsg, sup, L, gc, ns), eye).reshape(nsg, lw, sup * ns)
    qr_sg = jnp.einsum('xantc,ab->xantbc', bf(q_r).reshape(nsg, sup, ns, L, gc), eye).reshape(nsg, sup * ns, lw)
    qi_sg = jnp.einsum('xantc,ab->xantbc', bf(q_i).reshape(nsg, sup, ns, L, gc), eye).reshape(nsg, sup * ns, lw)
    p = jnp.concatenate([pr_sg, pi_sg], axis=2)
    q = jnp.concatenate([qr_sg, qi_sg], axis=1)
    return lag_sg, p, q, pr[L].reshape(nsg, 1, sup * ns), pi[L].reshape(nsg, 1, sup * ns)


def kernel(x, norm_mix_g, w_in, cmp_pe, cmp_k_w1, cmp_k_w2, cmp_v_w1, cmp_v_w2, ssm_lam_re, ssm_lam_im, ssm_log_step, ssm_b_re, ssm_b_im, ssm_c_re, ssm_c_im, ssm_d, w_attn_branch, w_ssm_val, w_ssm_gate, w_out, norm_mlp_g, w_up, w_down, norm_final_g):
    batch, seq, d = x.shape
    depth = w_in.shape[0]
    assert depth == 1, "the final rmsnorm is fused into the single layer's mlp kernel"
    nsa_w = NSA_HEADS * HEAD_DIM
    ssm_w = ssm_d.shape[1]
    o_q, o_kv, o_g, o_u = nsa_w, nsa_w + 6 * KV_LANES, nsa_w + 6 * KV_LANES + 3 * NSA_HEADS, 0
    o_u = o_g + ssm_w
    cos, slo, shi = _rope_tables(seq)
    ovt = _selection_constants(seq)
    head_order = np.array([g * HEADS_PER_GROUP + h for h in range(HEADS_PER_GROUP) for g in range(NSA_KV_HEADS)])
    x2 = x.reshape(batch * seq, d)
    for l in range(depth):
        wl = w_in[l]
        wq = _pad_heads_to_group_lanes(wl[:, :o_q]).astype(BF16)
        wkv = wl[:, o_q:o_kv].astype(BF16)
        wg = jnp.pad(wl[:, o_kv:o_g], ((0, 0), (0, V7X_LANES - 3 * NSA_HEADS))).astype(BF16)
        wu = wl[:, o_g:o_u].astype(BF16)
        wga = wl[:, o_u:o_u + d].astype(BF16)
        wgb = wl[:, o_u + d:].astype(BF16)
        g_mix = norm_mix_g[l].reshape(1, d)
        qraw, qrot, kc_raw, vc_raw, ksat, vs, kwt, vw, gates, u = _proj_call(
            x2, g_mix, wq, wkv, wg, wu, cos, slo, shi, batch, seq)

        pea, peb, kw1a, kw1b, kw2 = _compress_weights(cmp_pe[l], cmp_k_w1[l], cmp_k_w2[l])
        _, _, vw1a, vw1b, vw2 = _compress_weights(cmp_pe[l], cmp_v_w1[l], cmp_v_w2[l])
        kct, vc = _compress_call(kc_raw, vc_raw, pea, peb, kw1a, kw1b, kw2.T, vw1a, vw1b, vw2, batch, seq)

        b3 = lambda a: a.reshape(batch, seq, a.shape[-1])
        attn = _nsa_call(b3(qraw), b3(qrot), b3(gates), kct, vc, ksat, b3(vs), kwt, b3(vw), ovt)
        attn = attn.reshape(batch * seq, nsa_w)

        lag, pm, qm, lr, li = _s5_matrices(ssm_lam_re[l], ssm_lam_im[l], ssm_log_step[l], ssm_b_re[l], ssm_b_im[l],
                                      ssm_c_re[l], ssm_c_im[l])
        y_ssm = _s5_call(u, lag, pm, qm, lr, li, ssm_d[l].reshape(1, ssm_w), batch, seq)

        wattn = w_attn_branch[l].reshape(NSA_HEADS, HEAD_DIM, d)[head_order].reshape(nsa_w, d).astype(BF16)
        x1 = _merge_call(x2, g_mix, attn, y_ssm, wga, wgb, wattn, w_ssm_val[l].astype(BF16),
                         w_ssm_gate[l].astype(BF16), w_out[l].astype(BF16))
        x2 = _mlp_call(x1, norm_mlp_g[l].reshape(1, d), w_up[l].astype(BF16), w_down[l].astype(BF16),
                       norm_final_g.reshape(1, d))
    return x2.reshape(batch, seq, d)
```

```python
import functools
import math

import jax
import jax.numpy as jnp
import numpy as np
from jax import lax
from jax.experimental import pallas as pl
from jax.experimental.pallas import tpu as pltpu

NSA_HEADS = 8
NSA_KV_HEADS = 2
HEAD_DIM = 64
CMP_BLOCK = 32
CMP_STRIDE = 16
CMP_HIDDEN = 256
SEL_BLOCK = 64
N_SEL = 16
WINDOW = 512
FORCE_BONUS = 1e3
NEG_INF = -1e30
ROPE_THETA = 500000.0
ROPE_DIM = HEAD_DIM // 4
SSM_GROUP = 16
SSM_STATE = 64
EPS = 1e-6

HEADS_PER_GROUP = NSA_HEADS // NSA_KV_HEADS
KV_LANES = NSA_KV_HEADS * HEAD_DIM
Q_LANES = NSA_HEADS * HEAD_DIM

V7X_LANES = 128
V7X_MXU = 256
V7X_VMEM_BYTES = 64 * 1024 * 1024
VMEM_LIMIT = V7X_VMEM_BYTES - 8 * 1024 * 1024

ROW_TILE = 512
SEL_TILE = 1024
NSA_QUERIES = 256
SSM_CHUNK = 16
SSM_SUPER = V7X_LANES // SSM_GROUP
FF_CHUNK = 1024

LOG2E = math.log2(math.e)
MASK_BIAS = 1e30

BF16 = jnp.bfloat16
F32 = jnp.float32
_NT = (((1,), (1,)), ((), ()))


def _cparams(semantics):
    return pltpu.CompilerParams(dimension_semantics=semantics, vmem_limit_bytes=VMEM_LIMIT)


def _rmsnorm(x, g):
    return x * lax.rsqrt(jnp.mean(x * x, axis=-1, keepdims=True) + EPS) * g


def _gelu(x):
    return jax.nn.gelu(x)


def _rope_cols(x, cos, sin_lo, sin_hi):
    cols = []
    for c in range(x.shape[1] // V7X_LANES):
        xc = x[:, c * V7X_LANES:(c + 1) * V7X_LANES]
        up = pltpu.roll(xc, V7X_LANES - ROPE_DIM // 2, axis=1)
        dn = pltpu.roll(xc, ROPE_DIM // 2, axis=1)
        cols.append(xc * cos + up * sin_lo + dn * sin_hi)
    return jnp.concatenate(cols, axis=1) if len(cols) > 1 else cols[0]


def _values_with_ones(v):
    lane_head = lax.broadcasted_iota(jnp.int32, v.shape, 1) // HEAD_DIM
    return jnp.concatenate([jnp.where(lane_head == g, v, 1.0) for g in range(NSA_KV_HEADS)], axis=1).astype(BF16)


def _proj_kernel(x_ref, g_ref, wq_ref, wkv_ref, wg_ref, wu_ref, cos_ref, slo_ref, shi_ref,
                 qraw_ref, qrot_ref, kc_ref, vc_ref, ksat_ref, vs_ref, kwt_ref, vw_ref, gate_ref, u_ref, *, seq_tiles):
    hb = _rmsnorm(x_ref[...], g_ref[...]).astype(BF16)
    cos, slo, shi = cos_ref[...], slo_ref[...], shi_ref[...]
    q = jnp.dot(hb, wq_ref[...], preferred_element_type=F32) * (HEAD_DIM ** -0.5 * LOG2E)
    qraw_ref[...] = q.astype(BF16)
    qrot_ref[...] = _rope_cols(q, cos, slo, shi).astype(BF16)
    kv = jnp.dot(hb, wkv_ref[...], preferred_element_type=F32)
    w = KV_LANES
    kc_ref[...] = kv[:, 0 * w:1 * w]
    vc_ref[...] = kv[:, 1 * w:2 * w]
    tm = x_ref.shape[0]
    pos = (pl.program_id(0) % seq_tiles) * tm + lax.broadcasted_iota(jnp.int32, (V7X_LANES, tm), 1)
    onehot = jnp.where(lax.broadcasted_iota(jnp.int32, (V7X_LANES, tm), 0) == pos // SEL_BLOCK, 1.0, 0.0)
    ksat_ref[0, :w, :] = _rope_cols(kv[:, 2 * w:3 * w], cos, slo, shi).T.astype(BF16)
    ksat_ref[0, w:, :] = onehot.astype(BF16)
    vs_ref[...] = _values_with_ones(kv[:, 3 * w:4 * w])
    kwt_ref[0] = _rope_cols(kv[:, 4 * w:5 * w], cos, slo, shi).T.astype(BF16)
    vw_ref[...] = _values_with_ones(kv[:, 5 * w:6 * w])
    gate_ref[...] = jax.nn.sigmoid(jnp.dot(hb, wg_ref[...], preferred_element_type=F32))
    u_ref[...] = jnp.dot(hb, wu_ref[...], preferred_element_type=F32)


def _proj_call(x2, g, wq, wkv, wg, wu, cos, slo, shi, batch, seq):
    t, d = x2.shape
    tm = ROW_TILE
    s_tiles = seq // tm
    row = lambda i: (i, 0)
    pos = lambda i: (i % s_tiles, 0)
    trans = lambda i: (i // s_tiles, 0, i % s_tiles)
    wspec = lambda a: pl.BlockSpec(a.shape, lambda i: (0, 0))
    rows_out = lambda n, dt: (jax.ShapeDtypeStruct((t, n), dt), pl.BlockSpec((tm, n), row))
    trans_out = lambda n: (jax.ShapeDtypeStruct((batch, n, seq), BF16), pl.BlockSpec((1, n, tm), trans))
    outs = [rows_out(Q_LANES, BF16), rows_out(Q_LANES, BF16), rows_out(KV_LANES, F32), rows_out(KV_LANES, F32),
            trans_out(KV_LANES + V7X_LANES), rows_out(NSA_KV_HEADS * KV_LANES, BF16),
            trans_out(KV_LANES), rows_out(NSA_KV_HEADS * KV_LANES, BF16),
            rows_out(V7X_LANES, F32), rows_out(wu.shape[1], F32)]
    assert seq // SEL_BLOCK <= V7X_LANES, "the selection-block one-hot must fit one lane tile"
    return pl.pallas_call(
        functools.partial(_proj_kernel, seq_tiles=s_tiles),
        grid=(t // tm,),
        in_specs=[pl.BlockSpec((tm, d), row), wspec(g), wspec(wq), wspec(wkv), wspec(wg), wspec(wu),
                  pl.BlockSpec((tm, V7X_LANES), pos), pl.BlockSpec((tm, V7X_LANES), pos),
                  pl.BlockSpec((tm, V7X_LANES), pos)],
        out_specs=[o[1] for o in outs],
        out_shape=[o[0] for o in outs],
        compiler_params=_cparams(("arbitrary",)),
        name="proj",
    )(x2, g, wq, wkv, wg, wu, cos, slo, shi)


def _compress_kernel(k_ref, v_ref, pea_ref, peb_ref, kw1a_ref, kw1b_ref, kw2t_ref, vw1a_ref, vw1b_ref, vw2_ref,
                     kct_ref, vc_ref, ca_ref, cb_ref):
    nch = ca_ref.shape[0]

    def hidden(src_ref, w1a_ref, w1b_ref):
        for t in range(CMP_STRIDE):
            rows = src_ref[pl.ds(t, nch, stride=CMP_STRIDE), :]
            sl = slice(t * KV_LANES, (t + 1) * KV_LANES)
            ca_ref[:, sl] = (rows + pea_ref[:, sl]).astype(BF16)
            cb_ref[:, sl] = (rows + peb_ref[:, sl]).astype(BF16)
        ha = jnp.dot(ca_ref[...], w1a_ref[...], preferred_element_type=F32)
        hb = jnp.dot(cb_ref[...], w1b_ref[...], preferred_element_type=F32)
        return _gelu(ha + pltpu.roll(hb, nch - 1, axis=0)).astype(BF16)

    kct_ref[0] = lax.dot_general(kw2t_ref[...], hidden(k_ref, kw1a_ref, kw1b_ref), _NT,
                                 preferred_element_type=F32).astype(BF16)
    vc_ref[0] = jnp.dot(hidden(v_ref, vw1a_ref, vw1b_ref), vw2_ref[...], preferred_element_type=F32).astype(BF16)


def _compress_call(kc_raw, vc_raw, pea, peb, kw1a, kw1b, kw2, vw1a, vw1b, vw2, batch, seq):
    nch = seq // CMP_STRIDE
    wspec = lambda a: pl.BlockSpec(a.shape, lambda b: (0, 0))
    return pl.pallas_call(
        _compress_kernel,
        grid=(batch,),
        in_specs=[pl.BlockSpec((seq, KV_LANES), lambda b: (b, 0)), pl.BlockSpec((seq, KV_LANES), lambda b: (b, 0)),
                  wspec(pea), wspec(peb), wspec(kw1a), wspec(kw1b), wspec(kw2), wspec(vw1a), wspec(vw1b), wspec(vw2)],
        out_specs=[pl.BlockSpec((1, KV_LANES, nch), lambda b: (b, 0, 0)),
                   pl.BlockSpec((1, nch, KV_LANES), lambda b: (b, 0, 0))],
        out_shape=[jax.ShapeDtypeStruct((batch, KV_LANES, nch), BF16), jax.ShapeDtypeStruct((batch, nch, KV_LANES), BF16)],
        scratch_shapes=[pltpu.VMEM((nch, CMP_STRIDE * KV_LANES), BF16), pltpu.VMEM((nch, CMP_STRIDE * KV_LANES), BF16)],
        compiler_params=_cparams(("arbitrary",)),
        name="compress",
    )(kc_raw, vc_raw, pea, peb, kw1a, kw1b, kw2, vw1a, vw1b, vw2)


def _window_keys(nq):
    return WINDOW + nq


def _nsa_kernel(qraw_ref, qrot_ref, gate_ref, kct_ref, vc_ref, ksat_ref, vs_ref, kwt_ref, vw_ref, ovt_ref,
                o_ref, score_ref, qa_ref, *, n_sel):
    i = pl.program_id(1)
    nq = qraw_ref.shape[1]
    hg = HEADS_PER_GROUP
    rows = hg * nq
    lanes_gq = NSA_KV_HEADS * nq
    nb = score_ref.shape[0]
    nc = kct_ref.shape[2]
    t0 = i * nq
    blk_first = t0 // SEL_BLOCK
    blk_last = blk_first + nq // SEL_BLOCK - 1
    tq_rows = t0 + lax.broadcasted_iota(jnp.int32, (rows, 1), 0) % nq

    own_lanes = [lax.broadcasted_iota(jnp.int32, (nq, KV_LANES), 1) // HEAD_DIM == g for g in range(NSA_KV_HEADS)]

    def group_q(ref, g):
        return jnp.concatenate([jnp.where(own_lanes[g], ref[0, :, h * KV_LANES:(h + 1) * KV_LANES], 0.0).astype(BF16)
                                for h in range(hg)], axis=0)

    def compressed_branch(width):
        cmp_end = lax.broadcasted_iota(jnp.int32, (1, width), 1) * CMP_STRIDE + (CMP_BLOCK - 1)
        bias_c = jnp.where(cmp_end <= tq_rows, 0.0, NEG_INF)
        any_valid = tq_rows >= CMP_BLOCK - 1
        outs, p_sum = [], []
        for g in range(NSA_KV_HEADS):
            s = jnp.dot(group_q(qraw_ref, g), kct_ref[0, :, :width], preferred_element_type=F32) + bias_c
            e = jnp.exp2(s - jnp.max(s, axis=-1, keepdims=True))
            p = e * jnp.where(any_valid, 1.0 / jnp.sum(e, axis=-1, keepdims=True), 0.0)
            outs.append(jnp.dot(p.astype(BF16), vc_ref[0, :width, :], preferred_element_type=F32))
            ph = p[0:nq]
            for h in range(1, hg):
                ph = ph + p[h * nq:(h + 1) * nq]
            p_sum.append(ph)
        p2 = jnp.concatenate(p_sum, axis=0)
        p_hi = p2.astype(BF16)
        p_lo = (p2 - p_hi.astype(F32)).astype(BF16)
        ovt = ovt_ref[:, :width]
        imp = (lax.dot_general(ovt, p_hi, _NT, preferred_element_type=F32)
               + lax.dot_general(ovt, p_lo, _NT, preferred_element_type=F32))
        return tuple(outs) + (imp,)

    *o_cmp, imp_t = compressed_branch(nc)

    blk_q = (t0 + lax.broadcasted_iota(jnp.int32, (1, lanes_gq), 1) % nq) // SEL_BLOCK
    jfull = lax.broadcasted_iota(jnp.int32, (nb, lanes_gq), 0)
    valid = jfull <= blk_q
    forced = (jfull == 0) | (jfull == blk_q) | (jfull == blk_q - 1)
    score = jnp.where(valid, imp_t + jnp.where(forced, FORCE_BONUS, 0.0), NEG_INF)

    rest = score
    for _ in range(n_sel):
        thr = jnp.max(rest, axis=0, keepdims=True)
        rest = jnp.where(rest >= thr, NEG_INF, rest)
    sel_t = jnp.where(valid, jnp.where(score >= thr, 1.0, 0.0), 0.0)

    def ranked_members():
        def count_above_or_tied_earlier(jp, cnt):
            row = jnp.broadcast_to(score_ref[pl.ds(jp, 1), :], score.shape)
            tie = jnp.where(jp < jfull, 1.0, 0.0)
            return cnt + jnp.where(row > score, 1.0, jnp.where(row == score, tie, 0.0))

        score_ref[...] = score
        cnt = lax.fori_loop(0, blk_last + 1, count_above_or_tied_earlier, jnp.zeros(score.shape, F32))
        return jnp.where(valid, jnp.where(cnt < float(n_sel), 1.0, 0.0), 0.0)

    picked = jnp.sum(sel_t, axis=0, keepdims=True)
    wrong = jnp.sum(jnp.where(picked == jnp.minimum(blk_q + 1, n_sel).astype(F32), 0.0, 1.0))
    sel_t = lax.cond(wrong == 0.0, lambda: sel_t, ranked_members)
    bias = ((sel_t.T - 1.0) * MASK_BIAS).astype(BF16)

    n_tiles = blk_last // (SEL_TILE // SEL_BLOCK) + 1
    lane_t = lax.broadcasted_iota(jnp.int32, (1, SEL_TILE), 1)
    for g in range(NSA_KV_HEADS):
        qa_ref[g, :, :KV_LANES] = group_q(qrot_ref, g)
        qa_ref[g, :, KV_LANES:] = jnp.concatenate([bias[g * nq:(g + 1) * nq]] * hg, axis=0)

    def normalized(pv):
        return pv * (1.0 / pltpu.roll(pv, HEAD_DIM, axis=1))

    def tile_body(c, carry, causal):
        k0 = pl.multiple_of(c * SEL_TILE, SEL_TILE)
        ka = ksat_ref[0, :, pl.ds(k0, SEL_TILE)]
        new = []
        for g in range(NSA_KV_HEADS):
            m, acc = carry[g]
            s = jnp.dot(qa_ref[g], ka, preferred_element_type=F32)
            if causal:
                s = jnp.where((k0 + lane_t) <= tq_rows, s, NEG_INF)
            m_new = jnp.maximum(m, jnp.max(s, axis=-1, keepdims=True))
            p = jnp.exp2(s - m_new)
            v = vs_ref[0, pl.ds(k0, SEL_TILE), g * KV_LANES:(g + 1) * KV_LANES]
            acc = jnp.exp2(m - m_new) * acc + jnp.dot(p.astype(BF16), v, preferred_element_type=F32)
            new.append((m_new, acc))
        return tuple(new)

    init = tuple((jnp.full((rows, 1), NEG_INF, F32), jnp.zeros((rows, KV_LANES), F32)) for _ in range(NSA_KV_HEADS))
    carry = lax.fori_loop(0, n_tiles - 1, functools.partial(tile_body, causal=False), init)
    carry = tile_body(n_tiles - 1, carry, causal=True)
    o_sel = [normalized(acc) for _, acc in carry]

    nwin = _window_keys(nq)
    w0 = pl.multiple_of(jnp.maximum(blk_first - WINDOW // SEL_BLOCK, 0) * SEL_BLOCK, V7X_LANES)
    kp = w0 + lax.broadcasted_iota(jnp.int32, (1, nwin), 1)
    bias_w = jnp.where(kp <= tq_rows, jnp.where(kp > tq_rows - WINDOW, 0.0, NEG_INF), NEG_INF)
    o_win = []
    for g in range(NSA_KV_HEADS):
        s = jnp.dot(group_q(qrot_ref, g), kwt_ref[0, :, pl.ds(w0, nwin)], preferred_element_type=F32) + bias_w
        e = jnp.exp2(s - jnp.max(s, axis=-1, keepdims=True))
        v = vw_ref[0, pl.ds(w0, nwin), g * KV_LANES:(g + 1) * KV_LANES]
        o_win.append(normalized(jnp.dot(e.astype(BF16), v, preferred_element_type=F32)))

    gt = gate_ref[0]
    lane = lax.broadcasted_iota(jnp.int32, (nq, KV_LANES), 1)
    for h in range(hg):
        r = slice(h * nq, (h + 1) * nq)
        parts = []
        for g in range(NSA_KV_HEADS):
            c = (g * hg + h) * 3
            parts.append(gt[:, c:c + 1] * o_cmp[g][r] + gt[:, c + 1:c + 2] * o_sel[g][r] + gt[:, c + 2:c + 3] * o_win[g][r])
        out = parts[0]
        for g in range(1, NSA_KV_HEADS):
            out = jnp.where(lane >= g * HEAD_DIM, parts[g], out)
        o_ref[0, :, h * KV_LANES:(h + 1) * KV_LANES] = out.astype(BF16)


def _nsa_call(qraw, qrot, gates, kct, vc, ksat, vs, kwt, vw, ovt):
    batch, seq, _ = qraw.shape
    nq = NSA_QUERIES
    assert nq % V7X_LANES == 0 and seq % nq == 0 and seq >= _window_keys(nq) and seq % SEL_TILE == 0
    step = lambda n: pl.BlockSpec((1, nq, n), lambda b, i: (b, i, 0))
    whole = lambda a: pl.BlockSpec((1,) + a.shape[1:], lambda b, i: (b, 0, 0))
    return pl.pallas_call(
        functools.partial(_nsa_kernel, n_sel=min(N_SEL, seq // SEL_BLOCK)),
        grid=(batch, seq // nq),
        in_specs=[step(Q_LANES), step(Q_LANES), step(V7X_LANES),
                  whole(kct), whole(vc), whole(ksat), whole(vs), whole(kwt), whole(vw),
                  pl.BlockSpec(ovt.shape, lambda b, i: (0, 0))],
        out_specs=step(HEADS_PER_GROUP * KV_LANES),
        out_shape=jax.ShapeDtypeStruct((batch, seq, HEADS_PER_GROUP * KV_LANES), BF16),
        scratch_shapes=[pltpu.VMEM((V7X_LANES, NSA_KV_HEADS * nq), F32),
                        pltpu.VMEM((NSA_KV_HEADS, HEADS_PER_GROUP * nq, KV_LANES + V7X_LANES), BF16)],
        compiler_params=_cparams(("arbitrary", "arbitrary")),
        name="nsa",
    )(qraw, qrot, gates, kct, vc, ksat, vs, kwt, vw, ovt)


def _s5_kernel(u_ref, lag_ref, p_ref, q_ref, lr_ref, li_ref, d_ref, o_ref, m_ref, x_ref, y_ref, z_ref, sp_ref):
    nch = x_ref.shape[0]
    half = z_ref.shape[1] // 2
    lanes = V7X_LANES

    @pl.when(pl.program_id(1) == 0)
    def _():
        for s in range(SSM_CHUNK):
            for t in range(SSM_CHUNK):
                tile = lag_ref[0, t - s] if t >= s else jnp.zeros((lanes, lanes), BF16)
                m_ref[s * lanes:(s + 1) * lanes, t * lanes:(t + 1) * lanes] = tile

    for t in range(SSM_CHUNK):
        x_ref[:, t * lanes:(t + 1) * lanes] = u_ref[pl.ds(t, nch, stride=SSM_CHUNK), :].astype(BF16)
    for c0 in range(0, SSM_CHUNK * lanes, V7X_MXU):
        c1 = c0 + V7X_MXU
        y_ref[:, c0:c1] = jnp.dot(x_ref[:, :c1], m_ref[:c1, c0:c1], preferred_element_type=F32)
    z_ref[...] = jnp.dot(x_ref[...], p_ref[0], preferred_element_type=F32)
    lr, li = lr_ref[0], li_ref[0]

    def scan_body(k, carry):
        sr, si = carry
        sp_ref[pl.ds(k, 1), 0:half] = sr
        sp_ref[pl.ds(k, 1), half:2 * half] = si
        zr = z_ref[pl.ds(k, 1), 0:half]
        zi = z_ref[pl.ds(k, 1), half:2 * half]
        return lr * sr - li * si + zr, lr * si + li * sr + zi

    zero = jnp.zeros((1, half), F32)
    lax.fori_loop(0, nch, scan_body, (zero, zero))
    y = y_ref[...] + jnp.dot(sp_ref[...].astype(BF16), q_ref[0], preferred_element_type=F32)
    d = d_ref[...]
    for t in range(SSM_CHUNK):
        yt = y[:, t * lanes:(t + 1) * lanes] + d * u_ref[pl.ds(t, nch, stride=SSM_CHUNK), :]
        o_ref[pl.ds(t, nch, stride=SSM_CHUNK), :] = _gelu(yt)


def _s5_call(u, lag, p, q, lr, li, dskip, batch, seq):
    nsg = lag.shape[0]
    nch = seq // SSM_CHUNK
    nstate = q.shape[1]
    lw = SSM_CHUNK * V7X_LANES
    slab = lambda a: pl.BlockSpec((1,) + a.shape[1:], lambda g, b: (g,) + (0,) * (a.ndim - 1),
                                  pipeline_mode=pl.Buffered(1))
    return pl.pallas_call(
        _s5_kernel,
        grid=(nsg, batch),
        in_specs=[pl.BlockSpec((seq, V7X_LANES), lambda g, b: (b, g)), slab(lag), slab(p), slab(q),
                  pl.BlockSpec((1, 1, nstate // 2), lambda g, b: (g, 0, 0)),
                  pl.BlockSpec((1, 1, nstate // 2), lambda g, b: (g, 0, 0)),
                  pl.BlockSpec((1, V7X_LANES), lambda g, b: (0, g))],
        out_specs=pl.BlockSpec((seq, V7X_LANES), lambda g, b: (b, g)),
        out_shape=jax.ShapeDtypeStruct(u.shape, F32),
        scratch_shapes=[pltpu.VMEM((lw, lw), BF16), pltpu.VMEM((nch, lw), BF16), pltpu.VMEM((nch, lw), F32),
                        pltpu.VMEM((nch, nstate), F32), pltpu.VMEM((nch, nstate), F32)],
        compiler_params=_cparams(("arbitrary", "arbitrary")),
        name="s5",
    )(u, lag, p, q, lr, li, dskip)


def _merge_kernel(x_ref, g_ref, attn_ref, ssm_ref, wga_ref, wgb_ref, wattn_ref, wval_ref, wgate_ref, wout_ref, o_ref):
    x = x_ref[...]
    hb = _rmsnorm(x, g_ref[...]).astype(BF16)
    dot = functools.partial(jnp.dot, preferred_element_type=F32)
    y_a = dot(attn_ref[...], wattn_ref[...])
    ys = ssm_ref[...].astype(BF16)
    y_b = dot(ys, wval_ref[...]) * jax.nn.sigmoid(dot(ys, wgate_ref[...]))
    merged = jax.nn.sigmoid(dot(hb, wga_ref[...])) * y_a + jax.nn.sigmoid(dot(hb, wgb_ref[...])) * y_b
    o_ref[...] = x + dot(merged.astype(BF16), wout_ref[...])


def _merge_call(x2, g, attn, ssm, wga, wgb, wattn, wval, wgate, wout):
    t, d = x2.shape
    tm = ROW_TILE
    row = lambda i: (i, 0)
    wspec = lambda a: pl.BlockSpec(a.shape, lambda i: (0, 0))
    return pl.pallas_call(
        _merge_kernel,
        grid=(t // tm,),
        in_specs=[pl.BlockSpec((tm, d), row), wspec(g), pl.BlockSpec((tm, attn.shape[1]), row),
                  pl.BlockSpec((tm, ssm.shape[1]), row), wspec(wga), wspec(wgb), wspec(wattn), wspec(wval),
                  wspec(wgate), wspec(wout)],
        out_specs=pl.BlockSpec((tm, d), row),
        out_shape=jax.ShapeDtypeStruct((t, d), F32),
        compiler_params=_cparams(("arbitrary",)),
        name="merge",
    )(x2, g, attn, ssm, wga, wgb, wattn, wval, wgate, wout)


def _mlp_kernel(x_ref, g_ref, wup_ref, wdown_ref, gf_ref, o_ref):
    x = x_ref[...]
    hb = _rmsnorm(x, g_ref[...]).astype(BF16)
    acc = x
    for c in range(wup_ref.shape[1] // FF_CHUNK):
        sl = slice(c * FF_CHUNK, (c + 1) * FF_CHUNK)
        up = jnp.maximum(jnp.dot(hb, wup_ref[:, sl], preferred_element_type=F32), 0.0)
        acc = acc + jnp.dot((up * up).astype(BF16), wdown_ref[sl, :], preferred_element_type=F32)
    o_ref[...] = _rmsnorm(acc, gf_ref[...])


def _mlp_call(x1, g, wup, wdown, gf):
    t, d = x1.shape
    tm = ROW_TILE
    row = lambda i: (i, 0)
    wspec = lambda a: pl.BlockSpec(a.shape, lambda i: (0, 0))
    return pl.pallas_call(
        _mlp_kernel,
        grid=(t // tm,),
        in_specs=[pl.BlockSpec((tm, d), row), wspec(g), wspec(wup), wspec(wdown), wspec(gf)],
        out_specs=pl.BlockSpec((tm, d), row),
        out_shape=jax.ShapeDtypeStruct((t, d), F32),
        compiler_params=_cparams(("arbitrary",)),
        name="mlp",
    )(x1, g, wup, wdown, gf)


def _interleave_heads(wq):
    d = wq.shape[0]
    return wq.reshape(d, NSA_KV_HEADS, HEADS_PER_GROUP, HEAD_DIM).transpose(0, 2, 1, 3).reshape(d, Q_LANES)


def _rope_tables(seq):
    half = ROPE_DIM // 2
    inv = ROPE_THETA ** (-(jnp.arange(half, dtype=F32) * 2.0) / ROPE_DIM)
    ang = jnp.arange(seq, dtype=F32)[:, None] * inv[None, :]
    cos, sin = jnp.cos(ang), jnp.sin(ang)
    rest = HEAD_DIM - ROPE_DIM
    cos_h = jnp.concatenate([cos, cos, jnp.ones((seq, rest), F32)], axis=1)
    slo_h = jnp.concatenate([-sin, jnp.zeros((seq, half + rest), F32)], axis=1)
    shi_h = jnp.concatenate([jnp.zeros((seq, half), F32), sin, jnp.zeros((seq, rest), F32)], axis=1)
    reps = V7X_LANES // HEAD_DIM
    return jnp.tile(cos_h, (1, reps)), jnp.tile(slo_h, (1, reps)), jnp.tile(shi_h, (1, reps))


def _compress_weights(pe, w1, w2):
    eye = jnp.eye(NSA_KV_HEADS, dtype=F32)
    w1e = jnp.einsum('tdj,gk->tgdkj', w1.reshape(CMP_BLOCK, HEAD_DIM, CMP_HIDDEN), eye)
    w1e = w1e.reshape(CMP_BLOCK * KV_LANES, NSA_KV_HEADS * CMP_HIDDEN).astype(BF16)
    w2e = jnp.einsum('jd,gk->gjkd', w2, eye).reshape(NSA_KV_HEADS * CMP_HIDDEN, KV_LANES).astype(BF16)
    pee = jnp.tile(pe, (1, NSA_KV_HEADS)).reshape(1, CMP_BLOCK * KV_LANES)
    halfw = CMP_STRIDE * KV_LANES
    return pee[:, :halfw], pee[:, halfw:], w1e[:halfw], w1e[halfw:], w2e


def _selection_constants(seq):
    nc = seq // CMP_STRIDE - 1
    nb = seq // SEL_BLOCK
    n_np = np.arange(nc)[:, None] * CMP_STRIDE
    j_np = np.arange(nb)[None, :] * SEL_BLOCK
    overlap = ((n_np < j_np + SEL_BLOCK) & (n_np + CMP_BLOCK > j_np)).astype(np.float32)
    ovt = np.zeros((V7X_LANES, nc + 1), np.float32)
    ovt[:nb, :nc] = overlap.T
    return jnp.asarray(ovt, BF16)


def _s5_matrices(lam_re, lam_im, log_step, b_re, b_im, c_re, c_im):
    hp = lax.Precision.HIGHEST
    ng, ns = lam_re.shape
    gc = b_re.shape[-1]
    L = SSM_CHUNK
    step = jnp.exp(log_step)[:, None]
    a, b = lam_re * step, lam_im * step
    k = jnp.arange(L + 1, dtype=F32)[:, None, None]
    mag = jnp.exp(a[None] * k)
    pr, pi = mag * jnp.cos(b[None] * k), mag * jnp.sin(b[None] * k)
    nr, ni = pr[1] - 1.0, pi[1]
    den = lam_re * lam_re + lam_im * lam_im
    cr, ci = (nr * lam_re + ni * lam_im) / den, (ni * lam_re - nr * lam_im) / den
    bbr = cr[..., None] * b_re - ci[..., None] * b_im
    bbi = cr[..., None] * b_im + ci[..., None] * b_re
    cpr = c_re[None] * pr[:, :, None, :] - c_im[None] * pi[:, :, None, :]
    cpi = c_re[None] * pi[:, :, None, :] + c_im[None] * pr[:, :, None, :]
    kk = (jnp.einsum('kgcn,gnd->kgcd', cpr[:L], bbr, precision=hp)
          - jnp.einsum('kgcn,gnd->kgcd', cpi[:L], bbi, precision=hp))
    pw_r, pw_i = pr[L - 1 - np.arange(L)], pi[L - 1 - np.arange(L)]
    p_r = pw_r[:, :, :, None] * bbr[None] - pw_i[:, :, :, None] * bbi[None]
    p_i = pw_r[:, :, :, None] * bbi[None] + pw_i[:, :, :, None] * bbr[None]
    p_r, p_i = p_r.transpose(1, 0, 3, 2), p_i.transpose(1, 0, 3, 2)
    q_r = cpr[1:].transpose(1, 3, 0, 2)
    q_i = -cpi[1:].transpose(1, 3, 0, 2)
    sup = SSM_SUPER
    nsg = ng // sup
    eye = jnp.eye(sup, dtype=BF16)
    lw = L * sup * gc
    bf = lambda a: a.astype(BF16)
    lag_sg = jnp.einsum('kxacd,ab->xkadbc', bf(kk).reshape(L, nsg, sup, gc, gc), eye).reshape(nsg, L, sup * gc, sup * gc)
    slab_rows = lambda a: bf(a).reshape(nsg, sup, L, gc, -1).transpose(0, 2, 1, 3, 4).reshape(nsg, lw, -1)
    p_c = jnp.concatenate([slab_rows(p_r), slab_rows(p_i)], axis=2)
    rep_p = np.kron(np.eye(2), np.tile(np.eye(ns), (1, sup)))
    own_p = (np.arange(lw)[:, None] // gc) % sup == (np.arange(2 * sup * ns)[None, :] // ns) % sup
    p = jnp.where(own_p, jnp.matmul(p_c, jnp.asarray(rep_p, BF16)), 0)
    q_c = jnp.concatenate([bf(q_r).reshape(nsg, sup * ns, L * gc), bf(q_i).reshape(nsg, sup * ns, L * gc)], axis=1)
    rep_q = np.einsum('ts,cd,b->tcsbd', np.eye(L), np.eye(gc), np.ones(sup)).reshape(L * gc, lw)
    own_q = (np.arange(2 * sup * ns)[:, None] // ns) % sup == (np.arange(lw)[None, :] // gc) % sup
    q = jnp.where(own_q, jnp.matmul(q_c, jnp.asarray(rep_q, BF16)), 0)
    return lag_sg, p, q, pr[L].reshape(nsg, 1, sup * ns), pi[L].reshape(nsg, 1, sup * ns)


def kernel(x, norm_mix_g, w_in, cmp_pe, cmp_k_w1, cmp_k_w2, cmp_v_w1, cmp_v_w2, ssm_lam_re, ssm_lam_im, ssm_log_step, ssm_b_re, ssm_b_im, ssm_c_re, ssm_c_im, ssm_d, w_attn_branch, w_ssm_val, w_ssm_gate, w_out, norm_mlp_g, w_up, w_down, norm_final_g):
    batch, seq, d = x.shape
    depth = w_in.shape[0]
    assert depth == 1, "the final rmsnorm is fused into the single layer's mlp kernel"
    nsa_w = NSA_HEADS * HEAD_DIM
    ssm_w = ssm_d.shape[1]
    o_q, o_kv, o_g, o_u = nsa_w, nsa_w + 6 * KV_LANES, nsa_w + 6 * KV_LANES + 3 * NSA_HEADS, 0
    o_u = o_g + ssm_w
    cos, slo, shi = _rope_tables(seq)
    ovt = _selection_constants(seq)
    head_order = np.array([g * HEADS_PER_GROUP + h for h in range(HEADS_PER_GROUP) for g in range(NSA_KV_HEADS)])
    x2 = x.reshape(batch * seq, d)
    for l in range(depth):
        wl = w_in[l]
        wq = _interleave_heads(wl[:, :o_q]).astype(BF16)
        wkv = wl[:, o_q:o_kv].astype(BF16)
        wg = jnp.pad(wl[:, o_kv:o_g], ((0, 0), (0, V7X_LANES - 3 * NSA_HEADS))).astype(BF16)
        wu = wl[:, o_g:o_u].astype(BF16)
        wga = wl[:, o_u:o_u + d].astype(BF16)
        wgb = wl[:, o_u + d:].astype(BF16)
        g_mix = norm_mix_g[l].reshape(1, d)
        qraw, qrot, kc_raw, vc_raw, ksat, vs, kwt, vw, gates, u = _proj_call(
            x2, g_mix, wq, wkv, wg, wu, cos, slo, shi, batch, seq)

        pea, peb, kw1a, kw1b, kw2 = _compress_weights(cmp_pe[l], cmp_k_w1[l], cmp_k_w2[l])
        _, _, vw1a, vw1b, vw2 = _compress_weights(cmp_pe[l], cmp_v_w1[l], cmp_v_w2[l])
        kct, vc = _compress_call(kc_raw, vc_raw, pea, peb, kw1a, kw1b, kw2.T, vw1a, vw1b, vw2, batch, seq)

        b3 = lambda a: a.reshape(batch, seq, a.shape[-1])
        attn = _nsa_call(b3(qraw), b3(qrot), b3(gates), kct, vc, ksat, b3(vs), kwt, b3(vw), ovt)
        attn = attn.reshape(batch * seq, nsa_w)

        lag, pm, qm, lr, li = _s5_matrices(ssm_lam_re[l], ssm_lam_im[l], ssm_log_step[l], ssm_b_re[l], ssm_b_im[l],
                                      ssm_c_re[l], ssm_c_im[l])
        y_ssm = _s5_call(u, lag, pm, qm, lr, li, ssm_d[l].reshape(1, ssm_w), batch, seq)

        wattn = w_attn_branch[l].reshape(NSA_HEADS, HEAD_DIM, d)[head_order].reshape(nsa_w, d).astype(BF16)
        x1 = _merge_call(x2, g_mix, attn, y_ssm, wga, wgb, wattn, w_ssm_val[l].astype(BF16),
                         w_ssm_gate[l].astype(BF16), w_out[l].astype(BF16))
        x2 = _mlp_call(x1, norm_mlp_g[l].reshape(1, d), w_up[l].astype(BF16), w_down[l].astype(BF16),
                       norm_final_g.reshape(1, d))
    return x2.reshape(batch, seq, d)
```

```python
import functools
import math

import jax
import jax.numpy as jnp
import numpy as np
from jax import lax
from jax.experimental import pallas as pl
from jax.experimental.pallas import tpu as pltpu

NSA_HEADS = 8
NSA_KV_HEADS = 2
HEAD_DIM = 64
CMP_BLOCK = 32
CMP_STRIDE = 16
CMP_HIDDEN = 256
SEL_BLOCK = 64
N_SEL = 16
WINDOW = 512
FORCE_BONUS = 1e3
NEG_INF = -1e30
ROPE_THETA = 500000.0
ROPE_DIM = HEAD_DIM // 4
SSM_GROUP = 16
SSM_STATE = 64
EPS = 1e-6

HEADS_PER_GROUP = NSA_HEADS // NSA_KV_HEADS
KV_LANES = NSA_KV_HEADS * HEAD_DIM
Q_LANES = NSA_HEADS * HEAD_DIM

V7X_LANES = 128
V7X_MXU = 256
V7X_VMEM_BYTES = 64 * 1024 * 1024
VMEM_LIMIT = V7X_VMEM_BYTES - 8 * 1024 * 1024

ROW_TILE = 512
SEL_TILE = 1024
NSA_QUERIES = 256
SSM_CHUNK = 16
SSM_SUPER = V7X_LANES // SSM_GROUP
FF_CHUNK = 1024

LOG2E = math.log2(math.e)
MASK_BIAS = 1e30

BF16 = jnp.bfloat16
F32 = jnp.float32
_NT = (((1,), (1,)), ((), ()))


def _cparams(semantics):
    return pltpu.CompilerParams(dimension_semantics=semantics, vmem_limit_bytes=VMEM_LIMIT)


def _rmsnorm(x, g):
    return x * lax.rsqrt(jnp.mean(x * x, axis=-1, keepdims=True) + EPS) * g


def _gelu(x):
    return jax.nn.gelu(x)


def _rope_cols(x, cos, sin_lo, sin_hi):
    cols = []
    for c in range(x.shape[1] // V7X_LANES):
        xc = x[:, c * V7X_LANES:(c + 1) * V7X_LANES]
        up = pltpu.roll(xc, V7X_LANES - ROPE_DIM // 2, axis=1)
        dn = pltpu.roll(xc, ROPE_DIM // 2, axis=1)
        cols.append(xc * cos + up * sin_lo + dn * sin_hi)
    return jnp.concatenate(cols, axis=1) if len(cols) > 1 else cols[0]


def _values_with_ones(v):
    lane_head = lax.broadcasted_iota(jnp.int32, v.shape, 1) // HEAD_DIM
    return jnp.concatenate([jnp.where(lane_head == g, v, 1.0) for g in range(NSA_KV_HEADS)], axis=1).astype(BF16)


def _proj_kernel(x_ref, g_ref, wq_ref, wkv_ref, wg_ref, wu_ref, cos_ref, slo_ref, shi_ref,
                 qraw_ref, qrot_ref, kc_ref, vc_ref, ksat_ref, vs_ref, kwt_ref, vw_ref, gate_ref, u_ref, *, seq_tiles):
    hb = _rmsnorm(x_ref[...], g_ref[...]).astype(BF16)
    cos, slo, shi = cos_ref[...], slo_ref[...], shi_ref[...]
    q = jnp.dot(hb, wq_ref[...], preferred_element_type=F32) * (HEAD_DIM ** -0.5 * LOG2E)
    qraw_ref[...] = q.astype(BF16)
    qrot_ref[...] = _rope_cols(q, cos, slo, shi).astype(BF16)
    kv = jnp.dot(hb, wkv_ref[...], preferred_element_type=F32)
    w = KV_LANES
    kc_ref[...] = kv[:, 0 * w:1 * w]
    vc_ref[...] = kv[:, 1 * w:2 * w]
    tm = x_ref.shape[0]
    pos = (pl.program_id(0) % seq_tiles) * tm + lax.broadcasted_iota(jnp.int32, (V7X_LANES, tm), 1)
    onehot = jnp.where(lax.broadcasted_iota(jnp.int32, (V7X_LANES, tm), 0) == pos // SEL_BLOCK, 1.0, 0.0)
    ksat_ref[0, :w, :] = _rope_cols(kv[:, 2 * w:3 * w], cos, slo, shi).T.astype(BF16)
    ksat_ref[0, w:, :] = onehot.astype(BF16)
    vs_ref[...] = _values_with_ones(kv[:, 3 * w:4 * w])
    kwt_ref[0] = _rope_cols(kv[:, 4 * w:5 * w], cos, slo, shi).T.astype(BF16)
    vw_ref[...] = _values_with_ones(kv[:, 5 * w:6 * w])
    gate_ref[...] = jax.nn.sigmoid(jnp.dot(hb, wg_ref[...], preferred_element_type=F32))
    u_ref[...] = jnp.dot(hb, wu_ref[...], preferred_element_type=F32)


def _proj_call(x2, g, wq, wkv, wg, wu, cos, slo, shi, batch, seq):
    t, d = x2.shape
    tm = ROW_TILE
    s_tiles = seq // tm
    row = lambda i: (i, 0)
    pos = lambda i: (i % s_tiles, 0)
    trans = lambda i: (i // s_tiles, 0, i % s_tiles)
    wspec = lambda a: pl.BlockSpec(a.shape, lambda i: (0, 0))
    rows_out = lambda n, dt: (jax.ShapeDtypeStruct((t, n), dt), pl.BlockSpec((tm, n), row))
    trans_out = lambda n: (jax.ShapeDtypeStruct((batch, n, seq), BF16), pl.BlockSpec((1, n, tm), trans))
    outs = [rows_out(Q_LANES, BF16), rows_out(Q_LANES, BF16), rows_out(KV_LANES, F32), rows_out(KV_LANES, F32),
            trans_out(KV_LANES + V7X_LANES), rows_out(NSA_KV_HEADS * KV_LANES, BF16),
            trans_out(KV_LANES), rows_out(NSA_KV_HEADS * KV_LANES, BF16),
            rows_out(V7X_LANES, F32), rows_out(wu.shape[1], F32)]
    assert seq // SEL_BLOCK <= V7X_LANES, "the selection-block one-hot must fit one lane tile"
    return pl.pallas_call(
        functools.partial(_proj_kernel, seq_tiles=s_tiles),
        grid=(t // tm,),
        in_specs=[pl.BlockSpec((tm, d), row), wspec(g), wspec(wq), wspec(wkv), wspec(wg), wspec(wu),
                  pl.BlockSpec((tm, V7X_LANES), pos), pl.BlockSpec((tm, V7X_LANES), pos),
                  pl.BlockSpec((tm, V7X_LANES), pos)],
        out_specs=[o[1] for o in outs],
        out_shape=[o[0] for o in outs],
        compiler_params=_cparams(("arbitrary",)),
        name="proj",
    )(x2, g, wq, wkv, wg, wu, cos, slo, shi)


def _compress_kernel(k_ref, v_ref, pea_ref, peb_ref, kw1a_ref, kw1b_ref, kw2t_ref, vw1a_ref, vw1b_ref, vw2_ref,
                     kct_ref, vc_ref, ca_ref, cb_ref):
    nch = ca_ref.shape[0]

    def hidden(src_ref, w1a_ref, w1b_ref):
        for t in range(CMP_STRIDE):
            rows = src_ref[pl.ds(t, nch, stride=CMP_STRIDE), :]
            sl = slice(t * KV_LANES, (t + 1) * KV_LANES)
            ca_ref[:, sl] = (rows + pea_ref[:, sl]).astype(BF16)
            cb_ref[:, sl] = (rows + peb_ref[:, sl]).astype(BF16)
        ha = jnp.dot(ca_ref[...], w1a_ref[...], preferred_element_type=F32)
        hb = jnp.dot(cb_ref[...], w1b_ref[...], preferred_element_type=F32)
        return _gelu(ha + pltpu.roll(hb, nch - 1, axis=0)).astype(BF16)

    kct_ref[0] = lax.dot_general(kw2t_ref[...], hidden(k_ref, kw1a_ref, kw1b_ref), _NT,
                                 preferred_element_type=F32).astype(BF16)
    vc_ref[0] = jnp.dot(hidden(v_ref, vw1a_ref, vw1b_ref), vw2_ref[...], preferred_element_type=F32).astype(BF16)


def _compress_call(kc_raw, vc_raw, pea, peb, kw1a, kw1b, kw2, vw1a, vw1b, vw2, batch, seq):
    nch = seq // CMP_STRIDE
    wspec = lambda a: pl.BlockSpec(a.shape, lambda b: (0, 0))
    return pl.pallas_call(
        _compress_kernel,
        grid=(batch,),
        in_specs=[pl.BlockSpec((seq, KV_LANES), lambda b: (b, 0)), pl.BlockSpec((seq, KV_LANES), lambda b: (b, 0)),
                  wspec(pea), wspec(peb), wspec(kw1a), wspec(kw1b), wspec(kw2), wspec(vw1a), wspec(vw1b), wspec(vw2)],
        out_specs=[pl.BlockSpec((1, KV_LANES, nch), lambda b: (b, 0, 0)),
                   pl.BlockSpec((1, nch, KV_LANES), lambda b: (b, 0, 0))],
        out_shape=[jax.ShapeDtypeStruct((batch, KV_LANES, nch), BF16), jax.ShapeDtypeStruct((batch, nch, KV_LANES), BF16)],
        scratch_shapes=[pltpu.VMEM((nch, CMP_STRIDE * KV_LANES), BF16), pltpu.VMEM((nch, CMP_STRIDE * KV_LANES), BF16)],
        compiler_params=_cparams(("arbitrary",)),
        name="compress",
    )(kc_raw, vc_raw, pea, peb, kw1a, kw1b, kw2, vw1a, vw1b, vw2)


def _window_keys(nq):
    return WINDOW + nq


def _nsa_kernel(qraw_ref, qrot_ref, gate_ref, kct_ref, vc_ref, ksat_ref, vs_ref, kwt_ref, vw_ref, ovt_ref,
                o_ref, score_ref, qa_ref, *, n_sel):
    i = pl.program_id(1)
    nq = qraw_ref.shape[1]
    hg = HEADS_PER_GROUP
    rows = hg * nq
    lanes_gq = NSA_KV_HEADS * nq
    nb = score_ref.shape[0]
    nc = kct_ref.shape[2]
    t0 = i * nq
    blk_first = t0 // SEL_BLOCK
    blk_last = blk_first + nq // SEL_BLOCK - 1
    tq_rows = t0 + lax.broadcasted_iota(jnp.int32, (rows, 1), 0) % nq

    own_lanes = [lax.broadcasted_iota(jnp.int32, (nq, KV_LANES), 1) // HEAD_DIM == g for g in range(NSA_KV_HEADS)]

    def group_q(ref, g):
        return jnp.concatenate([jnp.where(own_lanes[g], ref[0, :, h * KV_LANES:(h + 1) * KV_LANES], 0.0).astype(BF16)
                                for h in range(hg)], axis=0)

    def normalized(pv):
        return pv * (1.0 / pltpu.roll(pv, HEAD_DIM, axis=1))

    nwin = _window_keys(nq)
    w0 = pl.multiple_of(jnp.maximum(blk_first - WINDOW // SEL_BLOCK, 0) * SEL_BLOCK, V7X_LANES)
    kp = w0 + lax.broadcasted_iota(jnp.int32, (1, nwin), 1)
    bias_w = jnp.where(kp <= tq_rows, jnp.where(kp > tq_rows - WINDOW, 0.0, NEG_INF), NEG_INF)
    o_win = []
    for g in range(NSA_KV_HEADS):
        s = jnp.dot(group_q(qrot_ref, g), kwt_ref[0, :, pl.ds(w0, nwin)], preferred_element_type=F32) + bias_w
        e = jnp.exp2(s - jnp.max(s, axis=-1, keepdims=True))
        v = vw_ref[0, pl.ds(w0, nwin), g * KV_LANES:(g + 1) * KV_LANES]
        o_win.append(normalized(jnp.dot(e.astype(BF16), v, preferred_element_type=F32)))

    def compressed_branch(width):
        cmp_end = lax.broadcasted_iota(jnp.int32, (1, width), 1) * CMP_STRIDE + (CMP_BLOCK - 1)
        bias_c = jnp.where(cmp_end <= tq_rows, 0.0, NEG_INF)
        any_valid = tq_rows >= CMP_BLOCK - 1
        outs, p_sum = [], []
        for g in range(NSA_KV_HEADS):
            s = jnp.dot(group_q(qraw_ref, g), kct_ref[0, :, :width], preferred_element_type=F32) + bias_c
            e = jnp.exp2(s - jnp.max(s, axis=-1, keepdims=True))
            p = e * jnp.where(any_valid, 1.0 / jnp.sum(e, axis=-1, keepdims=True), 0.0)
            outs.append(jnp.dot(p.astype(BF16), vc_ref[0, :width, :], preferred_element_type=F32))
            ph = p[0:nq]
            for h in range(1, hg):
                ph = ph + p[h * nq:(h + 1) * nq]
            p_sum.append(ph)
        p2 = jnp.concatenate(p_sum, axis=0)
        p_hi = p2.astype(BF16)
        p_lo = (p2 - p_hi.astype(F32)).astype(BF16)
        ovt = ovt_ref[:, :width]
        imp = (lax.dot_general(ovt, p_hi, _NT, preferred_element_type=F32)
               + lax.dot_general(ovt, p_lo, _NT, preferred_element_type=F32))
        return tuple(outs) + (imp,)

    *o_cmp, imp_t = compressed_branch(nc)

    blk_q = (t0 + lax.broadcasted_iota(jnp.int32, (1, lanes_gq), 1) % nq) // SEL_BLOCK
    jfull = lax.broadcasted_iota(jnp.int32, (nb, lanes_gq), 0)
    valid = jfull <= blk_q
    forced = (jfull == 0) | (jfull == blk_q) | (jfull == blk_q - 1)
    score = jnp.where(valid, imp_t + jnp.where(forced, FORCE_BONUS, 0.0), NEG_INF)

    rest = score
    for _ in range(n_sel):
        thr = jnp.max(rest, axis=0, keepdims=True)
        rest = jnp.where(rest >= thr, NEG_INF, rest)
    sel_t = jnp.where(valid, jnp.where(score >= thr, 1.0, 0.0), 0.0)

    def ranked_members():
        def count_above_or_tied_earlier(jp, cnt):
            row = jnp.broadcast_to(score_ref[pl.ds(jp, 1), :], score.shape)
            tie = jnp.where(jp < jfull, 1.0, 0.0)
            return cnt + jnp.where(row > score, 1.0, jnp.where(row == score, tie, 0.0))

        score_ref[...] = score
        cnt = lax.fori_loop(0, blk_last + 1, count_above_or_tied_earlier, jnp.zeros(score.shape, F32))
        return jnp.where(valid, jnp.where(cnt < float(n_sel), 1.0, 0.0), 0.0)

    picked = jnp.sum(sel_t, axis=0, keepdims=True)
    wrong = jnp.sum(jnp.where(picked == jnp.minimum(blk_q + 1, n_sel).astype(F32), 0.0, 1.0))
    sel_t = lax.cond(wrong == 0.0, lambda: sel_t, ranked_members)
    bias = ((sel_t.T - 1.0) * MASK_BIAS).astype(BF16)

    n_tiles = blk_last // (SEL_TILE // SEL_BLOCK) + 1
    lane_t = lax.broadcasted_iota(jnp.int32, (1, SEL_TILE), 1)
    for g in range(NSA_KV_HEADS):
        qa_ref[g, :, :KV_LANES] = group_q(qrot_ref, g)
        qa_ref[g, :, KV_LANES:] = jnp.concatenate([bias[g * nq:(g + 1) * nq]] * hg, axis=0)

    def tile_body(c, carry, causal):
        k0 = pl.multiple_of(c * SEL_TILE, SEL_TILE)
        ka = ksat_ref[0, :, pl.ds(k0, SEL_TILE)]
        new = []
        for g in range(NSA_KV_HEADS):
            m, acc = carry[g]
            s = jnp.dot(qa_ref[g], ka, preferred_element_type=F32)
            if causal:
                s = jnp.where((k0 + lane_t) <= tq_rows, s, NEG_INF)
            m_new = jnp.maximum(m, jnp.max(s, axis=-1, keepdims=True))
            p = jnp.exp2(s - m_new)
            v = vs_ref[0, pl.ds(k0, SEL_TILE), g * KV_LANES:(g + 1) * KV_LANES]
            acc = jnp.exp2(m - m_new) * acc + jnp.dot(p.astype(BF16), v, preferred_element_type=F32)
            new.append((m_new, acc))
        return tuple(new)

    init = tuple((jnp.full((rows, 1), NEG_INF, F32), jnp.zeros((rows, KV_LANES), F32)) for _ in range(NSA_KV_HEADS))
    carry = lax.fori_loop(0, n_tiles - 1, functools.partial(tile_body, causal=False), init)
    carry = tile_body(n_tiles - 1, carry, causal=True)
    o_sel = [normalized(acc) for _, acc in carry]

    gt = gate_ref[0]
    for h in range(hg):
        r = slice(h * nq, (h + 1) * nq)
        parts = []
        for g in range(NSA_KV_HEADS):
            c = (g * hg + h) * 3
            parts.append(gt[:, c:c + 1] * o_cmp[g][r] + gt[:, c + 1:c + 2] * o_sel[g][r] + gt[:, c + 2:c + 3] * o_win[g][r])
        out = parts[0]
        for g in range(1, NSA_KV_HEADS):
            out = jnp.where(own_lanes[g], parts[g], out)
        o_ref[0, :, h * KV_LANES:(h + 1) * KV_LANES] = out.astype(BF16)


def _nsa_call(qraw, qrot, gates, kct, vc, ksat, vs, kwt, vw, ovt):
    batch, seq, _ = qraw.shape
    nq = NSA_QUERIES
    assert nq % V7X_LANES == 0 and seq % nq == 0 and seq >= _window_keys(nq) and seq % SEL_TILE == 0
    step = lambda n: pl.BlockSpec((1, nq, n), lambda b, i: (b, i, 0))
    whole = lambda a: pl.BlockSpec((1,) + a.shape[1:], lambda b, i: (b, 0, 0))
    return pl.pallas_call(
        functools.partial(_nsa_kernel, n_sel=min(N_SEL, seq // SEL_BLOCK)),
        grid=(batch, seq // nq),
        in_specs=[step(Q_LANES), step(Q_LANES), step(V7X_LANES),
                  whole(kct), whole(vc), whole(ksat), whole(vs), whole(kwt), whole(vw),
                  pl.BlockSpec(ovt.shape, lambda b, i: (0, 0))],
        out_specs=step(HEADS_PER_GROUP * KV_LANES),
        out_shape=jax.ShapeDtypeStruct((batch, seq, HEADS_PER_GROUP * KV_LANES), BF16),
        scratch_shapes=[pltpu.VMEM((V7X_LANES, NSA_KV_HEADS * nq), F32),
                        pltpu.VMEM((NSA_KV_HEADS, HEADS_PER_GROUP * nq, KV_LANES + V7X_LANES), BF16)],
        compiler_params=_cparams(("arbitrary", "arbitrary")),
        name="nsa",
    )(qraw, qrot, gates, kct, vc, ksat, vs, kwt, vw, ovt)


def _s5_kernel(u_ref, lag_ref, p_ref, q_ref, lr_ref, li_ref, d_ref, o_ref, m_ref, x_ref, y_ref, z_ref, sp_ref):
    nch = x_ref.shape[0]
    half = z_ref.shape[1] // 2
    lanes = V7X_LANES

    @pl.when(pl.program_id(1) == 0)
    def _():
        for s in range(SSM_CHUNK):
            for t in range(SSM_CHUNK):
                tile = lag_ref[0, t - s] if t >= s else jnp.zeros((lanes, lanes), BF16)
                m_ref[s * lanes:(s + 1) * lanes, t * lanes:(t + 1) * lanes] = tile

    for t in range(SSM_CHUNK):
        x_ref[:, t * lanes:(t + 1) * lanes] = u_ref[pl.ds(t, nch, stride=SSM_CHUNK), :].astype(BF16)
    for c0 in range(0, SSM_CHUNK * lanes, V7X_MXU):
        c1 = c0 + V7X_MXU
        y_ref[:, c0:c1] = jnp.dot(x_ref[:, :c1], m_ref[:c1, c0:c1], preferred_element_type=F32)
    z_ref[...] = jnp.dot(x_ref[...], p_ref[0], preferred_element_type=F32)
    lr, li = lr_ref[0], li_ref[0]

    def scan_body(k, carry):
        sr, si = carry
        sp_ref[pl.ds(k, 1), 0:half] = sr
        sp_ref[pl.ds(k, 1), half:2 * half] = si
        zr = z_ref[pl.ds(k, 1), 0:half]
        zi = z_ref[pl.ds(k, 1), half:2 * half]
        return lr * sr - li * si + zr, lr * si + li * sr + zi

    zero = jnp.zeros((1, half), F32)
    lax.fori_loop(0, nch, scan_body, (zero, zero))
    y = y_ref[...] + jnp.dot(sp_ref[...].astype(BF16), q_ref[0], preferred_element_type=F32)
    d = d_ref[...]
    for t in range(SSM_CHUNK):
        yt = y[:, t * lanes:(t + 1) * lanes] + d * u_ref[pl.ds(t, nch, stride=SSM_CHUNK), :]
        o_ref[pl.ds(t, nch, stride=SSM_CHUNK), :] = _gelu(yt)


def _s5_call(u, lag, p, q, lr, li, dskip, batch, seq):
    nsg = lag.shape[0]
    nch = seq // SSM_CHUNK
    nstate = q.shape[1]
    lw = SSM_CHUNK * V7X_LANES
    slab = lambda a: pl.BlockSpec((1,) + a.shape[1:], lambda g, b: (g,) + (0,) * (a.ndim - 1),
                                  pipeline_mode=pl.Buffered(1))
    return pl.pallas_call(
        _s5_kernel,
        grid=(nsg, batch),
        in_specs=[pl.BlockSpec((seq, V7X_LANES), lambda g, b: (b, g)), slab(lag), slab(p), slab(q),
                  pl.BlockSpec((1, 1, nstate // 2), lambda g, b: (g, 0, 0)),
                  pl.BlockSpec((1, 1, nstate // 2), lambda g, b: (g, 0, 0)),
                  pl.BlockSpec((1, V7X_LANES), lambda g, b: (0, g))],
        out_specs=pl.BlockSpec((seq, V7X_LANES), lambda g, b: (b, g)),
        out_shape=jax.ShapeDtypeStruct(u.shape, F32),
        scratch_shapes=[pltpu.VMEM((lw, lw), BF16), pltpu.VMEM((nch, lw), BF16), pltpu.VMEM((nch, lw), F32),
                        pltpu.VMEM((nch, nstate), F32), pltpu.VMEM((nch, nstate), F32)],
        compiler_params=_cparams(("arbitrary", "arbitrary")),
        name="s5",
    )(u, lag, p, q, lr, li, dskip)


def _merge_kernel(x_ref, g_ref, attn_ref, ssm_ref, wga_ref, wgb_ref, wattn_ref, wval_ref, wgate_ref, wout_ref, o_ref):
    x = x_ref[...]
    hb = _rmsnorm(x, g_ref[...]).astype(BF16)
    dot = functools.partial(jnp.dot, preferred_element_type=F32)
    y_a = dot(attn_ref[...], wattn_ref[...])
    ys = ssm_ref[...].astype(BF16)
    y_b = dot(ys, wval_ref[...]) * jax.nn.sigmoid(dot(ys, wgate_ref[...]))
    merged = jax.nn.sigmoid(dot(hb, wga_ref[...])) * y_a + jax.nn.sigmoid(dot(hb, wgb_ref[...])) * y_b
    o_ref[...] = x + dot(merged.astype(BF16), wout_ref[...])


def _merge_call(x2, g, attn, ssm, wga, wgb, wattn, wval, wgate, wout):
    t, d = x2.shape
    tm = ROW_TILE
    row = lambda i: (i, 0)
    wspec = lambda a: pl.BlockSpec(a.shape, lambda i: (0, 0))
    return pl.pallas_call(
        _merge_kernel,
        grid=(t // tm,),
        in_specs=[pl.BlockSpec((tm, d), row), wspec(g), pl.BlockSpec((tm, attn.shape[1]), row),
                  pl.BlockSpec((tm, ssm.shape[1]), row), wspec(wga), wspec(wgb), wspec(wattn), wspec(wval),
                  wspec(wgate), wspec(wout)],
        out_specs=pl.BlockSpec((tm, d), row),
        out_shape=jax.ShapeDtypeStruct((t, d), F32),
        compiler_params=_cparams(("arbitrary",)),
        name="merge",
    )(x2, g, attn, ssm, wga, wgb, wattn, wval, wgate, wout)


def _mlp_kernel(x_ref, g_ref, wup_ref, wdown_ref, gf_ref, o_ref):
    x = x_ref[...]
    hb = _rmsnorm(x, g_ref[...]).astype(BF16)
    acc = x
    for c in range(wup_ref.shape[1] // FF_CHUNK):
        sl = slice(c * FF_CHUNK, (c + 1) * FF_CHUNK)
        up = jnp.maximum(jnp.dot(hb, wup_ref[:, sl], preferred_element_type=F32), 0.0)
        acc = acc + jnp.dot((up * up).astype(BF16), wdown_ref[sl, :], preferred_element_type=F32)
    o_ref[...] = _rmsnorm(acc, gf_ref[...])


def _mlp_call(x1, g, wup, wdown, gf):
    t, d = x1.shape
    tm = ROW_TILE
    row = lambda i: (i, 0)
    wspec = lambda a: pl.BlockSpec(a.shape, lambda i: (0, 0))
    return pl.pallas_call(
        _mlp_kernel,
        grid=(t // tm,),
        in_specs=[pl.BlockSpec((tm, d), row), wspec(g), wspec(wup), wspec(wdown), wspec(gf)],
        out_specs=pl.BlockSpec((tm, d), row),
        out_shape=jax.ShapeDtypeStruct((t, d), F32),
        compiler_params=_cparams(("arbitrary",)),
        name="mlp",
    )(x1, g, wup, wdown, gf)


def _interleave_heads(wq):
    d = wq.shape[0]
    return wq.reshape(d, NSA_KV_HEADS, HEADS_PER_GROUP, HEAD_DIM).transpose(0, 2, 1, 3).reshape(d, Q_LANES)


def _rope_tables(seq):
    half = ROPE_DIM // 2
    inv = ROPE_THETA ** (-(jnp.arange(half, dtype=F32) * 2.0) / ROPE_DIM)
    ang = jnp.arange(seq, dtype=F32)[:, None] * inv[None, :]
    cos, sin = jnp.cos(ang), jnp.sin(ang)
    rest = HEAD_DIM - ROPE_DIM
    cos_h = jnp.concatenate([cos, cos, jnp.ones((seq, rest), F32)], axis=1)
    slo_h = jnp.concatenate([-sin, jnp.zeros((seq, half + rest), F32)], axis=1)
    shi_h = jnp.concatenate([jnp.zeros((seq, half), F32), sin, jnp.zeros((seq, rest), F32)], axis=1)
    reps = V7X_LANES // HEAD_DIM
    return jnp.tile(cos_h, (1, reps)), jnp.tile(slo_h, (1, reps)), jnp.tile(shi_h, (1, reps))


def _compress_weights(pe, w1, w2):
    eye = jnp.eye(NSA_KV_HEADS, dtype=F32)
    w1e = jnp.einsum('tdj,gk->tgdkj', w1.reshape(CMP_BLOCK, HEAD_DIM, CMP_HIDDEN), eye)
    w1e = w1e.reshape(CMP_BLOCK * KV_LANES, NSA_KV_HEADS * CMP_HIDDEN).astype(BF16)
    w2e = jnp.einsum('jd,gk->gjkd', w2, eye).reshape(NSA_KV_HEADS * CMP_HIDDEN, KV_LANES).astype(BF16)
    pee = jnp.tile(pe, (1, NSA_KV_HEADS)).reshape(1, CMP_BLOCK * KV_LANES)
    halfw = CMP_STRIDE * KV_LANES
    return pee[:, :halfw], pee[:, halfw:], w1e[:halfw], w1e[halfw:], w2e


def _selection_constants(seq):
    nc = seq // CMP_STRIDE - 1
    nb = seq // SEL_BLOCK
    n_np = np.arange(nc)[:, None] * CMP_STRIDE
    j_np = np.arange(nb)[None, :] * SEL_BLOCK
    overlap = ((n_np < j_np + SEL_BLOCK) & (n_np + CMP_BLOCK > j_np)).astype(np.float32)
    ovt = np.zeros((V7X_LANES, nc + 1), np.float32)
    ovt[:nb, :nc] = overlap.T
    return jnp.asarray(ovt, BF16)


def _s5_matrices(lam_re, lam_im, log_step, b_re, b_im, c_re, c_im):
    hp = lax.Precision.HIGHEST
    ng, ns = lam_re.shape
    gc = b_re.shape[-1]
    L = SSM_CHUNK
    step = jnp.exp(log_step)[:, None]
    a, b = lam_re * step, lam_im * step
    k = jnp.arange(L + 1, dtype=F32)[:, None, None]
    mag = jnp.exp(a[None] * k)
    pr, pi = mag * jnp.cos(b[None] * k), mag * jnp.sin(b[None] * k)
    nr, ni = pr[1] - 1.0, pi[1]
    den = lam_re * lam_re + lam_im * lam_im
    cr, ci = (nr * lam_re + ni * lam_im) / den, (ni * lam_re - nr * lam_im) / den
    bbr = cr[..., None] * b_re - ci[..., None] * b_im
    bbi = cr[..., None] * b_im + ci[..., None] * b_re
    cpr = c_re[None] * pr[:, :, None, :] - c_im[None] * pi[:, :, None, :]
    cpi = c_re[None] * pi[:, :, None, :] + c_im[None] * pr[:, :, None, :]
    kk = (jnp.einsum('kgcn,gnd->kgcd', cpr[:L], bbr, precision=hp)
          - jnp.einsum('kgcn,gnd->kgcd', cpi[:L], bbi, precision=hp))
    pw_r, pw_i = pr[L - 1 - np.arange(L)], pi[L - 1 - np.arange(L)]
    p_r = pw_r[:, :, :, None] * bbr[None] - pw_i[:, :, :, None] * bbi[None]
    p_i = pw_r[:, :, :, None] * bbi[None] + pw_i[:, :, :, None] * bbr[None]
    p_r, p_i = p_r.transpose(1, 0, 3, 2), p_i.transpose(1, 0, 3, 2)
    q_r = cpr[1:].transpose(1, 3, 0, 2)
    q_i = -cpi[1:].transpose(1, 3, 0, 2)
    sup = SSM_SUPER
    nsg = ng // sup
    eye = jnp.eye(sup, dtype=BF16)
    lw = L * sup * gc
    bf = lambda a: a.astype(BF16)
    lag_sg = jnp.einsum('kxacd,ab->xkadbc', bf(kk).reshape(L, nsg, sup, gc, gc), eye).reshape(nsg, L, sup * gc, sup * gc)
    slab_rows = lambda a: bf(a).reshape(nsg, sup, L, gc, -1).transpose(0, 2, 1, 3, 4).reshape(nsg, lw, -1)
    p_c = jnp.concatenate([slab_rows(p_r), slab_rows(p_i)], axis=2)
    rep_p = np.kron(np.eye(2), np.tile(np.eye(ns), (1, sup)))
    own_p = (np.arange(lw)[:, None] // gc) % sup == (np.arange(2 * sup * ns)[None, :] // ns) % sup
    p = jnp.where(own_p, jnp.matmul(p_c, jnp.asarray(rep_p, BF16)), 0)
    q_c = jnp.concatenate([bf(q_r).reshape(nsg, sup * ns, L * gc), bf(q_i).reshape(nsg, sup * ns, L * gc)], axis=1)
    rep_q = np.einsum('ts,cd,b->tcsbd', np.eye(L), np.eye(gc), np.ones(sup)).reshape(L * gc, lw)
    own_q = (np.arange(2 * sup * ns)[:, None] // ns) % sup == (np.arange(lw)[None, :] // gc) % sup
    q = jnp.where(own_q, jnp.matmul(q_c, jnp.asarray(rep_q, BF16)), 0)
    return lag_sg, p, q, pr[L].reshape(nsg, 1, sup * ns), pi[L].reshape(nsg, 1, sup * ns)


def kernel(x, norm_mix_g, w_in, cmp_pe, cmp_k_w1, cmp_k_w2, cmp_v_w1, cmp_v_w2, ssm_lam_re, ssm_lam_im, ssm_log_step, ssm_b_re, ssm_b_im, ssm_c_re, ssm_c_im, ssm_d, w_attn_branch, w_ssm_val, w_ssm_gate, w_out, norm_mlp_g, w_up, w_down, norm_final_g):
    batch, seq, d = x.shape
    depth = w_in.shape[0]
    assert depth == 1, "the final rmsnorm is fused into the single layer's mlp kernel"
    nsa_w = NSA_HEADS * HEAD_DIM
    ssm_w = ssm_d.shape[1]
    o_q, o_kv, o_g, o_u = nsa_w, nsa_w + 6 * KV_LANES, nsa_w + 6 * KV_LANES + 3 * NSA_HEADS, 0
    o_u = o_g + ssm_w
    cos, slo, shi = _rope_tables(seq)
    ovt = _selection_constants(seq)
    head_order = np.array([g * HEADS_PER_GROUP + h for h in range(HEADS_PER_GROUP) for g in range(NSA_KV_HEADS)])
    x2 = x.reshape(batch * seq, d)
    for l in range(depth):
        wl = w_in[l]
        wq = _interleave_heads(wl[:, :o_q]).astype(BF16)
        wkv = wl[:, o_q:o_kv].astype(BF16)
        wg = jnp.pad(wl[:, o_kv:o_g], ((0, 0), (0, V7X_LANES - 3 * NSA_HEADS))).astype(BF16)
        wu = wl[:, o_g:o_u].astype(BF16)
        wga = wl[:, o_u:o_u + d].astype(BF16)
        wgb = wl[:, o_u + d:].astype(BF16)
        g_mix = norm_mix_g[l].reshape(1, d)
        qraw, qrot, kc_raw, vc_raw, ksat, vs, kwt, vw, gates, u = _proj_call(
            x2, g_mix, wq, wkv, wg, wu, cos, slo, shi, batch, seq)

        pea, peb, kw1a, kw1b, kw2 = _compress_weights(cmp_pe[l], cmp_k_w1[l], cmp_k_w2[l])
        _, _, vw1a, vw1b, vw2 = _compress_weights(cmp_pe[l], cmp_v_w1[l], cmp_v_w2[l])
        kct, vc = _compress_call(kc_raw, vc_raw, pea, peb, kw1a, kw1b, kw2.T, vw1a, vw1b, vw2, batch, seq)

        b3 = lambda a: a.reshape(batch, seq, a.shape[-1])
        attn = _nsa_call(b3(qraw), b3(qrot), b3(gates), kct, vc, ksat, b3(vs), kwt, b3(vw), ovt)
        attn = attn.reshape(batch * seq, nsa_w)

        lag, pm, qm, lr, li = _s5_matrices(ssm_lam_re[l], ssm_lam_im[l], ssm_log_step[l], ssm_b_re[l], ssm_b_im[l],
                                      ssm_c_re[l], ssm_c_im[l])
        y_ssm = _s5_call(u, lag, pm, qm, lr, li, ssm_d[l].reshape(1, ssm_w), batch, seq)

        wattn = w_attn_branch[l].reshape(NSA_HEADS, HEAD_DIM, d)[head_order].reshape(nsa_w, d).astype(BF16)
        x1 = _merge_call(x2, g_mix, attn, y_ssm, wga, wgb, wattn, w_ssm_val[l].astype(BF16),
                         w_ssm_gate[l].astype(BF16), w_out[l].astype(BF16))
        x2 = _mlp_call(x1, norm_mlp_g[l].reshape(1, d), w_up[l].astype(BF16), w_down[l].astype(BF16),
                       norm_final_g.reshape(1, d))
    return x2.reshape(batch, seq, d)
```

```python
import functools
import math

import jax
import jax.numpy as jnp
import numpy as np
from jax import lax
from jax.experimental import pallas as pl
from jax.experimental.pallas import tpu as pltpu

NSA_HEADS = 8
NSA_KV_HEADS = 2
HEAD_DIM = 64
CMP_BLOCK = 32
CMP_STRIDE = 16
CMP_HIDDEN = 256
SEL_BLOCK = 64
N_SEL = 16
WINDOW = 512
FORCE_BONUS = 1e3
NEG_INF = -1e30
ROPE_THETA = 500000.0
ROPE_DIM = HEAD_DIM // 4
SSM_GROUP = 16
SSM_STATE = 64
EPS = 1e-6

HEADS_PER_GROUP = NSA_HEADS // NSA_KV_HEADS
KV_LANES = NSA_KV_HEADS * HEAD_DIM
Q_LANES = NSA_HEADS * HEAD_DIM

V7X_LANES = 128
V7X_MXU = 256
V7X_VMEM_BYTES = 64 * 1024 * 1024
VMEM_LIMIT = V7X_VMEM_BYTES - 8 * 1024 * 1024

ROW_TILE = 512
SEL_TILE = 1024
NSA_QUERIES = 256
SSM_CHUNK = 16
SSM_SUPER = V7X_LANES // SSM_GROUP
FF_CHUNK = 1024

LOG2E = math.log2(math.e)
MASK_BIAS = 1e30

BF16 = jnp.bfloat16
F32 = jnp.float32
_NT = (((1,), (1,)), ((), ()))


def _cparams(semantics):
    return pltpu.CompilerParams(dimension_semantics=semantics, vmem_limit_bytes=VMEM_LIMIT)


def _rmsnorm(x, g):
    return x * lax.rsqrt(jnp.mean(x * x, axis=-1, keepdims=True) + EPS) * g


def _gelu(x):
    return jax.nn.gelu(x)


def _rope_cols(x, cos, sin_lo, sin_hi):
    cols = []
    for c in range(x.shape[1] // V7X_LANES):
        xc = x[:, c * V7X_LANES:(c + 1) * V7X_LANES]
        up = pltpu.roll(xc, V7X_LANES - ROPE_DIM // 2, axis=1)
        dn = pltpu.roll(xc, ROPE_DIM // 2, axis=1)
        cols.append(xc * cos + up * sin_lo + dn * sin_hi)
    return jnp.concatenate(cols, axis=1) if len(cols) > 1 else cols[0]


def _values_with_ones(v):
    lane_head = lax.broadcasted_iota(jnp.int32, v.shape, 1) // HEAD_DIM
    return jnp.concatenate([jnp.where(lane_head == g, v, 1.0) for g in range(NSA_KV_HEADS)], axis=1).astype(BF16)


def _proj_kernel(x_ref, g_ref, wq_ref, wkv_ref, wg_ref, wu_ref, cos_ref, slo_ref, shi_ref,
                 qraw_ref, qrot_ref, kc_ref, vc_ref, ksat_ref, vs_ref, kwt_ref, vw_ref, gate_ref, u_ref, *, seq_tiles):
    hb = _rmsnorm(x_ref[...], g_ref[...]).astype(BF16)
    cos, slo, shi = cos_ref[...], slo_ref[...], shi_ref[...]
    q = jnp.dot(hb, wq_ref[...], preferred_element_type=F32) * (HEAD_DIM ** -0.5 * LOG2E)
    qraw_ref[...] = q.astype(BF16)
    qrot_ref[...] = _rope_cols(q, cos, slo, shi).astype(BF16)
    kv = jnp.dot(hb, wkv_ref[...], preferred_element_type=F32)
    w = KV_LANES
    kc_ref[...] = kv[:, 0 * w:1 * w]
    vc_ref[...] = kv[:, 1 * w:2 * w]
    tm = x_ref.shape[0]
    pos = (pl.program_id(0) % seq_tiles) * tm + lax.broadcasted_iota(jnp.int32, (V7X_LANES, tm), 1)
    onehot = jnp.where(lax.broadcasted_iota(jnp.int32, (V7X_LANES, tm), 0) == pos // SEL_BLOCK, 1.0, 0.0)
    ksat_ref[0, :w, :] = _rope_cols(kv[:, 2 * w:3 * w], cos, slo, shi).T.astype(BF16)
    ksat_ref[0, w:, :] = onehot.astype(BF16)
    vs_ref[...] = _values_with_ones(kv[:, 3 * w:4 * w])
    kwt_ref[0] = _rope_cols(kv[:, 4 * w:5 * w], cos, slo, shi).T.astype(BF16)
    vw_ref[...] = _values_with_ones(kv[:, 5 * w:6 * w])
    gate_ref[...] = jax.nn.sigmoid(jnp.dot(hb, wg_ref[...], preferred_element_type=F32))
    u = jnp.dot(hb, wu_ref[...], preferred_element_type=F32)
    for s in range(u_ref.shape[0]):
        u_ref[s] = u[:, s * V7X_LANES:(s + 1) * V7X_LANES]


def _proj_call(x2, g, wq, wkv, wg, wu, cos, slo, shi, batch, seq):
    t, d = x2.shape
    tm = ROW_TILE
    s_tiles = seq // tm
    row = lambda i: (i, 0)
    pos = lambda i: (i % s_tiles, 0)
    trans = lambda i: (i // s_tiles, 0, i % s_tiles)
    wspec = lambda a: pl.BlockSpec(a.shape, lambda i: (0, 0))
    rows_out = lambda n, dt: (jax.ShapeDtypeStruct((t, n), dt), pl.BlockSpec((tm, n), row))
    trans_out = lambda n: (jax.ShapeDtypeStruct((batch, n, seq), BF16), pl.BlockSpec((1, n, tm), trans))
    outs = [rows_out(Q_LANES, BF16), rows_out(Q_LANES, BF16), rows_out(KV_LANES, F32), rows_out(KV_LANES, F32),
            trans_out(KV_LANES + V7X_LANES), rows_out(NSA_KV_HEADS * KV_LANES, BF16),
            trans_out(KV_LANES), rows_out(NSA_KV_HEADS * KV_LANES, BF16),
            rows_out(V7X_LANES, F32),
            (jax.ShapeDtypeStruct((wu.shape[1] // V7X_LANES, t, V7X_LANES), F32),
             pl.BlockSpec((wu.shape[1] // V7X_LANES, tm, V7X_LANES), lambda i: (0, i, 0)))]
    assert seq // SEL_BLOCK <= V7X_LANES, "the selection-block one-hot must fit one lane tile"
    return pl.pallas_call(
        functools.partial(_proj_kernel, seq_tiles=s_tiles),
        grid=(t // tm,),
        in_specs=[pl.BlockSpec((tm, d), row), wspec(g), wspec(wq), wspec(wkv), wspec(wg), wspec(wu),
                  pl.BlockSpec((tm, V7X_LANES), pos), pl.BlockSpec((tm, V7X_LANES), pos),
                  pl.BlockSpec((tm, V7X_LANES), pos)],
        out_specs=[o[1] for o in outs],
        out_shape=[o[0] for o in outs],
        compiler_params=_cparams(("arbitrary",)),
        name="proj",
    )(x2, g, wq, wkv, wg, wu, cos, slo, shi)


def _compress_kernel(k_ref, v_ref, pea_ref, peb_ref, kw1a_ref, kw1b_ref, kw2t_ref, vw1a_ref, vw1b_ref, vw2_ref,
                     kct_ref, vc_ref):
    nch = k_ref.shape[0]

    def hidden(src_ref, w1a_ref, w1b_ref):
        chunks = src_ref[...]
        ha = jnp.dot((chunks + pea_ref[...]).astype(BF16), w1a_ref[...], preferred_element_type=F32)
        hb = jnp.dot((chunks + peb_ref[...]).astype(BF16), w1b_ref[...], preferred_element_type=F32)
        return _gelu(ha + pltpu.roll(hb, nch - 1, axis=0)).astype(BF16)

    kct_ref[0] = lax.dot_general(kw2t_ref[...], hidden(k_ref, kw1a_ref, kw1b_ref), _NT,
                                 preferred_element_type=F32).astype(BF16)
    vc_ref[0] = jnp.dot(hidden(v_ref, vw1a_ref, vw1b_ref), vw2_ref[...], preferred_element_type=F32).astype(BF16)


def _compress_call(kc_raw, vc_raw, pea, peb, kw1a, kw1b, kw2, vw1a, vw1b, vw2, batch, seq):
    nch = seq // CMP_STRIDE
    wspec = lambda a: pl.BlockSpec(a.shape, lambda b: (0, 0))
    chunked = lambda a: a.reshape(batch * nch, CMP_STRIDE * KV_LANES)
    chunk_spec = pl.BlockSpec((nch, CMP_STRIDE * KV_LANES), lambda b: (b, 0))
    return pl.pallas_call(
        _compress_kernel,
        grid=(batch,),
        in_specs=[chunk_spec, chunk_spec,
                  wspec(pea), wspec(peb), wspec(kw1a), wspec(kw1b), wspec(kw2), wspec(vw1a), wspec(vw1b), wspec(vw2)],
        out_specs=[pl.BlockSpec((1, KV_LANES, nch), lambda b: (b, 0, 0)),
                   pl.BlockSpec((1, nch, KV_LANES), lambda b: (b, 0, 0))],
        out_shape=[jax.ShapeDtypeStruct((batch, KV_LANES, nch), BF16), jax.ShapeDtypeStruct((batch, nch, KV_LANES), BF16)],
        compiler_params=_cparams(("arbitrary",)),
        name="compress",
    )(chunked(kc_raw), chunked(vc_raw), pea, peb, kw1a, kw1b, kw2, vw1a, vw1b, vw2)


def _window_keys(nq):
    return WINDOW + nq


def _nsa_kernel(qraw_ref, qrot_ref, gate_ref, kct_ref, vc_ref, ksat_ref, vs_ref, kwt_ref, vw_ref, ovt_ref,
                o_ref, score_ref, qa_ref, *, n_sel):
    i = pl.program_id(1)
    nq = qraw_ref.shape[1]
    hg = HEADS_PER_GROUP
    rows = hg * nq
    lanes_gq = NSA_KV_HEADS * nq
    nb = score_ref.shape[0]
    nc = kct_ref.shape[2]
    t0 = i * nq
    blk_first = t0 // SEL_BLOCK
    blk_last = blk_first + nq // SEL_BLOCK - 1
    tq_rows = t0 + lax.broadcasted_iota(jnp.int32, (rows, 1), 0) % nq

    own_lanes = [lax.broadcasted_iota(jnp.int32, (nq, KV_LANES), 1) // HEAD_DIM == g for g in range(NSA_KV_HEADS)]

    def group_q(ref, g):
        return jnp.concatenate([jnp.where(own_lanes[g], ref[0, :, h * KV_LANES:(h + 1) * KV_LANES], 0.0).astype(BF16)
                                for h in range(hg)], axis=0)

    def normalized(pv):
        return pv * (1.0 / pltpu.roll(pv, HEAD_DIM, axis=1))

    nwin = _window_keys(nq)
    w0 = pl.multiple_of(jnp.maximum(blk_first - WINDOW // SEL_BLOCK, 0) * SEL_BLOCK, V7X_LANES)
    kp = w0 + lax.broadcasted_iota(jnp.int32, (1, nwin), 1)
    bias_w = jnp.where(kp <= tq_rows, jnp.where(kp > tq_rows - WINDOW, 0.0, NEG_INF), NEG_INF)
    o_win = []
    for g in range(NSA_KV_HEADS):
        s = jnp.dot(group_q(qrot_ref, g), kwt_ref[0, :, pl.ds(w0, nwin)], preferred_element_type=F32) + bias_w
        e = jnp.exp2(s - jnp.max(s, axis=-1, keepdims=True))
        v = vw_ref[0, pl.ds(w0, nwin), g * KV_LANES:(g + 1) * KV_LANES]
        o_win.append(normalized(jnp.dot(e.astype(BF16), v, preferred_element_type=F32)))

    def compressed_branch(width):
        cmp_end = lax.broadcasted_iota(jnp.int32, (1, width), 1) * CMP_STRIDE + (CMP_BLOCK - 1)
        bias_c = jnp.where(cmp_end <= tq_rows, 0.0, NEG_INF)
        any_valid = tq_rows >= CMP_BLOCK - 1
        outs, p_sum = [], []
        for g in range(NSA_KV_HEADS):
            s = jnp.dot(group_q(qraw_ref, g), kct_ref[0, :, :width], preferred_element_type=F32) + bias_c
            e = jnp.exp2(s - jnp.max(s, axis=-1, keepdims=True))
            p = e * jnp.where(any_valid, 1.0 / jnp.sum(e, axis=-1, keepdims=True), 0.0)
            outs.append(jnp.dot(p.astype(BF16), vc_ref[0, :width, :], preferred_element_type=F32))
            ph = p[0:nq]
            for h in range(1, hg):
                ph = ph + p[h * nq:(h + 1) * nq]
            p_sum.append(ph)
        p2 = jnp.concatenate(p_sum, axis=0)
        p_hi = p2.astype(BF16)
        p_lo = (p2 - p_hi.astype(F32)).astype(BF16)
        ovt = ovt_ref[:, :width]
        imp = (lax.dot_general(ovt, p_hi, _NT, preferred_element_type=F32)
               + lax.dot_general(ovt, p_lo, _NT, preferred_element_type=F32))
        return tuple(outs) + (imp,)

    *o_cmp, imp_t = compressed_branch(nc)

    blk_q = (t0 + lax.broadcasted_iota(jnp.int32, (1, lanes_gq), 1) % nq) // SEL_BLOCK
    jfull = lax.broadcasted_iota(jnp.int32, (nb, lanes_gq), 0)
    valid = jfull <= blk_q
    forced = (jfull == 0) | (jfull == blk_q) | (jfull == blk_q - 1)
    score = jnp.where(valid, imp_t + jnp.where(forced, FORCE_BONUS, 0.0), NEG_INF)

    rest = score
    for _ in range(n_sel):
        thr = jnp.max(rest, axis=0, keepdims=True)
        rest = jnp.where(rest >= thr, NEG_INF, rest)
    sel_t = jnp.where(valid, jnp.where(score >= thr, 1.0, 0.0), 0.0)

    def ranked_members():
        def count_above_or_tied_earlier(jp, cnt):
            row = jnp.broadcast_to(score_ref[pl.ds(jp, 1), :], score.shape)
            tie = jnp.where(jp < jfull, 1.0, 0.0)
            return cnt + jnp.where(row > score, 1.0, jnp.where(row == score, tie, 0.0))

        score_ref[...] = score
        cnt = lax.fori_loop(0, blk_last + 1, count_above_or_tied_earlier, jnp.zeros(score.shape, F32))
        return jnp.where(valid, jnp.where(cnt < float(n_sel), 1.0, 0.0), 0.0)

    picked = jnp.sum(sel_t, axis=0, keepdims=True)
    wrong = jnp.sum(jnp.where(picked == jnp.minimum(blk_q + 1, n_sel).astype(F32), 0.0, 1.0))
    sel_t = lax.cond(wrong == 0.0, lambda: sel_t, ranked_members)
    bias = ((sel_t.T - 1.0) * MASK_BIAS).astype(BF16)

    n_tiles = blk_last // (SEL_TILE // SEL_BLOCK) + 1
    lane_t = lax.broadcasted_iota(jnp.int32, (1, SEL_TILE), 1)
    for g in range(NSA_KV_HEADS):
        qa_ref[g, :, :KV_LANES] = group_q(qrot_ref, g)
        qa_ref[g, :, KV_LANES:] = jnp.concatenate([bias[g * nq:(g + 1) * nq]] * hg, axis=0)

    def tile_body(c, carry, causal):
        k0 = pl.multiple_of(c * SEL_TILE, SEL_TILE)
        ka = ksat_ref[0, :, pl.ds(k0, SEL_TILE)]
        new = []
        for g in range(NSA_KV_HEADS):
            m, acc = carry[g]
            s = jnp.dot(qa_ref[g], ka, preferred_element_type=F32)
            if causal:
                s = jnp.where((k0 + lane_t) <= tq_rows, s, NEG_INF)
            m_new = jnp.maximum(m, jnp.max(s, axis=-1, keepdims=True))
            p = jnp.exp2(s - m_new)
            v = vs_ref[0, pl.ds(k0, SEL_TILE), g * KV_LANES:(g + 1) * KV_LANES]
            acc = jnp.exp2(m - m_new) * acc + jnp.dot(p.astype(BF16), v, preferred_element_type=F32)
            new.append((m_new, acc))
        return tuple(new)

    init = tuple((jnp.full((rows, 1), NEG_INF, F32), jnp.zeros((rows, KV_LANES), F32)) for _ in range(NSA_KV_HEADS))
    carry = lax.fori_loop(0, n_tiles - 1, functools.partial(tile_body, causal=False), init)
    carry = tile_body(n_tiles - 1, carry, causal=True)
    o_sel = [normalized(acc) for _, acc in carry]

    gt = gate_ref[0]
    for h in range(hg):
        r = slice(h * nq, (h + 1) * nq)
        parts = []
        for g in range(NSA_KV_HEADS):
            c = (g * hg + h) * 3
            parts.append(gt[:, c:c + 1] * o_cmp[g][r] + gt[:, c + 1:c + 2] * o_sel[g][r] + gt[:, c + 2:c + 3] * o_win[g][r])
        out = parts[0]
        for g in range(1, NSA_KV_HEADS):
            out = jnp.where(own_lanes[g], parts[g], out)
        o_ref[0, :, h * KV_LANES:(h + 1) * KV_LANES] = out.astype(BF16)


def _nsa_call(qraw, qrot, gates, kct, vc, ksat, vs, kwt, vw, ovt):
    batch, seq, _ = qraw.shape
    nq = NSA_QUERIES
    assert nq % V7X_LANES == 0 and seq % nq == 0 and seq >= _window_keys(nq) and seq % SEL_TILE == 0
    step = lambda n: pl.BlockSpec((1, nq, n), lambda b, i: (b, i, 0))
    whole = lambda a: pl.BlockSpec((1,) + a.shape[1:], lambda b, i: (b, 0, 0))
    return pl.pallas_call(
        functools.partial(_nsa_kernel, n_sel=min(N_SEL, seq // SEL_BLOCK)),
        grid=(batch, seq // nq),
        in_specs=[step(Q_LANES), step(Q_LANES), step(V7X_LANES),
                  whole(kct), whole(vc), whole(ksat), whole(vs), whole(kwt), whole(vw),
                  pl.BlockSpec(ovt.shape, lambda b, i: (0, 0))],
        out_specs=step(HEADS_PER_GROUP * KV_LANES),
        out_shape=jax.ShapeDtypeStruct((batch, seq, HEADS_PER_GROUP * KV_LANES), BF16),
        scratch_shapes=[pltpu.VMEM((V7X_LANES, NSA_KV_HEADS * nq), F32),
                        pltpu.VMEM((NSA_KV_HEADS, HEADS_PER_GROUP * nq, KV_LANES + V7X_LANES), BF16)],
        compiler_params=_cparams(("arbitrary", "arbitrary")),
        name="nsa",
    )(qraw, qrot, gates, kct, vc, ksat, vs, kwt, vw, ovt)


def _s5_kernel(u_ref, lag_ref, p_ref, q_ref, lr_ref, li_ref, d_ref, o_ref, m_ref, x_ref, y_ref, z_ref, sp_ref):
    nch = x_ref.shape[0]
    half = z_ref.shape[1] // 2
    lanes = V7X_LANES

    @pl.when(pl.program_id(1) == 0)
    def _():
        for s in range(SSM_CHUNK):
            for t in range(SSM_CHUNK):
                tile = lag_ref[0, t - s] if t >= s else jnp.zeros((lanes, lanes), BF16)
                m_ref[s * lanes:(s + 1) * lanes, t * lanes:(t + 1) * lanes] = tile

    x_ref[...] = u_ref[0].astype(BF16)
    for c0 in range(0, SSM_CHUNK * lanes, V7X_MXU):
        c1 = c0 + V7X_MXU
        y_ref[:, c0:c1] = jnp.dot(x_ref[:, :c1], m_ref[:c1, c0:c1], preferred_element_type=F32)
    z_ref[...] = jnp.dot(x_ref[...], p_ref[0], preferred_element_type=F32)
    lr, li = lr_ref[0], li_ref[0]

    def scan_body(k, carry):
        sr, si = carry
        sp_ref[pl.ds(k, 1), 0:half] = sr
        sp_ref[pl.ds(k, 1), half:2 * half] = si
        zr = z_ref[pl.ds(k, 1), 0:half]
        zi = z_ref[pl.ds(k, 1), half:2 * half]
        return lr * sr - li * si + zr, lr * si + li * sr + zi

    zero = jnp.zeros((1, half), F32)
    lax.fori_loop(0, nch, scan_body, (zero, zero))
    y = y_ref[...] + jnp.dot(sp_ref[...].astype(BF16), q_ref[0], preferred_element_type=F32)
    o_ref[0] = _gelu(y + d_ref[0] * u_ref[0])


def _s5_call(u, lag, p, q, lr, li, dskip, batch, seq):
    nsg = lag.shape[0]
    nch = seq // SSM_CHUNK
    nstate = q.shape[1]
    lw = SSM_CHUNK * V7X_LANES
    slab = lambda a: pl.BlockSpec((1,) + a.shape[1:], lambda g, b: (g,) + (0,) * (a.ndim - 1),
                                  pipeline_mode=pl.Buffered(1))
    chunks = pl.BlockSpec((1, nch, lw), lambda g, b: (g, b, 0))
    d_chunk = jnp.tile(dskip.reshape(nsg, 1, V7X_LANES), (1, 1, SSM_CHUNK))
    out = pl.pallas_call(
        _s5_kernel,
        grid=(nsg, batch),
        in_specs=[chunks, slab(lag), slab(p), slab(q),
                  pl.BlockSpec((1, 1, nstate // 2), lambda g, b: (g, 0, 0)),
                  pl.BlockSpec((1, 1, nstate // 2), lambda g, b: (g, 0, 0)),
                  pl.BlockSpec((1, 1, lw), lambda g, b: (g, 0, 0))],
        out_specs=chunks,
        out_shape=jax.ShapeDtypeStruct((nsg, batch * nch, lw), F32),
        scratch_shapes=[pltpu.VMEM((lw, lw), BF16), pltpu.VMEM((nch, lw), BF16), pltpu.VMEM((nch, lw), F32),
                        pltpu.VMEM((nch, nstate), F32), pltpu.VMEM((nch, nstate), F32)],
        compiler_params=_cparams(("arbitrary", "arbitrary")),
        name="s5",
    )(u.reshape(nsg, batch * nch, lw), lag, p, q, lr, li, d_chunk)
    return out.reshape(u.shape)


def _merge_kernel(x_ref, g_ref, attn_ref, ssm_ref, wga_ref, wgb_ref, wattn_ref, wval_ref, wgate_ref, wout_ref, o_ref):
    x = x_ref[...]
    hb = _rmsnorm(x, g_ref[...]).astype(BF16)
    dot = functools.partial(jnp.dot, preferred_element_type=F32)
    y_a = dot(attn_ref[...], wattn_ref[...])
    ys = jnp.concatenate([ssm_ref[s] for s in range(ssm_ref.shape[0])], axis=1).astype(BF16)
    y_b = dot(ys, wval_ref[...]) * jax.nn.sigmoid(dot(ys, wgate_ref[...]))
    merged = jax.nn.sigmoid(dot(hb, wga_ref[...])) * y_a + jax.nn.sigmoid(dot(hb, wgb_ref[...])) * y_b
    o_ref[...] = x + dot(merged.astype(BF16), wout_ref[...])


def _merge_call(x2, g, attn, ssm, wga, wgb, wattn, wval, wgate, wout):
    t, d = x2.shape
    tm = ROW_TILE
    row = lambda i: (i, 0)
    wspec = lambda a: pl.BlockSpec(a.shape, lambda i: (0, 0))
    return pl.pallas_call(
        _merge_kernel,
        grid=(t // tm,),
        in_specs=[pl.BlockSpec((tm, d), row), wspec(g), pl.BlockSpec((tm, attn.shape[1]), row),
                  pl.BlockSpec((ssm.shape[0], tm, ssm.shape[2]), lambda i: (0, i, 0)), wspec(wga), wspec(wgb), wspec(wattn), wspec(wval),
                  wspec(wgate), wspec(wout)],
        out_specs=pl.BlockSpec((tm, d), row),
        out_shape=jax.ShapeDtypeStruct((t, d), F32),
        compiler_params=_cparams(("arbitrary",)),
        name="merge",
    )(x2, g, attn, ssm, wga, wgb, wattn, wval, wgate, wout)


def _mlp_kernel(x_ref, g_ref, wup_ref, wdown_ref, gf_ref, o_ref):
    x = x_ref[...]
    hb = _rmsnorm(x, g_ref[...]).astype(BF16)
    acc = x
    for c in range(wup_ref.shape[1] // FF_CHUNK):
        sl = slice(c * FF_CHUNK, (c + 1) * FF_CHUNK)
        up = jnp.maximum(jnp.dot(hb, wup_ref[:, sl], preferred_element_type=F32), 0.0)
        acc = acc + jnp.dot((up * up).astype(BF16), wdown_ref[sl, :], preferred_element_type=F32)
    o_ref[...] = _rmsnorm(acc, gf_ref[...])


def _mlp_call(x1, g, wup, wdown, gf):
    t, d = x1.shape
    tm = ROW_TILE
    row = lambda i: (i, 0)
    wspec = lambda a: pl.BlockSpec(a.shape, lambda i: (0, 0))
    return pl.pallas_call(
        _mlp_kernel,
        grid=(t // tm,),
        in_specs=[pl.BlockSpec((tm, d), row), wspec(g), wspec(wup), wspec(wdown), wspec(gf)],
        out_specs=pl.BlockSpec((tm, d), row),
        out_shape=jax.ShapeDtypeStruct((t, d), F32),
        compiler_params=_cparams(("arbitrary",)),
        name="mlp",
    )(x1, g, wup, wdown, gf)


def _interleave_heads(wq):
    d = wq.shape[0]
    return wq.reshape(d, NSA_KV_HEADS, HEADS_PER_GROUP, HEAD_DIM).transpose(0, 2, 1, 3).reshape(d, Q_LANES)


def _rope_tables(seq):
    half = ROPE_DIM // 2
    inv = ROPE_THETA ** (-(jnp.arange(half, dtype=F32) * 2.0) / ROPE_DIM)
    ang = jnp.arange(seq, dtype=F32)[:, None] * inv[None, :]
    cos, sin = jnp.cos(ang), jnp.sin(ang)
    rest = HEAD_DIM - ROPE_DIM
    cos_h = jnp.concatenate([cos, cos, jnp.ones((seq, rest), F32)], axis=1)
    slo_h = jnp.concatenate([-sin, jnp.zeros((seq, half + rest), F32)], axis=1)
    shi_h = jnp.concatenate([jnp.zeros((seq, half), F32), sin, jnp.zeros((seq, rest), F32)], axis=1)
    reps = V7X_LANES // HEAD_DIM
    return jnp.tile(cos_h, (1, reps)), jnp.tile(slo_h, (1, reps)), jnp.tile(shi_h, (1, reps))


def _compress_weights(pe, w1, w2):
    eye = jnp.eye(NSA_KV_HEADS, dtype=F32)
    w1e = jnp.einsum('tdj,gk->tgdkj', w1.reshape(CMP_BLOCK, HEAD_DIM, CMP_HIDDEN), eye)
    w1e = w1e.reshape(CMP_BLOCK * KV_LANES, NSA_KV_HEADS * CMP_HIDDEN).astype(BF16)
    w2e = jnp.einsum('jd,gk->gjkd', w2, eye).reshape(NSA_KV_HEADS * CMP_HIDDEN, KV_LANES).astype(BF16)
    pee = jnp.tile(pe, (1, NSA_KV_HEADS)).reshape(1, CMP_BLOCK * KV_LANES)
    halfw = CMP_STRIDE * KV_LANES
    return pee[:, :halfw], pee[:, halfw:], w1e[:halfw], w1e[halfw:], w2e


def _selection_constants(seq):
    nc = seq // CMP_STRIDE - 1
    nb = seq // SEL_BLOCK
    n_np = np.arange(nc)[:, None] * CMP_STRIDE
    j_np = np.arange(nb)[None, :] * SEL_BLOCK
    overlap = ((n_np < j_np + SEL_BLOCK) & (n_np + CMP_BLOCK > j_np)).astype(np.float32)
    ovt = np.zeros((V7X_LANES, nc + 1), np.float32)
    ovt[:nb, :nc] = overlap.T
    return jnp.asarray(ovt, BF16)


def _s5_matrices(lam_re, lam_im, log_step, b_re, b_im, c_re, c_im):
    hp = lax.Precision.HIGHEST
    ng, ns = lam_re.shape
    gc = b_re.shape[-1]
    L = SSM_CHUNK
    step = jnp.exp(log_step)[:, None]
    a, b = lam_re * step, lam_im * step
    k = jnp.arange(L + 1, dtype=F32)[:, None, None]
    mag = jnp.exp(a[None] * k)
    pr, pi = mag * jnp.cos(b[None] * k), mag * jnp.sin(b[None] * k)
    nr, ni = pr[1] - 1.0, pi[1]
    den = lam_re * lam_re + lam_im * lam_im
    cr, ci = (nr * lam_re + ni * lam_im) / den, (ni * lam_re - nr * lam_im) / den
    bbr = cr[..., None] * b_re - ci[..., None] * b_im
    bbi = cr[..., None] * b_im + ci[..., None] * b_re
    cpr = c_re[None] * pr[:, :, None, :] - c_im[None] * pi[:, :, None, :]
    cpi = c_re[None] * pi[:, :, None, :] + c_im[None] * pr[:, :, None, :]
    kk = (jnp.einsum('kgcn,gnd->kgcd', cpr[:L], bbr, precision=hp)
          - jnp.einsum('kgcn,gnd->kgcd', cpi[:L], bbi, precision=hp))
    pw_r, pw_i = pr[L - 1 - np.arange(L)], pi[L - 1 - np.arange(L)]
    p_r = pw_r[:, :, :, None] * bbr[None] - pw_i[:, :, :, None] * bbi[None]
    p_i = pw_r[:, :, :, None] * bbi[None] + pw_i[:, :, :, None] * bbr[None]
    p_r, p_i = p_r.transpose(1, 0, 3, 2), p_i.transpose(1, 0, 3, 2)
    q_r = cpr[1:].transpose(1, 3, 0, 2)
    q_i = -cpi[1:].transpose(1, 3, 0, 2)
    sup = SSM_SUPER
    nsg = ng // sup
    eye = jnp.eye(sup, dtype=BF16)
    lw = L * sup * gc
    bf = lambda a: a.astype(BF16)
    lag_sg = jnp.einsum('kxacd,ab->xkadbc', bf(kk).reshape(L, nsg, sup, gc, gc), eye).reshape(nsg, L, sup * gc, sup * gc)
    slab_rows = lambda a: bf(a).reshape(nsg, sup, L, gc, -1).transpose(0, 2, 1, 3, 4).reshape(nsg, lw, -1)
    p_c = jnp.concatenate([slab_rows(p_r), slab_rows(p_i)], axis=2)
    rep_p = np.kron(np.eye(2), np.tile(np.eye(ns), (1, sup)))
    own_p = (np.arange(lw)[:, None] // gc) % sup == (np.arange(2 * sup * ns)[None, :] // ns) % sup
    p = jnp.where(own_p, jnp.matmul(p_c, jnp.asarray(rep_p, BF16)), 0)
    q_c = jnp.concatenate([bf(q_r).reshape(nsg, sup * ns, L * gc), bf(q_i).reshape(nsg, sup * ns, L * gc)], axis=1)
    rep_q = np.einsum('ts,cd,b->tcsbd', np.eye(L), np.eye(gc), np.ones(sup)).reshape(L * gc, lw)
    own_q = (np.arange(2 * sup * ns)[:, None] // ns) % sup == (np.arange(lw)[None, :] // gc) % sup
    q = jnp.where(own_q, jnp.matmul(q_c, jnp.asarray(rep_q, BF16)), 0)
    return lag_sg, p, q, pr[L].reshape(nsg, 1, sup * ns), pi[L].reshape(nsg, 1, sup * ns)


def kernel(x, norm_mix_g, w_in, cmp_pe, cmp_k_w1, cmp_k_w2, cmp_v_w1, cmp_v_w2, ssm_lam_re, ssm_lam_im, ssm_log_step, ssm_b_re, ssm_b_im, ssm_c_re, ssm_c_im, ssm_d, w_attn_branch, w_ssm_val, w_ssm_gate, w_out, norm_mlp_g, w_up, w_down, norm_final_g):
    batch, seq, d = x.shape
    depth = w_in.shape[0]
    assert depth == 1, "the final rmsnorm is fused into the single layer's mlp kernel"
    nsa_w = NSA_HEADS * HEAD_DIM
    ssm_w = ssm_d.shape[1]
    o_q, o_kv, o_g, o_u = nsa_w, nsa_w + 6 * KV_LANES, nsa_w + 6 * KV_LANES + 3 * NSA_HEADS, 0
    o_u = o_g + ssm_w
    cos, slo, shi = _rope_tables(seq)
    ovt = _selection_constants(seq)
    head_order = np.array([g * HEADS_PER_GROUP + h for h in range(HEADS_PER_GROUP) for g in range(NSA_KV_HEADS)])
    x2 = x.reshape(batch * seq, d)
    for l in range(depth):
        wl = w_in[l]
        wq = _interleave_heads(wl[:, :o_q]).astype(BF16)
        wkv = wl[:, o_q:o_kv].astype(BF16)
        wg = jnp.pad(wl[:, o_kv:o_g], ((0, 0), (0, V7X_LANES - 3 * NSA_HEADS))).astype(BF16)
        wu = wl[:, o_g:o_u].astype(BF16)
        wga = wl[:, o_u:o_u + d].astype(BF16)
        wgb = wl[:, o_u + d:].astype(BF16)
        g_mix = norm_mix_g[l].reshape(1, d)
        qraw, qrot, kc_raw, vc_raw, ksat, vs, kwt, vw, gates, u = _proj_call(
            x2, g_mix, wq, wkv, wg, wu, cos, slo, shi, batch, seq)

        pea, peb, kw1a, kw1b, kw2 = _compress_weights(cmp_pe[l], cmp_k_w1[l], cmp_k_w2[l])
        _, _, vw1a, vw1b, vw2 = _compress_weights(cmp_pe[l], cmp_v_w1[l], cmp_v_w2[l])
        kct, vc = _compress_call(kc_raw, vc_raw, pea, peb, kw1a, kw1b, kw2.T, vw1a, vw1b, vw2, batch, seq)

        b3 = lambda a: a.reshape(batch, seq, a.shape[-1])
        attn = _nsa_call(b3(qraw), b3(qrot), b3(gates), kct, vc, ksat, b3(vs), kwt, b3(vw), ovt)
        attn = attn.reshape(batch * seq, nsa_w)

        lag, pm, qm, lr, li = _s5_matrices(ssm_lam_re[l], ssm_lam_im[l], ssm_log_step[l], ssm_b_re[l], ssm_b_im[l],
                                      ssm_c_re[l], ssm_c_im[l])
        y_ssm = _s5_call(u, lag, pm, qm, lr, li, ssm_d[l], batch, seq)

        wattn = w_attn_branch[l].reshape(NSA_HEADS, HEAD_DIM, d)[head_order].reshape(nsa_w, d).astype(BF16)
        x1 = _merge_call(x2, g_mix, attn, y_ssm, wga, wgb, wattn, w_ssm_val[l].astype(BF16),
                         w_ssm_gate[l].astype(BF16), w_out[l].astype(BF16))
        x2 = _mlp_call(x1, norm_mlp_g[l].reshape(1, d), w_up[l].astype(BF16), w_down[l].astype(BF16),
                       norm_final_g.reshape(1, d))
    return x2.reshape(batch, seq, d)
```

```python
import functools
import math

import jax
import jax.numpy as jnp
import numpy as np
from jax import lax
from jax.experimental import pallas as pl
from jax.experimental.pallas import tpu as pltpu

NSA_HEADS = 8
NSA_KV_HEADS = 2
HEAD_DIM = 64
CMP_BLOCK = 32
CMP_STRIDE = 16
CMP_HIDDEN = 256
SEL_BLOCK = 64
N_SEL = 16
WINDOW = 512
FORCE_BONUS = 1e3
NEG_INF = -1e30
ROPE_THETA = 500000.0
ROPE_DIM = HEAD_DIM // 4
SSM_GROUP = 16
SSM_STATE = 64
EPS = 1e-6

HEADS_PER_GROUP = NSA_HEADS // NSA_KV_HEADS
KV_LANES = NSA_KV_HEADS * HEAD_DIM
Q_LANES = NSA_HEADS * HEAD_DIM

V7X_LANES = 128
V7X_MXU = 256
V7X_VMEM_BYTES = 64 * 1024 * 1024
VMEM_LIMIT = V7X_VMEM_BYTES - 8 * 1024 * 1024

ROW_TILE = 512
SEL_TILE = 1024
NSA_QUERIES = 256
SSM_CHUNK = 16
SSM_SUPER = V7X_LANES // SSM_GROUP
FF_CHUNK = 1024

LOG2E = math.log2(math.e)
MASK_BIAS = 1e30

BF16 = jnp.bfloat16
F32 = jnp.float32
_NT = (((1,), (1,)), ((), ()))


def _cparams(semantics):
    return pltpu.CompilerParams(dimension_semantics=semantics, vmem_limit_bytes=VMEM_LIMIT)


def _rmsnorm(x, g):
    return x * lax.rsqrt(jnp.mean(x * x, axis=-1, keepdims=True) + EPS) * g


def _gelu(x):
    return jax.nn.gelu(x)


def _rope_cols(x, cos, sin_lo, sin_hi):
    cols = []
    for c in range(x.shape[1] // V7X_LANES):
        xc = x[:, c * V7X_LANES:(c + 1) * V7X_LANES]
        up = pltpu.roll(xc, V7X_LANES - ROPE_DIM // 2, axis=1)
        dn = pltpu.roll(xc, ROPE_DIM // 2, axis=1)
        cols.append(xc * cos + up * sin_lo + dn * sin_hi)
    return jnp.concatenate(cols, axis=1) if len(cols) > 1 else cols[0]


def _values_with_ones(v):
    lane_head = lax.broadcasted_iota(jnp.int32, v.shape, 1) // HEAD_DIM
    return jnp.concatenate([jnp.where(lane_head == g, v, 1.0) for g in range(NSA_KV_HEADS)], axis=1).astype(BF16)


def _proj_kernel(x_ref, g_ref, wq_ref, wkv_ref, wg_ref, wu_ref, cos_ref, slo_ref, shi_ref,
                 qraw_ref, qrot_ref, kc_ref, vc_ref, ksat_ref, vs_ref, kwt_ref, vw_ref, gate_ref, u_ref, *, seq_tiles):
    hb = _rmsnorm(x_ref[...], g_ref[...]).astype(BF16)
    cos, slo, shi = cos_ref[...], slo_ref[...], shi_ref[...]
    q = jnp.dot(hb, wq_ref[...], preferred_element_type=F32) * (HEAD_DIM ** -0.5 * LOG2E)
    qraw_ref[...] = q.astype(BF16)
    qrot_ref[...] = _rope_cols(q, cos, slo, shi).astype(BF16)
    kv = jnp.dot(hb, wkv_ref[...], preferred_element_type=F32)
    w = KV_LANES
    kc_ref[...] = kv[:, 0 * w:1 * w]
    vc_ref[...] = kv[:, 1 * w:2 * w]
    tm = x_ref.shape[0]
    pos = (pl.program_id(0) % seq_tiles) * tm + lax.broadcasted_iota(jnp.int32, (V7X_LANES, tm), 1)
    onehot = jnp.where(lax.broadcasted_iota(jnp.int32, (V7X_LANES, tm), 0) == pos // SEL_BLOCK, 1.0, 0.0)
    ksat_ref[0, :w, :] = _rope_cols(kv[:, 2 * w:3 * w], cos, slo, shi).T.astype(BF16)
    ksat_ref[0, w:, :] = onehot.astype(BF16)
    vs_ref[...] = _values_with_ones(kv[:, 3 * w:4 * w])
    kwt_ref[0] = _rope_cols(kv[:, 4 * w:5 * w], cos, slo, shi).T.astype(BF16)
    vw_ref[...] = _values_with_ones(kv[:, 5 * w:6 * w])
    gate_ref[...] = jax.nn.sigmoid(jnp.dot(hb, wg_ref[...], preferred_element_type=F32))
    u_ref[...] = jnp.dot(hb, wu_ref[...], preferred_element_type=F32)


def _proj_call(x2, g, wq, wkv, wg, wu, cos, slo, shi, batch, seq):
    t, d = x2.shape
    tm = ROW_TILE
    s_tiles = seq // tm
    row = lambda i: (i, 0)
    pos = lambda i: (i % s_tiles, 0)
    trans = lambda i: (i // s_tiles, 0, i % s_tiles)
    wspec = lambda a: pl.BlockSpec(a.shape, lambda i: (0, 0))
    rows_out = lambda n, dt: (jax.ShapeDtypeStruct((t, n), dt), pl.BlockSpec((tm, n), row))
    trans_out = lambda n: (jax.ShapeDtypeStruct((batch, n, seq), BF16), pl.BlockSpec((1, n, tm), trans))
    outs = [rows_out(Q_LANES, BF16), rows_out(Q_LANES, BF16), rows_out(KV_LANES, F32), rows_out(KV_LANES, F32),
            trans_out(KV_LANES + V7X_LANES), rows_out(NSA_KV_HEADS * KV_LANES, BF16),
            trans_out(KV_LANES), rows_out(NSA_KV_HEADS * KV_LANES, BF16),
            rows_out(V7X_LANES, F32), rows_out(wu.shape[1], F32)]
    assert seq // SEL_BLOCK <= V7X_LANES, "the selection-block one-hot must fit one lane tile"
    return pl.pallas_call(
        functools.partial(_proj_kernel, seq_tiles=s_tiles),
        grid=(t // tm,),
        in_specs=[pl.BlockSpec((tm, d), row), wspec(g), wspec(wq), wspec(wkv), wspec(wg), wspec(wu),
                  pl.BlockSpec((tm, V7X_LANES), pos), pl.BlockSpec((tm, V7X_LANES), pos),
                  pl.BlockSpec((tm, V7X_LANES), pos)],
        out_specs=[o[1] for o in outs],
        out_shape=[o[0] for o in outs],
        compiler_params=_cparams(("arbitrary",)),
        name="proj",
    )(x2, g, wq, wkv, wg, wu, cos, slo, shi)


def _compress_kernel(k_ref, v_ref, pea_ref, peb_ref, kw1a_ref, kw1b_ref, kw2t_ref, vw1a_ref, vw1b_ref, vw2_ref,
                     kct_ref, vc_ref, ca_ref, cb_ref):
    nch = ca_ref.shape[0]

    def hidden(src_ref, w1a_ref, w1b_ref):
        for t in range(CMP_STRIDE):
            rows = src_ref[pl.ds(t, nch, stride=CMP_STRIDE), :]
            sl = slice(t * KV_LANES, (t + 1) * KV_LANES)
            ca_ref[:, sl] = (rows + pea_ref[:, sl]).astype(BF16)
            cb_ref[:, sl] = (rows + peb_ref[:, sl]).astype(BF16)
        ha = jnp.dot(ca_ref[...], w1a_ref[...], preferred_element_type=F32)
        hb = jnp.dot(cb_ref[...], w1b_ref[...], preferred_element_type=F32)
        return _gelu(ha + pltpu.roll(hb, nch - 1, axis=0)).astype(BF16)

    kct_ref[0] = lax.dot_general(kw2t_ref[...], hidden(k_ref, kw1a_ref, kw1b_ref), _NT,
                                 preferred_element_type=F32).astype(BF16)
    vc_ref[0] = jnp.dot(hidden(v_ref, vw1a_ref, vw1b_ref), vw2_ref[...], preferred_element_type=F32).astype(BF16)


def _compress_call(kc_raw, vc_raw, pea, peb, kw1a, kw1b, kw2, vw1a, vw1b, vw2, batch, seq):
    nch = seq // CMP_STRIDE
    wspec = lambda a: pl.BlockSpec(a.shape, lambda b: (0, 0))
    return pl.pallas_call(
        _compress_kernel,
        grid=(batch,),
        in_specs=[pl.BlockSpec((seq, KV_LANES), lambda b: (b, 0)), pl.BlockSpec((seq, KV_LANES), lambda b: (b, 0)),
                  wspec(pea), wspec(peb), wspec(kw1a), wspec(kw1b), wspec(kw2), wspec(vw1a), wspec(vw1b), wspec(vw2)],
        out_specs=[pl.BlockSpec((1, KV_LANES, nch), lambda b: (b, 0, 0)),
                   pl.BlockSpec((1, nch, KV_LANES), lambda b: (b, 0, 0))],
        out_shape=[jax.ShapeDtypeStruct((batch, KV_LANES, nch), BF16), jax.ShapeDtypeStruct((batch, nch, KV_LANES), BF16)],
        scratch_shapes=[pltpu.VMEM((nch, CMP_STRIDE * KV_LANES), BF16), pltpu.VMEM((nch, CMP_STRIDE * KV_LANES), BF16)],
        compiler_params=_cparams(("arbitrary",)),
        name="compress",
    )(kc_raw, vc_raw, pea, peb, kw1a, kw1b, kw2, vw1a, vw1b, vw2)


def _window_keys(nq):
    return WINDOW + nq


def _nsa_kernel(qraw_ref, qrot_ref, gate_ref, kct_ref, vc_ref, ksat_ref, vs_ref, kwt_ref, vw_ref, ovt_ref,
                o_ref, score_ref, qa_ref, *, n_sel):
    i = pl.program_id(1)
    nq = qraw_ref.shape[1]
    hg = HEADS_PER_GROUP
    rows = hg * nq
    lanes_gq = NSA_KV_HEADS * nq
    nb = score_ref.shape[0]
    nc = kct_ref.shape[2]
    t0 = i * nq
    blk_first = t0 // SEL_BLOCK
    blk_last = blk_first + nq // SEL_BLOCK - 1
    tq_rows = t0 + lax.broadcasted_iota(jnp.int32, (rows, 1), 0) % nq

    own_lanes = [lax.broadcasted_iota(jnp.int32, (nq, KV_LANES), 1) // HEAD_DIM == g for g in range(NSA_KV_HEADS)]

    def group_q(ref, g):
        return jnp.concatenate([jnp.where(own_lanes[g], ref[0, :, h * KV_LANES:(h + 1) * KV_LANES], 0.0).astype(BF16)
                                for h in range(hg)], axis=0)

    def normalized(pv):
        return pv * (1.0 / pltpu.roll(pv, HEAD_DIM, axis=1))

    nwin = _window_keys(nq)
    w0 = pl.multiple_of(jnp.maximum(blk_first - WINDOW // SEL_BLOCK, 0) * SEL_BLOCK, V7X_LANES)
    kp = w0 + lax.broadcasted_iota(jnp.int32, (1, nwin), 1)
    bias_w = jnp.where(kp <= tq_rows, jnp.where(kp > tq_rows - WINDOW, 0.0, NEG_INF), NEG_INF)
    o_win = []
    for g in range(NSA_KV_HEADS):
        s = jnp.dot(group_q(qrot_ref, g), kwt_ref[0, :, pl.ds(w0, nwin)], preferred_element_type=F32) + bias_w
        e = jnp.exp2(s - jnp.max(s, axis=-1, keepdims=True))
        v = vw_ref[0, pl.ds(w0, nwin), g * KV_LANES:(g + 1) * KV_LANES]
        o_win.append(normalized(jnp.dot(e.astype(BF16), v, preferred_element_type=F32)))

    def compressed_branch(width):
        cmp_end = lax.broadcasted_iota(jnp.int32, (1, width), 1) * CMP_STRIDE + (CMP_BLOCK - 1)
        bias_c = jnp.where(cmp_end <= tq_rows, 0.0, NEG_INF)
        any_valid = tq_rows >= CMP_BLOCK - 1
        outs, p_sum = [], []
        for g in range(NSA_KV_HEADS):
            s = jnp.dot(group_q(qraw_ref, g), kct_ref[0, :, :width], preferred_element_type=F32) + bias_c
            e = jnp.exp2(s - jnp.max(s, axis=-1, keepdims=True))
            p = e * jnp.where(any_valid, 1.0 / jnp.sum(e, axis=-1, keepdims=True), 0.0)
            outs.append(jnp.dot(p.astype(BF16), vc_ref[0, :width, :], preferred_element_type=F32))
            ph = p[0:nq]
            for h in range(1, hg):
                ph = ph + p[h * nq:(h + 1) * nq]
            p_sum.append(ph)
        p2 = jnp.concatenate(p_sum, axis=0)
        p_hi = p2.astype(BF16)
        p_lo = (p2 - p_hi.astype(F32)).astype(BF16)
        ovt = ovt_ref[:, :width]
        imp = (lax.dot_general(ovt, p_hi, _NT, preferred_element_type=F32)
               + lax.dot_general(ovt, p_lo, _NT, preferred_element_type=F32))
        return tuple(outs) + (imp,)

    *o_cmp, imp_t = compressed_branch(nc)

    blk_q = (t0 + lax.broadcasted_iota(jnp.int32, (1, lanes_gq), 1) % nq) // SEL_BLOCK
    jfull = lax.broadcasted_iota(jnp.int32, (nb, lanes_gq), 0)
    valid = jfull <= blk_q
    forced = (jfull == 0) | (jfull == blk_q) | (jfull == blk_q - 1)
    score = jnp.where(valid, imp_t + jnp.where(forced, FORCE_BONUS, 0.0), NEG_INF)

    free = jnp.where(valid, jnp.where(forced, NEG_INF, imp_t), NEG_INF)
    n_forced = 1 + jnp.where(blk_q >= 1, 1, 0) + jnp.where(blk_q >= 2, 1, 0)
    rest, cut = free, {}
    for k in range(1, n_sel):
        thr = jnp.max(rest, axis=0, keepdims=True)
        rest = jnp.where(rest >= thr, NEG_INF, rest)
        cut[k] = thr
    thr = cut[n_sel - 1]
    for nf in (2, 3):
        thr = jnp.where(n_forced == nf, cut[n_sel - nf], thr)
    sel_t = jnp.where(valid, jnp.where(forced, 1.0, jnp.where(free >= thr, 1.0, 0.0)), 0.0)

    def ranked_members():
        def count_above_or_tied_earlier(jp, cnt):
            row = jnp.broadcast_to(score_ref[pl.ds(jp, 1), :], score.shape)
            tie = jnp.where(jp < jfull, 1.0, 0.0)
            return cnt + jnp.where(row > score, 1.0, jnp.where(row == score, tie, 0.0))

        score_ref[...] = score
        cnt = lax.fori_loop(0, blk_last + 1, count_above_or_tied_earlier, jnp.zeros(score.shape, F32))
        return jnp.where(valid, jnp.where(cnt < float(n_sel), 1.0, 0.0), 0.0)

    picked = jnp.sum(sel_t, axis=0, keepdims=True)
    wrong = jnp.sum(jnp.where(picked == jnp.minimum(blk_q + 1, n_sel).astype(F32), 0.0, 1.0))
    sel_t = lax.cond(wrong == 0.0, lambda: sel_t, ranked_members)
    bias = ((sel_t.T - 1.0) * MASK_BIAS).astype(BF16)

    n_tiles = blk_last // (SEL_TILE // SEL_BLOCK) + 1
    lane_t = lax.broadcasted_iota(jnp.int32, (1, SEL_TILE), 1)
    for g in range(NSA_KV_HEADS):
        qa_ref[g * rows:(g + 1) * rows, :KV_LANES] = group_q(qrot_ref, g)
        qa_ref[g * rows:(g + 1) * rows, KV_LANES:] = jnp.concatenate([bias[g * nq:(g + 1) * nq]] * hg, axis=0)

    def tile_body(c, carry, causal):
        k0 = pl.multiple_of(c * SEL_TILE, SEL_TILE)
        ka = ksat_ref[0, :, pl.ds(k0, SEL_TILE)]
        new = []
        for g in range(NSA_KV_HEADS):
            m, acc = carry[g]
            s = jnp.dot(qa_ref[g * rows:(g + 1) * rows], ka, preferred_element_type=F32)
            if causal:
                s = jnp.where((k0 + lane_t) <= tq_rows, s, NEG_INF)
            m_new = jnp.maximum(m, jnp.max(s, axis=-1, keepdims=True))
            p = jnp.exp2(s - m_new)
            v = vs_ref[0, pl.ds(k0, SEL_TILE), g * KV_LANES:(g + 1) * KV_LANES]
            acc = jnp.exp2(m - m_new) * acc + jnp.dot(p.astype(BF16), v, preferred_element_type=F32)
            new.append((m_new, acc))
        return tuple(new)

    init = tuple((jnp.full((rows, 1), NEG_INF, F32), jnp.zeros((rows, KV_LANES), F32)) for _ in range(NSA_KV_HEADS))
    carry = lax.fori_loop(0, n_tiles - 1, functools.partial(tile_body, causal=False), init)
    carry = tile_body(n_tiles - 1, carry, causal=True)
    o_sel = [normalized(acc) for _, acc in carry]

    gt = gate_ref[0]
    for h in range(hg):
        r = slice(h * nq, (h + 1) * nq)
        parts = []
        for g in range(NSA_KV_HEADS):
            c = (g * hg + h) * 3
            parts.append(gt[:, c:c + 1] * o_cmp[g][r] + gt[:, c + 1:c + 2] * o_sel[g][r] + gt[:, c + 2:c + 3] * o_win[g][r])
        out = parts[0]
        for g in range(1, NSA_KV_HEADS):
            out = jnp.where(own_lanes[g], parts[g], out)
        o_ref[0, :, h * KV_LANES:(h + 1) * KV_LANES] = out.astype(BF16)


def _nsa_call(qraw, qrot, gates, kct, vc, ksat, vs, kwt, vw, ovt):
    batch, seq, _ = qraw.shape
    nq = NSA_QUERIES
    assert nq % V7X_LANES == 0 and seq % nq == 0 and seq >= _window_keys(nq) and seq % SEL_TILE == 0
    step = lambda n: pl.BlockSpec((1, nq, n), lambda b, i: (b, i, 0))
    whole = lambda a: pl.BlockSpec((1,) + a.shape[1:], lambda b, i: (b, 0, 0))
    return pl.pallas_call(
        functools.partial(_nsa_kernel, n_sel=min(N_SEL, seq // SEL_BLOCK)),
        grid=(batch, seq // nq),
        in_specs=[step(Q_LANES), step(Q_LANES), step(V7X_LANES),
                  whole(kct), whole(vc), whole(ksat), whole(vs), whole(kwt), whole(vw),
                  pl.BlockSpec(ovt.shape, lambda b, i: (0, 0))],
        out_specs=step(HEADS_PER_GROUP * KV_LANES),
        out_shape=jax.ShapeDtypeStruct((batch, seq, HEADS_PER_GROUP * KV_LANES), BF16),
        scratch_shapes=[pltpu.VMEM((V7X_LANES, NSA_KV_HEADS * nq), F32),
                        pltpu.VMEM((NSA_KV_HEADS * HEADS_PER_GROUP * nq, KV_LANES + V7X_LANES), BF16)],
        compiler_params=_cparams(("arbitrary", "arbitrary")),
        name="nsa",
    )(qraw, qrot, gates, kct, vc, ksat, vs, kwt, vw, ovt)


def _s5_kernel(u_ref, lag_ref, p_ref, q_ref, lr_ref, li_ref, d_ref, o_ref, m_ref, x_ref, y_ref, z_ref, sp_ref):
    nch = x_ref.shape[0]
    half = z_ref.shape[1] // 2
    lanes = V7X_LANES

    @pl.when(pl.program_id(1) == 0)
    def _():
        for s in range(SSM_CHUNK):
            for t in range(SSM_CHUNK):
                tile = lag_ref[0, t - s] if t >= s else jnp.zeros((lanes, lanes), BF16)
                m_ref[s * lanes:(s + 1) * lanes, t * lanes:(t + 1) * lanes] = tile

    for t in range(SSM_CHUNK):
        x_ref[:, t * lanes:(t + 1) * lanes] = u_ref[pl.ds(t, nch, stride=SSM_CHUNK), :].astype(BF16)
    for c0 in range(0, SSM_CHUNK * lanes, V7X_MXU):
        c1 = c0 + V7X_MXU
        y_ref[:, c0:c1] = jnp.dot(x_ref[:, :c1], m_ref[:c1, c0:c1], preferred_element_type=F32)
    z_ref[...] = jnp.dot(x_ref[...], p_ref[0], preferred_element_type=F32)
    lr, li = lr_ref[0], li_ref[0]

    def scan_body(k, carry):
        sr, si = carry
        sp_ref[pl.ds(k, 1), 0:half] = sr
        sp_ref[pl.ds(k, 1), half:2 * half] = si
        zr = z_ref[pl.ds(k, 1), 0:half]
        zi = z_ref[pl.ds(k, 1), half:2 * half]
        return lr * sr - li * si + zr, lr * si + li * sr + zi

    zero = jnp.zeros((1, half), F32)
    lax.fori_loop(0, nch, scan_body, (zero, zero))
    y = y_ref[...] + jnp.dot(sp_ref[...].astype(BF16), q_ref[0], preferred_element_type=F32)
    d = d_ref[...]
    for t in range(SSM_CHUNK):
        yt = y[:, t * lanes:(t + 1) * lanes] + d * u_ref[pl.ds(t, nch, stride=SSM_CHUNK), :]
        o_ref[pl.ds(t, nch, stride=SSM_CHUNK), :] = _gelu(yt)


def _s5_call(u, lag, p, q, lr, li, dskip, batch, seq):
    nsg = lag.shape[0]
    nch = seq // SSM_CHUNK
    nstate = q.shape[1]
    lw = SSM_CHUNK * V7X_LANES
    slab = lambda a: pl.BlockSpec((1,) + a.shape[1:], lambda g, b: (g,) + (0,) * (a.ndim - 1),
                                  pipeline_mode=pl.Buffered(1))
    return pl.pallas_call(
        _s5_kernel,
        grid=(nsg, batch),
        in_specs=[pl.BlockSpec((seq, V7X_LANES), lambda g, b: (b, g)), slab(lag), slab(p), slab(q),
                  pl.BlockSpec((1, 1, nstate // 2), lambda g, b: (g, 0, 0)),
                  pl.BlockSpec((1, 1, nstate // 2), lambda g, b: (g, 0, 0)),
                  pl.BlockSpec((1, V7X_LANES), lambda g, b: (0, g))],
        out_specs=pl.BlockSpec((seq, V7X_LANES), lambda g, b: (b, g)),
        out_shape=jax.ShapeDtypeStruct(u.shape, F32),
        scratch_shapes=[pltpu.VMEM((lw, lw), BF16), pltpu.VMEM((nch, lw), BF16), pltpu.VMEM((nch, lw), F32),
                        pltpu.VMEM((nch, nstate), F32), pltpu.VMEM((nch, nstate), F32)],
        compiler_params=_cparams(("arbitrary", "arbitrary")),
        name="s5",
    )(u, lag, p, q, lr, li, dskip)


def _merge_kernel(x_ref, g_ref, attn_ref, ssm_ref, wga_ref, wgb_ref, wattn_ref, wval_ref, wgate_ref, wout_ref, o_ref):
    x = x_ref[...]
    hb = _rmsnorm(x, g_ref[...]).astype(BF16)
    dot = functools.partial(jnp.dot, preferred_element_type=F32)
    y_a = dot(attn_ref[...], wattn_ref[...])
    ys = ssm_ref[...].astype(BF16)
    y_b = dot(ys, wval_ref[...]) * jax.nn.sigmoid(dot(ys, wgate_ref[...]))
    merged = jax.nn.sigmoid(dot(hb, wga_ref[...])) * y_a + jax.nn.sigmoid(dot(hb, wgb_ref[...])) * y_b
    o_ref[...] = x + dot(merged.astype(BF16), wout_ref[...])


def _merge_call(x2, g, attn, ssm, wga, wgb, wattn, wval, wgate, wout):
    t, d = x2.shape
    tm = ROW_TILE
    row = lambda i: (i, 0)
    wspec = lambda a: pl.BlockSpec(a.shape, lambda i: (0, 0))
    return pl.pallas_call(
        _merge_kernel,
        grid=(t // tm,),
        in_specs=[pl.BlockSpec((tm, d), row), wspec(g), pl.BlockSpec((tm, attn.shape[1]), row),
                  pl.BlockSpec((tm, ssm.shape[1]), row), wspec(wga), wspec(wgb), wspec(wattn), wspec(wval),
                  wspec(wgate), wspec(wout)],
        out_specs=pl.BlockSpec((tm, d), row),
        out_shape=jax.ShapeDtypeStruct((t, d), F32),
        compiler_params=_cparams(("arbitrary",)),
        name="merge",
    )(x2, g, attn, ssm, wga, wgb, wattn, wval, wgate, wout)


def _mlp_kernel(x_ref, g_ref, wup_ref, wdown_ref, gf_ref, o_ref):
    x = x_ref[...]
    hb = _rmsnorm(x, g_ref[...]).astype(BF16)
    acc = x
    for c in range(wup_ref.shape[1] // FF_CHUNK):
        sl = slice(c * FF_CHUNK, (c + 1) * FF_CHUNK)
        up = jnp.maximum(jnp.dot(hb, wup_ref[:, sl], preferred_element_type=F32), 0.0)
        acc = acc + jnp.dot((up * up).astype(BF16), wdown_ref[sl, :], preferred_element_type=F32)
    o_ref[...] = _rmsnorm(acc, gf_ref[...])


def _mlp_call(x1, g, wup, wdown, gf):
    t, d = x1.shape
    tm = ROW_TILE
    row = lambda i: (i, 0)
    wspec = lambda a: pl.BlockSpec(a.shape, lambda i: (0, 0))
    return pl.pallas_call(
        _mlp_kernel,
        grid=(t // tm,),
        in_specs=[pl.BlockSpec((tm, d), row), wspec(g), wspec(wup), wspec(wdown), wspec(gf)],
        out_specs=pl.BlockSpec((tm, d), row),
        out_shape=jax.ShapeDtypeStruct((t, d), F32),
        compiler_params=_cparams(("arbitrary",)),
        name="mlp",
    )(x1, g, wup, wdown, gf)


def _interleave_heads(wq):
    d = wq.shape[0]
    return wq.reshape(d, NSA_KV_HEADS, HEADS_PER_GROUP, HEAD_DIM).transpose(0, 2, 1, 3).reshape(d, Q_LANES)


def _rope_tables(seq):
    half = ROPE_DIM // 2
    inv = ROPE_THETA ** (-(jnp.arange(half, dtype=F32) * 2.0) / ROPE_DIM)
    ang = jnp.arange(seq, dtype=F32)[:, None] * inv[None, :]
    cos, sin = jnp.cos(ang), jnp.sin(ang)
    rest = HEAD_DIM - ROPE_DIM
    cos_h = jnp.concatenate([cos, cos, jnp.ones((seq, rest), F32)], axis=1)
    slo_h = jnp.concatenate([-sin, jnp.zeros((seq, half + rest), F32)], axis=1)
    shi_h = jnp.concatenate([jnp.zeros((seq, half), F32), sin, jnp.zeros((seq, rest), F32)], axis=1)
    reps = V7X_LANES // HEAD_DIM
    return jnp.tile(cos_h, (1, reps)), jnp.tile(slo_h, (1, reps)), jnp.tile(shi_h, (1, reps))


def _compress_weights(pe, w1, w2):
    eye = jnp.eye(NSA_KV_HEADS, dtype=F32)
    w1e = jnp.einsum('tdj,gk->tgdkj', w1.reshape(CMP_BLOCK, HEAD_DIM, CMP_HIDDEN), eye)
    w1e = w1e.reshape(CMP_BLOCK * KV_LANES, NSA_KV_HEADS * CMP_HIDDEN).astype(BF16)
    w2e = jnp.einsum('jd,gk->gjkd', w2, eye).reshape(NSA_KV_HEADS * CMP_HIDDEN, KV_LANES).astype(BF16)
    pee = jnp.tile(pe, (1, NSA_KV_HEADS)).reshape(1, CMP_BLOCK * KV_LANES)
    halfw = CMP_STRIDE * KV_LANES
    return pee[:, :halfw], pee[:, halfw:], w1e[:halfw], w1e[halfw:], w2e


def _selection_constants(seq):
    nc = seq // CMP_STRIDE - 1
    nb = seq // SEL_BLOCK
    n_np = np.arange(nc)[:, None] * CMP_STRIDE
    j_np = np.arange(nb)[None, :] * SEL_BLOCK
    overlap = ((n_np < j_np + SEL_BLOCK) & (n_np + CMP_BLOCK > j_np)).astype(np.float32)
    ovt = np.zeros((V7X_LANES, nc + 1), np.float32)
    ovt[:nb, :nc] = overlap.T
    return jnp.asarray(ovt, BF16)


def _s5_matrices(lam_re, lam_im, log_step, b_re, b_im, c_re, c_im):
    hp = lax.Precision.HIGHEST
    ng, ns = lam_re.shape
    gc = b_re.shape[-1]
    L = SSM_CHUNK
    step = jnp.exp(log_step)[:, None]
    a, b = lam_re * step, lam_im * step
    k = jnp.arange(L + 1, dtype=F32)[:, None, None]
    mag = jnp.exp(a[None] * k)
    pr, pi = mag * jnp.cos(b[None] * k), mag * jnp.sin(b[None] * k)
    nr, ni = pr[1] - 1.0, pi[1]
    den = lam_re * lam_re + lam_im * lam_im
    cr, ci = (nr * lam_re + ni * lam_im) / den, (ni * lam_re - nr * lam_im) / den
    bbr = cr[..., None] * b_re - ci[..., None] * b_im
    bbi = cr[..., None] * b_im + ci[..., None] * b_re
    cpr = c_re[None] * pr[:, :, None, :] - c_im[None] * pi[:, :, None, :]
    cpi = c_re[None] * pi[:, :, None, :] + c_im[None] * pr[:, :, None, :]
    kk = (jnp.einsum('kgcn,gnd->kgcd', cpr[:L], bbr, precision=hp)
          - jnp.einsum('kgcn,gnd->kgcd', cpi[:L], bbi, precision=hp))
    pw_r, pw_i = pr[L - 1 - np.arange(L)], pi[L - 1 - np.arange(L)]
    p_r = pw_r[:, :, :, None] * bbr[None] - pw_i[:, :, :, None] * bbi[None]
    p_i = pw_r[:, :, :, None] * bbi[None] + pw_i[:, :, :, None] * bbr[None]
    p_r, p_i = p_r.transpose(1, 0, 3, 2), p_i.transpose(1, 0, 3, 2)
    q_r = cpr[1:].transpose(1, 3, 0, 2)
    q_i = -cpi[1:].transpose(1, 3, 0, 2)
    sup = SSM_SUPER
    nsg = ng // sup
    eye = jnp.eye(sup, dtype=BF16)
    lw = L * sup * gc
    bf = lambda a: a.astype(BF16)
    lag_sg = jnp.einsum('kxacd,ab->xkadbc', bf(kk).reshape(L, nsg, sup, gc, gc), eye).reshape(nsg, L, sup * gc, sup * gc)
    slab_rows = lambda a: bf(a).reshape(nsg, sup, L, gc, -1).transpose(0, 2, 1, 3, 4).reshape(nsg, lw, -1)
    p_c = jnp.concatenate([slab_rows(p_r), slab_rows(p_i)], axis=2)
    rep_p = np.kron(np.eye(2), np.tile(np.eye(ns), (1, sup)))
    own_p = (np.arange(lw)[:, None] // gc) % sup == (np.arange(2 * sup * ns)[None, :] // ns) % sup
    p = jnp.where(own_p, jnp.matmul(p_c, jnp.asarray(rep_p, BF16)), 0)
    q_c = jnp.concatenate([bf(q_r).reshape(nsg, sup * ns, L * gc), bf(q_i).reshape(nsg, sup * ns, L * gc)], axis=1)
    rep_q = np.einsum('ts,cd,b->tcsbd', np.eye(L), np.eye(gc), np.ones(sup)).reshape(L * gc, lw)
    own_q = (np.arange(2 * sup * ns)[:, None] // ns) % sup == (np.arange(lw)[None, :] // gc) % sup
    q = jnp.where(own_q, jnp.matmul(q_c, jnp.asarray(rep_q, BF16)), 0)
    return lag_sg, p, q, pr[L].reshape(nsg, 1, sup * ns), pi[L].reshape(nsg, 1, sup * ns)


def kernel(x, norm_mix_g, w_in, cmp_pe, cmp_k_w1, cmp_k_w2, cmp_v_w1, cmp_v_w2, ssm_lam_re, ssm_lam_im, ssm_log_step, ssm_b_re, ssm_b_im, ssm_c_re, ssm_c_im, ssm_d, w_attn_branch, w_ssm_val, w_ssm_gate, w_out, norm_mlp_g, w_up, w_down, norm_final_g):
    batch, seq, d = x.shape
    depth = w_in.shape[0]
    assert depth == 1, "the final rmsnorm is fused into the single layer's mlp kernel"
    nsa_w = NSA_HEADS * HEAD_DIM
    ssm_w = ssm_d.shape[1]
    o_q, o_kv, o_g = nsa_w, nsa_w + 6 * KV_LANES, nsa_w + 6 * KV_LANES + 3 * NSA_HEADS
    o_u = o_g + ssm_w
    cos, slo, shi = _rope_tables(seq)
    ovt = _selection_constants(seq)
    head_order = np.array([g * HEADS_PER_GROUP + h for h in range(HEADS_PER_GROUP) for g in range(NSA_KV_HEADS)])
    x2 = x.reshape(batch * seq, d)
    for l in range(depth):
        wl = w_in[l]
        wq = _interleave_heads(wl[:, :o_q]).astype(BF16)
        wkv = wl[:, o_q:o_kv].astype(BF16)
        wg = jnp.pad(wl[:, o_kv:o_g], ((0, 0), (0, V7X_LANES - 3 * NSA_HEADS))).astype(BF16)
        wu = wl[:, o_g:o_u].astype(BF16)
        wga = wl[:, o_u:o_u + d].astype(BF16)
        wgb = wl[:, o_u + d:].astype(BF16)
        g_mix = norm_mix_g[l].reshape(1, d)
        qraw, qrot, kc_raw, vc_raw, ksat, vs, kwt, vw, gates, u = _proj_call(
            x2, g_mix, wq, wkv, wg, wu, cos, slo, shi, batch, seq)

        pea, peb, kw1a, kw1b, kw2 = _compress_weights(cmp_pe[l], cmp_k_w1[l], cmp_k_w2[l])
        _, _, vw1a, vw1b, vw2 = _compress_weights(cmp_pe[l], cmp_v_w1[l], cmp_v_w2[l])
        kct, vc = _compress_call(kc_raw, vc_raw, pea, peb, kw1a, kw1b, kw2.T, vw1a, vw1b, vw2, batch, seq)

        b3 = lambda a: a.reshape(batch, seq, a.shape[-1])
        attn = _nsa_call(b3(qraw), b3(qrot), b3(gates), kct, vc, ksat, b3(vs), kwt, b3(vw), ovt)
        attn = attn.reshape(batch * seq, nsa_w)

        lag, pm, qm, lr, li = _s5_matrices(ssm_lam_re[l], ssm_lam_im[l], ssm_log_step[l], ssm_b_re[l], ssm_b_im[l],
                                           ssm_c_re[l], ssm_c_im[l])
        y_ssm = _s5_call(u, lag, pm, qm, lr, li, ssm_d[l].reshape(1, ssm_w), batch, seq)

        wattn = w_attn_branch[l].reshape(NSA_HEADS, HEAD_DIM, d)[head_order].reshape(nsa_w, d).astype(BF16)
        x1 = _merge_call(x2, g_mix, attn, y_ssm, wga, wgb, wattn, w_ssm_val[l].astype(BF16),
                         w_ssm_gate[l].astype(BF16), w_out[l].astype(BF16))
        x2 = _mlp_call(x1, norm_mlp_g[l].reshape(1, d), w_up[l].astype(BF16), w_down[l].astype(BF16),
                       norm_final_g.reshape(1, d))
    return x2.reshape(batch, seq, d)
```

```python
import functools
import math

import jax
import jax.numpy as jnp
import numpy as np
from jax import lax
from jax.experimental import pallas as pl
from jax.experimental.pallas import tpu as pltpu

NSA_HEADS = 8
NSA_KV_HEADS = 2
HEAD_DIM = 64
CMP_BLOCK = 32
CMP_STRIDE = 16
CMP_HIDDEN = 256
SEL_BLOCK = 64
N_SEL = 16
WINDOW = 512
FORCE_BONUS = 1e3
NEG_INF = -1e30
ROPE_THETA = 500000.0
ROPE_DIM = HEAD_DIM // 4
SSM_GROUP = 16
SSM_STATE = 64
EPS = 1e-6

HEADS_PER_GROUP = NSA_HEADS // NSA_KV_HEADS
KV_LANES = NSA_KV_HEADS * HEAD_DIM
Q_LANES = NSA_HEADS * HEAD_DIM

V7X_LANES = 128
V7X_MXU = 256
V7X_VMEM_BYTES = 64 * 1024 * 1024
VMEM_LIMIT = V7X_VMEM_BYTES - 8 * 1024 * 1024

ROW_TILE = 1024
SEL_TILE = 1024
NSA_QUERIES = 256
SSM_CHUNK = 16
SSM_SUPER = V7X_LANES // SSM_GROUP
FF_CHUNK = 1024

LOG2E = math.log2(math.e)
MASK_BIAS = 1e30

BF16 = jnp.bfloat16
F32 = jnp.float32
_NT = (((1,), (1,)), ((), ()))


def _cparams(semantics):
    return pltpu.CompilerParams(dimension_semantics=semantics, vmem_limit_bytes=VMEM_LIMIT)


def _rmsnorm(x, g):
    return x * lax.rsqrt(jnp.mean(x * x, axis=-1, keepdims=True) + EPS) * g


def _gelu(x):
    return jax.nn.gelu(x)


def _rope_cols(x, cos, sin_lo, sin_hi):
    cols = []
    for c in range(x.shape[1] // V7X_LANES):
        xc = x[:, c * V7X_LANES:(c + 1) * V7X_LANES]
        up = pltpu.roll(xc, V7X_LANES - ROPE_DIM // 2, axis=1)
        dn = pltpu.roll(xc, ROPE_DIM // 2, axis=1)
        cols.append(xc * cos + up * sin_lo + dn * sin_hi)
    return jnp.concatenate(cols, axis=1) if len(cols) > 1 else cols[0]


def _values_with_ones(v):
    lane_head = lax.broadcasted_iota(jnp.int32, v.shape, 1) // HEAD_DIM
    return jnp.concatenate([jnp.where(lane_head == g, v, 1.0) for g in range(NSA_KV_HEADS)], axis=1).astype(BF16)


def _proj_kernel(x_ref, g_ref, wq_ref, wkv_ref, wg_ref, wu_ref, cos_ref, slo_ref, shi_ref,
                 qraw_ref, qrot_ref, kc_ref, vc_ref, ksat_ref, vs_ref, kwt_ref, vw_ref, gate_ref, u_ref, *, seq_tiles):
    hb = _rmsnorm(x_ref[...], g_ref[...]).astype(BF16)
    cos, slo, shi = cos_ref[...], slo_ref[...], shi_ref[...]
    q = jnp.dot(hb, wq_ref[...], preferred_element_type=F32) * (HEAD_DIM ** -0.5 * LOG2E)
    qraw_ref[...] = q.astype(BF16)
    qrot_ref[...] = _rope_cols(q, cos, slo, shi).astype(BF16)
    kv = jnp.dot(hb, wkv_ref[...], preferred_element_type=F32)
    w = KV_LANES
    kc_ref[...] = kv[:, 0 * w:1 * w]
    vc_ref[...] = kv[:, 1 * w:2 * w]
    tm = x_ref.shape[0]
    pos = (pl.program_id(0) % seq_tiles) * tm + lax.broadcasted_iota(jnp.int32, (V7X_LANES, tm), 1)
    onehot = jnp.where(lax.broadcasted_iota(jnp.int32, (V7X_LANES, tm), 0) == pos // SEL_BLOCK, 1.0, 0.0)
    ksat_ref[0, :w, :] = _rope_cols(kv[:, 2 * w:3 * w], cos, slo, shi).T.astype(BF16)
    ksat_ref[0, w:, :] = onehot.astype(BF16)
    vs_ref[...] = _values_with_ones(kv[:, 3 * w:4 * w])
    kwt_ref[0] = _rope_cols(kv[:, 4 * w:5 * w], cos, slo, shi).T.astype(BF16)
    vw_ref[...] = _values_with_ones(kv[:, 5 * w:6 * w])
    gate_ref[...] = jax.nn.sigmoid(jnp.dot(hb, wg_ref[...], preferred_element_type=F32))
    u_ref[...] = jnp.dot(hb, wu_ref[...], preferred_element_type=F32)


def _proj_call(x2, g, wq, wkv, wg, wu, cos, slo, shi, batch, seq):
    t, d = x2.shape
    tm = ROW_TILE
    s_tiles = seq // tm
    row = lambda i: (i, 0)
    pos = lambda i: (i % s_tiles, 0)
    trans = lambda i: (i // s_tiles, 0, i % s_tiles)
    wspec = lambda a: pl.BlockSpec(a.shape, lambda i: (0, 0), pipeline_mode=pl.Buffered(1))
    rows_out = lambda n, dt: (jax.ShapeDtypeStruct((t, n), dt), pl.BlockSpec((tm, n), row))
    trans_out = lambda n: (jax.ShapeDtypeStruct((batch, n, seq), BF16), pl.BlockSpec((1, n, tm), trans))
    outs = [rows_out(Q_LANES, BF16), rows_out(Q_LANES, BF16), rows_out(KV_LANES, F32), rows_out(KV_LANES, F32),
            trans_out(KV_LANES + V7X_LANES), rows_out(NSA_KV_HEADS * KV_LANES, BF16),
            trans_out(KV_LANES), rows_out(NSA_KV_HEADS * KV_LANES, BF16),
            rows_out(V7X_LANES, F32), rows_out(wu.shape[1], F32)]
    assert seq // SEL_BLOCK <= V7X_LANES, "the selection-block one-hot must fit one lane tile"
    return pl.pallas_call(
        functools.partial(_proj_kernel, seq_tiles=s_tiles),
        grid=(t // tm,),
        in_specs=[pl.BlockSpec((tm, d), row), wspec(g), wspec(wq), wspec(wkv), wspec(wg), wspec(wu),
                  pl.BlockSpec((tm, V7X_LANES), pos), pl.BlockSpec((tm, V7X_LANES), pos),
                  pl.BlockSpec((tm, V7X_LANES), pos)],
        out_specs=[o[1] for o in outs],
        out_shape=[o[0] for o in outs],
        compiler_params=_cparams(("arbitrary",)),
        name="proj",
    )(x2, g, wq, wkv, wg, wu, cos, slo, shi)


def _compress_kernel(k_ref, v_ref, pea_ref, peb_ref, kw1a_ref, kw1b_ref, kw2t_ref, vw1a_ref, vw1b_ref, vw2_ref,
                     kct_ref, vc_ref, ca_ref, cb_ref):
    nch = ca_ref.shape[0]

    def hidden(src_ref, w1a_ref, w1b_ref):
        for t in range(CMP_STRIDE):
            rows = src_ref[pl.ds(t, nch, stride=CMP_STRIDE), :]
            sl = slice(t * KV_LANES, (t + 1) * KV_LANES)
            ca_ref[:, sl] = (rows + pea_ref[:, sl]).astype(BF16)
            cb_ref[:, sl] = (rows + peb_ref[:, sl]).astype(BF16)
        ha = jnp.dot(ca_ref[...], w1a_ref[...], preferred_element_type=F32)
        hb = jnp.dot(cb_ref[...], w1b_ref[...], preferred_element_type=F32)
        return _gelu(ha + pltpu.roll(hb, nch - 1, axis=0)).astype(BF16)

    kct_ref[0] = lax.dot_general(kw2t_ref[...], hidden(k_ref, kw1a_ref, kw1b_ref), _NT,
                                 preferred_element_type=F32).astype(BF16)
    vc_ref[0] = jnp.dot(hidden(v_ref, vw1a_ref, vw1b_ref), vw2_ref[...], preferred_element_type=F32).astype(BF16)


def _compress_call(kc_raw, vc_raw, pea, peb, kw1a, kw1b, kw2, vw1a, vw1b, vw2, batch, seq):
    nch = seq // CMP_STRIDE
    wspec = lambda a: pl.BlockSpec(a.shape, lambda b: (0, 0))
    return pl.pallas_call(
        _compress_kernel,
        grid=(batch,),
        in_specs=[pl.BlockSpec((seq, KV_LANES), lambda b: (b, 0)), pl.BlockSpec((seq, KV_LANES), lambda b: (b, 0)),
                  wspec(pea), wspec(peb), wspec(kw1a), wspec(kw1b), wspec(kw2), wspec(vw1a), wspec(vw1b), wspec(vw2)],
        out_specs=[pl.BlockSpec((1, KV_LANES, nch), lambda b: (b, 0, 0)),
                   pl.BlockSpec((1, nch, KV_LANES), lambda b: (b, 0, 0))],
        out_shape=[jax.ShapeDtypeStruct((batch, KV_LANES, nch), BF16), jax.ShapeDtypeStruct((batch, nch, KV_LANES), BF16)],
        scratch_shapes=[pltpu.VMEM((nch, CMP_STRIDE * KV_LANES), BF16), pltpu.VMEM((nch, CMP_STRIDE * KV_LANES), BF16)],
        compiler_params=_cparams(("arbitrary",)),
        name="compress",
    )(kc_raw, vc_raw, pea, peb, kw1a, kw1b, kw2, vw1a, vw1b, vw2)


def _window_keys(nq):
    return WINDOW + nq


def _nsa_kernel(qraw_ref, qrot_ref, gate_ref, kct_ref, vc_ref, ksat_ref, vs_ref, kwt_ref, vw_ref, ovt_ref,
                o_ref, score_ref, qa_ref, *, n_sel):
    i = pl.program_id(1)
    nq = qraw_ref.shape[1]
    hg = HEADS_PER_GROUP
    rows = hg * nq
    lanes_gq = NSA_KV_HEADS * nq
    nb = score_ref.shape[0]
    nc = kct_ref.shape[2]
    t0 = i * nq
    blk_first = t0 // SEL_BLOCK
    blk_last = blk_first + nq // SEL_BLOCK - 1
    tq_rows = t0 + lax.broadcasted_iota(jnp.int32, (rows, 1), 0) % nq

    own_lanes = [lax.broadcasted_iota(jnp.int32, (nq, KV_LANES), 1) // HEAD_DIM == g for g in range(NSA_KV_HEADS)]

    def group_q(ref, g):
        return jnp.concatenate([jnp.where(own_lanes[g], ref[0, :, h * KV_LANES:(h + 1) * KV_LANES], 0.0).astype(BF16)
                                for h in range(hg)], axis=0)

    def normalized(pv):
        return pv * (1.0 / pltpu.roll(pv, HEAD_DIM, axis=1))

    nwin = _window_keys(nq)
    w0 = pl.multiple_of(jnp.maximum(blk_first - WINDOW // SEL_BLOCK, 0) * SEL_BLOCK, V7X_LANES)
    kp = w0 + lax.broadcasted_iota(jnp.int32, (1, nwin), 1)
    bias_w = jnp.where(kp <= tq_rows, jnp.where(kp > tq_rows - WINDOW, 0.0, NEG_INF), NEG_INF)
    o_win = []
    for g in range(NSA_KV_HEADS):
        s = jnp.dot(group_q(qrot_ref, g), kwt_ref[0, :, pl.ds(w0, nwin)], preferred_element_type=F32) + bias_w
        e = jnp.exp2(s - jnp.max(s, axis=-1, keepdims=True))
        v = vw_ref[0, pl.ds(w0, nwin), g * KV_LANES:(g + 1) * KV_LANES]
        o_win.append(normalized(jnp.dot(e.astype(BF16), v, preferred_element_type=F32)))

    def compressed_branch(width):
        cmp_end = lax.broadcasted_iota(jnp.int32, (1, width), 1) * CMP_STRIDE + (CMP_BLOCK - 1)
        bias_c = jnp.where(cmp_end <= tq_rows, 0.0, NEG_INF)
        any_valid = tq_rows >= CMP_BLOCK - 1
        outs, p_sum = [], []
        for g in range(NSA_KV_HEADS):
            s = jnp.dot(group_q(qraw_ref, g), kct_ref[0, :, :width], preferred_element_type=F32) + bias_c
            e = jnp.exp2(s - jnp.max(s, axis=-1, keepdims=True))
            p = e * jnp.where(any_valid, 1.0 / jnp.sum(e, axis=-1, keepdims=True), 0.0)
            outs.append(jnp.dot(p.astype(BF16), vc_ref[0, :width, :], preferred_element_type=F32))
            ph = p[0:nq]
            for h in range(1, hg):
                ph = ph + p[h * nq:(h + 1) * nq]
            p_sum.append(ph)
        p2 = jnp.concatenate(p_sum, axis=0)
        p_hi = p2.astype(BF16)
        p_lo = (p2 - p_hi.astype(F32)).astype(BF16)
        ovt = ovt_ref[:, :width]
        imp = (lax.dot_general(ovt, p_hi, _NT, preferred_element_type=F32)
               + lax.dot_general(ovt, p_lo, _NT, preferred_element_type=F32))
        return tuple(outs) + (imp,)

    *o_cmp, imp_t = compressed_branch(nc)

    blk_q = (t0 + lax.broadcasted_iota(jnp.int32, (1, lanes_gq), 1) % nq) // SEL_BLOCK
    jfull = lax.broadcasted_iota(jnp.int32, (nb, lanes_gq), 0)
    valid = jfull <= blk_q
    forced = (jfull == 0) | (jfull == blk_q) | (jfull == blk_q - 1)
    score = jnp.where(valid, imp_t + jnp.where(forced, FORCE_BONUS, 0.0), NEG_INF)

    free = jnp.where(valid, jnp.where(forced, NEG_INF, imp_t), NEG_INF)
    n_forced = 1 + jnp.where(blk_q >= 1, 1, 0) + jnp.where(blk_q >= 2, 1, 0)
    rest, cut = free, {}
    for k in range(1, n_sel):
        thr = jnp.max(rest, axis=0, keepdims=True)
        rest = jnp.where(rest >= thr, NEG_INF, rest)
        cut[k] = thr
    thr = cut[n_sel - 1]
    for nf in (2, 3):
        thr = jnp.where(n_forced == nf, cut[n_sel - nf], thr)
    sel_t = jnp.where(valid, jnp.where(forced, 1.0, jnp.where(free >= thr, 1.0, 0.0)), 0.0)

    def ranked_members():
        def count_above_or_tied_earlier(jp, cnt):
            row = jnp.broadcast_to(score_ref[pl.ds(jp, 1), :], score.shape)
            tie = jnp.where(jp < jfull, 1.0, 0.0)
            return cnt + jnp.where(row > score, 1.0, jnp.where(row == score, tie, 0.0))

        score_ref[...] = score
        cnt = lax.fori_loop(0, blk_last + 1, count_above_or_tied_earlier, jnp.zeros(score.shape, F32))
        return jnp.where(valid, jnp.where(cnt < float(n_sel), 1.0, 0.0), 0.0)

    picked = jnp.sum(sel_t, axis=0, keepdims=True)
    wrong = jnp.sum(jnp.where(picked == jnp.minimum(blk_q + 1, n_sel).astype(F32), 0.0, 1.0))
    sel_t = lax.cond(wrong == 0.0, lambda: sel_t, ranked_members)
    bias = ((sel_t.T - 1.0) * MASK_BIAS).astype(BF16)

    n_tiles = blk_last // (SEL_TILE // SEL_BLOCK) + 1
    lane_t = lax.broadcasted_iota(jnp.int32, (1, SEL_TILE), 1)
    for g in range(NSA_KV_HEADS):
        qa_ref[g * rows:(g + 1) * rows, :KV_LANES] = group_q(qrot_ref, g)
        qa_ref[g * rows:(g + 1) * rows, KV_LANES:] = jnp.concatenate([bias[g * nq:(g + 1) * nq]] * hg, axis=0)

    def tile_body(c, carry, causal):
        k0 = pl.multiple_of(c * SEL_TILE, SEL_TILE)
        ka = ksat_ref[0, :, pl.ds(k0, SEL_TILE)]
        new = []
        for g in range(NSA_KV_HEADS):
            m, acc = carry[g]
            s = jnp.dot(qa_ref[g * rows:(g + 1) * rows], ka, preferred_element_type=F32)
            if causal:
                s = jnp.where((k0 + lane_t) <= tq_rows, s, NEG_INF)
            m_new = jnp.maximum(m, jnp.max(s, axis=-1, keepdims=True))
            p = jnp.exp2(s - m_new)
            v = vs_ref[0, pl.ds(k0, SEL_TILE), g * KV_LANES:(g + 1) * KV_LANES]
            acc = jnp.exp2(m - m_new) * acc + jnp.dot(p.astype(BF16), v, preferred_element_type=F32)
            new.append((m_new, acc))
        return tuple(new)

    init = tuple((jnp.full((rows, 1), NEG_INF, F32), jnp.zeros((rows, KV_LANES), F32)) for _ in range(NSA_KV_HEADS))
    carry = lax.fori_loop(0, n_tiles - 1, functools.partial(tile_body, causal=False), init)
    carry = tile_body(n_tiles - 1, carry, causal=True)
    o_sel = [normalized(acc) for _, acc in carry]

    gt = gate_ref[0]
    for h in range(hg):
        r = slice(h * nq, (h + 1) * nq)
        parts = []
        for g in range(NSA_KV_HEADS):
            c = (g * hg + h) * 3
            parts.append(gt[:, c:c + 1] * o_cmp[g][r] + gt[:, c + 1:c + 2] * o_sel[g][r] + gt[:, c + 2:c + 3] * o_win[g][r])
        out = parts[0]
        for g in range(1, NSA_KV_HEADS):
            out = jnp.where(own_lanes[g], parts[g], out)
        o_ref[0, :, h * KV_LANES:(h + 1) * KV_LANES] = out.astype(BF16)


def _nsa_call(qraw, qrot, gates, kct, vc, ksat, vs, kwt, vw, ovt):
    batch, seq, _ = qraw.shape
    nq = NSA_QUERIES
    assert nq % V7X_LANES == 0 and seq % nq == 0 and seq >= _window_keys(nq) and seq % SEL_TILE == 0
    step = lambda n: pl.BlockSpec((1, nq, n), lambda b, i: (b, i, 0))
    whole = lambda a: pl.BlockSpec((1,) + a.shape[1:], lambda b, i: (b, 0, 0))
    return pl.pallas_call(
        functools.partial(_nsa_kernel, n_sel=min(N_SEL, seq // SEL_BLOCK)),
        grid=(batch, seq // nq),
        in_specs=[step(Q_LANES), step(Q_LANES), step(V7X_LANES),
                  whole(kct), whole(vc), whole(ksat), whole(vs), whole(kwt), whole(vw),
                  pl.BlockSpec(ovt.shape, lambda b, i: (0, 0))],
        out_specs=step(HEADS_PER_GROUP * KV_LANES),
        out_shape=jax.ShapeDtypeStruct((batch, seq, HEADS_PER_GROUP * KV_LANES), BF16),
        scratch_shapes=[pltpu.VMEM((V7X_LANES, NSA_KV_HEADS * nq), F32),
                        pltpu.VMEM((NSA_KV_HEADS * HEADS_PER_GROUP * nq, KV_LANES + V7X_LANES), BF16)],
        compiler_params=_cparams(("arbitrary", "arbitrary")),
        name="nsa",
    )(qraw, qrot, gates, kct, vc, ksat, vs, kwt, vw, ovt)


def _s5_kernel(u_ref, lag_ref, p_ref, q_ref, lr_ref, li_ref, d_ref, o_ref, m_ref, x_ref, y_ref, z_ref, sp_ref):
    nch = x_ref.shape[0]
    half = z_ref.shape[1] // 2
    lanes = V7X_LANES

    @pl.when(pl.program_id(1) == 0)
    def _():
        for s in range(SSM_CHUNK):
            for t in range(SSM_CHUNK):
                tile = lag_ref[0, t - s] if t >= s else jnp.zeros((lanes, lanes), BF16)
                m_ref[s * lanes:(s + 1) * lanes, t * lanes:(t + 1) * lanes] = tile

    for t in range(SSM_CHUNK):
        x_ref[:, t * lanes:(t + 1) * lanes] = u_ref[pl.ds(t, nch, stride=SSM_CHUNK), :].astype(BF16)
    for c0 in range(0, SSM_CHUNK * lanes, V7X_MXU):
        c1 = c0 + V7X_MXU
        y_ref[:, c0:c1] = jnp.dot(x_ref[:, :c1], m_ref[:c1, c0:c1], preferred_element_type=F32)
    z_ref[...] = jnp.dot(x_ref[...], p_ref[0], preferred_element_type=F32)
    lr, li = lr_ref[0], li_ref[0]

    def scan_body(k, carry):
        sr, si = carry
        sp_ref[pl.ds(k, 1), 0:half] = sr
        sp_ref[pl.ds(k, 1), half:2 * half] = si
        zr = z_ref[pl.ds(k, 1), 0:half]
        zi = z_ref[pl.ds(k, 1), half:2 * half]
        return lr * sr - li * si + zr, lr * si + li * sr + zi

    zero = jnp.zeros((1, half), F32)
    lax.fori_loop(0, nch, scan_body, (zero, zero))
    y = y_ref[...] + jnp.dot(sp_ref[...].astype(BF16), q_ref[0], preferred_element_type=F32)
    d = d_ref[...]
    for t in range(SSM_CHUNK):
        yt = y[:, t * lanes:(t + 1) * lanes] + d * u_ref[pl.ds(t, nch, stride=SSM_CHUNK), :]
        o_ref[pl.ds(t, nch, stride=SSM_CHUNK), :] = _gelu(yt)


def _s5_call(u, lag, p, q, lr, li, dskip, batch, seq):
    nsg = lag.shape[0]
    nch = seq // SSM_CHUNK
    nstate = q.shape[1]
    lw = SSM_CHUNK * V7X_LANES
    slab = lambda a: pl.BlockSpec((1,) + a.shape[1:], lambda g, b: (g,) + (0,) * (a.ndim - 1),
                                  pipeline_mode=pl.Buffered(1))
    return pl.pallas_call(
        _s5_kernel,
        grid=(nsg, batch),
        in_specs=[pl.BlockSpec((seq, V7X_LANES), lambda g, b: (b, g)), slab(lag), slab(p), slab(q),
                  pl.BlockSpec((1, 1, nstate // 2), lambda g, b: (g, 0, 0)),
                  pl.BlockSpec((1, 1, nstate // 2), lambda g, b: (g, 0, 0)),
                  pl.BlockSpec((1, V7X_LANES), lambda g, b: (0, g))],
        out_specs=pl.BlockSpec((seq, V7X_LANES), lambda g, b: (b, g)),
        out_shape=jax.ShapeDtypeStruct(u.shape, F32),
        scratch_shapes=[pltpu.VMEM((lw, lw), BF16), pltpu.VMEM((nch, lw), BF16), pltpu.VMEM((nch, lw), F32),
                        pltpu.VMEM((nch, nstate), F32), pltpu.VMEM((nch, nstate), F32)],
        compiler_params=_cparams(("arbitrary", "arbitrary")),
        name="s5",
    )(u, lag, p, q, lr, li, dskip)


def _merge_kernel(x_ref, g_ref, attn_ref, ssm_ref, wga_ref, wgb_ref, wattn_ref, wval_ref, wgate_ref, wout_ref, o_ref):
    x = x_ref[...]
    hb = _rmsnorm(x, g_ref[...]).astype(BF16)
    dot = functools.partial(jnp.dot, preferred_element_type=F32)
    y_a = dot(attn_ref[...], wattn_ref[...])
    ys = ssm_ref[...].astype(BF16)
    y_b = dot(ys, wval_ref[...]) * jax.nn.sigmoid(dot(ys, wgate_ref[...]))
    merged = jax.nn.sigmoid(dot(hb, wga_ref[...])) * y_a + jax.nn.sigmoid(dot(hb, wgb_ref[...])) * y_b
    o_ref[...] = x + dot(merged.astype(BF16), wout_ref[...])


def _merge_call(x2, g, attn, ssm, wga, wgb, wattn, wval, wgate, wout):
    t, d = x2.shape
    tm = ROW_TILE
    row = lambda i: (i, 0)
    wspec = lambda a: pl.BlockSpec(a.shape, lambda i: (0, 0), pipeline_mode=pl.Buffered(1))
    return pl.pallas_call(
        _merge_kernel,
        grid=(t // tm,),
        in_specs=[pl.BlockSpec((tm, d), row), wspec(g), pl.BlockSpec((tm, attn.shape[1]), row),
                  pl.BlockSpec((tm, ssm.shape[1]), row), wspec(wga), wspec(wgb), wspec(wattn), wspec(wval),
                  wspec(wgate), wspec(wout)],
        out_specs=pl.BlockSpec((tm, d), row),
        out_shape=jax.ShapeDtypeStruct((t, d), F32),
        compiler_params=_cparams(("arbitrary",)),
        name="merge",
    )(x2, g, attn, ssm, wga, wgb, wattn, wval, wgate, wout)


def _mlp_kernel(x_ref, g_ref, wup_ref, wdown_ref, gf_ref, o_ref):
    x = x_ref[...]
    hb = _rmsnorm(x, g_ref[...]).astype(BF16)
    acc = x
    for c in range(wup_ref.shape[1] // FF_CHUNK):
        sl = slice(c * FF_CHUNK, (c + 1) * FF_CHUNK)
        up = jnp.maximum(jnp.dot(hb, wup_ref[:, sl], preferred_element_type=F32), 0.0)
        acc = acc + jnp.dot((up * up).astype(BF16), wdown_ref[sl, :], preferred_element_type=F32)
    o_ref[...] = _rmsnorm(acc, gf_ref[...])


def _mlp_call(x1, g, wup, wdown, gf):
    t, d = x1.shape
    tm = ROW_TILE
    row = lambda i: (i, 0)
    wspec = lambda a: pl.BlockSpec(a.shape, lambda i: (0, 0), pipeline_mode=pl.Buffered(1))
    return pl.pallas_call(
        _mlp_kernel,
        grid=(t // tm,),
        in_specs=[pl.BlockSpec((tm, d), row), wspec(g), wspec(wup), wspec(wdown), wspec(gf)],
        out_specs=pl.BlockSpec((tm, d), row),
        out_shape=jax.ShapeDtypeStruct((t, d), F32),
        compiler_params=_cparams(("arbitrary",)),
        name="mlp",
    )(x1, g, wup, wdown, gf)


def _interleave_heads(wq):
    d = wq.shape[0]
    return wq.reshape(d, NSA_KV_HEADS, HEADS_PER_GROUP, HEAD_DIM).transpose(0, 2, 1, 3).reshape(d, Q_LANES)


def _rope_tables(seq):
    half = ROPE_DIM // 2
    inv = ROPE_THETA ** (-(jnp.arange(half, dtype=F32) * 2.0) / ROPE_DIM)
    ang = jnp.arange(seq, dtype=F32)[:, None] * inv[None, :]
    cos, sin = jnp.cos(ang), jnp.sin(ang)
    rest = HEAD_DIM - ROPE_DIM
    cos_h = jnp.concatenate([cos, cos, jnp.ones((seq, rest), F32)], axis=1)
    slo_h = jnp.concatenate([-sin, jnp.zeros((seq, half + rest), F32)], axis=1)
    shi_h = jnp.concatenate([jnp.zeros((seq, half), F32), sin, jnp.zeros((seq, rest), F32)], axis=1)
    reps = V7X_LANES // HEAD_DIM
    return jnp.tile(cos_h, (1, reps)), jnp.tile(slo_h, (1, reps)), jnp.tile(shi_h, (1, reps))


def _compress_weights(pe, w1, w2):
    eye = jnp.eye(NSA_KV_HEADS, dtype=F32)
    w1e = jnp.einsum('tdj,gk->tgdkj', w1.reshape(CMP_BLOCK, HEAD_DIM, CMP_HIDDEN), eye)
    w1e = w1e.reshape(CMP_BLOCK * KV_LANES, NSA_KV_HEADS * CMP_HIDDEN).astype(BF16)
    w2e = jnp.einsum('jd,gk->gjkd', w2, eye).reshape(NSA_KV_HEADS * CMP_HIDDEN, KV_LANES).astype(BF16)
    pee = jnp.tile(pe, (1, NSA_KV_HEADS)).reshape(1, CMP_BLOCK * KV_LANES)
    halfw = CMP_STRIDE * KV_LANES
    return pee[:, :halfw], pee[:, halfw:], w1e[:halfw], w1e[halfw:], w2e


def _selection_constants(seq):
    nc = seq // CMP_STRIDE - 1
    nb = seq // SEL_BLOCK
    n_np = np.arange(nc)[:, None] * CMP_STRIDE
    j_np = np.arange(nb)[None, :] * SEL_BLOCK
    overlap = ((n_np < j_np + SEL_BLOCK) & (n_np + CMP_BLOCK > j_np)).astype(np.float32)
    ovt = np.zeros((V7X_LANES, nc + 1), np.float32)
    ovt[:nb, :nc] = overlap.T
    return jnp.asarray(ovt, BF16)


def _s5_matrices(lam_re, lam_im, log_step, b_re, b_im, c_re, c_im):
    hp = lax.Precision.HIGHEST
    ng, ns = lam_re.shape
    gc = b_re.shape[-1]
    L = SSM_CHUNK
    step = jnp.exp(log_step)[:, None]
    a, b = lam_re * step, lam_im * step
    k = jnp.arange(L + 1, dtype=F32)[:, None, None]
    mag = jnp.exp(a[None] * k)
    pr, pi = mag * jnp.cos(b[None] * k), mag * jnp.sin(b[None] * k)
    nr, ni = pr[1] - 1.0, pi[1]
    den = lam_re * lam_re + lam_im * lam_im
    cr, ci = (nr * lam_re + ni * lam_im) / den, (ni * lam_re - nr * lam_im) / den
    bbr = cr[..., None] * b_re - ci[..., None] * b_im
    bbi = cr[..., None] * b_im + ci[..., None] * b_re
    cpr = c_re[None] * pr[:, :, None, :] - c_im[None] * pi[:, :, None, :]
    cpi = c_re[None] * pi[:, :, None, :] + c_im[None] * pr[:, :, None, :]
    kk = jnp.einsum('kgcn,gnd->kgcd', jnp.concatenate([cpr[:L], -cpi[:L]], axis=-1),
                    jnp.concatenate([bbr, bbi], axis=1), precision=hp)
    pw_r, pw_i = pr[L - 1 - np.arange(L)], pi[L - 1 - np.arange(L)]
    p_r = pw_r[:, :, :, None] * bbr[None] - pw_i[:, :, :, None] * bbi[None]
    p_i = pw_r[:, :, :, None] * bbi[None] + pw_i[:, :, :, None] * bbr[None]
    p_r, p_i = p_r.transpose(1, 0, 3, 2), p_i.transpose(1, 0, 3, 2)
    q_r = cpr[1:].transpose(1, 3, 0, 2)
    q_i = -cpi[1:].transpose(1, 3, 0, 2)
    sup = SSM_SUPER
    nsg = ng // sup
    lw = L * sup * gc
    bf = lambda a: a.astype(BF16)
    lag_c = bf(kk).reshape(L, nsg, sup, gc, gc).transpose(1, 0, 2, 4, 3).reshape(nsg, L, sup * gc, gc)
    own_lag = np.arange(sup * gc)[:, None] // gc == np.arange(sup * gc)[None, :] // gc
    lag_sg = jnp.where(own_lag, jnp.matmul(lag_c, jnp.asarray(np.tile(np.eye(gc), (1, sup)), BF16)), 0)
    slab_rows = lambda a: bf(a).reshape(nsg, sup, L, gc, -1).transpose(0, 2, 1, 3, 4).reshape(nsg, lw, -1)
    p_c = jnp.concatenate([slab_rows(p_r), slab_rows(p_i)], axis=2)
    rep_p = np.kron(np.eye(2), np.tile(np.eye(ns), (1, sup)))
    own_p = (np.arange(lw)[:, None] // gc) % sup == (np.arange(2 * sup * ns)[None, :] // ns) % sup
    p = jnp.where(own_p, jnp.matmul(p_c, jnp.asarray(rep_p, BF16)), 0)
    q_c = jnp.concatenate([bf(q_r).reshape(nsg, sup * ns, L * gc), bf(q_i).reshape(nsg, sup * ns, L * gc)], axis=1)
    rep_q = np.einsum('ts,cd,b->tcsbd', np.eye(L), np.eye(gc), np.ones(sup)).reshape(L * gc, lw)
    own_q = (np.arange(2 * sup * ns)[:, None] // ns) % sup == (np.arange(lw)[None, :] // gc) % sup
    q = jnp.where(own_q, jnp.matmul(q_c, jnp.asarray(rep_q, BF16)), 0)
    return lag_sg, p, q, pr[L].reshape(nsg, 1, sup * ns), pi[L].reshape(nsg, 1, sup * ns)


def kernel(x, norm_mix_g, w_in, cmp_pe, cmp_k_w1, cmp_k_w2, cmp_v_w1, cmp_v_w2, ssm_lam_re, ssm_lam_im, ssm_log_step, ssm_b_re, ssm_b_im, ssm_c_re, ssm_c_im, ssm_d, w_attn_branch, w_ssm_val, w_ssm_gate, w_out, norm_mlp_g, w_up, w_down, norm_final_g):
    batch, seq, d = x.shape
    depth = w_in.shape[0]
    assert depth == 1, "the final rmsnorm is fused into the single layer's mlp kernel"
    nsa_w = NSA_HEADS * HEAD_DIM
    ssm_w = ssm_d.shape[1]
    o_q, o_kv, o_g = nsa_w, nsa_w + 6 * KV_LANES, nsa_w + 6 * KV_LANES + 3 * NSA_HEADS
    o_u = o_g + ssm_w
    cos, slo, shi = _rope_tables(seq)
    ovt = _selection_constants(seq)
    head_order = np.array([g * HEADS_PER_GROUP + h for h in range(HEADS_PER_GROUP) for g in range(NSA_KV_HEADS)])
    x2 = x.reshape(batch * seq, d)
    for l in range(depth):
        wl = w_in[l]
        wq = _interleave_heads(wl[:, :o_q]).astype(BF16)
        wkv = wl[:, o_q:o_kv].astype(BF16)
        wg = jnp.pad(wl[:, o_kv:o_g], ((0, 0), (0, V7X_LANES - 3 * NSA_HEADS))).astype(BF16)
        wu = wl[:, o_g:o_u].astype(BF16)
        wga = wl[:, o_u:o_u + d].astype(BF16)
        wgb = wl[:, o_u + d:].astype(BF16)
        g_mix = norm_mix_g[l].reshape(1, d)
        qraw, qrot, kc_raw, vc_raw, ksat, vs, kwt, vw, gates, u = _proj_call(
            x2, g_mix, wq, wkv, wg, wu, cos, slo, shi, batch, seq)

        pea, peb, kw1a, kw1b, kw2 = _compress_weights(cmp_pe[l], cmp_k_w1[l], cmp_k_w2[l])
        _, _, vw1a, vw1b, vw2 = _compress_weights(cmp_pe[l], cmp_v_w1[l], cmp_v_w2[l])
        kct, vc = _compress_call(kc_raw, vc_raw, pea, peb, kw1a, kw1b, kw2.T, vw1a, vw1b, vw2, batch, seq)

        b3 = lambda a: a.reshape(batch, seq, a.shape[-1])
        attn = _nsa_call(b3(qraw), b3(qrot), b3(gates), kct, vc, ksat, b3(vs), kwt, b3(vw), ovt)
        attn = attn.reshape(batch * seq, nsa_w)

        lag, pm, qm, lr, li = _s5_matrices(ssm_lam_re[l], ssm_lam_im[l], ssm_log_step[l], ssm_b_re[l], ssm_b_im[l],
                                           ssm_c_re[l], ssm_c_im[l])
        y_ssm = _s5_call(u, lag, pm, qm, lr, li, ssm_d[l].reshape(1, ssm_w), batch, seq)

        wattn = w_attn_branch[l].reshape(NSA_HEADS, HEAD_DIM, d)[head_order].reshape(nsa_w, d).astype(BF16)
        x1 = _merge_call(x2, g_mix, attn, y_ssm, wga, wgb, wattn, w_ssm_val[l].astype(BF16),
                         w_ssm_gate[l].astype(BF16), w_out[l].astype(BF16))
        x2 = _mlp_call(x1, norm_mlp_g[l].reshape(1, d), w_up[l].astype(BF16), w_down[l].astype(BF16),
                       norm_final_g.reshape(1, d))
    return x2.reshape(batch, seq, d)
```

```python
import functools
import math

import jax
import jax.numpy as jnp
import numpy as np
from jax import lax
from jax.experimental import pallas as pl
from jax.experimental.pallas import tpu as pltpu

NSA_HEADS = 8
NSA_KV_HEADS = 2
HEAD_DIM = 64
CMP_BLOCK = 32
CMP_STRIDE = 16
CMP_HIDDEN = 256
SEL_BLOCK = 64
N_SEL = 16
WINDOW = 512
FORCE_BONUS = 1e3
NEG_INF = -1e30
ROPE_THETA = 500000.0
ROPE_DIM = HEAD_DIM // 4
SSM_GROUP = 16
SSM_STATE = 64
EPS = 1e-6

HEADS_PER_GROUP = NSA_HEADS // NSA_KV_HEADS
KV_LANES = NSA_KV_HEADS * HEAD_DIM
Q_LANES = NSA_HEADS * HEAD_DIM

V7X_LANES = 128
V7X_MXU = 256
V7X_VMEM_BYTES = 64 * 1024 * 1024
VMEM_LIMIT = V7X_VMEM_BYTES - 8 * 1024 * 1024

ROW_TILE = 1024
SEL_TILE = 1024
NSA_QUERIES = 256
SSM_CHUNK = 16
SSM_SUPER = V7X_LANES // SSM_GROUP
FF_CHUNK = 1024

LOG2E = math.log2(math.e)
MASK_BIAS = 1e30

BF16 = jnp.bfloat16
F32 = jnp.float32
_NT = (((1,), (1,)), ((), ()))


def _cparams(semantics):
    return pltpu.CompilerParams(dimension_semantics=semantics, vmem_limit_bytes=VMEM_LIMIT)


def _rmsnorm(x, g):
    return x * lax.rsqrt(jnp.mean(x * x, axis=-1, keepdims=True) + EPS) * g


def _gelu(x):
    return jax.nn.gelu(x)


def _rope_cols(x, cos, sin_lo, sin_hi):
    cols = []
    for c in range(x.shape[1] // V7X_LANES):
        xc = x[:, c * V7X_LANES:(c + 1) * V7X_LANES]
        up = pltpu.roll(xc, V7X_LANES - ROPE_DIM // 2, axis=1)
        dn = pltpu.roll(xc, ROPE_DIM // 2, axis=1)
        cols.append(xc * cos + up * sin_lo + dn * sin_hi)
    return jnp.concatenate(cols, axis=1) if len(cols) > 1 else cols[0]


def _values_with_ones(v):
    lane_head = lax.broadcasted_iota(jnp.int32, v.shape, 1) // HEAD_DIM
    return jnp.concatenate([jnp.where(lane_head == g, v, 1.0) for g in range(NSA_KV_HEADS)], axis=1).astype(BF16)


def _proj_kernel(x_ref, g_ref, wq_ref, wkv_ref, wg_ref, wu_ref, cos_ref, slo_ref, shi_ref,
                 qraw_ref, qrot_ref, kc_ref, vc_ref, ksat_ref, vs_ref, kwt_ref, vw_ref, gate_ref, u_ref, *, seq_tiles):
    hb = _rmsnorm(x_ref[...], g_ref[...]).astype(BF16)
    cos, slo, shi = cos_ref[...], slo_ref[...], shi_ref[...]
    q = jnp.dot(hb, wq_ref[...], preferred_element_type=F32) * (HEAD_DIM ** -0.5 * LOG2E)
    qraw_ref[...] = q.astype(BF16)
    qrot_ref[...] = _rope_cols(q, cos, slo, shi).astype(BF16)
    kv = jnp.dot(hb, wkv_ref[...], preferred_element_type=F32)
    w = KV_LANES
    kc_ref[...] = kv[:, 0 * w:1 * w]
    vc_ref[...] = kv[:, 1 * w:2 * w]
    tm = x_ref.shape[0]
    pos = (pl.program_id(0) % seq_tiles) * tm + lax.broadcasted_iota(jnp.int32, (V7X_LANES, tm), 1)
    onehot = jnp.where(lax.broadcasted_iota(jnp.int32, (V7X_LANES, tm), 0) == pos // SEL_BLOCK, 1.0, 0.0)
    ksat_ref[0, :w, :] = _rope_cols(kv[:, 2 * w:3 * w], cos, slo, shi).T.astype(BF16)
    ksat_ref[0, w:, :] = onehot.astype(BF16)
    vs_ref[...] = _values_with_ones(kv[:, 3 * w:4 * w])
    kwt_ref[0] = _rope_cols(kv[:, 4 * w:5 * w], cos, slo, shi).T.astype(BF16)
    vw_ref[...] = _values_with_ones(kv[:, 5 * w:6 * w])
    gate_ref[...] = jax.nn.sigmoid(jnp.dot(hb, wg_ref[...], preferred_element_type=F32))
    u_ref[...] = jnp.dot(hb, wu_ref[...], preferred_element_type=F32)


def _proj_call(x2, g, wq, wkv, wg, wu, cos, slo, shi, batch, seq):
    t, d = x2.shape
    tm = ROW_TILE
    s_tiles = seq // tm
    row = lambda i: (i, 0)
    pos = lambda i: (i % s_tiles, 0)
    trans = lambda i: (i // s_tiles, 0, i % s_tiles)
    wspec = lambda a: pl.BlockSpec(a.shape, lambda i: (0, 0), pipeline_mode=pl.Buffered(1))
    rows_out = lambda n, dt: (jax.ShapeDtypeStruct((t, n), dt), pl.BlockSpec((tm, n), row))
    trans_out = lambda n: (jax.ShapeDtypeStruct((batch, n, seq), BF16), pl.BlockSpec((1, n, tm), trans))
    outs = [rows_out(Q_LANES, BF16), rows_out(Q_LANES, BF16), rows_out(KV_LANES, F32), rows_out(KV_LANES, F32),
            trans_out(KV_LANES + V7X_LANES), rows_out(NSA_KV_HEADS * KV_LANES, BF16),
            trans_out(KV_LANES), rows_out(NSA_KV_HEADS * KV_LANES, BF16),
            rows_out(V7X_LANES, F32), rows_out(wu.shape[1], F32)]
    assert seq // SEL_BLOCK <= V7X_LANES, "the selection-block one-hot must fit one lane tile"
    return pl.pallas_call(
        functools.partial(_proj_kernel, seq_tiles=s_tiles),
        grid=(t // tm,),
        in_specs=[pl.BlockSpec((tm, d), row), wspec(g), wspec(wq), wspec(wkv), wspec(wg), wspec(wu),
                  pl.BlockSpec((tm, V7X_LANES), pos), pl.BlockSpec((tm, V7X_LANES), pos),
                  pl.BlockSpec((tm, V7X_LANES), pos)],
        out_specs=[o[1] for o in outs],
        out_shape=[o[0] for o in outs],
        compiler_params=_cparams(("arbitrary",)),
        name="proj",
    )(x2, g, wq, wkv, wg, wu, cos, slo, shi)


def _compress_kernel(k_ref, v_ref, pea_ref, peb_ref, kw1a_ref, kw1b_ref, kw2t_ref, vw1a_ref, vw1b_ref, vw2_ref,
                     kct_ref, vc_ref, ca_ref, cb_ref):
    nch = ca_ref.shape[0]

    def hidden(src_ref, w1a_ref, w1b_ref):
        for t in range(CMP_STRIDE):
            rows = src_ref[pl.ds(t, nch, stride=CMP_STRIDE), :]
            sl = slice(t * KV_LANES, (t + 1) * KV_LANES)
            ca_ref[:, sl] = (rows + pea_ref[:, sl]).astype(BF16)
            cb_ref[:, sl] = (rows + peb_ref[:, sl]).astype(BF16)
        ha = jnp.dot(ca_ref[...], w1a_ref[...], preferred_element_type=F32)
        hb = jnp.dot(cb_ref[...], w1b_ref[...], preferred_element_type=F32)
        return _gelu(ha + pltpu.roll(hb, nch - 1, axis=0)).astype(BF16)

    kct_ref[0] = lax.dot_general(kw2t_ref[...], hidden(k_ref, kw1a_ref, kw1b_ref), _NT,
                                 preferred_element_type=F32).astype(BF16)
    vc_ref[0] = jnp.dot(hidden(v_ref, vw1a_ref, vw1b_ref), vw2_ref[...], preferred_element_type=F32).astype(BF16)


def _compress_call(kc_raw, vc_raw, pea, peb, kw1a, kw1b, kw2, vw1a, vw1b, vw2, batch, seq):
    nch = seq // CMP_STRIDE
    wspec = lambda a: pl.BlockSpec(a.shape, lambda b: (0, 0))
    return pl.pallas_call(
        _compress_kernel,
        grid=(batch,),
        in_specs=[pl.BlockSpec((seq, KV_LANES), lambda b: (b, 0)), pl.BlockSpec((seq, KV_LANES), lambda b: (b, 0)),
                  wspec(pea), wspec(peb), wspec(kw1a), wspec(kw1b), wspec(kw2), wspec(vw1a), wspec(vw1b), wspec(vw2)],
        out_specs=[pl.BlockSpec((1, KV_LANES, nch), lambda b: (b, 0, 0)),
                   pl.BlockSpec((1, nch, KV_LANES), lambda b: (b, 0, 0))],
        out_shape=[jax.ShapeDtypeStruct((batch, KV_LANES, nch), BF16), jax.ShapeDtypeStruct((batch, nch, KV_LANES), BF16)],
        scratch_shapes=[pltpu.VMEM((nch, CMP_STRIDE * KV_LANES), BF16), pltpu.VMEM((nch, CMP_STRIDE * KV_LANES), BF16)],
        compiler_params=_cparams(("arbitrary",)),
        name="compress",
    )(kc_raw, vc_raw, pea, peb, kw1a, kw1b, kw2, vw1a, vw1b, vw2)


def _window_keys(nq):
    return WINDOW + nq


def _nsa_kernel(qraw_ref, qrot_ref, gate_ref, kct_ref, vc_ref, ksat_ref, vs_ref, kwt_ref, vw_ref, ovt_ref,
                o_ref, score_ref, qa_ref, *, n_sel):
    i = pl.program_id(1)
    nq = qraw_ref.shape[1]
    hg = HEADS_PER_GROUP
    rows = hg * nq
    lanes_gq = NSA_KV_HEADS * nq
    nb = score_ref.shape[0]
    nc = kct_ref.shape[2]
    t0 = i * nq
    blk_first = t0 // SEL_BLOCK
    blk_last = blk_first + nq // SEL_BLOCK - 1
    tq = t0 + lax.broadcasted_iota(jnp.int32, (nq, 1), 0)
    tq_rows = t0 + lax.broadcasted_iota(jnp.int32, (rows, 1), 0) % nq

    def add_per_query(s, bias):
        return (s.reshape(hg, nq, s.shape[-1]) + bias[None]).reshape(s.shape)

    own_lanes = [lax.broadcasted_iota(jnp.int32, (nq, KV_LANES), 1) // HEAD_DIM == g for g in range(NSA_KV_HEADS)]

    def group_q(ref, g):
        return jnp.concatenate([jnp.where(own_lanes[g], ref[0, :, h * KV_LANES:(h + 1) * KV_LANES], 0.0).astype(BF16)
                                for h in range(hg)], axis=0)

    def normalized(pv):
        return pv * (1.0 / pltpu.roll(pv, HEAD_DIM, axis=1))

    nwin = _window_keys(nq)
    w0 = pl.multiple_of(jnp.maximum(blk_first - WINDOW // SEL_BLOCK, 0) * SEL_BLOCK, V7X_LANES)
    kp = w0 + lax.broadcasted_iota(jnp.int32, (1, nwin), 1)
    bias_w = jnp.where(kp <= tq, jnp.where(kp > tq - WINDOW, 0.0, NEG_INF), NEG_INF)
    o_win = []
    for g in range(NSA_KV_HEADS):
        s = jnp.dot(group_q(qrot_ref, g), kwt_ref[0, :, pl.ds(w0, nwin)], preferred_element_type=F32)
        s = add_per_query(s, bias_w)
        e = jnp.exp2(s - jnp.max(s, axis=-1, keepdims=True))
        v = vw_ref[0, pl.ds(w0, nwin), g * KV_LANES:(g + 1) * KV_LANES]
        o_win.append(normalized(jnp.dot(e.astype(BF16), v, preferred_element_type=F32)))

    def compressed_branch(width):
        cmp_end = lax.broadcasted_iota(jnp.int32, (1, width), 1) * CMP_STRIDE + (CMP_BLOCK - 1)
        bias_c = jnp.where(cmp_end <= tq, 0.0, NEG_INF)
        any_valid = tq_rows >= CMP_BLOCK - 1
        outs, p_sum = [], []
        for g in range(NSA_KV_HEADS):
            s = jnp.dot(group_q(qraw_ref, g), kct_ref[0, :, :width], preferred_element_type=F32)
            s = add_per_query(s, bias_c)
            e = jnp.exp2(s - jnp.max(s, axis=-1, keepdims=True))
            p = e * jnp.where(any_valid, 1.0 / jnp.sum(e, axis=-1, keepdims=True), 0.0)
            outs.append(jnp.dot(p.astype(BF16), vc_ref[0, :width, :], preferred_element_type=F32))
            ph = p[0:nq]
            for h in range(1, hg):
                ph = ph + p[h * nq:(h + 1) * nq]
            p_sum.append(ph)
        p2 = jnp.concatenate(p_sum, axis=0)
        p_hi = p2.astype(BF16)
        p_lo = (p2 - p_hi.astype(F32)).astype(BF16)
        ovt = ovt_ref[:, :width]
        imp = (lax.dot_general(ovt, p_hi, _NT, preferred_element_type=F32)
               + lax.dot_general(ovt, p_lo, _NT, preferred_element_type=F32))
        return tuple(outs) + (imp,)

    *o_cmp, imp_t = compressed_branch(nc)

    blk_q = (t0 + lax.broadcasted_iota(jnp.int32, (1, lanes_gq), 1) % nq) // SEL_BLOCK
    jfull = lax.broadcasted_iota(jnp.int32, (nb, lanes_gq), 0)
    valid = jfull <= blk_q
    forced = (jfull == 0) | (jfull == blk_q) | (jfull == blk_q - 1)
    score = jnp.where(valid, imp_t + jnp.where(forced, FORCE_BONUS, 0.0), NEG_INF)

    free = jnp.where(valid, jnp.where(forced, NEG_INF, imp_t), NEG_INF)
    n_forced = 1 + jnp.where(blk_q >= 1, 1, 0) + jnp.where(blk_q >= 2, 1, 0)
    rest, cut = free, {}
    for k in range(1, n_sel):
        thr = jnp.max(rest, axis=0, keepdims=True)
        rest = jnp.where(rest >= thr, NEG_INF, rest)
        cut[k] = thr
    thr = cut[n_sel - 1]
    for nf in (2, 3):
        thr = jnp.where(n_forced == nf, cut[n_sel - nf], thr)
    sel_t = jnp.where(valid, jnp.where(forced, 1.0, jnp.where(free >= thr, 1.0, 0.0)), 0.0)

    def ranked_members():
        def count_above_or_tied_earlier(jp, cnt):
            row = jnp.broadcast_to(score_ref[pl.ds(jp, 1), :], score.shape)
            tie = jnp.where(jp < jfull, 1.0, 0.0)
            return cnt + jnp.where(row > score, 1.0, jnp.where(row == score, tie, 0.0))

        score_ref[...] = score
        cnt = lax.fori_loop(0, blk_last + 1, count_above_or_tied_earlier, jnp.zeros(score.shape, F32))
        return jnp.where(valid, jnp.where(cnt < float(n_sel), 1.0, 0.0), 0.0)

    picked = jnp.sum(sel_t, axis=0, keepdims=True)
    wrong = jnp.sum(jnp.where(picked == jnp.minimum(blk_q + 1, n_sel).astype(F32), 0.0, 1.0))
    sel_t = lax.cond(wrong == 0.0, lambda: sel_t, ranked_members)
    bias = ((sel_t.T - 1.0) * MASK_BIAS).astype(BF16)

    n_tiles = blk_last // (SEL_TILE // SEL_BLOCK) + 1
    lane_t = lax.broadcasted_iota(jnp.int32, (1, SEL_TILE), 1)
    for g in range(NSA_KV_HEADS):
        qa_ref[g * rows:(g + 1) * rows, :KV_LANES] = group_q(qrot_ref, g)
        qa_ref[g * rows:(g + 1) * rows, KV_LANES:] = jnp.concatenate([bias[g * nq:(g + 1) * nq]] * hg, axis=0)

    def tile_body(c, carry, causal):
        k0 = pl.multiple_of(c * SEL_TILE, SEL_TILE)
        ka = ksat_ref[0, :, pl.ds(k0, SEL_TILE)]
        if causal:
            bias_d = jnp.where((k0 + lane_t) <= tq, 0.0, NEG_INF)
        new = []
        for g in range(NSA_KV_HEADS):
            m, acc = carry[g]
            s = jnp.dot(qa_ref[g * rows:(g + 1) * rows], ka, preferred_element_type=F32)
            if causal:
                s = add_per_query(s, bias_d)
            m_new = jnp.maximum(m, jnp.max(s, axis=-1, keepdims=True))
            p = jnp.exp2(s - m_new)
            v = vs_ref[0, pl.ds(k0, SEL_TILE), g * KV_LANES:(g + 1) * KV_LANES]
            acc = jnp.exp2(m - m_new) * acc + jnp.dot(p.astype(BF16), v, preferred_element_type=F32)
            new.append((m_new, acc))
        return tuple(new)

    init = tuple((jnp.full((rows, 1), NEG_INF, F32), jnp.zeros((rows, KV_LANES), F32)) for _ in range(NSA_KV_HEADS))
    carry = lax.fori_loop(0, n_tiles - 1, functools.partial(tile_body, causal=False), init)
    carry = tile_body(n_tiles - 1, carry, causal=True)
    o_sel = [normalized(acc) for _, acc in carry]

    gt = gate_ref[0]
    for h in range(hg):
        r = slice(h * nq, (h + 1) * nq)
        parts = []
        for g in range(NSA_KV_HEADS):
            c = (g * hg + h) * 3
            parts.append(gt[:, c:c + 1] * o_cmp[g][r] + gt[:, c + 1:c + 2] * o_sel[g][r] + gt[:, c + 2:c + 3] * o_win[g][r])
        out = parts[0]
        for g in range(1, NSA_KV_HEADS):
            out = jnp.where(own_lanes[g], parts[g], out)
        o_ref[0, :, h * KV_LANES:(h + 1) * KV_LANES] = out.astype(BF16)


def _nsa_call(qraw, qrot, gates, kct, vc, ksat, vs, kwt, vw, ovt):
    batch, seq, _ = qraw.shape
    nq = NSA_QUERIES
    assert nq % V7X_LANES == 0 and seq % nq == 0 and seq >= _window_keys(nq) and seq % SEL_TILE == 0
    step = lambda n: pl.BlockSpec((1, nq, n), lambda b, i: (b, i, 0))
    whole = lambda a: pl.BlockSpec((1,) + a.shape[1:], lambda b, i: (b, 0, 0))
    return pl.pallas_call(
        functools.partial(_nsa_kernel, n_sel=min(N_SEL, seq // SEL_BLOCK)),
        grid=(batch, seq // nq),
        in_specs=[step(Q_LANES), step(Q_LANES), step(V7X_LANES),
                  whole(kct), whole(vc), whole(ksat), whole(vs), whole(kwt), whole(vw),
                  pl.BlockSpec(ovt.shape, lambda b, i: (0, 0))],
        out_specs=step(HEADS_PER_GROUP * KV_LANES),
        out_shape=jax.ShapeDtypeStruct((batch, seq, HEADS_PER_GROUP * KV_LANES), BF16),
        scratch_shapes=[pltpu.VMEM((V7X_LANES, NSA_KV_HEADS * nq), F32),
                        pltpu.VMEM((NSA_KV_HEADS * HEADS_PER_GROUP * nq, KV_LANES + V7X_LANES), BF16)],
        compiler_params=_cparams(("arbitrary", "arbitrary")),
        name="nsa",
    )(qraw, qrot, gates, kct, vc, ksat, vs, kwt, vw, ovt)


def _s5_kernel(u_ref, lag_ref, p_ref, q_ref, lr_ref, li_ref, d_ref, o_ref, m_ref, x_ref, y_ref, z_ref, sp_ref):
    nch = x_ref.shape[0]
    half = z_ref.shape[1] // 2
    lanes = V7X_LANES

    @pl.when(pl.program_id(1) == 0)
    def _():
        for s in range(SSM_CHUNK):
            for t in range(SSM_CHUNK):
                tile = lag_ref[0, t - s] if t >= s else jnp.zeros((lanes, lanes), BF16)
                m_ref[s * lanes:(s + 1) * lanes, t * lanes:(t + 1) * lanes] = tile

    for t in range(SSM_CHUNK):
        x_ref[:, t * lanes:(t + 1) * lanes] = u_ref[pl.ds(t, nch, stride=SSM_CHUNK), :].astype(BF16)
    for c0 in range(0, SSM_CHUNK * lanes, V7X_MXU):
        c1 = c0 + V7X_MXU
        y_ref[:, c0:c1] = jnp.dot(x_ref[:, :c1], m_ref[:c1, c0:c1], preferred_element_type=F32)
    z_ref[...] = jnp.dot(x_ref[...], p_ref[0], preferred_element_type=F32)
    lr, li = lr_ref[0], li_ref[0]

    def scan_body(k, carry):
        sr, si = carry
        sp_ref[pl.ds(k, 1), 0:half] = sr
        sp_ref[pl.ds(k, 1), half:2 * half] = si
        zr = z_ref[pl.ds(k, 1), 0:half]
        zi = z_ref[pl.ds(k, 1), half:2 * half]
        return lr * sr - li * si + zr, lr * si + li * sr + zi

    zero = jnp.zeros((1, half), F32)
    lax.fori_loop(0, nch, scan_body, (zero, zero))
    y = y_ref[...] + jnp.dot(sp_ref[...].astype(BF16), q_ref[0], preferred_element_type=F32)
    d = d_ref[...]
    for t in range(SSM_CHUNK):
        yt = y[:, t * lanes:(t + 1) * lanes] + d * u_ref[pl.ds(t, nch, stride=SSM_CHUNK), :]
        o_ref[pl.ds(t, nch, stride=SSM_CHUNK), :] = _gelu(yt)


def _s5_call(u, lag, p, q, lr, li, dskip, batch, seq):
    nsg = lag.shape[0]
    nch = seq // SSM_CHUNK
    nstate = q.shape[1]
    lw = SSM_CHUNK * V7X_LANES
    slab = lambda a: pl.BlockSpec((1,) + a.shape[1:], lambda g, b: (g,) + (0,) * (a.ndim - 1),
                                  pipeline_mode=pl.Buffered(1))
    return pl.pallas_call(
        _s5_kernel,
        grid=(nsg, batch),
        in_specs=[pl.BlockSpec((seq, V7X_LANES), lambda g, b: (b, g)), slab(lag), slab(p), slab(q),
                  pl.BlockSpec((1, 1, nstate // 2), lambda g, b: (g, 0, 0)),
                  pl.BlockSpec((1, 1, nstate // 2), lambda g, b: (g, 0, 0)),
                  pl.BlockSpec((1, V7X_LANES), lambda g, b: (0, g))],
        out_specs=pl.BlockSpec((seq, V7X_LANES), lambda g, b: (b, g)),
        out_shape=jax.ShapeDtypeStruct(u.shape, F32),
        scratch_shapes=[pltpu.VMEM((lw, lw), BF16), pltpu.VMEM((nch, lw), BF16), pltpu.VMEM((nch, lw), F32),
                        pltpu.VMEM((nch, nstate), F32), pltpu.VMEM((nch, nstate), F32)],
        compiler_params=_cparams(("arbitrary", "arbitrary")),
        name="s5",
    )(u, lag, p, q, lr, li, dskip)


def _merge_kernel(x_ref, g_ref, attn_ref, ssm_ref, wga_ref, wgb_ref, wattn_ref, wval_ref, wgate_ref, wout_ref, o_ref):
    x = x_ref[...]
    hb = _rmsnorm(x, g_ref[...]).astype(BF16)
    dot = functools.partial(jnp.dot, preferred_element_type=F32)
    y_a = dot(attn_ref[...], wattn_ref[...])
    ys = ssm_ref[...].astype(BF16)
    y_b = dot(ys, wval_ref[...]) * jax.nn.sigmoid(dot(ys, wgate_ref[...]))
    merged = jax.nn.sigmoid(dot(hb, wga_ref[...])) * y_a + jax.nn.sigmoid(dot(hb, wgb_ref[...])) * y_b
    o_ref[...] = x + dot(merged.astype(BF16), wout_ref[...])


def _merge_call(x2, g, attn, ssm, wga, wgb, wattn, wval, wgate, wout):
    t, d = x2.shape
    tm = ROW_TILE
    row = lambda i: (i, 0)
    wspec = lambda a: pl.BlockSpec(a.shape, lambda i: (0, 0), pipeline_mode=pl.Buffered(1))
    return pl.pallas_call(
        _merge_kernel,
        grid=(t // tm,),
        in_specs=[pl.BlockSpec((tm, d), row), wspec(g), pl.BlockSpec((tm, attn.shape[1]), row),
                  pl.BlockSpec((tm, ssm.shape[1]), row), wspec(wga), wspec(wgb), wspec(wattn), wspec(wval),
                  wspec(wgate), wspec(wout)],
        out_specs=pl.BlockSpec((tm, d), row),
        out_shape=jax.ShapeDtypeStruct((t, d), F32),
        compiler_params=_cparams(("arbitrary",)),
        name="merge",
    )(x2, g, attn, ssm, wga, wgb, wattn, wval, wgate, wout)


def _mlp_kernel(x_ref, g_ref, wup_ref, wdown_ref, gf_ref, o_ref):
    x = x_ref[...]
    hb = _rmsnorm(x, g_ref[...]).astype(BF16)
    acc = x
    for c in range(wup_ref.shape[1] // FF_CHUNK):
        sl = slice(c * FF_CHUNK, (c + 1) * FF_CHUNK)
        up = jnp.maximum(jnp.dot(hb, wup_ref[:, sl], preferred_element_type=F32), 0.0)
        acc = acc + jnp.dot((up * up).astype(BF16), wdown_ref[sl, :], preferred_element_type=F32)
    o_ref[...] = _rmsnorm(acc, gf_ref[...])


def _mlp_call(x1, g, wup, wdown, gf):
    t, d = x1.shape
    tm = ROW_TILE
    row = lambda i: (i, 0)
    wspec = lambda a: pl.BlockSpec(a.shape, lambda i: (0, 0), pipeline_mode=pl.Buffered(1))
    return pl.pallas_call(
        _mlp_kernel,
        grid=(t // tm,),
        in_specs=[pl.BlockSpec((tm, d), row), wspec(g), wspec(wup), wspec(wdown), wspec(gf)],
        out_specs=pl.BlockSpec((tm, d), row),
        out_shape=jax.ShapeDtypeStruct((t, d), F32),
        compiler_params=_cparams(("arbitrary",)),
        name="mlp",
    )(x1, g, wup, wdown, gf)


def _interleave_heads(wq):
    d = wq.shape[0]
    return wq.reshape(d, NSA_KV_HEADS, HEADS_PER_GROUP, HEAD_DIM).transpose(0, 2, 1, 3).reshape(d, Q_LANES)


def _rope_tables(seq):
    half = ROPE_DIM // 2
    inv = ROPE_THETA ** (-(jnp.arange(half, dtype=F32) * 2.0) / ROPE_DIM)
    ang = jnp.arange(seq, dtype=F32)[:, None] * inv[None, :]
    cos, sin = jnp.cos(ang), jnp.sin(ang)
    rest = HEAD_DIM - ROPE_DIM
    cos_h = jnp.concatenate([cos, cos, jnp.ones((seq, rest), F32)], axis=1)
    slo_h = jnp.concatenate([-sin, jnp.zeros((seq, half + rest), F32)], axis=1)
    shi_h = jnp.concatenate([jnp.zeros((seq, half), F32), sin, jnp.zeros((seq, rest), F32)], axis=1)
    reps = V7X_LANES // HEAD_DIM
    return jnp.tile(cos_h, (1, reps)), jnp.tile(slo_h, (1, reps)), jnp.tile(shi_h, (1, reps))


def _compress_weights(pe, w1, w2):
    eye = jnp.eye(NSA_KV_HEADS, dtype=F32)
    w1e = jnp.einsum('tdj,gk->tgdkj', w1.reshape(CMP_BLOCK, HEAD_DIM, CMP_HIDDEN), eye)
    w1e = w1e.reshape(CMP_BLOCK * KV_LANES, NSA_KV_HEADS * CMP_HIDDEN).astype(BF16)
    w2e = jnp.einsum('jd,gk->gjkd', w2, eye).reshape(NSA_KV_HEADS * CMP_HIDDEN, KV_LANES).astype(BF16)
    pee = jnp.tile(pe, (1, NSA_KV_HEADS)).reshape(1, CMP_BLOCK * KV_LANES)
    halfw = CMP_STRIDE * KV_LANES
    return pee[:, :halfw], pee[:, halfw:], w1e[:halfw], w1e[halfw:], w2e


def _selection_constants(seq):
    nc = seq // CMP_STRIDE - 1
    nb = seq // SEL_BLOCK
    n_np = np.arange(nc)[:, None] * CMP_STRIDE
    j_np = np.arange(nb)[None, :] * SEL_BLOCK
    overlap = ((n_np < j_np + SEL_BLOCK) & (n_np + CMP_BLOCK > j_np)).astype(np.float32)
    ovt = np.zeros((V7X_LANES, nc + 1), np.float32)
    ovt[:nb, :nc] = overlap.T
    return jnp.asarray(ovt, BF16)


def _s5_matrices(lam_re, lam_im, log_step, b_re, b_im, c_re, c_im):
    hp = lax.Precision.HIGHEST
    ng, ns = lam_re.shape
    gc = b_re.shape[-1]
    L = SSM_CHUNK
    step = jnp.exp(log_step)[:, None]
    a, b = lam_re * step, lam_im * step
    k = jnp.arange(L + 1, dtype=F32)[:, None, None]
    mag = jnp.exp(a[None] * k)
    pr, pi = mag * jnp.cos(b[None] * k), mag * jnp.sin(b[None] * k)
    nr, ni = pr[1] - 1.0, pi[1]
    den = lam_re * lam_re + lam_im * lam_im
    cr, ci = (nr * lam_re + ni * lam_im) / den, (ni * lam_re - nr * lam_im) / den
    bbr = cr[..., None] * b_re - ci[..., None] * b_im
    bbi = cr[..., None] * b_im + ci[..., None] * b_re
    cpr = c_re[None] * pr[:, :, None, :] - c_im[None] * pi[:, :, None, :]
    cpi = c_re[None] * pi[:, :, None, :] + c_im[None] * pr[:, :, None, :]
    kk = jnp.einsum('kgcn,gnd->kgcd', jnp.concatenate([cpr[:L], -cpi[:L]], axis=-1),
                    jnp.concatenate([bbr, bbi], axis=1), precision=hp)
    pw_r, pw_i = pr[L - 1 - np.arange(L)], pi[L - 1 - np.arange(L)]
    p_r = pw_r[:, :, :, None] * bbr[None] - pw_i[:, :, :, None] * bbi[None]
    p_i = pw_r[:, :, :, None] * bbi[None] + pw_i[:, :, :, None] * bbr[None]
    p_r, p_i = p_r.transpose(1, 0, 3, 2), p_i.transpose(1, 0, 3, 2)
    q_r = cpr[1:].transpose(1, 3, 0, 2)
    q_i = -cpi[1:].transpose(1, 3, 0, 2)
    sup = SSM_SUPER
    nsg = ng // sup
    lw = L * sup * gc
    bf = lambda a: a.astype(BF16)
    lag_c = bf(kk).reshape(L, nsg, sup, gc, gc).transpose(1, 0, 2, 4, 3).reshape(nsg, L, sup * gc, gc)
    own_lag = np.arange(sup * gc)[:, None] // gc == np.arange(sup * gc)[None, :] // gc
    lag_sg = jnp.where(own_lag, jnp.matmul(lag_c, jnp.asarray(np.tile(np.eye(gc), (1, sup)), BF16)), 0)
    slab_rows = lambda a: bf(a).reshape(nsg, sup, L, gc, -1).transpose(0, 2, 1, 3, 4).reshape(nsg, lw, -1)
    p_c = jnp.concatenate([slab_rows(p_r), slab_rows(p_i)], axis=2)
    rep_p = np.kron(np.eye(2), np.tile(np.eye(ns), (1, sup)))
    own_p = (np.arange(lw)[:, None] // gc) % sup == (np.arange(2 * sup * ns)[None, :] // ns) % sup
    p = jnp.where(own_p, jnp.matmul(p_c, jnp.asarray(rep_p, BF16)), 0)
    q_c = jnp.concatenate([bf(q_r).reshape(nsg, sup * ns, L * gc), bf(q_i).reshape(nsg, sup * ns, L * gc)], axis=1)
    rep_q = np.einsum('ts,cd,b->tcsbd', np.eye(L), np.eye(gc), np.ones(sup)).reshape(L * gc, lw)
    own_q = (np.arange(2 * sup * ns)[:, None] // ns) % sup == (np.arange(lw)[None, :] // gc) % sup
    q = jnp.where(own_q, jnp.matmul(q_c, jnp.asarray(rep_q, BF16)), 0)
    return lag_sg, p, q, pr[L].reshape(nsg, 1, sup * ns), pi[L].reshape(nsg, 1, sup * ns)


def kernel(x, norm_mix_g, w_in, cmp_pe, cmp_k_w1, cmp_k_w2, cmp_v_w1, cmp_v_w2, ssm_lam_re, ssm_lam_im, ssm_log_step, ssm_b_re, ssm_b_im, ssm_c_re, ssm_c_im, ssm_d, w_attn_branch, w_ssm_val, w_ssm_gate, w_out, norm_mlp_g, w_up, w_down, norm_final_g):
    batch, seq, d = x.shape
    depth = w_in.shape[0]
    assert depth == 1, "the final rmsnorm is fused into the single layer's mlp kernel"
    nsa_w = NSA_HEADS * HEAD_DIM
    ssm_w = ssm_d.shape[1]
    o_q, o_kv, o_g = nsa_w, nsa_w + 6 * KV_LANES, nsa_w + 6 * KV_LANES + 3 * NSA_HEADS
    o_u = o_g + ssm_w
    cos, slo, shi = _rope_tables(seq)
    ovt = _selection_constants(seq)
    head_order = np.array([g * HEADS_PER_GROUP + h for h in range(HEADS_PER_GROUP) for g in range(NSA_KV_HEADS)])
    x2 = x.reshape(batch * seq, d)
    for l in range(depth):
        wl = w_in[l]
        wq = _interleave_heads(wl[:, :o_q]).astype(BF16)
        wkv = wl[:, o_q:o_kv].astype(BF16)
        wg = jnp.pad(wl[:, o_kv:o_g], ((0, 0), (0, V7X_LANES - 3 * NSA_HEADS))).astype(BF16)
        wu = wl[:, o_g:o_u].astype(BF16)
        wga = wl[:, o_u:o_u + d].astype(BF16)
        wgb = wl[:, o_u + d:].astype(BF16)
        g_mix = norm_mix_g[l].reshape(1, d)
        qraw, qrot, kc_raw, vc_raw, ksat, vs, kwt, vw, gates, u = _proj_call(
            x2, g_mix, wq, wkv, wg, wu, cos, slo, shi, batch, seq)

        pea, peb, kw1a, kw1b, kw2 = _compress_weights(cmp_pe[l], cmp_k_w1[l], cmp_k_w2[l])
        _, _, vw1a, vw1b, vw2 = _compress_weights(cmp_pe[l], cmp_v_w1[l], cmp_v_w2[l])
        kct, vc = _compress_call(kc_raw, vc_raw, pea, peb, kw1a, kw1b, kw2.T, vw1a, vw1b, vw2, batch, seq)

        b3 = lambda a: a.reshape(batch, seq, a.shape[-1])
        attn = _nsa_call(b3(qraw), b3(qrot), b3(gates), kct, vc, ksat, b3(vs), kwt, b3(vw), ovt)
        attn = attn.reshape(batch * seq, nsa_w)

        lag, pm, qm, lr, li = _s5_matrices(ssm_lam_re[l], ssm_lam_im[l], ssm_log_step[l], ssm_b_re[l], ssm_b_im[l],
                                           ssm_c_re[l], ssm_c_im[l])
        y_ssm = _s5_call(u, lag, pm, qm, lr, li, ssm_d[l].reshape(1, ssm_w), batch, seq)

        wattn = w_attn_branch[l].reshape(NSA_HEADS, HEAD_DIM, d)[head_order].reshape(nsa_w, d).astype(BF16)
        x1 = _merge_call(x2, g_mix, attn, y_ssm, wga, wgb, wattn, w_ssm_val[l].astype(BF16),
                         w_ssm_gate[l].astype(BF16), w_out[l].astype(BF16))
        x2 = _mlp_call(x1, norm_mlp_g[l].reshape(1, d), w_up[l].astype(BF16), w_down[l].astype(BF16),
                       norm_final_g.reshape(1, d))
    return x2.reshape(batch, seq, d)
```

```python
import functools
import math

import jax
import jax.numpy as jnp
import numpy as np
from jax import lax
from jax.experimental import pallas as pl
from jax.experimental.pallas import tpu as pltpu

NSA_HEADS = 8
NSA_KV_HEADS = 2
HEAD_DIM = 64
CMP_BLOCK = 32
CMP_STRIDE = 16
CMP_HIDDEN = 256
SEL_BLOCK = 64
N_SEL = 16
WINDOW = 512
FORCE_BONUS = 1e3
NEG_INF = -1e30
ROPE_THETA = 500000.0
ROPE_DIM = HEAD_DIM // 4
SSM_GROUP = 16
SSM_STATE = 64
EPS = 1e-6

HEADS_PER_GROUP = NSA_HEADS // NSA_KV_HEADS
KV_LANES = NSA_KV_HEADS * HEAD_DIM
Q_LANES = NSA_HEADS * HEAD_DIM

V7X_LANES = 128
V7X_MXU = 256
V7X_VMEM_BYTES = 64 * 1024 * 1024
VMEM_LIMIT = V7X_VMEM_BYTES - 8 * 1024 * 1024

ROW_TILE = 1024
SEL_TILE = 1024
NSA_QUERIES = 256
SSM_CHUNK = 8
SSM_SUPER = V7X_LANES // SSM_GROUP
FF_CHUNK = 1024

LOG2E = math.log2(math.e)
MASK_BIAS = 1e30

BF16 = jnp.bfloat16
F32 = jnp.float32
_NT = (((1,), (1,)), ((), ()))


def _cparams(semantics):
    return pltpu.CompilerParams(dimension_semantics=semantics, vmem_limit_bytes=VMEM_LIMIT)


def _rmsnorm(x, g):
    return x * lax.rsqrt(jnp.mean(x * x, axis=-1, keepdims=True) + EPS) * g


def _gelu(x):
    return jax.nn.gelu(x)


def _rope_cols(x, cos, sin_lo, sin_hi):
    cols = []
    for c in range(x.shape[1] // V7X_LANES):
        xc = x[:, c * V7X_LANES:(c + 1) * V7X_LANES]
        up = pltpu.roll(xc, V7X_LANES - ROPE_DIM // 2, axis=1)
        dn = pltpu.roll(xc, ROPE_DIM // 2, axis=1)
        cols.append(xc * cos + up * sin_lo + dn * sin_hi)
    return jnp.concatenate(cols, axis=1) if len(cols) > 1 else cols[0]


def _values_with_ones(v):
    lane_head = lax.broadcasted_iota(jnp.int32, v.shape, 1) // HEAD_DIM
    return jnp.concatenate([jnp.where(lane_head == g, v, 1.0) for g in range(NSA_KV_HEADS)], axis=1).astype(BF16)


def _proj_kernel(x_ref, g_ref, wq_ref, wkv_ref, wg_ref, wu_ref, cos_ref, slo_ref, shi_ref,
                 qraw_ref, qrot_ref, kc_ref, vc_ref, ksat_ref, vs_ref, kwt_ref, vw_ref, gate_ref, u_ref, *, seq_tiles):
    hb = _rmsnorm(x_ref[...], g_ref[...]).astype(BF16)
    cos, slo, shi = cos_ref[...], slo_ref[...], shi_ref[...]
    q = jnp.dot(hb, wq_ref[...], preferred_element_type=F32) * (HEAD_DIM ** -0.5 * LOG2E)
    qraw_ref[...] = q.astype(BF16)
    qrot_ref[...] = _rope_cols(q, cos, slo, shi).astype(BF16)
    kv = jnp.dot(hb, wkv_ref[...], preferred_element_type=F32)
    w = KV_LANES
    kc_ref[...] = kv[:, 0 * w:1 * w]
    vc_ref[...] = kv[:, 1 * w:2 * w]
    tm = x_ref.shape[0]
    pos = (pl.program_id(0) % seq_tiles) * tm + lax.broadcasted_iota(jnp.int32, (V7X_LANES, tm), 1)
    onehot = jnp.where(lax.broadcasted_iota(jnp.int32, (V7X_LANES, tm), 0) == pos // SEL_BLOCK, 1.0, 0.0)
    ksat_ref[0, :w, :] = _rope_cols(kv[:, 2 * w:3 * w], cos, slo, shi).T.astype(BF16)
    ksat_ref[0, w:, :] = onehot.astype(BF16)
    vs_ref[...] = _values_with_ones(kv[:, 3 * w:4 * w])
    kwt_ref[0] = _rope_cols(kv[:, 4 * w:5 * w], cos, slo, shi).T.astype(BF16)
    vw_ref[...] = _values_with_ones(kv[:, 5 * w:6 * w])
    gate_ref[...] = jax.nn.sigmoid(jnp.dot(hb, wg_ref[...], preferred_element_type=F32))
    u_ref[...] = jnp.dot(hb, wu_ref[...], preferred_element_type=F32)


def _proj_call(x2, g, wq, wkv, wg, wu, cos, slo, shi, batch, seq):
    t, d = x2.shape
    tm = ROW_TILE
    s_tiles = seq // tm
    row = lambda i: (i, 0)
    pos = lambda i: (i % s_tiles, 0)
    trans = lambda i: (i // s_tiles, 0, i % s_tiles)
    wspec = lambda a: pl.BlockSpec(a.shape, lambda i: (0, 0), pipeline_mode=pl.Buffered(1))
    rows_out = lambda n, dt: (jax.ShapeDtypeStruct((t, n), dt), pl.BlockSpec((tm, n), row))
    trans_out = lambda n: (jax.ShapeDtypeStruct((batch, n, seq), BF16), pl.BlockSpec((1, n, tm), trans))
    outs = [rows_out(Q_LANES, BF16), rows_out(Q_LANES, BF16), rows_out(KV_LANES, F32), rows_out(KV_LANES, F32),
            trans_out(KV_LANES + V7X_LANES), rows_out(NSA_KV_HEADS * KV_LANES, BF16),
            trans_out(KV_LANES), rows_out(NSA_KV_HEADS * KV_LANES, BF16),
            rows_out(V7X_LANES, F32), rows_out(wu.shape[1], F32)]
    assert seq // SEL_BLOCK <= V7X_LANES, "the selection-block one-hot must fit one lane tile"
    return pl.pallas_call(
        functools.partial(_proj_kernel, seq_tiles=s_tiles),
        grid=(t // tm,),
        in_specs=[pl.BlockSpec((tm, d), row), wspec(g), wspec(wq), wspec(wkv), wspec(wg), wspec(wu),
                  pl.BlockSpec((tm, V7X_LANES), pos), pl.BlockSpec((tm, V7X_LANES), pos),
                  pl.BlockSpec((tm, V7X_LANES), pos)],
        out_specs=[o[1] for o in outs],
        out_shape=[o[0] for o in outs],
        compiler_params=_cparams(("arbitrary",)),
        name="proj",
    )(x2, g, wq, wkv, wg, wu, cos, slo, shi)


def _compress_kernel(k_ref, v_ref, pea_ref, peb_ref, kw1a_ref, kw1b_ref, kw2t_ref, vw1a_ref, vw1b_ref, vw2_ref,
                     kct_ref, vc_ref, ca_ref, cb_ref):
    nch = ca_ref.shape[0]

    def hidden(src_ref, w1a_ref, w1b_ref):
        for t in range(CMP_STRIDE):
            rows = src_ref[pl.ds(t, nch, stride=CMP_STRIDE), :]
            sl = slice(t * KV_LANES, (t + 1) * KV_LANES)
            ca_ref[:, sl] = (rows + pea_ref[:, sl]).astype(BF16)
            cb_ref[:, sl] = (rows + peb_ref[:, sl]).astype(BF16)
        ha = jnp.dot(ca_ref[...], w1a_ref[...], preferred_element_type=F32)
        hb = jnp.dot(cb_ref[...], w1b_ref[...], preferred_element_type=F32)
        return _gelu(ha + pltpu.roll(hb, nch - 1, axis=0)).astype(BF16)

    kct_ref[0] = lax.dot_general(kw2t_ref[...], hidden(k_ref, kw1a_ref, kw1b_ref), _NT,
                                 preferred_element_type=F32).astype(BF16)
    vc_ref[0] = jnp.dot(hidden(v_ref, vw1a_ref, vw1b_ref), vw2_ref[...], preferred_element_type=F32).astype(BF16)


def _compress_call(kc_raw, vc_raw, pea, peb, kw1a, kw1b, kw2, vw1a, vw1b, vw2, batch, seq):
    nch = seq // CMP_STRIDE
    wspec = lambda a: pl.BlockSpec(a.shape, lambda b: (0, 0))
    return pl.pallas_call(
        _compress_kernel,
        grid=(batch,),
        in_specs=[pl.BlockSpec((seq, KV_LANES), lambda b: (b, 0)), pl.BlockSpec((seq, KV_LANES), lambda b: (b, 0)),
                  wspec(pea), wspec(peb), wspec(kw1a), wspec(kw1b), wspec(kw2), wspec(vw1a), wspec(vw1b), wspec(vw2)],
        out_specs=[pl.BlockSpec((1, KV_LANES, nch), lambda b: (b, 0, 0)),
                   pl.BlockSpec((1, nch, KV_LANES), lambda b: (b, 0, 0))],
        out_shape=[jax.ShapeDtypeStruct((batch, KV_LANES, nch), BF16), jax.ShapeDtypeStruct((batch, nch, KV_LANES), BF16)],
        scratch_shapes=[pltpu.VMEM((nch, CMP_STRIDE * KV_LANES), BF16), pltpu.VMEM((nch, CMP_STRIDE * KV_LANES), BF16)],
        compiler_params=_cparams(("arbitrary",)),
        name="compress",
    )(kc_raw, vc_raw, pea, peb, kw1a, kw1b, kw2, vw1a, vw1b, vw2)


def _window_keys(nq):
    return WINDOW + nq


def _nsa_kernel(qraw_ref, qrot_ref, gate_ref, kct_ref, vc_ref, ksat_ref, vs_ref, kwt_ref, vw_ref, ovt_ref,
                o_ref, score_ref, qa_ref, *, n_sel):
    i = pl.program_id(1)
    nq = qraw_ref.shape[1]
    hg = HEADS_PER_GROUP
    rows = hg * nq
    lanes_gq = NSA_KV_HEADS * nq
    nb = score_ref.shape[0]
    nc = kct_ref.shape[2]
    t0 = i * nq
    blk_first = t0 // SEL_BLOCK
    blk_last = blk_first + nq // SEL_BLOCK - 1
    tq = t0 + lax.broadcasted_iota(jnp.int32, (nq, 1), 0)
    tq_rows = t0 + lax.broadcasted_iota(jnp.int32, (rows, 1), 0) % nq

    def add_per_query(s, bias):
        return (s.reshape(hg, nq, s.shape[-1]) + bias[None]).reshape(s.shape)

    own_lanes = [lax.broadcasted_iota(jnp.int32, (nq, KV_LANES), 1) // HEAD_DIM == g for g in range(NSA_KV_HEADS)]

    def group_q(ref, g):
        return jnp.concatenate([jnp.where(own_lanes[g], ref[0, :, h * KV_LANES:(h + 1) * KV_LANES], 0.0).astype(BF16)
                                for h in range(hg)], axis=0)

    def normalized(pv):
        return pv * (1.0 / pltpu.roll(pv, HEAD_DIM, axis=1))

    nwin = _window_keys(nq)
    w0 = pl.multiple_of(jnp.maximum(blk_first - WINDOW // SEL_BLOCK, 0) * SEL_BLOCK, V7X_LANES)
    kp = w0 + lax.broadcasted_iota(jnp.int32, (1, nwin), 1)
    bias_w = jnp.where(kp <= tq, jnp.where(kp > tq - WINDOW, 0.0, NEG_INF), NEG_INF)
    o_win = []
    for g in range(NSA_KV_HEADS):
        s = jnp.dot(group_q(qrot_ref, g), kwt_ref[0, :, pl.ds(w0, nwin)], preferred_element_type=F32)
        s = add_per_query(s, bias_w)
        e = jnp.exp2(s - jnp.max(s, axis=-1, keepdims=True))
        v = vw_ref[0, pl.ds(w0, nwin), g * KV_LANES:(g + 1) * KV_LANES]
        o_win.append(normalized(jnp.dot(e.astype(BF16), v, preferred_element_type=F32)))

    def compressed_branch(width):
        cmp_end = lax.broadcasted_iota(jnp.int32, (1, width), 1) * CMP_STRIDE + (CMP_BLOCK - 1)
        bias_c = jnp.where(cmp_end <= tq, 0.0, NEG_INF)
        any_valid = tq_rows >= CMP_BLOCK - 1
        outs, p_sum = [], []
        for g in range(NSA_KV_HEADS):
            s = jnp.dot(group_q(qraw_ref, g), kct_ref[0, :, :width], preferred_element_type=F32)
            s = add_per_query(s, bias_c)
            e = jnp.exp2(s - jnp.max(s, axis=-1, keepdims=True))
            p = e * jnp.where(any_valid, 1.0 / jnp.sum(e, axis=-1, keepdims=True), 0.0)
            outs.append(jnp.dot(p.astype(BF16), vc_ref[0, :width, :], preferred_element_type=F32))
            ph = p[0:nq]
            for h in range(1, hg):
                ph = ph + p[h * nq:(h + 1) * nq]
            p_sum.append(ph)
        p2 = jnp.concatenate(p_sum, axis=0)
        p_hi = p2.astype(BF16)
        p_lo = (p2 - p_hi.astype(F32)).astype(BF16)
        ovt = ovt_ref[:, :width]
        imp = (lax.dot_general(ovt, p_hi, _NT, preferred_element_type=F32)
               + lax.dot_general(ovt, p_lo, _NT, preferred_element_type=F32))
        return tuple(outs) + (imp,)

    *o_cmp, imp_t = compressed_branch(nc)

    blk_q = (t0 + lax.broadcasted_iota(jnp.int32, (1, lanes_gq), 1) % nq) // SEL_BLOCK
    jfull = lax.broadcasted_iota(jnp.int32, (nb, lanes_gq), 0)
    valid = jfull <= blk_q
    forced = (jfull == 0) | (jfull == blk_q) | (jfull == blk_q - 1)
    score = jnp.where(valid, imp_t + jnp.where(forced, FORCE_BONUS, 0.0), NEG_INF)

    free = jnp.where(valid, jnp.where(forced, NEG_INF, imp_t), NEG_INF)
    n_forced = 1 + jnp.where(blk_q >= 1, 1, 0) + jnp.where(blk_q >= 2, 1, 0)
    rest, cut = free, {}
    for k in range(1, n_sel):
        thr = jnp.max(rest, axis=0, keepdims=True)
        rest = jnp.where(rest >= thr, NEG_INF, rest)
        cut[k] = thr
    thr = cut[n_sel - 1]
    for nf in (2, 3):
        thr = jnp.where(n_forced == nf, cut[n_sel - nf], thr)
    sel_t = jnp.where(valid, jnp.where(forced, 1.0, jnp.where(free >= thr, 1.0, 0.0)), 0.0)

    def ranked_members():
        def count_above_or_tied_earlier(jp, cnt):
            row = jnp.broadcast_to(score_ref[pl.ds(jp, 1), :], score.shape)
            tie = jnp.where(jp < jfull, 1.0, 0.0)
            return cnt + jnp.where(row > score, 1.0, jnp.where(row == score, tie, 0.0))

        score_ref[...] = score
        cnt = lax.fori_loop(0, blk_last + 1, count_above_or_tied_earlier, jnp.zeros(score.shape, F32))
        return jnp.where(valid, jnp.where(cnt < float(n_sel), 1.0, 0.0), 0.0)

    picked = jnp.sum(sel_t, axis=0, keepdims=True)
    wrong = jnp.sum(jnp.where(picked == jnp.minimum(blk_q + 1, n_sel).astype(F32), 0.0, 1.0))
    sel_t = lax.cond(wrong == 0.0, lambda: sel_t, ranked_members)
    bias = ((sel_t.T - 1.0) * MASK_BIAS).astype(BF16)

    n_tiles = blk_last // (SEL_TILE // SEL_BLOCK) + 1
    lane_t = lax.broadcasted_iota(jnp.int32, (1, SEL_TILE), 1)
    for g in range(NSA_KV_HEADS):
        qa_ref[g * rows:(g + 1) * rows, :KV_LANES] = group_q(qrot_ref, g)
        qa_ref[g * rows:(g + 1) * rows, KV_LANES:] = jnp.concatenate([bias[g * nq:(g + 1) * nq]] * hg, axis=0)

    def tile_body(c, carry, causal):
        k0 = pl.multiple_of(c * SEL_TILE, SEL_TILE)
        ka = ksat_ref[0, :, pl.ds(k0, SEL_TILE)]
        if causal:
            bias_d = jnp.where((k0 + lane_t) <= tq, 0.0, NEG_INF)
        new = []
        for g in range(NSA_KV_HEADS):
            m, acc = carry[g]
            s = jnp.dot(qa_ref[g * rows:(g + 1) * rows], ka, preferred_element_type=F32)
            if causal:
                s = add_per_query(s, bias_d)
            m_new = jnp.maximum(m, jnp.max(s, axis=-1, keepdims=True))
            p = jnp.exp2(s - m_new)
            v = vs_ref[0, pl.ds(k0, SEL_TILE), g * KV_LANES:(g + 1) * KV_LANES]
            acc = jnp.exp2(m - m_new) * acc + jnp.dot(p.astype(BF16), v, preferred_element_type=F32)
            new.append((m_new, acc))
        return tuple(new)

    init = tuple((jnp.full((rows, 1), NEG_INF, F32), jnp.zeros((rows, KV_LANES), F32)) for _ in range(NSA_KV_HEADS))
    carry = lax.fori_loop(0, n_tiles - 1, functools.partial(tile_body, causal=False), init)
    carry = tile_body(n_tiles - 1, carry, causal=True)
    o_sel = [normalized(acc) for _, acc in carry]

    gt = gate_ref[0]
    for h in range(hg):
        r = slice(h * nq, (h + 1) * nq)
        parts = []
        for g in range(NSA_KV_HEADS):
            c = (g * hg + h) * 3
            parts.append(gt[:, c:c + 1] * o_cmp[g][r] + gt[:, c + 1:c + 2] * o_sel[g][r] + gt[:, c + 2:c + 3] * o_win[g][r])
        out = parts[0]
        for g in range(1, NSA_KV_HEADS):
            out = jnp.where(own_lanes[g], parts[g], out)
        o_ref[0, :, h * KV_LANES:(h + 1) * KV_LANES] = out.astype(BF16)


def _nsa_call(qraw, qrot, gates, kct, vc, ksat, vs, kwt, vw, ovt):
    batch, seq, _ = qraw.shape
    nq = NSA_QUERIES
    assert nq % V7X_LANES == 0 and seq % nq == 0 and seq >= _window_keys(nq) and seq % SEL_TILE == 0
    step = lambda n: pl.BlockSpec((1, nq, n), lambda b, i: (b, i, 0))
    whole = lambda a: pl.BlockSpec((1,) + a.shape[1:], lambda b, i: (b, 0, 0))
    return pl.pallas_call(
        functools.partial(_nsa_kernel, n_sel=min(N_SEL, seq // SEL_BLOCK)),
        grid=(batch, seq // nq),
        in_specs=[step(Q_LANES), step(Q_LANES), step(V7X_LANES),
                  whole(kct), whole(vc), whole(ksat), whole(vs), whole(kwt), whole(vw),
                  pl.BlockSpec(ovt.shape, lambda b, i: (0, 0))],
        out_specs=step(HEADS_PER_GROUP * KV_LANES),
        out_shape=jax.ShapeDtypeStruct((batch, seq, HEADS_PER_GROUP * KV_LANES), BF16),
        scratch_shapes=[pltpu.VMEM((V7X_LANES, NSA_KV_HEADS * nq), F32),
                        pltpu.VMEM((NSA_KV_HEADS * HEADS_PER_GROUP * nq, KV_LANES + V7X_LANES), BF16)],
        compiler_params=_cparams(("arbitrary", "arbitrary")),
        name="nsa",
    )(qraw, qrot, gates, kct, vc, ksat, vs, kwt, vw, ovt)


def _s5_kernel(u_ref, lag_ref, p_ref, q_ref, lr_ref, li_ref, d_ref, o_ref, m_ref, x_ref, y_ref, z_ref, sp_ref):
    nch = x_ref.shape[0]
    half = z_ref.shape[1] // 2
    lanes = V7X_LANES

    @pl.when(pl.program_id(1) == 0)
    def _():
        for s in range(SSM_CHUNK):
            for t in range(SSM_CHUNK):
                tile = lag_ref[0, t - s] if t >= s else jnp.zeros((lanes, lanes), BF16)
                m_ref[s * lanes:(s + 1) * lanes, t * lanes:(t + 1) * lanes] = tile

    for t in range(SSM_CHUNK):
        x_ref[:, t * lanes:(t + 1) * lanes] = u_ref[pl.ds(t, nch, stride=SSM_CHUNK), :].astype(BF16)
    for c0 in range(0, SSM_CHUNK * lanes, V7X_MXU):
        c1 = c0 + V7X_MXU
        y_ref[:, c0:c1] = jnp.dot(x_ref[:, :c1], m_ref[:c1, c0:c1], preferred_element_type=F32)
    z_ref[...] = jnp.dot(x_ref[...], p_ref[0], preferred_element_type=F32)
    lr, li = lr_ref[0], li_ref[0]

    def scan_body(k, carry):
        sr, si = carry
        sp_ref[pl.ds(k, 1), 0:half] = sr
        sp_ref[pl.ds(k, 1), half:2 * half] = si
        zr = z_ref[pl.ds(k, 1), 0:half]
        zi = z_ref[pl.ds(k, 1), half:2 * half]
        return lr * sr - li * si + zr, lr * si + li * sr + zi

    zero = jnp.zeros((1, half), F32)
    lax.fori_loop(0, nch, scan_body, (zero, zero))
    y = y_ref[...] + jnp.dot(sp_ref[...].astype(BF16), q_ref[0], preferred_element_type=F32)
    d = d_ref[...]
    for t in range(SSM_CHUNK):
        yt = y[:, t * lanes:(t + 1) * lanes] + d * u_ref[pl.ds(t, nch, stride=SSM_CHUNK), :]
        o_ref[pl.ds(t, nch, stride=SSM_CHUNK), :] = _gelu(yt)


def _s5_call(u, lag, p, q, lr, li, dskip, batch, seq):
    nsg = lag.shape[0]
    nch = seq // SSM_CHUNK
    nstate = q.shape[1]
    lw = SSM_CHUNK * V7X_LANES
    slab = lambda a: pl.BlockSpec((1,) + a.shape[1:], lambda g, b: (g,) + (0,) * (a.ndim - 1),
                                  pipeline_mode=pl.Buffered(1))
    return pl.pallas_call(
        _s5_kernel,
        grid=(nsg, batch),
        in_specs=[pl.BlockSpec((seq, V7X_LANES), lambda g, b: (b, g)), slab(lag), slab(p), slab(q),
                  pl.BlockSpec((1, 1, nstate // 2), lambda g, b: (g, 0, 0)),
                  pl.BlockSpec((1, 1, nstate // 2), lambda g, b: (g, 0, 0)),
                  pl.BlockSpec((1, V7X_LANES), lambda g, b: (0, g))],
        out_specs=pl.BlockSpec((seq, V7X_LANES), lambda g, b: (b, g)),
        out_shape=jax.ShapeDtypeStruct(u.shape, F32),
        scratch_shapes=[pltpu.VMEM((lw, lw), BF16), pltpu.VMEM((nch, lw), BF16), pltpu.VMEM((nch, lw), F32),
                        pltpu.VMEM((nch, nstate), F32), pltpu.VMEM((nch, nstate), F32)],
        compiler_params=_cparams(("arbitrary", "arbitrary")),
        name="s5",
    )(u, lag, p, q, lr, li, dskip)


def _merge_kernel(x_ref, g_ref, attn_ref, ssm_ref, wga_ref, wgb_ref, wattn_ref, wval_ref, wgate_ref, wout_ref, o_ref):
    x = x_ref[...]
    hb = _rmsnorm(x, g_ref[...]).astype(BF16)
    dot = functools.partial(jnp.dot, preferred_element_type=F32)
    y_a = dot(attn_ref[...], wattn_ref[...])
    ys = ssm_ref[...].astype(BF16)
    y_b = dot(ys, wval_ref[...]) * jax.nn.sigmoid(dot(ys, wgate_ref[...]))
    merged = jax.nn.sigmoid(dot(hb, wga_ref[...])) * y_a + jax.nn.sigmoid(dot(hb, wgb_ref[...])) * y_b
    o_ref[...] = x + dot(merged.astype(BF16), wout_ref[...])


def _merge_call(x2, g, attn, ssm, wga, wgb, wattn, wval, wgate, wout):
    t, d = x2.shape
    tm = ROW_TILE
    row = lambda i: (i, 0)
    wspec = lambda a: pl.BlockSpec(a.shape, lambda i: (0, 0), pipeline_mode=pl.Buffered(1))
    return pl.pallas_call(
        _merge_kernel,
        grid=(t // tm,),
        in_specs=[pl.BlockSpec((tm, d), row), wspec(g), pl.BlockSpec((tm, attn.shape[1]), row),
                  pl.BlockSpec((tm, ssm.shape[1]), row), wspec(wga), wspec(wgb), wspec(wattn), wspec(wval),
                  wspec(wgate), wspec(wout)],
        out_specs=pl.BlockSpec((tm, d), row),
        out_shape=jax.ShapeDtypeStruct((t, d), F32),
        compiler_params=_cparams(("arbitrary",)),
        name="merge",
    )(x2, g, attn, ssm, wga, wgb, wattn, wval, wgate, wout)


def _mlp_kernel(x_ref, g_ref, wup_ref, wdown_ref, gf_ref, o_ref):
    x = x_ref[...]
    hb = _rmsnorm(x, g_ref[...]).astype(BF16)
    acc = x
    for c in range(wup_ref.shape[1] // FF_CHUNK):
        sl = slice(c * FF_CHUNK, (c + 1) * FF_CHUNK)
        up = jnp.maximum(jnp.dot(hb, wup_ref[:, sl], preferred_element_type=F32), 0.0)
        acc = acc + jnp.dot((up * up).astype(BF16), wdown_ref[sl, :], preferred_element_type=F32)
    o_ref[...] = _rmsnorm(acc, gf_ref[...])


def _mlp_call(x1, g, wup, wdown, gf):
    t, d = x1.shape
    tm = ROW_TILE
    row = lambda i: (i, 0)
    wspec = lambda a: pl.BlockSpec(a.shape, lambda i: (0, 0), pipeline_mode=pl.Buffered(1))
    return pl.pallas_call(
        _mlp_kernel,
        grid=(t // tm,),
        in_specs=[pl.BlockSpec((tm, d), row), wspec(g), wspec(wup), wspec(wdown), wspec(gf)],
        out_specs=pl.BlockSpec((tm, d), row),
        out_shape=jax.ShapeDtypeStruct((t, d), F32),
        compiler_params=_cparams(("arbitrary",)),
        name="mlp",
    )(x1, g, wup, wdown, gf)


def _interleave_heads(wq):
    d = wq.shape[0]
    return wq.reshape(d, NSA_KV_HEADS, HEADS_PER_GROUP, HEAD_DIM).transpose(0, 2, 1, 3).reshape(d, Q_LANES)


def _rope_tables(seq):
    half = ROPE_DIM // 2
    inv = ROPE_THETA ** (-(jnp.arange(half, dtype=F32) * 2.0) / ROPE_DIM)
    ang = jnp.arange(seq, dtype=F32)[:, None] * inv[None, :]
    cos, sin = jnp.cos(ang), jnp.sin(ang)
    rest = HEAD_DIM - ROPE_DIM
    cos_h = jnp.concatenate([cos, cos, jnp.ones((seq, rest), F32)], axis=1)
    slo_h = jnp.concatenate([-sin, jnp.zeros((seq, half + rest), F32)], axis=1)
    shi_h = jnp.concatenate([jnp.zeros((seq, half), F32), sin, jnp.zeros((seq, rest), F32)], axis=1)
    reps = V7X_LANES // HEAD_DIM
    return jnp.tile(cos_h, (1, reps)), jnp.tile(slo_h, (1, reps)), jnp.tile(shi_h, (1, reps))


def _compress_weights(pe, w1, w2):
    eye = jnp.eye(NSA_KV_HEADS, dtype=F32)
    w1e = jnp.einsum('tdj,gk->tgdkj', w1.reshape(CMP_BLOCK, HEAD_DIM, CMP_HIDDEN), eye)
    w1e = w1e.reshape(CMP_BLOCK * KV_LANES, NSA_KV_HEADS * CMP_HIDDEN).astype(BF16)
    w2e = jnp.einsum('jd,gk->gjkd', w2, eye).reshape(NSA_KV_HEADS * CMP_HIDDEN, KV_LANES).astype(BF16)
    pee = jnp.tile(pe, (1, NSA_KV_HEADS)).reshape(1, CMP_BLOCK * KV_LANES)
    halfw = CMP_STRIDE * KV_LANES
    return pee[:, :halfw], pee[:, halfw:], w1e[:halfw], w1e[halfw:], w2e


def _selection_constants(seq):
    nc = seq // CMP_STRIDE - 1
    nb = seq // SEL_BLOCK
    n_np = np.arange(nc)[:, None] * CMP_STRIDE
    j_np = np.arange(nb)[None, :] * SEL_BLOCK
    overlap = ((n_np < j_np + SEL_BLOCK) & (n_np + CMP_BLOCK > j_np)).astype(np.float32)
    ovt = np.zeros((V7X_LANES, nc + 1), np.float32)
    ovt[:nb, :nc] = overlap.T
    return jnp.asarray(ovt, BF16)


def _s5_matrices(lam_re, lam_im, log_step, b_re, b_im, c_re, c_im):
    hp = lax.Precision.HIGHEST
    ng, ns = lam_re.shape
    gc = b_re.shape[-1]
    L = SSM_CHUNK
    step = jnp.exp(log_step)[:, None]
    a, b = lam_re * step, lam_im * step
    k = jnp.arange(L + 1, dtype=F32)[:, None, None]
    mag = jnp.exp(a[None] * k)
    pr, pi = mag * jnp.cos(b[None] * k), mag * jnp.sin(b[None] * k)
    nr, ni = pr[1] - 1.0, pi[1]
    den = lam_re * lam_re + lam_im * lam_im
    cr, ci = (nr * lam_re + ni * lam_im) / den, (ni * lam_re - nr * lam_im) / den
    bbr = cr[..., None] * b_re - ci[..., None] * b_im
    bbi = cr[..., None] * b_im + ci[..., None] * b_re
    cpr = c_re[None] * pr[:, :, None, :] - c_im[None] * pi[:, :, None, :]
    cpi = c_re[None] * pi[:, :, None, :] + c_im[None] * pr[:, :, None, :]
    kk = jnp.einsum('kgcn,gnd->kgcd', jnp.concatenate([cpr[:L], -cpi[:L]], axis=-1),
                    jnp.concatenate([bbr, bbi], axis=1), precision=hp)
    pw_r, pw_i = pr[L - 1 - np.arange(L)], pi[L - 1 - np.arange(L)]
    p_r = pw_r[:, :, :, None] * bbr[None] - pw_i[:, :, :, None] * bbi[None]
    p_i = pw_r[:, :, :, None] * bbi[None] + pw_i[:, :, :, None] * bbr[None]
    p_r, p_i = p_r.transpose(1, 0, 3, 2), p_i.transpose(1, 0, 3, 2)
    q_r = cpr[1:].transpose(1, 3, 0, 2)
    q_i = -cpi[1:].transpose(1, 3, 0, 2)
    sup = SSM_SUPER
    nsg = ng // sup
    lw = L * sup * gc
    bf = lambda a: a.astype(BF16)
    lag_c = bf(kk).reshape(L, nsg, sup, gc, gc).transpose(1, 0, 2, 4, 3).reshape(nsg, L, sup * gc, gc)
    own_lag = np.arange(sup * gc)[:, None] // gc == np.arange(sup * gc)[None, :] // gc
    lag_sg = jnp.where(own_lag, jnp.matmul(lag_c, jnp.asarray(np.tile(np.eye(gc), (1, sup)), BF16)), 0)
    slab_rows = lambda a: bf(a).reshape(nsg, sup, L, gc, -1).transpose(0, 2, 1, 3, 4).reshape(nsg, lw, -1)
    p_c = jnp.concatenate([slab_rows(p_r), slab_rows(p_i)], axis=2)
    rep_p = np.kron(np.eye(2), np.tile(np.eye(ns), (1, sup)))
    own_p = (np.arange(lw)[:, None] // gc) % sup == (np.arange(2 * sup * ns)[None, :] // ns) % sup
    p = jnp.where(own_p, jnp.matmul(p_c, jnp.asarray(rep_p, BF16)), 0)
    q_c = jnp.concatenate([bf(q_r).reshape(nsg, sup * ns, L * gc), bf(q_i).reshape(nsg, sup * ns, L * gc)], axis=1)
    rep_q = np.einsum('ts,cd,b->tcsbd', np.eye(L), np.eye(gc), np.ones(sup)).reshape(L * gc, lw)
    own_q = (np.arange(2 * sup * ns)[:, None] // ns) % sup == (np.arange(lw)[None, :] // gc) % sup
    q = jnp.where(own_q, jnp.matmul(q_c, jnp.asarray(rep_q, BF16)), 0)
    return lag_sg, p, q, pr[L].reshape(nsg, 1, sup * ns), pi[L].reshape(nsg, 1, sup * ns)


def kernel(x, norm_mix_g, w_in, cmp_pe, cmp_k_w1, cmp_k_w2, cmp_v_w1, cmp_v_w2, ssm_lam_re, ssm_lam_im, ssm_log_step, ssm_b_re, ssm_b_im, ssm_c_re, ssm_c_im, ssm_d, w_attn_branch, w_ssm_val, w_ssm_gate, w_out, norm_mlp_g, w_up, w_down, norm_final_g):
    batch, seq, d = x.shape
    depth = w_in.shape[0]
    assert depth == 1, "the final rmsnorm is fused into the single layer's mlp kernel"
    nsa_w = NSA_HEADS * HEAD_DIM
    ssm_w = ssm_d.shape[1]
    o_q, o_kv, o_g = nsa_w, nsa_w + 6 * KV_LANES, nsa_w + 6 * KV_LANES + 3 * NSA_HEADS
    o_u = o_g + ssm_w
    cos, slo, shi = _rope_tables(seq)
    ovt = _selection_constants(seq)
    head_order = np.array([g * HEADS_PER_GROUP + h for h in range(HEADS_PER_GROUP) for g in range(NSA_KV_HEADS)])
    x2 = x.reshape(batch * seq, d)
    for l in range(depth):
        wl = w_in[l]
        wq = _interleave_heads(wl[:, :o_q]).astype(BF16)
        wkv = wl[:, o_q:o_kv].astype(BF16)
        wg = jnp.pad(wl[:, o_kv:o_g], ((0, 0), (0, V7X_LANES - 3 * NSA_HEADS))).astype(BF16)
        wu = wl[:, o_g:o_u].astype(BF16)
        wga = wl[:, o_u:o_u + d].astype(BF16)
        wgb = wl[:, o_u + d:].astype(BF16)
        g_mix = norm_mix_g[l].reshape(1, d)
        qraw, qrot, kc_raw, vc_raw, ksat, vs, kwt, vw, gates, u = _proj_call(
            x2, g_mix, wq, wkv, wg, wu, cos, slo, shi, batch, seq)

        pea, peb, kw1a, kw1b, kw2 = _compress_weights(cmp_pe[l], cmp_k_w1[l], cmp_k_w2[l])
        _, _, vw1a, vw1b, vw2 = _compress_weights(cmp_pe[l], cmp_v_w1[l], cmp_v_w2[l])
        kct, vc = _compress_call(kc_raw, vc_raw, pea, peb, kw1a, kw1b, kw2.T, vw1a, vw1b, vw2, batch, seq)

        b3 = lambda a: a.reshape(batch, seq, a.shape[-1])
        attn = _nsa_call(b3(qraw), b3(qrot), b3(gates), kct, vc, ksat, b3(vs), kwt, b3(vw), ovt)
        attn = attn.reshape(batch * seq, nsa_w)

        lag, pm, qm, lr, li = _s5_matrices(ssm_lam_re[l], ssm_lam_im[l], ssm_log_step[l], ssm_b_re[l], ssm_b_im[l],
                                           ssm_c_re[l], ssm_c_im[l])
        y_ssm = _s5_call(u, lag, pm, qm, lr, li, ssm_d[l].reshape(1, ssm_w), batch, seq)

        wattn = w_attn_branch[l].reshape(NSA_HEADS, HEAD_DIM, d)[head_order].reshape(nsa_w, d).astype(BF16)
        x1 = _merge_call(x2, g_mix, attn, y_ssm, wga, wgb, wattn, w_ssm_val[l].astype(BF16),
                         w_ssm_gate[l].astype(BF16), w_out[l].astype(BF16))
        x2 = _mlp_call(x1, norm_mlp_g[l].reshape(1, d), w_up[l].astype(BF16), w_down[l].astype(BF16),
                       norm_final_g.reshape(1, d))
    return x2.reshape(batch, seq, d)
```

```python
import functools
import math

import jax
import jax.numpy as jnp
import numpy as np
from jax import lax
from jax.experimental import pallas as pl
from jax.experimental.pallas import tpu as pltpu

NSA_HEADS = 8
NSA_KV_HEADS = 2
HEAD_DIM = 64
CMP_BLOCK = 32
CMP_STRIDE = 16
CMP_HIDDEN = 256
SEL_BLOCK = 64
N_SEL = 16
WINDOW = 512
FORCE_BONUS = 1e3
NEG_INF = -1e30
ROPE_THETA = 500000.0
ROPE_DIM = HEAD_DIM // 4
SSM_GROUP = 16
SSM_STATE = 64
EPS = 1e-6

HEADS_PER_GROUP = NSA_HEADS // NSA_KV_HEADS
KV_LANES = NSA_KV_HEADS * HEAD_DIM
Q_LANES = NSA_HEADS * HEAD_DIM

V7X_LANES = 128
V7X_MXU = 256
V7X_VMEM_BYTES = 64 * 1024 * 1024
VMEM_LIMIT = V7X_VMEM_BYTES - 8 * 1024 * 1024

ROW_TILE = 1024
SEL_TILE = 1024
NSA_QUERIES = 256
SSM_CHUNK = 8
SSM_SUPER = V7X_LANES // SSM_GROUP
FF_CHUNK = 1024

LOG2E = math.log2(math.e)
MASK_BIAS = 1e30

BF16 = jnp.bfloat16
F32 = jnp.float32
_NT = (((1,), (1,)), ((), ()))


def _cparams(semantics):
    return pltpu.CompilerParams(dimension_semantics=semantics, vmem_limit_bytes=VMEM_LIMIT)


def _rmsnorm(x, g):
    return x * lax.rsqrt(jnp.mean(x * x, axis=-1, keepdims=True) + EPS) * g


def _gelu(x):
    return jax.nn.gelu(x)


def _rope_cols(x, cos, sin_lo, sin_hi):
    cols = []
    for c in range(x.shape[1] // V7X_LANES):
        xc = x[:, c * V7X_LANES:(c + 1) * V7X_LANES]
        up = pltpu.roll(xc, V7X_LANES - ROPE_DIM // 2, axis=1)
        dn = pltpu.roll(xc, ROPE_DIM // 2, axis=1)
        cols.append(xc * cos + up * sin_lo + dn * sin_hi)
    return jnp.concatenate(cols, axis=1) if len(cols) > 1 else cols[0]


def _values_with_ones(v):
    lane_head = lax.broadcasted_iota(jnp.int32, v.shape, 1) // HEAD_DIM
    return jnp.concatenate([jnp.where(lane_head == g, v, 1.0) for g in range(NSA_KV_HEADS)], axis=1).astype(BF16)


def _proj_kernel(x_ref, g_ref, wq_ref, wkv_ref, wg_ref, wu_ref, cos_ref, slo_ref, shi_ref,
                 qraw_ref, qrot_ref, kc_ref, vc_ref, ksat_ref, vs_ref, kwt_ref, vw_ref, gate_ref, u_ref, *, seq_tiles):
    hb = _rmsnorm(x_ref[...], g_ref[...]).astype(BF16)
    cos, slo, shi = cos_ref[...], slo_ref[...], shi_ref[...]
    q = jnp.dot(hb, wq_ref[...], preferred_element_type=F32) * (HEAD_DIM ** -0.5 * LOG2E)
    qraw_ref[...] = q.astype(BF16)
    qrot_ref[...] = _rope_cols(q, cos, slo, shi).astype(BF16)
    kv = jnp.dot(hb, wkv_ref[...], preferred_element_type=F32)
    w = KV_LANES
    kc_ref[...] = kv[:, 0 * w:1 * w]
    vc_ref[...] = kv[:, 1 * w:2 * w]
    tm = x_ref.shape[0]
    pos = (pl.program_id(0) % seq_tiles) * tm + lax.broadcasted_iota(jnp.int32, (V7X_LANES, tm), 1)
    onehot = jnp.where(lax.broadcasted_iota(jnp.int32, (V7X_LANES, tm), 0) == pos // SEL_BLOCK, 1.0, 0.0)
    ksat_ref[0, :w, :] = _rope_cols(kv[:, 2 * w:3 * w], cos, slo, shi).T.astype(BF16)
    ksat_ref[0, w:, :] = onehot.astype(BF16)
    vs_ref[...] = _values_with_ones(kv[:, 3 * w:4 * w])
    kwt_ref[0] = _rope_cols(kv[:, 4 * w:5 * w], cos, slo, shi).T.astype(BF16)
    vw_ref[...] = _values_with_ones(kv[:, 5 * w:6 * w])
    gate_ref[...] = jax.nn.sigmoid(jnp.dot(hb, wg_ref[...], preferred_element_type=F32))
    u_ref[...] = jnp.dot(hb, wu_ref[...], preferred_element_type=F32)


def _proj_call(x2, g, wq, wkv, wg, wu, cos, slo, shi, batch, seq):
    t, d = x2.shape
    tm = ROW_TILE
    s_tiles = seq // tm
    row = lambda i: (i, 0)
    pos = lambda i: (i % s_tiles, 0)
    trans = lambda i: (i // s_tiles, 0, i % s_tiles)
    wspec = lambda a: pl.BlockSpec(a.shape, lambda i: (0, 0), pipeline_mode=pl.Buffered(1))
    rows_out = lambda n, dt: (jax.ShapeDtypeStruct((t, n), dt), pl.BlockSpec((tm, n), row))
    trans_out = lambda n: (jax.ShapeDtypeStruct((batch, n, seq), BF16), pl.BlockSpec((1, n, tm), trans))
    outs = [rows_out(Q_LANES, BF16), rows_out(Q_LANES, BF16), rows_out(KV_LANES, F32), rows_out(KV_LANES, F32),
            trans_out(KV_LANES + V7X_LANES), rows_out(NSA_KV_HEADS * KV_LANES, BF16),
            trans_out(KV_LANES), rows_out(NSA_KV_HEADS * KV_LANES, BF16),
            rows_out(V7X_LANES, F32), rows_out(wu.shape[1], F32)]
    assert seq // SEL_BLOCK <= V7X_LANES, "the selection-block one-hot must fit one lane tile"
    return pl.pallas_call(
        functools.partial(_proj_kernel, seq_tiles=s_tiles),
        grid=(t // tm,),
        in_specs=[pl.BlockSpec((tm, d), row), wspec(g), wspec(wq), wspec(wkv), wspec(wg), wspec(wu),
                  pl.BlockSpec((tm, V7X_LANES), pos), pl.BlockSpec((tm, V7X_LANES), pos),
                  pl.BlockSpec((tm, V7X_LANES), pos)],
        out_specs=[o[1] for o in outs],
        out_shape=[o[0] for o in outs],
        compiler_params=_cparams(("arbitrary",)),
        name="proj",
    )(x2, g, wq, wkv, wg, wu, cos, slo, shi)


def _compress_kernel(k_ref, v_ref, pea_ref, peb_ref, kw1a_ref, kw1b_ref, kw2t_ref, vw1a_ref, vw1b_ref, vw2_ref,
                     kct_ref, vc_ref, ca_ref, cb_ref):
    nch = ca_ref.shape[0]

    def hidden(src_ref, w1a_ref, w1b_ref):
        for t in range(CMP_STRIDE):
            rows = src_ref[pl.ds(t, nch, stride=CMP_STRIDE), :]
            sl = slice(t * KV_LANES, (t + 1) * KV_LANES)
            ca_ref[:, sl] = (rows + pea_ref[:, sl]).astype(BF16)
            cb_ref[:, sl] = (rows + peb_ref[:, sl]).astype(BF16)
        ha = jnp.dot(ca_ref[...], w1a_ref[...], preferred_element_type=F32)
        hb = jnp.dot(cb_ref[...], w1b_ref[...], preferred_element_type=F32)
        return _gelu(ha + pltpu.roll(hb, nch - 1, axis=0)).astype(BF16)

    kct_ref[0] = lax.dot_general(kw2t_ref[...], hidden(k_ref, kw1a_ref, kw1b_ref), _NT,
                                 preferred_element_type=F32).astype(BF16)
    vc_ref[0] = jnp.dot(hidden(v_ref, vw1a_ref, vw1b_ref), vw2_ref[...], preferred_element_type=F32).astype(BF16)


def _compress_call(kc_raw, vc_raw, pea, peb, kw1a, kw1b, kw2, vw1a, vw1b, vw2, batch, seq):
    nch = seq // CMP_STRIDE
    wspec = lambda a: pl.BlockSpec(a.shape, lambda b: (0, 0))
    return pl.pallas_call(
        _compress_kernel,
        grid=(batch,),
        in_specs=[pl.BlockSpec((seq, KV_LANES), lambda b: (b, 0)), pl.BlockSpec((seq, KV_LANES), lambda b: (b, 0)),
                  wspec(pea), wspec(peb), wspec(kw1a), wspec(kw1b), wspec(kw2), wspec(vw1a), wspec(vw1b), wspec(vw2)],
        out_specs=[pl.BlockSpec((1, KV_LANES, nch), lambda b: (b, 0, 0)),
                   pl.BlockSpec((1, nch, KV_LANES), lambda b: (b, 0, 0))],
        out_shape=[jax.ShapeDtypeStruct((batch, KV_LANES, nch), BF16), jax.ShapeDtypeStruct((batch, nch, KV_LANES), BF16)],
        scratch_shapes=[pltpu.VMEM((nch, CMP_STRIDE * KV_LANES), BF16), pltpu.VMEM((nch, CMP_STRIDE * KV_LANES), BF16)],
        compiler_params=_cparams(("arbitrary",)),
        name="compress",
    )(kc_raw, vc_raw, pea, peb, kw1a, kw1b, kw2, vw1a, vw1b, vw2)


def _window_keys(nq):
    return WINDOW + nq


def _nsa_kernel(qraw_ref, qrot_ref, gate_ref, kct_ref, vc_ref, ksat_ref, vs_ref, kwt_ref, vw_ref, ovt_ref,
                o_ref, score_ref, qa_ref, *, n_sel):
    i = pl.program_id(1)
    nq = qraw_ref.shape[1]
    hg = HEADS_PER_GROUP
    rows = hg * nq
    lanes_gq = NSA_KV_HEADS * nq
    nb = score_ref.shape[0]
    nc = kct_ref.shape[2]
    t0 = i * nq
    blk_first = t0 // SEL_BLOCK
    blk_last = blk_first + nq // SEL_BLOCK - 1
    tq = t0 + lax.broadcasted_iota(jnp.int32, (nq, 1), 0)
    tq_rows = t0 + lax.broadcasted_iota(jnp.int32, (rows, 1), 0) % nq

    def add_per_query(s, bias):
        return (s.reshape(hg, nq, s.shape[-1]) + bias[None]).reshape(s.shape)

    own_lanes = [lax.broadcasted_iota(jnp.int32, (nq, KV_LANES), 1) // HEAD_DIM == g for g in range(NSA_KV_HEADS)]

    def group_q(ref, g):
        return jnp.concatenate([jnp.where(own_lanes[g], ref[0, :, h * KV_LANES:(h + 1) * KV_LANES], 0.0).astype(BF16)
                                for h in range(hg)], axis=0)

    def normalized(pv):
        return pv * (1.0 / pltpu.roll(pv, HEAD_DIM, axis=1))

    nwin = _window_keys(nq)
    w0 = pl.multiple_of(jnp.maximum(blk_first - WINDOW // SEL_BLOCK, 0) * SEL_BLOCK, V7X_LANES)
    kp = w0 + lax.broadcasted_iota(jnp.int32, (1, nwin), 1)
    bias_w = jnp.where(kp <= tq, jnp.where(kp > tq - WINDOW, 0.0, NEG_INF), NEG_INF)
    o_win = []
    for g in range(NSA_KV_HEADS):
        s = jnp.dot(group_q(qrot_ref, g), kwt_ref[0, :, pl.ds(w0, nwin)], preferred_element_type=F32)
        s = add_per_query(s, bias_w)
        e = jnp.exp2(s - jnp.max(s, axis=-1, keepdims=True))
        v = vw_ref[0, pl.ds(w0, nwin), g * KV_LANES:(g + 1) * KV_LANES]
        o_win.append(normalized(jnp.dot(e.astype(BF16), v, preferred_element_type=F32)))

    def compressed_branch(width):
        cmp_end = lax.broadcasted_iota(jnp.int32, (1, width), 1) * CMP_STRIDE + (CMP_BLOCK - 1)
        bias_c = jnp.where(cmp_end <= tq, 0.0, NEG_INF)
        any_valid = tq_rows >= CMP_BLOCK - 1
        outs, p_sum = [], []
        for g in range(NSA_KV_HEADS):
            s = jnp.dot(group_q(qraw_ref, g), kct_ref[0, :, :width], preferred_element_type=F32)
            s = add_per_query(s, bias_c)
            e = jnp.exp2(s - jnp.max(s, axis=-1, keepdims=True))
            p = e * jnp.where(any_valid, 1.0 / jnp.sum(e, axis=-1, keepdims=True), 0.0)
            outs.append(jnp.dot(p.astype(BF16), vc_ref[0, :width, :], preferred_element_type=F32))
            ph = p[0:nq]
            for h in range(1, hg):
                ph = ph + p[h * nq:(h + 1) * nq]
            p_sum.append(ph)
        p2 = jnp.concatenate(p_sum, axis=0)
        p_hi = p2.astype(BF16)
        p_lo = (p2 - p_hi.astype(F32)).astype(BF16)
        ovt = ovt_ref[:, :width]
        imp = (lax.dot_general(ovt, p_hi, _NT, preferred_element_type=F32)
               + lax.dot_general(ovt, p_lo, _NT, preferred_element_type=F32))
        return tuple(outs) + (imp,)

    *o_cmp, imp_t = compressed_branch(nc)

    blk_q = (t0 + lax.broadcasted_iota(jnp.int32, (1, lanes_gq), 1) % nq) // SEL_BLOCK
    jfull = lax.broadcasted_iota(jnp.int32, (nb, lanes_gq), 0)
    valid = jfull <= blk_q
    forced = (jfull == 0) | (jfull == blk_q) | (jfull == blk_q - 1)
    score = jnp.where(valid, imp_t + jnp.where(forced, FORCE_BONUS, 0.0), NEG_INF)

    free = jnp.where(valid, jnp.where(forced, NEG_INF, imp_t), NEG_INF)
    n_forced = 1 + jnp.where(blk_q >= 1, 1, 0) + jnp.where(blk_q >= 2, 1, 0)
    rest, cut = free, {}
    for k in range(1, n_sel):
        thr = jnp.max(rest, axis=0, keepdims=True)
        rest = jnp.where(rest >= thr, NEG_INF, rest)
        cut[k] = thr
    thr = cut[n_sel - 1]
    for nf in (2, 3):
        thr = jnp.where(n_forced == nf, cut[n_sel - nf], thr)
    sel_t = jnp.where(valid, jnp.where(forced, 1.0, jnp.where(free >= thr, 1.0, 0.0)), 0.0)

    def ranked_members():
        def count_above_or_tied_earlier(jp, cnt):
            row = jnp.broadcast_to(score_ref[pl.ds(jp, 1), :], score.shape)
            tie = jnp.where(jp < jfull, 1.0, 0.0)
            return cnt + jnp.where(row > score, 1.0, jnp.where(row == score, tie, 0.0))

        score_ref[...] = score
        cnt = lax.fori_loop(0, blk_last + 1, count_above_or_tied_earlier, jnp.zeros(score.shape, F32))
        return jnp.where(valid, jnp.where(cnt < float(n_sel), 1.0, 0.0), 0.0)

    picked = jnp.sum(sel_t, axis=0, keepdims=True)
    wrong = jnp.sum(jnp.where(picked == jnp.minimum(blk_q + 1, n_sel).astype(F32), 0.0, 1.0))
    sel_t = lax.cond(wrong == 0.0, lambda: sel_t, ranked_members)
    bias = ((sel_t.T - 1.0) * MASK_BIAS).astype(BF16)

    n_tiles = blk_last // (SEL_TILE // SEL_BLOCK) + 1
    lane_t = lax.broadcasted_iota(jnp.int32, (1, SEL_TILE), 1)
    for g in range(NSA_KV_HEADS):
        qa_ref[g * rows:(g + 1) * rows, :KV_LANES] = group_q(qrot_ref, g)
        qa_ref[g * rows:(g + 1) * rows, KV_LANES:] = jnp.concatenate([bias[g * nq:(g + 1) * nq]] * hg, axis=0)

    def tile_body(c, carry, causal):
        k0 = pl.multiple_of(c * SEL_TILE, SEL_TILE)
        ka = ksat_ref[0, :, pl.ds(k0, SEL_TILE)]
        if causal:
            bias_d = jnp.where((k0 + lane_t) <= tq, 0.0, NEG_INF)
        new = []
        for g in range(NSA_KV_HEADS):
            m, acc = carry[g]
            s = jnp.dot(qa_ref[g * rows:(g + 1) * rows], ka, preferred_element_type=F32)
            if causal:
                s = add_per_query(s, bias_d)
            m_new = jnp.maximum(m, jnp.max(s, axis=-1, keepdims=True))
            p = jnp.exp2(s - m_new)
            v = vs_ref[0, pl.ds(k0, SEL_TILE), g * KV_LANES:(g + 1) * KV_LANES]
            acc = jnp.exp2(m - m_new) * acc + jnp.dot(p.astype(BF16), v, preferred_element_type=F32)
            new.append((m_new, acc))
        return tuple(new)

    init = tuple((jnp.full((rows, 1), NEG_INF, F32), jnp.zeros((rows, KV_LANES), F32)) for _ in range(NSA_KV_HEADS))
    carry = lax.fori_loop(0, n_tiles - 1, functools.partial(tile_body, causal=False), init)
    carry = tile_body(n_tiles - 1, carry, causal=True)
    o_sel = [normalized(acc) for _, acc in carry]

    gt = gate_ref[0]
    for h in range(hg):
        r = slice(h * nq, (h + 1) * nq)
        parts = []
        for g in range(NSA_KV_HEADS):
            c = (g * hg + h) * 3
            parts.append(gt[:, c:c + 1] * o_cmp[g][r] + gt[:, c + 1:c + 2] * o_sel[g][r] + gt[:, c + 2:c + 3] * o_win[g][r])
        out = parts[0]
        for g in range(1, NSA_KV_HEADS):
            out = jnp.where(own_lanes[g], parts[g], out)
        o_ref[0, :, h * KV_LANES:(h + 1) * KV_LANES] = out.astype(BF16)


def _nsa_call(qraw, qrot, gates, kct, vc, ksat, vs, kwt, vw, ovt):
    batch, seq, _ = qraw.shape
    nq = NSA_QUERIES
    assert nq % V7X_LANES == 0 and seq % nq == 0 and seq >= _window_keys(nq) and seq % SEL_TILE == 0
    step = lambda n: pl.BlockSpec((1, nq, n), lambda b, i: (b, i, 0))
    whole = lambda a: pl.BlockSpec((1,) + a.shape[1:], lambda b, i: (b, 0, 0))
    return pl.pallas_call(
        functools.partial(_nsa_kernel, n_sel=min(N_SEL, seq // SEL_BLOCK)),
        grid=(batch, seq // nq),
        in_specs=[step(Q_LANES), step(Q_LANES), step(V7X_LANES),
                  whole(kct), whole(vc), whole(ksat), whole(vs), whole(kwt), whole(vw),
                  pl.BlockSpec(ovt.shape, lambda b, i: (0, 0))],
        out_specs=step(HEADS_PER_GROUP * KV_LANES),
        out_shape=jax.ShapeDtypeStruct((batch, seq, HEADS_PER_GROUP * KV_LANES), BF16),
        scratch_shapes=[pltpu.VMEM((V7X_LANES, NSA_KV_HEADS * nq), F32),
                        pltpu.VMEM((NSA_KV_HEADS * HEADS_PER_GROUP * nq, KV_LANES + V7X_LANES), BF16)],
        compiler_params=_cparams(("arbitrary", "arbitrary")),
        name="nsa",
    )(qraw, qrot, gates, kct, vc, ksat, vs, kwt, vw, ovt)


def _s5_kernel(u_ref, lag_ref, p_ref, q_ref, lr_ref, li_ref, d_ref, o_ref, m_ref, x_ref, y_ref, z_ref, sp_ref):
    nch = x_ref.shape[0]
    half = z_ref.shape[1] // 2
    lanes = V7X_LANES

    @pl.when(pl.program_id(1) == 0)
    def _():
        for s in range(SSM_CHUNK):
            for t in range(SSM_CHUNK):
                tile = lag_ref[0, t - s] if t >= s else jnp.zeros((lanes, lanes), BF16)
                m_ref[s * lanes:(s + 1) * lanes, t * lanes:(t + 1) * lanes] = tile

    for t in range(SSM_CHUNK):
        x_ref[:, t * lanes:(t + 1) * lanes] = u_ref[pl.ds(t, nch, stride=SSM_CHUNK), :].astype(BF16)
    for c0 in range(0, SSM_CHUNK * lanes, V7X_MXU):
        c1 = c0 + V7X_MXU
        y_ref[:, c0:c1] = jnp.dot(x_ref[:, :c1], m_ref[:c1, c0:c1], preferred_element_type=F32)
    z_ref[...] = jnp.dot(x_ref[...], p_ref[0], preferred_element_type=F32)
    lr, li = lr_ref[0], li_ref[0]

    def scan_body(k, carry):
        sr, si = carry
        sp_ref[pl.ds(k, 1), 0:half] = sr
        sp_ref[pl.ds(k, 1), half:2 * half] = si
        zr = z_ref[pl.ds(k, 1), 0:half]
        zi = z_ref[pl.ds(k, 1), half:2 * half]
        return lr * sr - li * si + zr, lr * si + li * sr + zi

    zero = jnp.zeros((1, half), F32)
    lax.fori_loop(0, nch, scan_body, (zero, zero), unroll=8)
    y = y_ref[...] + jnp.dot(sp_ref[...].astype(BF16), q_ref[0], preferred_element_type=F32)
    d = d_ref[...]
    for t in range(SSM_CHUNK):
        yt = y[:, t * lanes:(t + 1) * lanes] + d * u_ref[pl.ds(t, nch, stride=SSM_CHUNK), :]
        o_ref[pl.ds(t, nch, stride=SSM_CHUNK), :] = _gelu(yt)


def _s5_call(u, lag, p, q, lr, li, dskip, batch, seq):
    nsg = lag.shape[0]
    nch = seq // SSM_CHUNK
    nstate = q.shape[1]
    lw = SSM_CHUNK * V7X_LANES
    slab = lambda a: pl.BlockSpec((1,) + a.shape[1:], lambda g, b: (g,) + (0,) * (a.ndim - 1),
                                  pipeline_mode=pl.Buffered(1))
    return pl.pallas_call(
        _s5_kernel,
        grid=(nsg, batch),
        in_specs=[pl.BlockSpec((seq, V7X_LANES), lambda g, b: (b, g)), slab(lag), slab(p), slab(q),
                  pl.BlockSpec((1, 1, nstate // 2), lambda g, b: (g, 0, 0)),
                  pl.BlockSpec((1, 1, nstate // 2), lambda g, b: (g, 0, 0)),
                  pl.BlockSpec((1, V7X_LANES), lambda g, b: (0, g))],
        out_specs=pl.BlockSpec((seq, V7X_LANES), lambda g, b: (b, g)),
        out_shape=jax.ShapeDtypeStruct(u.shape, F32),
        scratch_shapes=[pltpu.VMEM((lw, lw), BF16), pltpu.VMEM((nch, lw), BF16), pltpu.VMEM((nch, lw), F32),
                        pltpu.VMEM((nch, nstate), F32), pltpu.VMEM((nch, nstate), F32)],
        compiler_params=_cparams(("arbitrary", "arbitrary")),
        name="s5",
    )(u, lag, p, q, lr, li, dskip)


def _merge_kernel(x_ref, g_ref, attn_ref, ssm_ref, wga_ref, wgb_ref, wattn_ref, wval_ref, wgate_ref, wout_ref, o_ref):
    x = x_ref[...]
    hb = _rmsnorm(x, g_ref[...]).astype(BF16)
    dot = functools.partial(jnp.dot, preferred_element_type=F32)
    y_a = dot(attn_ref[...], wattn_ref[...])
    ys = ssm_ref[...].astype(BF16)
    y_b = dot(ys, wval_ref[...]) * jax.nn.sigmoid(dot(ys, wgate_ref[...]))
    merged = jax.nn.sigmoid(dot(hb, wga_ref[...])) * y_a + jax.nn.sigmoid(dot(hb, wgb_ref[...])) * y_b
    o_ref[...] = x + dot(merged.astype(BF16), wout_ref[...])


def _merge_call(x2, g, attn, ssm, wga, wgb, wattn, wval, wgate, wout):
    t, d = x2.shape
    tm = ROW_TILE
    row = lambda i: (i, 0)
    wspec = lambda a: pl.BlockSpec(a.shape, lambda i: (0, 0), pipeline_mode=pl.Buffered(1))
    return pl.pallas_call(
        _merge_kernel,
        grid=(t // tm,),
        in_specs=[pl.BlockSpec((tm, d), row), wspec(g), pl.BlockSpec((tm, attn.shape[1]), row),
                  pl.BlockSpec((tm, ssm.shape[1]), row), wspec(wga), wspec(wgb), wspec(wattn), wspec(wval),
                  wspec(wgate), wspec(wout)],
        out_specs=pl.BlockSpec((tm, d), row),
        out_shape=jax.ShapeDtypeStruct((t, d), F32),
        compiler_params=_cparams(("arbitrary",)),
        name="merge",
    )(x2, g, attn, ssm, wga, wgb, wattn, wval, wgate, wout)


def _mlp_kernel(x_ref, g_ref, wup_ref, wdown_ref, gf_ref, o_ref):
    x = x_ref[...]
    hb = _rmsnorm(x, g_ref[...]).astype(BF16)
    acc = x
    for c in range(wup_ref.shape[1] // FF_CHUNK):
        sl = slice(c * FF_CHUNK, (c + 1) * FF_CHUNK)
        up = jnp.maximum(jnp.dot(hb, wup_ref[:, sl], preferred_element_type=F32), 0.0)
        acc = acc + jnp.dot((up * up).astype(BF16), wdown_ref[sl, :], preferred_element_type=F32)
    o_ref[...] = _rmsnorm(acc, gf_ref[...])


def _mlp_call(x1, g, wup, wdown, gf):
    t, d = x1.shape
    tm = ROW_TILE
    row = lambda i: (i, 0)
    wspec = lambda a: pl.BlockSpec(a.shape, lambda i: (0, 0), pipeline_mode=pl.Buffered(1))
    return pl.pallas_call(
        _mlp_kernel,
        grid=(t // tm,),
        in_specs=[pl.BlockSpec((tm, d), row), wspec(g), wspec(wup), wspec(wdown), wspec(gf)],
        out_specs=pl.BlockSpec((tm, d), row),
        out_shape=jax.ShapeDtypeStruct((t, d), F32),
        compiler_params=_cparams(("arbitrary",)),
        name="mlp",
    )(x1, g, wup, wdown, gf)


def _interleave_heads(wq):
    d = wq.shape[0]
    return wq.reshape(d, NSA_KV_HEADS, HEADS_PER_GROUP, HEAD_DIM).transpose(0, 2, 1, 3).reshape(d, Q_LANES)


def _rope_tables(seq):
    half = ROPE_DIM // 2
    inv = ROPE_THETA ** (-(jnp.arange(half, dtype=F32) * 2.0) / ROPE_DIM)
    ang = jnp.arange(seq, dtype=F32)[:, None] * inv[None, :]
    cos, sin = jnp.cos(ang), jnp.sin(ang)
    rest = HEAD_DIM - ROPE_DIM
    cos_h = jnp.concatenate([cos, cos, jnp.ones((seq, rest), F32)], axis=1)
    slo_h = jnp.concatenate([-sin, jnp.zeros((seq, half + rest), F32)], axis=1)
    shi_h = jnp.concatenate([jnp.zeros((seq, half), F32), sin, jnp.zeros((seq, rest), F32)], axis=1)
    reps = V7X_LANES // HEAD_DIM
    return jnp.tile(cos_h, (1, reps)), jnp.tile(slo_h, (1, reps)), jnp.tile(shi_h, (1, reps))


def _compress_weights(pe, w1, w2):
    eye = jnp.eye(NSA_KV_HEADS, dtype=F32)
    w1e = jnp.einsum('tdj,gk->tgdkj', w1.reshape(CMP_BLOCK, HEAD_DIM, CMP_HIDDEN), eye)
    w1e = w1e.reshape(CMP_BLOCK * KV_LANES, NSA_KV_HEADS * CMP_HIDDEN).astype(BF16)
    w2e = jnp.einsum('jd,gk->gjkd', w2, eye).reshape(NSA_KV_HEADS * CMP_HIDDEN, KV_LANES).astype(BF16)
    pee = jnp.tile(pe, (1, NSA_KV_HEADS)).reshape(1, CMP_BLOCK * KV_LANES)
    halfw = CMP_STRIDE * KV_LANES
    return pee[:, :halfw], pee[:, halfw:], w1e[:halfw], w1e[halfw:], w2e


def _selection_constants(seq):
    nc = seq // CMP_STRIDE - 1
    nb = seq // SEL_BLOCK
    n_np = np.arange(nc)[:, None] * CMP_STRIDE
    j_np = np.arange(nb)[None, :] * SEL_BLOCK
    overlap = ((n_np < j_np + SEL_BLOCK) & (n_np + CMP_BLOCK > j_np)).astype(np.float32)
    ovt = np.zeros((V7X_LANES, nc + 1), np.float32)
    ovt[:nb, :nc] = overlap.T
    return jnp.asarray(ovt, BF16)


def _s5_matrices(lam_re, lam_im, log_step, b_re, b_im, c_re, c_im):
    hp = lax.Precision.HIGHEST
    ng, ns = lam_re.shape
    gc = b_re.shape[-1]
    L = SSM_CHUNK
    step = jnp.exp(log_step)[:, None]
    a, b = lam_re * step, lam_im * step
    k = jnp.arange(L + 1, dtype=F32)[:, None, None]
    mag = jnp.exp(a[None] * k)
    pr, pi = mag * jnp.cos(b[None] * k), mag * jnp.sin(b[None] * k)
    nr, ni = pr[1] - 1.0, pi[1]
    den = lam_re * lam_re + lam_im * lam_im
    cr, ci = (nr * lam_re + ni * lam_im) / den, (ni * lam_re - nr * lam_im) / den
    bbr = cr[..., None] * b_re - ci[..., None] * b_im
    bbi = cr[..., None] * b_im + ci[..., None] * b_re
    cpr = c_re[None] * pr[:, :, None, :] - c_im[None] * pi[:, :, None, :]
    cpi = c_re[None] * pi[:, :, None, :] + c_im[None] * pr[:, :, None, :]
    kk = jnp.einsum('kgcn,gnd->kgcd', jnp.concatenate([cpr[:L], -cpi[:L]], axis=-1),
                    jnp.concatenate([bbr, bbi], axis=1), precision=hp)
    pw_r, pw_i = pr[L - 1 - np.arange(L)], pi[L - 1 - np.arange(L)]
    p_r = pw_r[:, :, :, None] * bbr[None] - pw_i[:, :, :, None] * bbi[None]
    p_i = pw_r[:, :, :, None] * bbi[None] + pw_i[:, :, :, None] * bbr[None]
    p_r, p_i = p_r.transpose(1, 0, 3, 2), p_i.transpose(1, 0, 3, 2)
    q_r = cpr[1:].transpose(1, 3, 0, 2)
    q_i = -cpi[1:].transpose(1, 3, 0, 2)
    sup = SSM_SUPER
    nsg = ng // sup
    lw = L * sup * gc
    bf = lambda a: a.astype(BF16)
    lag_c = bf(kk).reshape(L, nsg, sup, gc, gc).transpose(1, 0, 2, 4, 3).reshape(nsg, L, sup * gc, gc)
    own_lag = np.arange(sup * gc)[:, None] // gc == np.arange(sup * gc)[None, :] // gc
    lag_sg = jnp.where(own_lag, jnp.matmul(lag_c, jnp.asarray(np.tile(np.eye(gc), (1, sup)), BF16)), 0)
    slab_rows = lambda a: bf(a).reshape(nsg, sup, L, gc, -1).transpose(0, 2, 1, 3, 4).reshape(nsg, lw, -1)
    p_c = jnp.concatenate([slab_rows(p_r), slab_rows(p_i)], axis=2)
    rep_p = np.kron(np.eye(2), np.tile(np.eye(ns), (1, sup)))
    own_p = (np.arange(lw)[:, None] // gc) % sup == (np.arange(2 * sup * ns)[None, :] // ns) % sup
    p = jnp.where(own_p, jnp.matmul(p_c, jnp.asarray(rep_p, BF16)), 0)
    q_c = jnp.concatenate([bf(q_r).reshape(nsg, sup * ns, L * gc), bf(q_i).reshape(nsg, sup * ns, L * gc)], axis=1)
    rep_q = np.einsum('ts,cd,b->tcsbd', np.eye(L), np.eye(gc), np.ones(sup)).reshape(L * gc, lw)
    own_q = (np.arange(2 * sup * ns)[:, None] // ns) % sup == (np.arange(lw)[None, :] // gc) % sup
    q = jnp.where(own_q, jnp.matmul(q_c, jnp.asarray(rep_q, BF16)), 0)
    return lag_sg, p, q, pr[L].reshape(nsg, 1, sup * ns), pi[L].reshape(nsg, 1, sup * ns)


def kernel(x, norm_mix_g, w_in, cmp_pe, cmp_k_w1, cmp_k_w2, cmp_v_w1, cmp_v_w2, ssm_lam_re, ssm_lam_im, ssm_log_step, ssm_b_re, ssm_b_im, ssm_c_re, ssm_c_im, ssm_d, w_attn_branch, w_ssm_val, w_ssm_gate, w_out, norm_mlp_g, w_up, w_down, norm_final_g):
    batch, seq, d = x.shape
    depth = w_in.shape[0]
    assert depth == 1, "the final rmsnorm is fused into the single layer's mlp kernel"
    nsa_w = NSA_HEADS * HEAD_DIM
    ssm_w = ssm_d.shape[1]
    o_q, o_kv, o_g = nsa_w, nsa_w + 6 * KV_LANES, nsa_w + 6 * KV_LANES + 3 * NSA_HEADS
    o_u = o_g + ssm_w
    cos, slo, shi = _rope_tables(seq)
    ovt = _selection_constants(seq)
    head_order = np.array([g * HEADS_PER_GROUP + h for h in range(HEADS_PER_GROUP) for g in range(NSA_KV_HEADS)])
    x2 = x.reshape(batch * seq, d)
    for l in range(depth):
        wl = w_in[l]
        wq = _interleave_heads(wl[:, :o_q]).astype(BF16)
        wkv = wl[:, o_q:o_kv].astype(BF16)
        wg = jnp.pad(wl[:, o_kv:o_g], ((0, 0), (0, V7X_LANES - 3 * NSA_HEADS))).astype(BF16)
        wu = wl[:, o_g:o_u].astype(BF16)
        wga = wl[:, o_u:o_u + d].astype(BF16)
        wgb = wl[:, o_u + d:].astype(BF16)
        g_mix = norm_mix_g[l].reshape(1, d)
        qraw, qrot, kc_raw, vc_raw, ksat, vs, kwt, vw, gates, u = _proj_call(
            x2, g_mix, wq, wkv, wg, wu, cos, slo, shi, batch, seq)

        pea, peb, kw1a, kw1b, kw2 = _compress_weights(cmp_pe[l], cmp_k_w1[l], cmp_k_w2[l])
        _, _, vw1a, vw1b, vw2 = _compress_weights(cmp_pe[l], cmp_v_w1[l], cmp_v_w2[l])
        kct, vc = _compress_call(kc_raw, vc_raw, pea, peb, kw1a, kw1b, kw2.T, vw1a, vw1b, vw2, batch, seq)

        b3 = lambda a: a.reshape(batch, seq, a.shape[-1])
        attn = _nsa_call(b3(qraw), b3(qrot), b3(gates), kct, vc, ksat, b3(vs), kwt, b3(vw), ovt)
        attn = attn.reshape(batch * seq, nsa_w)

        lag, pm, qm, lr, li = _s5_matrices(ssm_lam_re[l], ssm_lam_im[l], ssm_log_step[l], ssm_b_re[l], ssm_b_im[l],
                                           ssm_c_re[l], ssm_c_im[l])
        y_ssm = _s5_call(u, lag, pm, qm, lr, li, ssm_d[l].reshape(1, ssm_w), batch, seq)

        wattn = w_attn_branch[l].reshape(NSA_HEADS, HEAD_DIM, d)[head_order].reshape(nsa_w, d).astype(BF16)
        x1 = _merge_call(x2, g_mix, attn, y_ssm, wga, wgb, wattn, w_ssm_val[l].astype(BF16),
                         w_ssm_gate[l].astype(BF16), w_out[l].astype(BF16))
        x2 = _mlp_call(x1, norm_mlp_g[l].reshape(1, d), w_up[l].astype(BF16), w_down[l].astype(BF16),
                       norm_final_g.reshape(1, d))
    return x2.reshape(batch, seq, d)
```

```python
import functools
import math

import jax
import jax.numpy as jnp
import numpy as np
from jax import lax
from jax.experimental import pallas as pl
from jax.experimental.pallas import tpu as pltpu

NSA_HEADS = 8
NSA_KV_HEADS = 2
HEAD_DIM = 64
CMP_BLOCK = 32
CMP_STRIDE = 16
CMP_HIDDEN = 256
SEL_BLOCK = 64
N_SEL = 16
WINDOW = 512
FORCE_BONUS = 1e3
NEG_INF = -1e30
ROPE_THETA = 500000.0
ROPE_DIM = HEAD_DIM // 4
SSM_GROUP = 16
SSM_STATE = 64
EPS = 1e-6

HEADS_PER_GROUP = NSA_HEADS // NSA_KV_HEADS
KV_LANES = NSA_KV_HEADS * HEAD_DIM
Q_LANES = NSA_HEADS * HEAD_DIM

V7X_LANES = 128
V7X_MXU = 256
V7X_VMEM_BYTES = 64 * 1024 * 1024
VMEM_LIMIT = V7X_VMEM_BYTES - 8 * 1024 * 1024

ROW_TILE = 1024
SEL_TILE = 1024
NSA_QUERIES = 256
SSM_CHUNK = 8
SSM_SUPER = V7X_LANES // SSM_GROUP
FF_CHUNK = 1024

LOG2E = math.log2(math.e)
MASK_BIAS = 1e30

BF16 = jnp.bfloat16
F32 = jnp.float32
_NT = (((1,), (1,)), ((), ()))


def _cparams(semantics):
    return pltpu.CompilerParams(dimension_semantics=semantics, vmem_limit_bytes=VMEM_LIMIT)


def _rmsnorm(x, g):
    return x * lax.rsqrt(jnp.mean(x * x, axis=-1, keepdims=True) + EPS) * g


def _gelu(x):
    return jax.nn.gelu(x)


def _rope_cols(x, cos, sin_lo, sin_hi):
    cols = []
    for c in range(x.shape[1] // V7X_LANES):
        xc = x[:, c * V7X_LANES:(c + 1) * V7X_LANES]
        up = pltpu.roll(xc, V7X_LANES - ROPE_DIM // 2, axis=1)
        dn = pltpu.roll(xc, ROPE_DIM // 2, axis=1)
        cols.append(xc * cos + up * sin_lo + dn * sin_hi)
    return jnp.concatenate(cols, axis=1) if len(cols) > 1 else cols[0]


def _values_with_ones(v):
    lane_head = lax.broadcasted_iota(jnp.int32, v.shape, 1) // HEAD_DIM
    return jnp.concatenate([jnp.where(lane_head == g, v, 1.0) for g in range(NSA_KV_HEADS)], axis=1).astype(BF16)


def _proj_kernel(x_ref, g_ref, wq_ref, wkv_ref, wg_ref, wu_ref, cos_ref, slo_ref, shi_ref,
                 qraw_ref, qrot_ref, kc_ref, vc_ref, ksat_ref, vs_ref, kwt_ref, vw_ref, gate_ref, u_ref, *, seq_tiles):
    hb = _rmsnorm(x_ref[...], g_ref[...]).astype(BF16)
    cos, slo, shi = cos_ref[...], slo_ref[...], shi_ref[...]
    q = jnp.dot(hb, wq_ref[...], preferred_element_type=F32) * (HEAD_DIM ** -0.5 * LOG2E)
    qraw_ref[...] = q.astype(BF16)
    qrot_ref[...] = _rope_cols(q, cos, slo, shi).astype(BF16)
    kv = jnp.dot(hb, wkv_ref[...], preferred_element_type=F32)
    w = KV_LANES
    kc_ref[...] = kv[:, 0 * w:1 * w]
    vc_ref[...] = kv[:, 1 * w:2 * w]
    tm = x_ref.shape[0]
    pos = (pl.program_id(0) % seq_tiles) * tm + lax.broadcasted_iota(jnp.int32, (V7X_LANES, tm), 1)
    onehot = jnp.where(lax.broadcasted_iota(jnp.int32, (V7X_LANES, tm), 0) == pos // SEL_BLOCK, 1.0, 0.0)
    ksat_ref[0, :w, :] = _rope_cols(kv[:, 2 * w:3 * w], cos, slo, shi).T.astype(BF16)
    ksat_ref[0, w:, :] = onehot.astype(BF16)
    vs_ref[...] = _values_with_ones(kv[:, 3 * w:4 * w])
    kwt_ref[0] = _rope_cols(kv[:, 4 * w:5 * w], cos, slo, shi).T.astype(BF16)
    vw_ref[...] = _values_with_ones(kv[:, 5 * w:6 * w])
    gate_ref[...] = jax.nn.sigmoid(jnp.dot(hb, wg_ref[...], preferred_element_type=F32))
    u_ref[...] = jnp.dot(hb, wu_ref[...], preferred_element_type=F32)


def _proj_call(x2, g, wq, wkv, wg, wu, cos, slo, shi, batch, seq):
    t, d = x2.shape
    tm = ROW_TILE
    s_tiles = seq // tm
    row = lambda i: (i, 0)
    pos = lambda i: (i % s_tiles, 0)
    trans = lambda i: (i // s_tiles, 0, i % s_tiles)
    wspec = lambda a: pl.BlockSpec(a.shape, lambda i: (0, 0), pipeline_mode=pl.Buffered(1))
    rows_out = lambda n, dt: (jax.ShapeDtypeStruct((t, n), dt), pl.BlockSpec((tm, n), row))
    trans_out = lambda n: (jax.ShapeDtypeStruct((batch, n, seq), BF16), pl.BlockSpec((1, n, tm), trans))
    outs = [rows_out(Q_LANES, BF16), rows_out(Q_LANES, BF16), rows_out(KV_LANES, F32), rows_out(KV_LANES, F32),
            trans_out(KV_LANES + V7X_LANES), rows_out(NSA_KV_HEADS * KV_LANES, BF16),
            trans_out(KV_LANES), rows_out(NSA_KV_HEADS * KV_LANES, BF16),
            rows_out(V7X_LANES, F32), rows_out(wu.shape[1], F32)]
    assert seq // SEL_BLOCK <= V7X_LANES, "the selection-block one-hot must fit one lane tile"
    return pl.pallas_call(
        functools.partial(_proj_kernel, seq_tiles=s_tiles),
        grid=(t // tm,),
        in_specs=[pl.BlockSpec((tm, d), row), wspec(g), wspec(wq), wspec(wkv), wspec(wg), wspec(wu),
                  pl.BlockSpec((tm, V7X_LANES), pos), pl.BlockSpec((tm, V7X_LANES), pos),
                  pl.BlockSpec((tm, V7X_LANES), pos)],
        out_specs=[o[1] for o in outs],
        out_shape=[o[0] for o in outs],
        compiler_params=_cparams(("arbitrary",)),
        name="proj",
    )(x2, g, wq, wkv, wg, wu, cos, slo, shi)


def _compress_kernel(k_ref, v_ref, pea_ref, peb_ref, kw1a_ref, kw1b_ref, kw2t_ref, vw1a_ref, vw1b_ref, vw2_ref,
                     kct_ref, vc_ref, ca_ref, cb_ref):
    nch = ca_ref.shape[0]

    def hidden(src_ref, w1a_ref, w1b_ref):
        for t in range(CMP_STRIDE):
            rows = src_ref[pl.ds(t, nch, stride=CMP_STRIDE), :]
            sl = slice(t * KV_LANES, (t + 1) * KV_LANES)
            ca_ref[:, sl] = (rows + pea_ref[:, sl]).astype(BF16)
            cb_ref[:, sl] = (rows + peb_ref[:, sl]).astype(BF16)
        ha = jnp.dot(ca_ref[...], w1a_ref[...], preferred_element_type=F32)
        hb = jnp.dot(cb_ref[...], w1b_ref[...], preferred_element_type=F32)
        return _gelu(ha + pltpu.roll(hb, nch - 1, axis=0)).astype(BF16)

    kct_ref[0] = lax.dot_general(kw2t_ref[...], hidden(k_ref, kw1a_ref, kw1b_ref), _NT,
                                 preferred_element_type=F32).astype(BF16)
    vc_ref[0] = jnp.dot(hidden(v_ref, vw1a_ref, vw1b_ref), vw2_ref[...], preferred_element_type=F32).astype(BF16)


def _compress_call(kc_raw, vc_raw, pea, peb, kw1a, kw1b, kw2, vw1a, vw1b, vw2, batch, seq):
    nch = seq // CMP_STRIDE
    wspec = lambda a: pl.BlockSpec(a.shape, lambda b: (0, 0))
    return pl.pallas_call(
        _compress_kernel,
        grid=(batch,),
        in_specs=[pl.BlockSpec((seq, KV_LANES), lambda b: (b, 0)), pl.BlockSpec((seq, KV_LANES), lambda b: (b, 0)),
                  wspec(pea), wspec(peb), wspec(kw1a), wspec(kw1b), wspec(kw2), wspec(vw1a), wspec(vw1b), wspec(vw2)],
        out_specs=[pl.BlockSpec((1, KV_LANES, nch), lambda b: (b, 0, 0)),
                   pl.BlockSpec((1, nch, KV_LANES), lambda b: (b, 0, 0))],
        out_shape=[jax.ShapeDtypeStruct((batch, KV_LANES, nch), BF16), jax.ShapeDtypeStruct((batch, nch, KV_LANES), BF16)],
        scratch_shapes=[pltpu.VMEM((nch, CMP_STRIDE * KV_LANES), BF16), pltpu.VMEM((nch, CMP_STRIDE * KV_LANES), BF16)],
        compiler_params=_cparams(("arbitrary",)),
        name="compress",
    )(kc_raw, vc_raw, pea, peb, kw1a, kw1b, kw2, vw1a, vw1b, vw2)


def _window_keys(nq):
    return WINDOW + nq


def _nsa_kernel(qraw_ref, qrot_ref, gate_ref, kct_ref, vc_ref, ksat_ref, vs_ref, kwt_ref, vw_ref, ovt_ref,
                o_ref, score_ref, qa_ref, *, n_sel):
    i = pl.program_id(1)
    nq = qraw_ref.shape[1]
    hg = HEADS_PER_GROUP
    rows = hg * nq
    lanes_gq = NSA_KV_HEADS * nq
    nb = score_ref.shape[0]
    nc = kct_ref.shape[2]
    t0 = i * nq
    blk_first = t0 // SEL_BLOCK
    blk_last = blk_first + nq // SEL_BLOCK - 1
    tq = t0 + lax.broadcasted_iota(jnp.int32, (nq, 1), 0)
    tq_rows = t0 + lax.broadcasted_iota(jnp.int32, (rows, 1), 0) % nq

    def add_per_query(s, bias):
        return (s.reshape(hg, nq, s.shape[-1]) + bias[None]).reshape(s.shape)

    own_lanes = [lax.broadcasted_iota(jnp.int32, (nq, KV_LANES), 1) // HEAD_DIM == g for g in range(NSA_KV_HEADS)]

    def group_q(ref, g):
        return jnp.concatenate([jnp.where(own_lanes[g], ref[0, :, h * KV_LANES:(h + 1) * KV_LANES], 0.0).astype(BF16)
                                for h in range(hg)], axis=0)

    def normalized(pv):
        return pv * (1.0 / pltpu.roll(pv, HEAD_DIM, axis=1))

    nwin = _window_keys(nq)
    w0 = pl.multiple_of(jnp.maximum(blk_first - WINDOW // SEL_BLOCK, 0) * SEL_BLOCK, V7X_LANES)
    kp = w0 + lax.broadcasted_iota(jnp.int32, (1, nwin), 1)
    bias_w = jnp.where(kp <= tq, jnp.where(kp > tq - WINDOW, 0.0, NEG_INF), NEG_INF)
    o_win = []
    for g in range(NSA_KV_HEADS):
        s = jnp.dot(group_q(qrot_ref, g), kwt_ref[0, :, pl.ds(w0, nwin)], preferred_element_type=F32)
        s = add_per_query(s, bias_w)
        e = jnp.exp2(s - jnp.max(s, axis=-1, keepdims=True))
        v = vw_ref[0, pl.ds(w0, nwin), g * KV_LANES:(g + 1) * KV_LANES]
        o_win.append(normalized(jnp.dot(e.astype(BF16), v, preferred_element_type=F32)))

    def compressed_branch(width):
        cmp_end = lax.broadcasted_iota(jnp.int32, (1, width), 1) * CMP_STRIDE + (CMP_BLOCK - 1)
        bias_c = jnp.where(cmp_end <= tq, 0.0, NEG_INF)
        any_valid = tq_rows >= CMP_BLOCK - 1
        outs, p_sum = [], []
        for g in range(NSA_KV_HEADS):
            s = jnp.dot(group_q(qraw_ref, g), kct_ref[0, :, :width], preferred_element_type=F32)
            s = add_per_query(s, bias_c)
            e = jnp.exp2(s - jnp.max(s, axis=-1, keepdims=True))
            p = e * jnp.where(any_valid, 1.0 / jnp.sum(e, axis=-1, keepdims=True), 0.0)
            outs.append(jnp.dot(p.astype(BF16), vc_ref[0, :width, :], preferred_element_type=F32))
            ph = p[0:nq]
            for h in range(1, hg):
                ph = ph + p[h * nq:(h + 1) * nq]
            p_sum.append(ph)
        p2 = jnp.concatenate(p_sum, axis=0)
        p_hi = p2.astype(BF16)
        p_lo = (p2 - p_hi.astype(F32)).astype(BF16)
        ovt = ovt_ref[:, :width]
        imp = (lax.dot_general(ovt, p_hi, _NT, preferred_element_type=F32)
               + lax.dot_general(ovt, p_lo, _NT, preferred_element_type=F32))
        return tuple(outs) + (imp,)

    *o_cmp, imp_t = compressed_branch(nc)

    blk_q = (t0 + lax.broadcasted_iota(jnp.int32, (1, lanes_gq), 1) % nq) // SEL_BLOCK
    jfull = lax.broadcasted_iota(jnp.int32, (nb, lanes_gq), 0)
    valid = jfull <= blk_q
    forced = (jfull == 0) | (jfull == blk_q) | (jfull == blk_q - 1)
    score = jnp.where(valid, imp_t + jnp.where(forced, FORCE_BONUS, 0.0), NEG_INF)

    free = jnp.where(valid, jnp.where(forced, NEG_INF, imp_t), NEG_INF)
    n_forced = 1 + jnp.where(blk_q >= 1, 1, 0) + jnp.where(blk_q >= 2, 1, 0)
    rest, cut = free, {}
    for k in range(1, n_sel):
        thr = jnp.max(rest, axis=0, keepdims=True)
        rest = jnp.where(rest >= thr, NEG_INF, rest)
        cut[k] = thr
    thr = cut[n_sel - 1]
    for nf in (2, 3):
        thr = jnp.where(n_forced == nf, cut[n_sel - nf], thr)
    sel_t = jnp.where(valid, jnp.where(forced, 1.0, jnp.where(free >= thr, 1.0, 0.0)), 0.0)

    def ranked_members():
        def count_above_or_tied_earlier(jp, cnt):
            row = jnp.broadcast_to(score_ref[pl.ds(jp, 1), :], score.shape)
            tie = jnp.where(jp < jfull, 1.0, 0.0)
            return cnt + jnp.where(row > score, 1.0, jnp.where(row == score, tie, 0.0))

        score_ref[...] = score
        cnt = lax.fori_loop(0, blk_last + 1, count_above_or_tied_earlier, jnp.zeros(score.shape, F32))
        return jnp.where(valid, jnp.where(cnt < float(n_sel), 1.0, 0.0), 0.0)

    picked = jnp.sum(sel_t, axis=0, keepdims=True)
    wrong = jnp.sum(jnp.where(picked == jnp.minimum(blk_q + 1, n_sel).astype(F32), 0.0, 1.0))
    sel_t = lax.cond(wrong == 0.0, lambda: sel_t, ranked_members)
    bias = ((sel_t.T - 1.0) * MASK_BIAS).astype(BF16)

    n_tiles = blk_last // (SEL_TILE // SEL_BLOCK) + 1
    lane_t = lax.broadcasted_iota(jnp.int32, (1, SEL_TILE), 1)
    for g in range(NSA_KV_HEADS):
        qa_ref[g * rows:(g + 1) * rows, :KV_LANES] = group_q(qrot_ref, g)
        qa_ref[g * rows:(g + 1) * rows, KV_LANES:] = jnp.concatenate([bias[g * nq:(g + 1) * nq]] * hg, axis=0)

    def tile_body(c, carry, causal):
        k0 = pl.multiple_of(c * SEL_TILE, SEL_TILE)
        ka = ksat_ref[0, :, pl.ds(k0, SEL_TILE)]
        if causal:
            bias_d = jnp.where((k0 + lane_t) <= tq, 0.0, NEG_INF)
        new = []
        for g in range(NSA_KV_HEADS):
            m, acc = carry[g]
            s = jnp.dot(qa_ref[g * rows:(g + 1) * rows], ka, preferred_element_type=F32)
            if causal:
                s = add_per_query(s, bias_d)
            m_new = jnp.maximum(m, jnp.max(s, axis=-1, keepdims=True))
            p = jnp.exp2(s - m_new)
            v = vs_ref[0, pl.ds(k0, SEL_TILE), g * KV_LANES:(g + 1) * KV_LANES]
            acc = jnp.exp2(m - m_new) * acc + jnp.dot(p.astype(BF16), v, preferred_element_type=F32)
            new.append((m_new, acc))
        return tuple(new)

    init = tuple((jnp.full((rows, 1), NEG_INF, F32), jnp.zeros((rows, KV_LANES), F32)) for _ in range(NSA_KV_HEADS))
    carry = lax.fori_loop(0, n_tiles - 1, functools.partial(tile_body, causal=False), init)
    carry = tile_body(n_tiles - 1, carry, causal=True)
    o_sel = [normalized(acc) for _, acc in carry]

    gt = gate_ref[0]
    for h in range(hg):
        r = slice(h * nq, (h + 1) * nq)
        parts = []
        for g in range(NSA_KV_HEADS):
            c = (g * hg + h) * 3
            parts.append(gt[:, c:c + 1] * o_cmp[g][r] + gt[:, c + 1:c + 2] * o_sel[g][r] + gt[:, c + 2:c + 3] * o_win[g][r])
        out = parts[0]
        for g in range(1, NSA_KV_HEADS):
            out = jnp.where(own_lanes[g], parts[g], out)
        o_ref[0, :, h * KV_LANES:(h + 1) * KV_LANES] = out.astype(BF16)


def _nsa_call(qraw, qrot, gates, kct, vc, ksat, vs, kwt, vw, ovt):
    batch, seq, _ = qraw.shape
    nq = NSA_QUERIES
    assert nq % V7X_LANES == 0 and seq % nq == 0 and seq >= _window_keys(nq) and seq % SEL_TILE == 0
    step = lambda n: pl.BlockSpec((1, nq, n), lambda b, i: (b, i, 0))
    whole = lambda a: pl.BlockSpec((1,) + a.shape[1:], lambda b, i: (b, 0, 0))
    return pl.pallas_call(
        functools.partial(_nsa_kernel, n_sel=min(N_SEL, seq // SEL_BLOCK)),
        grid=(batch, seq // nq),
        in_specs=[step(Q_LANES), step(Q_LANES), step(V7X_LANES),
                  whole(kct), whole(vc), whole(ksat), whole(vs), whole(kwt), whole(vw),
                  pl.BlockSpec(ovt.shape, lambda b, i: (0, 0))],
        out_specs=step(HEADS_PER_GROUP * KV_LANES),
        out_shape=jax.ShapeDtypeStruct((batch, seq, HEADS_PER_GROUP * KV_LANES), BF16),
        scratch_shapes=[pltpu.VMEM((V7X_LANES, NSA_KV_HEADS * nq), F32),
                        pltpu.VMEM((NSA_KV_HEADS * HEADS_PER_GROUP * nq, KV_LANES + V7X_LANES), BF16)],
        compiler_params=_cparams(("arbitrary", "arbitrary")),
        name="nsa",
    )(qraw, qrot, gates, kct, vc, ksat, vs, kwt, vw, ovt)


def _s5_kernel(u_ref, lag_ref, p_ref, q_ref, lr_ref, li_ref, d_ref, o_ref, m_ref, x_ref, y_ref, z_ref, sp_ref):
    nch = x_ref.shape[0]
    half = z_ref.shape[1] // 2
    lanes = V7X_LANES

    @pl.when(pl.program_id(1) == 0)
    def _():
        for s in range(SSM_CHUNK):
            for t in range(SSM_CHUNK):
                tile = lag_ref[0, t - s] if t >= s else jnp.zeros((lanes, lanes), BF16)
                m_ref[s * lanes:(s + 1) * lanes, t * lanes:(t + 1) * lanes] = tile

    for t in range(SSM_CHUNK):
        x_ref[:, t * lanes:(t + 1) * lanes] = u_ref[pl.ds(t, nch, stride=SSM_CHUNK), :].astype(BF16)
    for c0 in range(0, SSM_CHUNK * lanes, V7X_MXU):
        c1 = c0 + V7X_MXU
        y_ref[:, c0:c1] = jnp.dot(x_ref[:, :c1], m_ref[:c1, c0:c1], preferred_element_type=F32)
    z_ref[...] = jnp.dot(x_ref[...], p_ref[0], preferred_element_type=F32)
    lr, li = lr_ref[0], li_ref[0]

    def scan_body(k, carry):
        sr, si = carry
        sp_ref[pl.ds(k, 1), 0:half] = sr
        sp_ref[pl.ds(k, 1), half:2 * half] = si
        zr = z_ref[pl.ds(k, 1), 0:half]
        zi = z_ref[pl.ds(k, 1), half:2 * half]
        return lr * sr - li * si + zr, lr * si + li * sr + zi

    zero = jnp.zeros((1, half), F32)
    lax.fori_loop(0, nch, scan_body, (zero, zero), unroll=32)
    y = y_ref[...] + jnp.dot(sp_ref[...].astype(BF16), q_ref[0], preferred_element_type=F32)
    d = d_ref[...]
    for t in range(SSM_CHUNK):
        yt = y[:, t * lanes:(t + 1) * lanes] + d * u_ref[pl.ds(t, nch, stride=SSM_CHUNK), :]
        o_ref[pl.ds(t, nch, stride=SSM_CHUNK), :] = _gelu(yt)


def _s5_call(u, lag, p, q, lr, li, dskip, batch, seq):
    nsg = lag.shape[0]
    nch = seq // SSM_CHUNK
    nstate = q.shape[1]
    lw = SSM_CHUNK * V7X_LANES
    slab = lambda a: pl.BlockSpec((1,) + a.shape[1:], lambda g, b: (g,) + (0,) * (a.ndim - 1),
                                  pipeline_mode=pl.Buffered(1))
    return pl.pallas_call(
        _s5_kernel,
        grid=(nsg, batch),
        in_specs=[pl.BlockSpec((seq, V7X_LANES), lambda g, b: (b, g)), slab(lag), slab(p), slab(q),
                  pl.BlockSpec((1, 1, nstate // 2), lambda g, b: (g, 0, 0)),
                  pl.BlockSpec((1, 1, nstate // 2), lambda g, b: (g, 0, 0)),
                  pl.BlockSpec((1, V7X_LANES), lambda g, b: (0, g))],
        out_specs=pl.BlockSpec((seq, V7X_LANES), lambda g, b: (b, g)),
        out_shape=jax.ShapeDtypeStruct(u.shape, F32),
        scratch_shapes=[pltpu.VMEM((lw, lw), BF16), pltpu.VMEM((nch, lw), BF16), pltpu.VMEM((nch, lw), F32),
                        pltpu.VMEM((nch, nstate), F32), pltpu.VMEM((nch, nstate), F32)],
        compiler_params=_cparams(("arbitrary", "arbitrary")),
        name="s5",
    )(u, lag, p, q, lr, li, dskip)


def _merge_kernel(x_ref, g_ref, attn_ref, ssm_ref, wga_ref, wgb_ref, wattn_ref, wval_ref, wgate_ref, wout_ref, o_ref):
    x = x_ref[...]
    hb = _rmsnorm(x, g_ref[...]).astype(BF16)
    dot = functools.partial(jnp.dot, preferred_element_type=F32)
    y_a = dot(attn_ref[...], wattn_ref[...])
    ys = ssm_ref[...].astype(BF16)
    y_b = dot(ys, wval_ref[...]) * jax.nn.sigmoid(dot(ys, wgate_ref[...]))
    merged = jax.nn.sigmoid(dot(hb, wga_ref[...])) * y_a + jax.nn.sigmoid(dot(hb, wgb_ref[...])) * y_b
    o_ref[...] = x + dot(merged.astype(BF16), wout_ref[...])


def _merge_call(x2, g, attn, ssm, wga, wgb, wattn, wval, wgate, wout):
    t, d = x2.shape
    tm = ROW_TILE
    row = lambda i: (i, 0)
    wspec = lambda a: pl.BlockSpec(a.shape, lambda i: (0, 0), pipeline_mode=pl.Buffered(1))
    return pl.pallas_call(
        _merge_kernel,
        grid=(t // tm,),
        in_specs=[pl.BlockSpec((tm, d), row), wspec(g), pl.BlockSpec((tm, attn.shape[1]), row),
                  pl.BlockSpec((tm, ssm.shape[1]), row), wspec(wga), wspec(wgb), wspec(wattn), wspec(wval),
                  wspec(wgate), wspec(wout)],
        out_specs=pl.BlockSpec((tm, d), row),
        out_shape=jax.ShapeDtypeStruct((t, d), F32),
        compiler_params=_cparams(("arbitrary",)),
        name="merge",
    )(x2, g, attn, ssm, wga, wgb, wattn, wval, wgate, wout)


def _mlp_kernel(x_ref, g_ref, wup_ref, wdown_ref, gf_ref, o_ref):
    x = x_ref[...]
    hb = _rmsnorm(x, g_ref[...]).astype(BF16)
    acc = x
    for c in range(wup_ref.shape[1] // FF_CHUNK):
        sl = slice(c * FF_CHUNK, (c + 1) * FF_CHUNK)
        up = jnp.maximum(jnp.dot(hb, wup_ref[:, sl], preferred_element_type=F32), 0.0)
        acc = acc + jnp.dot((up * up).astype(BF16), wdown_ref[sl, :], preferred_element_type=F32)
    o_ref[...] = _rmsnorm(acc, gf_ref[...])


def _mlp_call(x1, g, wup, wdown, gf):
    t, d = x1.shape
    tm = ROW_TILE
    row = lambda i: (i, 0)
    wspec = lambda a: pl.BlockSpec(a.shape, lambda i: (0, 0), pipeline_mode=pl.Buffered(1))
    return pl.pallas_call(
        _mlp_kernel,
        grid=(t // tm,),
        in_specs=[pl.BlockSpec((tm, d), row), wspec(g), wspec(wup), wspec(wdown), wspec(gf)],
        out_specs=pl.BlockSpec((tm, d), row),
        out_shape=jax.ShapeDtypeStruct((t, d), F32),
        compiler_params=_cparams(("arbitrary",)),
        name="mlp",
    )(x1, g, wup, wdown, gf)


def _interleave_heads(wq):
    d = wq.shape[0]
    return wq.reshape(d, NSA_KV_HEADS, HEADS_PER_GROUP, HEAD_DIM).transpose(0, 2, 1, 3).reshape(d, Q_LANES)


def _rope_tables(seq):
    half = ROPE_DIM // 2
    inv = ROPE_THETA ** (-(jnp.arange(half, dtype=F32) * 2.0) / ROPE_DIM)
    ang = jnp.arange(seq, dtype=F32)[:, None] * inv[None, :]
    cos, sin = jnp.cos(ang), jnp.sin(ang)
    rest = HEAD_DIM - ROPE_DIM
    cos_h = jnp.concatenate([cos, cos, jnp.ones((seq, rest), F32)], axis=1)
    slo_h = jnp.concatenate([-sin, jnp.zeros((seq, half + rest), F32)], axis=1)
    shi_h = jnp.concatenate([jnp.zeros((seq, half), F32), sin, jnp.zeros((seq, rest), F32)], axis=1)
    reps = V7X_LANES // HEAD_DIM
    return jnp.tile(cos_h, (1, reps)), jnp.tile(slo_h, (1, reps)), jnp.tile(shi_h, (1, reps))


def _compress_weights(pe, w1, w2):
    eye = jnp.eye(NSA_KV_HEADS, dtype=F32)
    w1e = jnp.einsum('tdj,gk->tgdkj', w1.reshape(CMP_BLOCK, HEAD_DIM, CMP_HIDDEN), eye)
    w1e = w1e.reshape(CMP_BLOCK * KV_LANES, NSA_KV_HEADS * CMP_HIDDEN).astype(BF16)
    w2e = jnp.einsum('jd,gk->gjkd', w2, eye).reshape(NSA_KV_HEADS * CMP_HIDDEN, KV_LANES).astype(BF16)
    pee = jnp.tile(pe, (1, NSA_KV_HEADS)).reshape(1, CMP_BLOCK * KV_LANES)
    halfw = CMP_STRIDE * KV_LANES
    return pee[:, :halfw], pee[:, halfw:], w1e[:halfw], w1e[halfw:], w2e


def _selection_constants(seq):
    nc = seq // CMP_STRIDE - 1
    nb = seq // SEL_BLOCK
    n_np = np.arange(nc)[:, None] * CMP_STRIDE
    j_np = np.arange(nb)[None, :] * SEL_BLOCK
    overlap = ((n_np < j_np + SEL_BLOCK) & (n_np + CMP_BLOCK > j_np)).astype(np.float32)
    ovt = np.zeros((V7X_LANES, nc + 1), np.float32)
    ovt[:nb, :nc] = overlap.T
    return jnp.asarray(ovt, BF16)


def _s5_matrices(lam_re, lam_im, log_step, b_re, b_im, c_re, c_im):
    hp = lax.Precision.HIGHEST
    ng, ns = lam_re.shape
    gc = b_re.shape[-1]
    L = SSM_CHUNK
    step = jnp.exp(log_step)[:, None]
    a, b = lam_re * step, lam_im * step
    k = jnp.arange(L + 1, dtype=F32)[:, None, None]
    mag = jnp.exp(a[None] * k)
    pr, pi = mag * jnp.cos(b[None] * k), mag * jnp.sin(b[None] * k)
    nr, ni = pr[1] - 1.0, pi[1]
    den = lam_re * lam_re + lam_im * lam_im
    cr, ci = (nr * lam_re + ni * lam_im) / den, (ni * lam_re - nr * lam_im) / den
    bbr = cr[..., None] * b_re - ci[..., None] * b_im
    bbi = cr[..., None] * b_im + ci[..., None] * b_re
    cpr = c_re[None] * pr[:, :, None, :] - c_im[None] * pi[:, :, None, :]
    cpi = c_re[None] * pi[:, :, None, :] + c_im[None] * pr[:, :, None, :]
    kk = jnp.einsum('kgcn,gnd->kgcd', jnp.concatenate([cpr[:L], -cpi[:L]], axis=-1),
                    jnp.concatenate([bbr, bbi], axis=1), precision=hp)
    pw_r, pw_i = pr[L - 1 - np.arange(L)], pi[L - 1 - np.arange(L)]
    p_r = pw_r[:, :, :, None] * bbr[None] - pw_i[:, :, :, None] * bbi[None]
    p_i = pw_r[:, :, :, None] * bbi[None] + pw_i[:, :, :, None] * bbr[None]
    p_r, p_i = p_r.transpose(1, 0, 3, 2), p_i.transpose(1, 0, 3, 2)
    q_r = cpr[1:].transpose(1, 3, 0, 2)
    q_i = -cpi[1:].transpose(1, 3, 0, 2)
    sup = SSM_SUPER
    nsg = ng // sup
    lw = L * sup * gc
    bf = lambda a: a.astype(BF16)
    lag_c = bf(kk).reshape(L, nsg, sup, gc, gc).transpose(1, 0, 2, 4, 3).reshape(nsg, L, sup * gc, gc)
    own_lag = np.arange(sup * gc)[:, None] // gc == np.arange(sup * gc)[None, :] // gc
    lag_sg = jnp.where(own_lag, jnp.matmul(lag_c, jnp.asarray(np.tile(np.eye(gc), (1, sup)), BF16)), 0)
    slab_rows = lambda a: bf(a).reshape(nsg, sup, L, gc, -1).transpose(0, 2, 1, 3, 4).reshape(nsg, lw, -1)
    p_c = jnp.concatenate([slab_rows(p_r), slab_rows(p_i)], axis=2)
    rep_p = np.kron(np.eye(2), np.tile(np.eye(ns), (1, sup)))
    own_p = (np.arange(lw)[:, None] // gc) % sup == (np.arange(2 * sup * ns)[None, :] // ns) % sup
    p = jnp.where(own_p, jnp.matmul(p_c, jnp.asarray(rep_p, BF16)), 0)
    q_c = jnp.concatenate([bf(q_r).reshape(nsg, sup * ns, L * gc), bf(q_i).reshape(nsg, sup * ns, L * gc)], axis=1)
    rep_q = np.einsum('ts,cd,b->tcsbd', np.eye(L), np.eye(gc), np.ones(sup)).reshape(L * gc, lw)
    own_q = (np.arange(2 * sup * ns)[:, None] // ns) % sup == (np.arange(lw)[None, :] // gc) % sup
    q = jnp.where(own_q, jnp.matmul(q_c, jnp.asarray(rep_q, BF16)), 0)
    return lag_sg, p, q, pr[L].reshape(nsg, 1, sup * ns), pi[L].reshape(nsg, 1, sup * ns)


def kernel(x, norm_mix_g, w_in, cmp_pe, cmp_k_w1, cmp_k_w2, cmp_v_w1, cmp_v_w2, ssm_lam_re, ssm_lam_im, ssm_log_step, ssm_b_re, ssm_b_im, ssm_c_re, ssm_c_im, ssm_d, w_attn_branch, w_ssm_val, w_ssm_gate, w_out, norm_mlp_g, w_up, w_down, norm_final_g):
    batch, seq, d = x.shape
    depth = w_in.shape[0]
    assert depth == 1, "the final rmsnorm is fused into the single layer's mlp kernel"
    nsa_w = NSA_HEADS * HEAD_DIM
    ssm_w = ssm_d.shape[1]
    o_q, o_kv, o_g = nsa_w, nsa_w + 6 * KV_LANES, nsa_w + 6 * KV_LANES + 3 * NSA_HEADS
    o_u = o_g + ssm_w
    cos, slo, shi = _rope_tables(seq)
    ovt = _selection_constants(seq)
    head_order = np.array([g * HEADS_PER_GROUP + h for h in range(HEADS_PER_GROUP) for g in range(NSA_KV_HEADS)])
    x2 = x.reshape(batch * seq, d)
    for l in range(depth):
        wl = w_in[l]
        wq = _interleave_heads(wl[:, :o_q]).astype(BF16)
        wkv = wl[:, o_q:o_kv].astype(BF16)
        wg = jnp.pad(wl[:, o_kv:o_g], ((0, 0), (0, V7X_LANES - 3 * NSA_HEADS))).astype(BF16)
        wu = wl[:, o_g:o_u].astype(BF16)
        wga = wl[:, o_u:o_u + d].astype(BF16)
        wgb = wl[:, o_u + d:].astype(BF16)
        g_mix = norm_mix_g[l].reshape(1, d)
        qraw, qrot, kc_raw, vc_raw, ksat, vs, kwt, vw, gates, u = _proj_call(
            x2, g_mix, wq, wkv, wg, wu, cos, slo, shi, batch, seq)

        pea, peb, kw1a, kw1b, kw2 = _compress_weights(cmp_pe[l], cmp_k_w1[l], cmp_k_w2[l])
        _, _, vw1a, vw1b, vw2 = _compress_weights(cmp_pe[l], cmp_v_w1[l], cmp_v_w2[l])
        kct, vc = _compress_call(kc_raw, vc_raw, pea, peb, kw1a, kw1b, kw2.T, vw1a, vw1b, vw2, batch, seq)

        b3 = lambda a: a.reshape(batch, seq, a.shape[-1])
        attn = _nsa_call(b3(qraw), b3(qrot), b3(gates), kct, vc, ksat, b3(vs), kwt, b3(vw), ovt)
        attn = attn.reshape(batch * seq, nsa_w)

        lag, pm, qm, lr, li = _s5_matrices(ssm_lam_re[l], ssm_lam_im[l], ssm_log_step[l], ssm_b_re[l], ssm_b_im[l],
                                           ssm_c_re[l], ssm_c_im[l])
        y_ssm = _s5_call(u, lag, pm, qm, lr, li, ssm_d[l].reshape(1, ssm_w), batch, seq)

        wattn = w_attn_branch[l].reshape(NSA_HEADS, HEAD_DIM, d)[head_order].reshape(nsa_w, d).astype(BF16)
        x1 = _merge_call(x2, g_mix, attn, y_ssm, wga, wgb, wattn, w_ssm_val[l].astype(BF16),
                         w_ssm_gate[l].astype(BF16), w_out[l].astype(BF16))
        x2 = _mlp_call(x1, norm_mlp_g[l].reshape(1, d), w_up[l].astype(BF16), w_down[l].astype(BF16),
                       norm_final_g.reshape(1, d))
    return x2.reshape(batch, seq, d)
```

```python
import functools
import math

import jax
import jax.numpy as jnp
import numpy as np
from jax import lax
from jax.experimental import pallas as pl
from jax.experimental.pallas import tpu as pltpu

NSA_HEADS = 8
NSA_KV_HEADS = 2
HEAD_DIM = 64
CMP_BLOCK = 32
CMP_STRIDE = 16
CMP_HIDDEN = 256
SEL_BLOCK = 64
N_SEL = 16
WINDOW = 512
FORCE_BONUS = 1e3
NEG_INF = -1e30
ROPE_THETA = 500000.0
ROPE_DIM = HEAD_DIM // 4
SSM_GROUP = 16
SSM_STATE = 64
EPS = 1e-6

HEADS_PER_GROUP = NSA_HEADS // NSA_KV_HEADS
KV_LANES = NSA_KV_HEADS * HEAD_DIM
Q_LANES = NSA_HEADS * HEAD_DIM

V7X_LANES = 128
V7X_MXU = 256
V7X_VMEM_BYTES = 64 * 1024 * 1024
VMEM_LIMIT = V7X_VMEM_BYTES - 8 * 1024 * 1024

ROW_TILE = 1024
SEL_TILE = 1024
NSA_QUERIES = 256
SSM_CHUNK = 8
SSM_SUPER = V7X_LANES // SSM_GROUP
FF_CHUNK = 1024

LOG2E = math.log2(math.e)
MASK_BIAS = 1e30

BF16 = jnp.bfloat16
F32 = jnp.float32
_NT = (((1,), (1,)), ((), ()))


def _cparams(semantics):
    return pltpu.CompilerParams(dimension_semantics=semantics, vmem_limit_bytes=VMEM_LIMIT)


def _rmsnorm(x, g):
    return x * lax.rsqrt(jnp.mean(x * x, axis=-1, keepdims=True) + EPS) * g


def _gelu(x):
    return jax.nn.gelu(x)


def _rope_cols(x, cos, sin_lo, sin_hi):
    cols = []
    for c in range(x.shape[1] // V7X_LANES):
        xc = x[:, c * V7X_LANES:(c + 1) * V7X_LANES]
        up = pltpu.roll(xc, V7X_LANES - ROPE_DIM // 2, axis=1)
        dn = pltpu.roll(xc, ROPE_DIM // 2, axis=1)
        cols.append(xc * cos + up * sin_lo + dn * sin_hi)
    return jnp.concatenate(cols, axis=1) if len(cols) > 1 else cols[0]


def _values_with_ones(v):
    lane_head = lax.broadcasted_iota(jnp.int32, v.shape, 1) // HEAD_DIM
    return jnp.concatenate([jnp.where(lane_head == g, v, 1.0) for g in range(NSA_KV_HEADS)], axis=1).astype(BF16)


def _proj_kernel(x_ref, g_ref, wq_ref, wkv_ref, wg_ref, wu_ref, cos_ref, slo_ref, shi_ref,
                 qraw_ref, qrot_ref, kc_ref, vc_ref, ksat_ref, vs_ref, kwt_ref, vw_ref, gate_ref, u_ref, *, seq_tiles):
    hb = _rmsnorm(x_ref[...], g_ref[...]).astype(BF16)
    cos, slo, shi = cos_ref[...], slo_ref[...], shi_ref[...]
    q = jnp.dot(hb, wq_ref[...], preferred_element_type=F32) * (HEAD_DIM ** -0.5 * LOG2E)
    qraw_ref[...] = q.astype(BF16)
    qrot_ref[...] = _rope_cols(q, cos, slo, shi).astype(BF16)
    kv = jnp.dot(hb, wkv_ref[...], preferred_element_type=F32)
    w = KV_LANES
    kc_ref[...] = kv[:, 0 * w:1 * w]
    vc_ref[...] = kv[:, 1 * w:2 * w]
    tm = x_ref.shape[0]
    pos = (pl.program_id(0) % seq_tiles) * tm + lax.broadcasted_iota(jnp.int32, (V7X_LANES, tm), 1)
    onehot = jnp.where(lax.broadcasted_iota(jnp.int32, (V7X_LANES, tm), 0) == pos // SEL_BLOCK, 1.0, 0.0)
    ksat_ref[0, :w, :] = _rope_cols(kv[:, 2 * w:3 * w], cos, slo, shi).T.astype(BF16)
    ksat_ref[0, w:, :] = onehot.astype(BF16)
    vs_ref[...] = _values_with_ones(kv[:, 3 * w:4 * w])
    kwt_ref[0] = _rope_cols(kv[:, 4 * w:5 * w], cos, slo, shi).T.astype(BF16)
    vw_ref[...] = _values_with_ones(kv[:, 5 * w:6 * w])
    gate_ref[...] = jax.nn.sigmoid(jnp.dot(hb, wg_ref[...], preferred_element_type=F32))
    u_ref[...] = jnp.dot(hb, wu_ref[...], preferred_element_type=F32)


def _proj_call(x2, g, wq, wkv, wg, wu, cos, slo, shi, batch, seq):
    t, d = x2.shape
    tm = ROW_TILE
    s_tiles = seq // tm
    row = lambda i: (i, 0)
    pos = lambda i: (i % s_tiles, 0)
    trans = lambda i: (i // s_tiles, 0, i % s_tiles)
    wspec = lambda a: pl.BlockSpec(a.shape, lambda i: (0, 0), pipeline_mode=pl.Buffered(1))
    rows_out = lambda n, dt: (jax.ShapeDtypeStruct((t, n), dt), pl.BlockSpec((tm, n), row))
    trans_out = lambda n: (jax.ShapeDtypeStruct((batch, n, seq), BF16), pl.BlockSpec((1, n, tm), trans))
    outs = [rows_out(Q_LANES, BF16), rows_out(Q_LANES, BF16), rows_out(KV_LANES, F32), rows_out(KV_LANES, F32),
            trans_out(KV_LANES + V7X_LANES), rows_out(NSA_KV_HEADS * KV_LANES, BF16),
            trans_out(KV_LANES), rows_out(NSA_KV_HEADS * KV_LANES, BF16),
            rows_out(V7X_LANES, F32), rows_out(wu.shape[1], F32)]
    assert seq // SEL_BLOCK <= V7X_LANES, "the selection-block one-hot must fit one lane tile"
    return pl.pallas_call(
        functools.partial(_proj_kernel, seq_tiles=s_tiles),
        grid=(t // tm,),
        in_specs=[pl.BlockSpec((tm, d), row), wspec(g), wspec(wq), wspec(wkv), wspec(wg), wspec(wu),
                  pl.BlockSpec((tm, V7X_LANES), pos), pl.BlockSpec((tm, V7X_LANES), pos),
                  pl.BlockSpec((tm, V7X_LANES), pos)],
        out_specs=[o[1] for o in outs],
        out_shape=[o[0] for o in outs],
        compiler_params=_cparams(("arbitrary",)),
        name="proj",
    )(x2, g, wq, wkv, wg, wu, cos, slo, shi)


def _compress_kernel(k_ref, v_ref, pea_ref, peb_ref, kw1a_ref, kw1b_ref, kw2t_ref, vw1a_ref, vw1b_ref, vw2_ref,
                     kct_ref, vc_ref, ca_ref, cb_ref):
    nch = ca_ref.shape[0]

    def hidden(src_ref, w1a_ref, w1b_ref):
        for t in range(CMP_STRIDE):
            rows = src_ref[pl.ds(t, nch, stride=CMP_STRIDE), :]
            sl = slice(t * KV_LANES, (t + 1) * KV_LANES)
            ca_ref[:, sl] = (rows + pea_ref[:, sl]).astype(BF16)
            cb_ref[:, sl] = (rows + peb_ref[:, sl]).astype(BF16)
        ha = jnp.dot(ca_ref[...], w1a_ref[...], preferred_element_type=F32)
        hb = jnp.dot(cb_ref[...], w1b_ref[...], preferred_element_type=F32)
        return _gelu(ha + pltpu.roll(hb, nch - 1, axis=0)).astype(BF16)

    kct_ref[0] = lax.dot_general(kw2t_ref[...], hidden(k_ref, kw1a_ref, kw1b_ref), _NT,
                                 preferred_element_type=F32).astype(BF16)
    vc_ref[0] = jnp.dot(hidden(v_ref, vw1a_ref, vw1b_ref), vw2_ref[...], preferred_element_type=F32).astype(BF16)


def _compress_call(kc_raw, vc_raw, pea, peb, kw1a, kw1b, kw2, vw1a, vw1b, vw2, batch, seq):
    nch = seq // CMP_STRIDE
    wspec = lambda a: pl.BlockSpec(a.shape, lambda b: (0, 0))
    return pl.pallas_call(
        _compress_kernel,
        grid=(batch,),
        in_specs=[pl.BlockSpec((seq, KV_LANES), lambda b: (b, 0)), pl.BlockSpec((seq, KV_LANES), lambda b: (b, 0)),
                  wspec(pea), wspec(peb), wspec(kw1a), wspec(kw1b), wspec(kw2), wspec(vw1a), wspec(vw1b), wspec(vw2)],
        out_specs=[pl.BlockSpec((1, KV_LANES, nch), lambda b: (b, 0, 0)),
                   pl.BlockSpec((1, nch, KV_LANES), lambda b: (b, 0, 0))],
        out_shape=[jax.ShapeDtypeStruct((batch, KV_LANES, nch), BF16), jax.ShapeDtypeStruct((batch, nch, KV_LANES), BF16)],
        scratch_shapes=[pltpu.VMEM((nch, CMP_STRIDE * KV_LANES), BF16), pltpu.VMEM((nch, CMP_STRIDE * KV_LANES), BF16)],
        compiler_params=_cparams(("arbitrary",)),
        name="compress",
    )(kc_raw, vc_raw, pea, peb, kw1a, kw1b, kw2, vw1a, vw1b, vw2)


def _window_keys(nq):
    return WINDOW + nq


def _nsa_kernel(qraw_ref, qrot_ref, gate_ref, kct_ref, vc_ref, ksat_ref, vs_ref, kwt_ref, vw_ref, ovt_ref,
                o_ref, score_ref, qa_ref, *, n_sel):
    i = pl.program_id(1)
    nq = qraw_ref.shape[1]
    hg = HEADS_PER_GROUP
    rows = hg * nq
    lanes_gq = NSA_KV_HEADS * nq
    nb = score_ref.shape[0]
    nc = kct_ref.shape[2]
    t0 = i * nq
    blk_first = t0 // SEL_BLOCK
    blk_last = blk_first + nq // SEL_BLOCK - 1
    tq = t0 + lax.broadcasted_iota(jnp.int32, (nq, 1), 0)
    tq_rows = t0 + lax.broadcasted_iota(jnp.int32, (rows, 1), 0) % nq

    def add_per_query(s, bias):
        return (s.reshape(hg, nq, s.shape[-1]) + bias[None]).reshape(s.shape)

    own_lanes = [lax.broadcasted_iota(jnp.int32, (nq, KV_LANES), 1) // HEAD_DIM == g for g in range(NSA_KV_HEADS)]

    def group_q(ref, g):
        return jnp.concatenate([jnp.where(own_lanes[g], ref[0, :, h * KV_LANES:(h + 1) * KV_LANES], 0.0).astype(BF16)
                                for h in range(hg)], axis=0)

    def normalized(pv):
        return pv * (1.0 / pltpu.roll(pv, HEAD_DIM, axis=1))

    nwin = _window_keys(nq)
    w0 = pl.multiple_of(jnp.maximum(blk_first - WINDOW // SEL_BLOCK, 0) * SEL_BLOCK, V7X_LANES)
    kp = w0 + lax.broadcasted_iota(jnp.int32, (1, nwin), 1)
    bias_w = jnp.where(kp <= tq, jnp.where(kp > tq - WINDOW, 0.0, NEG_INF), NEG_INF)
    o_win = []
    for g in range(NSA_KV_HEADS):
        s = jnp.dot(group_q(qrot_ref, g), kwt_ref[0, :, pl.ds(w0, nwin)], preferred_element_type=F32)
        s = add_per_query(s, bias_w)
        e = jnp.exp2(s - jnp.max(s, axis=-1, keepdims=True))
        v = vw_ref[0, pl.ds(w0, nwin), g * KV_LANES:(g + 1) * KV_LANES]
        o_win.append(normalized(jnp.dot(e.astype(BF16), v, preferred_element_type=F32)))

    def compressed_branch(width):
        cmp_end = lax.broadcasted_iota(jnp.int32, (1, width), 1) * CMP_STRIDE + (CMP_BLOCK - 1)
        bias_c = jnp.where(cmp_end <= tq, 0.0, NEG_INF)
        any_valid = tq_rows >= CMP_BLOCK - 1
        outs, p_sum = [], []
        for g in range(NSA_KV_HEADS):
            s = jnp.dot(group_q(qraw_ref, g), kct_ref[0, :, :width], preferred_element_type=F32)
            s = add_per_query(s, bias_c)
            e = jnp.exp2(s - jnp.max(s, axis=-1, keepdims=True))
            p = e * jnp.where(any_valid, 1.0 / jnp.sum(e, axis=-1, keepdims=True), 0.0)
            outs.append(jnp.dot(p.astype(BF16), vc_ref[0, :width, :], preferred_element_type=F32))
            ph = p[0:nq]
            for h in range(1, hg):
                ph = ph + p[h * nq:(h + 1) * nq]
            p_sum.append(ph)
        p2 = jnp.concatenate(p_sum, axis=0)
        imp = lax.dot_general(ovt_ref[:, :width], p2.astype(BF16), _NT, preferred_element_type=F32)
        return tuple(outs) + (imp,)

    *o_cmp, imp_t = compressed_branch(nc)

    blk_q = (t0 + lax.broadcasted_iota(jnp.int32, (1, lanes_gq), 1) % nq) // SEL_BLOCK
    jfull = lax.broadcasted_iota(jnp.int32, (nb, lanes_gq), 0)
    valid = jfull <= blk_q
    forced = (jfull == 0) | (jfull == blk_q) | (jfull == blk_q - 1)
    score = jnp.where(valid, imp_t + jnp.where(forced, FORCE_BONUS, 0.0), NEG_INF)

    free = jnp.where(valid, jnp.where(forced, NEG_INF, imp_t), NEG_INF)
    n_forced = 1 + jnp.where(blk_q >= 1, 1, 0) + jnp.where(blk_q >= 2, 1, 0)
    rest, cut = free, {}
    for k in range(1, n_sel):
        thr = jnp.max(rest, axis=0, keepdims=True)
        rest = jnp.where(rest >= thr, NEG_INF, rest)
        cut[k] = thr
    thr = cut[n_sel - 1]
    for nf in (2, 3):
        thr = jnp.where(n_forced == nf, cut[n_sel - nf], thr)
    sel_t = jnp.where(valid, jnp.where(forced, 1.0, jnp.where(free >= thr, 1.0, 0.0)), 0.0)

    def ranked_members():
        def count_above_or_tied_earlier(jp, cnt):
            row = jnp.broadcast_to(score_ref[pl.ds(jp, 1), :], score.shape)
            tie = jnp.where(jp < jfull, 1.0, 0.0)
            return cnt + jnp.where(row > score, 1.0, jnp.where(row == score, tie, 0.0))

        score_ref[...] = score
        cnt = lax.fori_loop(0, blk_last + 1, count_above_or_tied_earlier, jnp.zeros(score.shape, F32))
        return jnp.where(valid, jnp.where(cnt < float(n_sel), 1.0, 0.0), 0.0)

    picked = jnp.sum(sel_t, axis=0, keepdims=True)
    wrong = jnp.sum(jnp.where(picked == jnp.minimum(blk_q + 1, n_sel).astype(F32), 0.0, 1.0))
    sel_t = lax.cond(wrong == 0.0, lambda: sel_t, ranked_members)
    bias = ((sel_t.T - 1.0) * MASK_BIAS).astype(BF16)

    n_tiles = blk_last // (SEL_TILE // SEL_BLOCK) + 1
    lane_t = lax.broadcasted_iota(jnp.int32, (1, SEL_TILE), 1)
    for g in range(NSA_KV_HEADS):
        qa_ref[g * rows:(g + 1) * rows, :KV_LANES] = group_q(qrot_ref, g)
        qa_ref[g * rows:(g + 1) * rows, KV_LANES:] = jnp.concatenate([bias[g * nq:(g + 1) * nq]] * hg, axis=0)

    def tile_body(c, carry, causal):
        k0 = pl.multiple_of(c * SEL_TILE, SEL_TILE)
        ka = ksat_ref[0, :, pl.ds(k0, SEL_TILE)]
        if causal:
            bias_d = jnp.where((k0 + lane_t) <= tq, 0.0, NEG_INF)
        new = []
        for g in range(NSA_KV_HEADS):
            m, acc = carry[g]
            s = jnp.dot(qa_ref[g * rows:(g + 1) * rows], ka, preferred_element_type=F32)
            if causal:
                s = add_per_query(s, bias_d)
            m_new = jnp.maximum(m, jnp.max(s, axis=-1, keepdims=True))
            p = jnp.exp2(s - m_new)
            v = vs_ref[0, pl.ds(k0, SEL_TILE), g * KV_LANES:(g + 1) * KV_LANES]
            acc = jnp.exp2(m - m_new) * acc + jnp.dot(p.astype(BF16), v, preferred_element_type=F32)
            new.append((m_new, acc))
        return tuple(new)

    init = tuple((jnp.full((rows, 1), NEG_INF, F32), jnp.zeros((rows, KV_LANES), F32)) for _ in range(NSA_KV_HEADS))
    carry = lax.fori_loop(0, n_tiles - 1, functools.partial(tile_body, causal=False), init)
    carry = tile_body(n_tiles - 1, carry, causal=True)
    o_sel = [normalized(acc) for _, acc in carry]

    gt = gate_ref[0]
    for h in range(hg):
        r = slice(h * nq, (h + 1) * nq)
        parts = []
        for g in range(NSA_KV_HEADS):
            c = (g * hg + h) * 3
            parts.append(gt[:, c:c + 1] * o_cmp[g][r] + gt[:, c + 1:c + 2] * o_sel[g][r] + gt[:, c + 2:c + 3] * o_win[g][r])
        out = parts[0]
        for g in range(1, NSA_KV_HEADS):
            out = jnp.where(own_lanes[g], parts[g], out)
        o_ref[0, :, h * KV_LANES:(h + 1) * KV_LANES] = out.astype(BF16)


def _nsa_call(qraw, qrot, gates, kct, vc, ksat, vs, kwt, vw, ovt):
    batch, seq, _ = qraw.shape
    nq = NSA_QUERIES
    assert nq % V7X_LANES == 0 and seq % nq == 0 and seq >= _window_keys(nq) and seq % SEL_TILE == 0
    step = lambda n: pl.BlockSpec((1, nq, n), lambda b, i: (b, i, 0))
    whole = lambda a: pl.BlockSpec((1,) + a.shape[1:], lambda b, i: (b, 0, 0))
    return pl.pallas_call(
        functools.partial(_nsa_kernel, n_sel=min(N_SEL, seq // SEL_BLOCK)),
        grid=(batch, seq // nq),
        in_specs=[step(Q_LANES), step(Q_LANES), step(V7X_LANES),
                  whole(kct), whole(vc), whole(ksat), whole(vs), whole(kwt), whole(vw),
                  pl.BlockSpec(ovt.shape, lambda b, i: (0, 0))],
        out_specs=step(HEADS_PER_GROUP * KV_LANES),
        out_shape=jax.ShapeDtypeStruct((batch, seq, HEADS_PER_GROUP * KV_LANES), BF16),
        scratch_shapes=[pltpu.VMEM((V7X_LANES, NSA_KV_HEADS * nq), F32),
                        pltpu.VMEM((NSA_KV_HEADS * HEADS_PER_GROUP * nq, KV_LANES + V7X_LANES), BF16)],
        compiler_params=_cparams(("arbitrary", "arbitrary")),
        name="nsa",
    )(qraw, qrot, gates, kct, vc, ksat, vs, kwt, vw, ovt)


def _s5_kernel(u_ref, lag_ref, p_ref, q_ref, lr_ref, li_ref, d_ref, o_ref, m_ref, x_ref, y_ref, z_ref, sp_ref):
    nch = x_ref.shape[0]
    half = z_ref.shape[1] // 2
    lanes = V7X_LANES

    @pl.when(pl.program_id(1) == 0)
    def _():
        for s in range(SSM_CHUNK):
            for t in range(SSM_CHUNK):
                tile = lag_ref[0, t - s] if t >= s else jnp.zeros((lanes, lanes), BF16)
                m_ref[s * lanes:(s + 1) * lanes, t * lanes:(t + 1) * lanes] = tile

    for t in range(SSM_CHUNK):
        x_ref[:, t * lanes:(t + 1) * lanes] = u_ref[pl.ds(t, nch, stride=SSM_CHUNK), :].astype(BF16)
    for c0 in range(0, SSM_CHUNK * lanes, V7X_MXU):
        c1 = c0 + V7X_MXU
        y_ref[:, c0:c1] = jnp.dot(x_ref[:, :c1], m_ref[:c1, c0:c1], preferred_element_type=F32)
    z_ref[...] = jnp.dot(x_ref[...], p_ref[0], preferred_element_type=F32)
    lr, li = lr_ref[0], li_ref[0]

    def scan_body(k, carry):
        sr, si = carry
        sp_ref[pl.ds(k, 1), 0:half] = sr
        sp_ref[pl.ds(k, 1), half:2 * half] = si
        zr = z_ref[pl.ds(k, 1), 0:half]
        zi = z_ref[pl.ds(k, 1), half:2 * half]
        return lr * sr - li * si + zr, lr * si + li * sr + zi

    zero = jnp.zeros((1, half), F32)
    lax.fori_loop(0, nch, scan_body, (zero, zero), unroll=32)
    y = y_ref[...] + jnp.dot(sp_ref[...].astype(BF16), q_ref[0], preferred_element_type=F32)
    d = d_ref[...]
    for t in range(SSM_CHUNK):
        yt = y[:, t * lanes:(t + 1) * lanes] + d * u_ref[pl.ds(t, nch, stride=SSM_CHUNK), :]
        o_ref[pl.ds(t, nch, stride=SSM_CHUNK), :] = _gelu(yt)


def _s5_call(u, lag, p, q, lr, li, dskip, batch, seq):
    nsg = lag.shape[0]
    nch = seq // SSM_CHUNK
    nstate = q.shape[1]
    lw = SSM_CHUNK * V7X_LANES
    slab = lambda a: pl.BlockSpec((1,) + a.shape[1:], lambda g, b: (g,) + (0,) * (a.ndim - 1),
                                  pipeline_mode=pl.Buffered(1))
    return pl.pallas_call(
        _s5_kernel,
        grid=(nsg, batch),
        in_specs=[pl.BlockSpec((seq, V7X_LANES), lambda g, b: (b, g)), slab(lag), slab(p), slab(q),
                  pl.BlockSpec((1, 1, nstate // 2), lambda g, b: (g, 0, 0)),
                  pl.BlockSpec((1, 1, nstate // 2), lambda g, b: (g, 0, 0)),
                  pl.BlockSpec((1, V7X_LANES), lambda g, b: (0, g))],
        out_specs=pl.BlockSpec((seq, V7X_LANES), lambda g, b: (b, g)),
        out_shape=jax.ShapeDtypeStruct(u.shape, F32),
        scratch_shapes=[pltpu.VMEM((lw, lw), BF16), pltpu.VMEM((nch, lw), BF16), pltpu.VMEM((nch, lw), F32),
                        pltpu.VMEM((nch, nstate), F32), pltpu.VMEM((nch, nstate), F32)],
        compiler_params=_cparams(("arbitrary", "arbitrary")),
        name="s5",
    )(u, lag, p, q, lr, li, dskip)


def _merge_kernel(x_ref, g_ref, attn_ref, ssm_ref, wga_ref, wgb_ref, wattn_ref, wval_ref, wgate_ref, wout_ref, o_ref):
    x = x_ref[...]
    hb = _rmsnorm(x, g_ref[...]).astype(BF16)
    dot = functools.partial(jnp.dot, preferred_element_type=F32)
    y_a = dot(attn_ref[...], wattn_ref[...])
    ys = ssm_ref[...].astype(BF16)
    y_b = dot(ys, wval_ref[...]) * jax.nn.sigmoid(dot(ys, wgate_ref[...]))
    merged = jax.nn.sigmoid(dot(hb, wga_ref[...])) * y_a + jax.nn.sigmoid(dot(hb, wgb_ref[...])) * y_b
    o_ref[...] = x + dot(merged.astype(BF16), wout_ref[...])


def _merge_call(x2, g, attn, ssm, wga, wgb, wattn, wval, wgate, wout):
    t, d = x2.shape
    tm = ROW_TILE
    row = lambda i: (i, 0)
    wspec = lambda a: pl.BlockSpec(a.shape, lambda i: (0, 0), pipeline_mode=pl.Buffered(1))
    return pl.pallas_call(
        _merge_kernel,
        grid=(t // tm,),
        in_specs=[pl.BlockSpec((tm, d), row), wspec(g), pl.BlockSpec((tm, attn.shape[1]), row),
                  pl.BlockSpec((tm, ssm.shape[1]), row), wspec(wga), wspec(wgb), wspec(wattn), wspec(wval),
                  wspec(wgate), wspec(wout)],
        out_specs=pl.BlockSpec((tm, d), row),
        out_shape=jax.ShapeDtypeStruct((t, d), F32),
        compiler_params=_cparams(("arbitrary",)),
        name="merge",
    )(x2, g, attn, ssm, wga, wgb, wattn, wval, wgate, wout)


def _mlp_kernel(x_ref, g_ref, wup_ref, wdown_ref, gf_ref, o_ref):
    x = x_ref[...]
    hb = _rmsnorm(x, g_ref[...]).astype(BF16)
    acc = x
    for c in range(wup_ref.shape[1] // FF_CHUNK):
        sl = slice(c * FF_CHUNK, (c + 1) * FF_CHUNK)
        up = jnp.maximum(jnp.dot(hb, wup_ref[:, sl], preferred_element_type=F32), 0.0)
        acc = acc + jnp.dot((up * up).astype(BF16), wdown_ref[sl, :], preferred_element_type=F32)
    o_ref[...] = _rmsnorm(acc, gf_ref[...])


def _mlp_call(x1, g, wup, wdown, gf):
    t, d = x1.shape
    tm = ROW_TILE
    row = lambda i: (i, 0)
    wspec = lambda a: pl.BlockSpec(a.shape, lambda i: (0, 0), pipeline_mode=pl.Buffered(1))
    return pl.pallas_call(
        _mlp_kernel,
        grid=(t // tm,),
        in_specs=[pl.BlockSpec((tm, d), row), wspec(g), wspec(wup), wspec(wdown), wspec(gf)],
        out_specs=pl.BlockSpec((tm, d), row),
        out_shape=jax.ShapeDtypeStruct((t, d), F32),
        compiler_params=_cparams(("arbitrary",)),
        name="mlp",
    )(x1, g, wup, wdown, gf)


def _interleave_heads(wq):
    d = wq.shape[0]
    return wq.reshape(d, NSA_KV_HEADS, HEADS_PER_GROUP, HEAD_DIM).transpose(0, 2, 1, 3).reshape(d, Q_LANES)


def _rope_tables(seq):
    half = ROPE_DIM // 2
    inv = ROPE_THETA ** (-(jnp.arange(half, dtype=F32) * 2.0) / ROPE_DIM)
    ang = jnp.arange(seq, dtype=F32)[:, None] * inv[None, :]
    cos, sin = jnp.cos(ang), jnp.sin(ang)
    rest = HEAD_DIM - ROPE_DIM
    cos_h = jnp.concatenate([cos, cos, jnp.ones((seq, rest), F32)], axis=1)
    slo_h = jnp.concatenate([-sin, jnp.zeros((seq, half + rest), F32)], axis=1)
    shi_h = jnp.concatenate([jnp.zeros((seq, half), F32), sin, jnp.zeros((seq, rest), F32)], axis=1)
    reps = V7X_LANES // HEAD_DIM
    return jnp.tile(cos_h, (1, reps)), jnp.tile(slo_h, (1, reps)), jnp.tile(shi_h, (1, reps))


def _compress_weights(pe, w1, w2):
    eye = jnp.eye(NSA_KV_HEADS, dtype=F32)
    w1e = jnp.einsum('tdj,gk->tgdkj', w1.reshape(CMP_BLOCK, HEAD_DIM, CMP_HIDDEN), eye)
    w1e = w1e.reshape(CMP_BLOCK * KV_LANES, NSA_KV_HEADS * CMP_HIDDEN).astype(BF16)
    w2e = jnp.einsum('jd,gk->gjkd', w2, eye).reshape(NSA_KV_HEADS * CMP_HIDDEN, KV_LANES).astype(BF16)
    pee = jnp.tile(pe, (1, NSA_KV_HEADS)).reshape(1, CMP_BLOCK * KV_LANES)
    halfw = CMP_STRIDE * KV_LANES
    return pee[:, :halfw], pee[:, halfw:], w1e[:halfw], w1e[halfw:], w2e


def _selection_constants(seq):
    nc = seq // CMP_STRIDE - 1
    nb = seq // SEL_BLOCK
    n_np = np.arange(nc)[:, None] * CMP_STRIDE
    j_np = np.arange(nb)[None, :] * SEL_BLOCK
    overlap = ((n_np < j_np + SEL_BLOCK) & (n_np + CMP_BLOCK > j_np)).astype(np.float32)
    ovt = np.zeros((V7X_LANES, nc + 1), np.float32)
    ovt[:nb, :nc] = overlap.T
    return jnp.asarray(ovt, BF16)


def _s5_matrices(lam_re, lam_im, log_step, b_re, b_im, c_re, c_im):
    hp = lax.Precision.HIGHEST
    ng, ns = lam_re.shape
    gc = b_re.shape[-1]
    L = SSM_CHUNK
    step = jnp.exp(log_step)[:, None]
    a, b = lam_re * step, lam_im * step
    k = jnp.arange(L + 1, dtype=F32)[:, None, None]
    mag = jnp.exp(a[None] * k)
    pr, pi = mag * jnp.cos(b[None] * k), mag * jnp.sin(b[None] * k)
    nr, ni = pr[1] - 1.0, pi[1]
    den = lam_re * lam_re + lam_im * lam_im
    cr, ci = (nr * lam_re + ni * lam_im) / den, (ni * lam_re - nr * lam_im) / den
    bbr = cr[..., None] * b_re - ci[..., None] * b_im
    bbi = cr[..., None] * b_im + ci[..., None] * b_re
    cpr = c_re[None] * pr[:, :, None, :] - c_im[None] * pi[:, :, None, :]
    cpi = c_re[None] * pi[:, :, None, :] + c_im[None] * pr[:, :, None, :]
    kk = jnp.einsum('kgcn,gnd->kgcd', jnp.concatenate([cpr[:L], -cpi[:L]], axis=-1),
                    jnp.concatenate([bbr, bbi], axis=1), precision=hp)
    pw_r, pw_i = pr[L - 1 - np.arange(L)], pi[L - 1 - np.arange(L)]
    p_r = pw_r[:, :, :, None] * bbr[None] - pw_i[:, :, :, None] * bbi[None]
    p_i = pw_r[:, :, :, None] * bbi[None] + pw_i[:, :, :, None] * bbr[None]
    p_r, p_i = p_r.transpose(1, 0, 3, 2), p_i.transpose(1, 0, 3, 2)
    q_r = cpr[1:].transpose(1, 3, 0, 2)
    q_i = -cpi[1:].transpose(1, 3, 0, 2)
    sup = SSM_SUPER
    nsg = ng // sup
    lw = L * sup * gc
    bf = lambda a: a.astype(BF16)
    lag_c = bf(kk).reshape(L, nsg, sup, gc, gc).transpose(1, 0, 2, 4, 3).reshape(nsg, L, sup * gc, gc)
    own_lag = np.arange(sup * gc)[:, None] // gc == np.arange(sup * gc)[None, :] // gc
    lag_sg = jnp.where(own_lag, jnp.matmul(lag_c, jnp.asarray(np.tile(np.eye(gc), (1, sup)), BF16)), 0)
    slab_rows = lambda a: bf(a).reshape(nsg, sup, L, gc, -1).transpose(0, 2, 1, 3, 4).reshape(nsg, lw, -1)
    p_c = jnp.concatenate([slab_rows(p_r), slab_rows(p_i)], axis=2)
    rep_p = np.kron(np.eye(2), np.tile(np.eye(ns), (1, sup)))
    own_p = (np.arange(lw)[:, None] // gc) % sup == (np.arange(2 * sup * ns)[None, :] // ns) % sup
    p = jnp.where(own_p, jnp.matmul(p_c, jnp.asarray(rep_p, BF16)), 0)
    q_c = jnp.concatenate([bf(q_r).reshape(nsg, sup * ns, L * gc), bf(q_i).reshape(nsg, sup * ns, L * gc)], axis=1)
    rep_q = np.einsum('ts,cd,b->tcsbd', np.eye(L), np.eye(gc), np.ones(sup)).reshape(L * gc, lw)
    own_q = (np.arange(2 * sup * ns)[:, None] // ns) % sup == (np.arange(lw)[None, :] // gc) % sup
    q = jnp.where(own_q, jnp.matmul(q_c, jnp.asarray(rep_q, BF16)), 0)
    return lag_sg, p, q, pr[L].reshape(nsg, 1, sup * ns), pi[L].reshape(nsg, 1, sup * ns)


def kernel(x, norm_mix_g, w_in, cmp_pe, cmp_k_w1, cmp_k_w2, cmp_v_w1, cmp_v_w2, ssm_lam_re, ssm_lam_im, ssm_log_step, ssm_b_re, ssm_b_im, ssm_c_re, ssm_c_im, ssm_d, w_attn_branch, w_ssm_val, w_ssm_gate, w_out, norm_mlp_g, w_up, w_down, norm_final_g):
    batch, seq, d = x.shape
    depth = w_in.shape[0]
    assert depth == 1, "the final rmsnorm is fused into the single layer's mlp kernel"
    nsa_w = NSA_HEADS * HEAD_DIM
    ssm_w = ssm_d.shape[1]
    o_q, o_kv, o_g = nsa_w, nsa_w + 6 * KV_LANES, nsa_w + 6 * KV_LANES + 3 * NSA_HEADS
    o_u = o_g + ssm_w
    cos, slo, shi = _rope_tables(seq)
    ovt = _selection_constants(seq)
    head_order = np.array([g * HEADS_PER_GROUP + h for h in range(HEADS_PER_GROUP) for g in range(NSA_KV_HEADS)])
    x2 = x.reshape(batch * seq, d)
    for l in range(depth):
        wl = w_in[l]
        wq = _interleave_heads(wl[:, :o_q]).astype(BF16)
        wkv = wl[:, o_q:o_kv].astype(BF16)
        wg = jnp.pad(wl[:, o_kv:o_g], ((0, 0), (0, V7X_LANES - 3 * NSA_HEADS))).astype(BF16)
        wu = wl[:, o_g:o_u].astype(BF16)
        wga = wl[:, o_u:o_u + d].astype(BF16)
        wgb = wl[:, o_u + d:].astype(BF16)
        g_mix = norm_mix_g[l].reshape(1, d)
        qraw, qrot, kc_raw, vc_raw, ksat, vs, kwt, vw, gates, u = _proj_call(
            x2, g_mix, wq, wkv, wg, wu, cos, slo, shi, batch, seq)

        pea, peb, kw1a, kw1b, kw2 = _compress_weights(cmp_pe[l], cmp_k_w1[l], cmp_k_w2[l])
        _, _, vw1a, vw1b, vw2 = _compress_weights(cmp_pe[l], cmp_v_w1[l], cmp_v_w2[l])
        kct, vc = _compress_call(kc_raw, vc_raw, pea, peb, kw1a, kw1b, kw2.T, vw1a, vw1b, vw2, batch, seq)

        b3 = lambda a: a.reshape(batch, seq, a.shape[-1])
        attn = _nsa_call(b3(qraw), b3(qrot), b3(gates), kct, vc, ksat, b3(vs), kwt, b3(vw), ovt)
        attn = attn.reshape(batch * seq, nsa_w)

        lag, pm, qm, lr, li = _s5_matrices(ssm_lam_re[l], ssm_lam_im[l], ssm_log_step[l], ssm_b_re[l], ssm_b_im[l],
                                           ssm_c_re[l], ssm_c_im[l])
        y_ssm = _s5_call(u, lag, pm, qm, lr, li, ssm_d[l].reshape(1, ssm_w), batch, seq)

        wattn = w_attn_branch[l].reshape(NSA_HEADS, HEAD_DIM, d)[head_order].reshape(nsa_w, d).astype(BF16)
        x1 = _merge_call(x2, g_mix, attn, y_ssm, wga, wgb, wattn, w_ssm_val[l].astype(BF16),
                         w_ssm_gate[l].astype(BF16), w_out[l].astype(BF16))
        x2 = _mlp_call(x1, norm_mlp_g[l].reshape(1, d), w_up[l].astype(BF16), w_down[l].astype(BF16),
                       norm_final_g.reshape(1, d))
    return x2.reshape(batch, seq, d)
```

```python
import functools
import math

import jax
import jax.numpy as jnp
import numpy as np
from jax import lax
from jax.experimental import pallas as pl
from jax.experimental.pallas import tpu as pltpu

NSA_HEADS = 8
NSA_KV_HEADS = 2
HEAD_DIM = 64
CMP_BLOCK = 32
CMP_STRIDE = 16
CMP_HIDDEN = 256
SEL_BLOCK = 64
N_SEL = 16
WINDOW = 512
FORCE_BONUS = 1e3
NEG_INF = -1e30
ROPE_THETA = 500000.0
ROPE_DIM = HEAD_DIM // 4
SSM_GROUP = 16
EPS = 1e-6

HEADS_PER_GROUP = NSA_HEADS // NSA_KV_HEADS
KV_LANES = NSA_KV_HEADS * HEAD_DIM
Q_LANES = NSA_HEADS * HEAD_DIM

V7X_LANES = 128
V7X_MXU = 256
V7X_VMEM_BYTES = 64 * 1024 * 1024
VMEM_LIMIT = V7X_VMEM_BYTES - 8 * 1024 * 1024

ROW_TILE = 1024
SEL_TILE = 1024
NSA_QUERIES = 256
SSM_CHUNK = 8
SSM_SUPER = V7X_LANES // SSM_GROUP
FF_CHUNK = 1024

LOG2E = math.log2(math.e)
MASK_BIAS = 1e30

BF16 = jnp.bfloat16
F32 = jnp.float32
_NT = (((1,), (1,)), ((), ()))


def _cparams(semantics):
    return pltpu.CompilerParams(dimension_semantics=semantics, vmem_limit_bytes=VMEM_LIMIT)


def _rmsnorm(x, g):
    return x * lax.rsqrt(jnp.mean(x * x, axis=-1, keepdims=True) + EPS) * g


def _gelu(x):
    return jax.nn.gelu(x)


def _rope_cols(x, cos, sin_lo, sin_hi):
    cols = []
    for c in range(x.shape[1] // V7X_LANES):
        xc = x[:, c * V7X_LANES:(c + 1) * V7X_LANES]
        up = pltpu.roll(xc, V7X_LANES - ROPE_DIM // 2, axis=1)
        dn = pltpu.roll(xc, ROPE_DIM // 2, axis=1)
        cols.append(xc * cos + up * sin_lo + dn * sin_hi)
    return jnp.concatenate(cols, axis=1) if len(cols) > 1 else cols[0]


def _values_with_ones(v):
    lane_head = lax.broadcasted_iota(jnp.int32, v.shape, 1) // HEAD_DIM
    return jnp.concatenate([jnp.where(lane_head == g, v, 1.0) for g in range(NSA_KV_HEADS)], axis=1).astype(BF16)


def _proj_kernel(x_ref, g_ref, wq_ref, wkv_ref, wg_ref, wu_ref, cos_ref, slo_ref, shi_ref,
                 qraw_ref, qrot_ref, kc_ref, vc_ref, ksat_ref, vs_ref, kwt_ref, vw_ref, gate_ref, u_ref, *, seq_tiles):
    hb = _rmsnorm(x_ref[...], g_ref[...]).astype(BF16)
    cos, slo, shi = cos_ref[...], slo_ref[...], shi_ref[...]
    q = jnp.dot(hb, wq_ref[...], preferred_element_type=F32) * (HEAD_DIM ** -0.5 * LOG2E)
    qraw_ref[...] = q.astype(BF16)
    qrot_ref[...] = _rope_cols(q, cos, slo, shi).astype(BF16)
    kv = jnp.dot(hb, wkv_ref[...], preferred_element_type=F32)
    w = KV_LANES
    kc_ref[...] = kv[:, 0 * w:1 * w]
    vc_ref[...] = kv[:, 1 * w:2 * w]
    tm = x_ref.shape[0]
    pos = (pl.program_id(0) % seq_tiles) * tm + lax.broadcasted_iota(jnp.int32, (V7X_LANES, tm), 1)
    onehot = jnp.where(lax.broadcasted_iota(jnp.int32, (V7X_LANES, tm), 0) == pos // SEL_BLOCK, 1.0, 0.0)
    ksat_ref[0, :w, :] = _rope_cols(kv[:, 2 * w:3 * w], cos, slo, shi).T.astype(BF16)
    ksat_ref[0, w:, :] = onehot.astype(BF16)
    vs_ref[...] = _values_with_ones(kv[:, 3 * w:4 * w])
    kwt_ref[0] = _rope_cols(kv[:, 4 * w:5 * w], cos, slo, shi).T.astype(BF16)
    vw_ref[...] = _values_with_ones(kv[:, 5 * w:6 * w])
    gate_ref[...] = jax.nn.sigmoid(jnp.dot(hb, wg_ref[...], preferred_element_type=F32))
    u_ref[...] = jnp.dot(hb, wu_ref[...], preferred_element_type=F32)


def _proj_call(x2, g, wq, wkv, wg, wu, cos, slo, shi, batch, seq):
    t, d = x2.shape
    tm = ROW_TILE
    s_tiles = seq // tm
    row = lambda i: (i, 0)
    pos = lambda i: (i % s_tiles, 0)
    trans = lambda i: (i // s_tiles, 0, i % s_tiles)
    wspec = lambda a: pl.BlockSpec(a.shape, lambda i: (0, 0), pipeline_mode=pl.Buffered(1))
    rows_out = lambda n, dt: (jax.ShapeDtypeStruct((t, n), dt), pl.BlockSpec((tm, n), row))
    trans_out = lambda n: (jax.ShapeDtypeStruct((batch, n, seq), BF16), pl.BlockSpec((1, n, tm), trans))
    outs = [rows_out(Q_LANES, BF16), rows_out(Q_LANES, BF16), rows_out(KV_LANES, F32), rows_out(KV_LANES, F32),
            trans_out(KV_LANES + V7X_LANES), rows_out(NSA_KV_HEADS * KV_LANES, BF16),
            trans_out(KV_LANES), rows_out(NSA_KV_HEADS * KV_LANES, BF16),
            rows_out(V7X_LANES, F32), rows_out(wu.shape[1], F32)]
    assert seq // SEL_BLOCK <= V7X_LANES, "the selection-block one-hot must fit one lane tile"
    return pl.pallas_call(
        functools.partial(_proj_kernel, seq_tiles=s_tiles),
        grid=(t // tm,),
        in_specs=[pl.BlockSpec((tm, d), row), wspec(g), wspec(wq), wspec(wkv), wspec(wg), wspec(wu),
                  pl.BlockSpec((tm, V7X_LANES), pos), pl.BlockSpec((tm, V7X_LANES), pos),
                  pl.BlockSpec((tm, V7X_LANES), pos)],
        out_specs=[o[1] for o in outs],
        out_shape=[o[0] for o in outs],
        compiler_params=_cparams(("arbitrary",)),
        name="proj",
    )(x2, g, wq, wkv, wg, wu, cos, slo, shi)


def _compress_kernel(k_ref, v_ref, pea_ref, peb_ref, kw1a_ref, kw1b_ref, kw2t_ref, vw1a_ref, vw1b_ref, vw2_ref,
                     kct_ref, vc_ref, ca_ref, cb_ref):
    nch = ca_ref.shape[0]

    def hidden(src_ref, w1a_ref, w1b_ref):
        for t in range(CMP_STRIDE):
            rows = src_ref[pl.ds(t, nch, stride=CMP_STRIDE), :]
            sl = slice(t * KV_LANES, (t + 1) * KV_LANES)
            ca_ref[:, sl] = (rows + pea_ref[:, sl]).astype(BF16)
            cb_ref[:, sl] = (rows + peb_ref[:, sl]).astype(BF16)
        ha = jnp.dot(ca_ref[...], w1a_ref[...], preferred_element_type=F32)
        hb = jnp.dot(cb_ref[...], w1b_ref[...], preferred_element_type=F32)
        return _gelu(ha + pltpu.roll(hb, nch - 1, axis=0)).astype(BF16)

    kct_ref[0] = lax.dot_general(kw2t_ref[...], hidden(k_ref, kw1a_ref, kw1b_ref), _NT,
                                 preferred_element_type=F32).astype(BF16)
    vc_ref[0] = jnp.dot(hidden(v_ref, vw1a_ref, vw1b_ref), vw2_ref[...], preferred_element_type=F32).astype(BF16)


def _compress_call(kc_raw, vc_raw, pea, peb, kw1a, kw1b, kw2, vw1a, vw1b, vw2, batch, seq):
    nch = seq // CMP_STRIDE
    wspec = lambda a: pl.BlockSpec(a.shape, lambda b: (0, 0))
    return pl.pallas_call(
        _compress_kernel,
        grid=(batch,),
        in_specs=[pl.BlockSpec((seq, KV_LANES), lambda b: (b, 0)), pl.BlockSpec((seq, KV_LANES), lambda b: (b, 0)),
                  wspec(pea), wspec(peb), wspec(kw1a), wspec(kw1b), wspec(kw2), wspec(vw1a), wspec(vw1b), wspec(vw2)],
        out_specs=[pl.BlockSpec((1, KV_LANES, nch), lambda b: (b, 0, 0)),
                   pl.BlockSpec((1, nch, KV_LANES), lambda b: (b, 0, 0))],
        out_shape=[jax.ShapeDtypeStruct((batch, KV_LANES, nch), BF16), jax.ShapeDtypeStruct((batch, nch, KV_LANES), BF16)],
        scratch_shapes=[pltpu.VMEM((nch, CMP_STRIDE * KV_LANES), BF16), pltpu.VMEM((nch, CMP_STRIDE * KV_LANES), BF16)],
        compiler_params=_cparams(("arbitrary",)),
        name="compress",
    )(kc_raw, vc_raw, pea, peb, kw1a, kw1b, kw2, vw1a, vw1b, vw2)


def _window_keys(nq):
    return WINDOW + nq


def _nsa_kernel(qraw_ref, qrot_ref, gate_ref, kct_ref, vc_ref, ksat_ref, vs_ref, kwt_ref, vw_ref, ovt_ref,
                o_ref, score_ref, qa_ref, *, n_sel):
    i = pl.program_id(1)
    nq = qraw_ref.shape[1]
    hg = HEADS_PER_GROUP
    rows = hg * nq
    lanes_gq = NSA_KV_HEADS * nq
    nb = score_ref.shape[0]
    nc = kct_ref.shape[2]
    t0 = i * nq
    blk_first = t0 // SEL_BLOCK
    blk_last = blk_first + nq // SEL_BLOCK - 1
    tq = t0 + lax.broadcasted_iota(jnp.int32, (nq, 1), 0)
    tq_rows = t0 + lax.broadcasted_iota(jnp.int32, (rows, 1), 0) % nq

    def add_per_query(s, bias):
        return (s.reshape(hg, nq, s.shape[-1]) + bias[None]).reshape(s.shape)

    own_lanes = [lax.broadcasted_iota(jnp.int32, (nq, KV_LANES), 1) // HEAD_DIM == g for g in range(NSA_KV_HEADS)]

    def group_q(ref, g):
        return jnp.concatenate([jnp.where(own_lanes[g], ref[0, :, h * KV_LANES:(h + 1) * KV_LANES], 0.0).astype(BF16)
                                for h in range(hg)], axis=0)

    def normalized(pv):
        return pv * (1.0 / pltpu.roll(pv, HEAD_DIM, axis=1))

    nwin = _window_keys(nq)
    w0 = pl.multiple_of(jnp.maximum(blk_first - WINDOW // SEL_BLOCK, 0) * SEL_BLOCK, V7X_LANES)
    kp = w0 + lax.broadcasted_iota(jnp.int32, (1, nwin), 1)
    bias_w = jnp.where(kp <= tq, jnp.where(kp > tq - WINDOW, 0.0, NEG_INF), NEG_INF)
    o_win = []
    for g in range(NSA_KV_HEADS):
        s = jnp.dot(group_q(qrot_ref, g), kwt_ref[0, :, pl.ds(w0, nwin)], preferred_element_type=F32)
        s = add_per_query(s, bias_w)
        e = jnp.exp2(s - jnp.max(s, axis=-1, keepdims=True))
        v = vw_ref[0, pl.ds(w0, nwin), g * KV_LANES:(g + 1) * KV_LANES]
        o_win.append(normalized(jnp.dot(e.astype(BF16), v, preferred_element_type=F32)))

    def compressed_branch(width):
        cmp_end = lax.broadcasted_iota(jnp.int32, (1, width), 1) * CMP_STRIDE + (CMP_BLOCK - 1)
        bias_c = jnp.where(cmp_end <= tq, 0.0, NEG_INF)
        any_valid = tq_rows >= CMP_BLOCK - 1
        outs, p_sum = [], []
        for g in range(NSA_KV_HEADS):
            s = jnp.dot(group_q(qraw_ref, g), kct_ref[0, :, :width], preferred_element_type=F32)
            s = add_per_query(s, bias_c)
            e = jnp.exp2(s - jnp.max(s, axis=-1, keepdims=True))
            p = e * jnp.where(any_valid, 1.0 / jnp.sum(e, axis=-1, keepdims=True), 0.0)
            outs.append(jnp.dot(p.astype(BF16), vc_ref[0, :width, :], preferred_element_type=F32))
            ph = p[0:nq]
            for h in range(1, hg):
                ph = ph + p[h * nq:(h + 1) * nq]
            p_sum.append(ph)
        p2 = jnp.concatenate(p_sum, axis=0)
        p_hi = p2.astype(BF16)
        p_lo = (p2 - p_hi.astype(F32)).astype(BF16)
        ovt = ovt_ref[:, :width]
        imp = (lax.dot_general(ovt, p_hi, _NT, preferred_element_type=F32)
               + lax.dot_general(ovt, p_lo, _NT, preferred_element_type=F32))
        return tuple(outs) + (imp,)

    *o_cmp, imp_t = compressed_branch(nc)

    blk_q = (t0 + lax.broadcasted_iota(jnp.int32, (1, lanes_gq), 1) % nq) // SEL_BLOCK
    jfull = lax.broadcasted_iota(jnp.int32, (nb, lanes_gq), 0)
    valid = jfull <= blk_q
    forced = (jfull == 0) | (jfull == blk_q) | (jfull == blk_q - 1)
    score = jnp.where(valid, imp_t + jnp.where(forced, FORCE_BONUS, 0.0), NEG_INF)

    free = jnp.where(valid, jnp.where(forced, NEG_INF, imp_t), NEG_INF)
    n_forced = 1 + jnp.where(blk_q >= 1, 1, 0) + jnp.where(blk_q >= 2, 1, 0)
    rest, cut = free, {}
    for k in range(1, n_sel):
        thr = jnp.max(rest, axis=0, keepdims=True)
        rest = jnp.where(rest >= thr, NEG_INF, rest)
        cut[k] = thr
    thr = cut[n_sel - 1]
    for nf in (2, 3):
        thr = jnp.where(n_forced == nf, cut[n_sel - nf], thr)
    sel_t = jnp.where(valid, jnp.where(forced, 1.0, jnp.where(free >= thr, 1.0, 0.0)), 0.0)

    def ranked_members():
        def count_above_or_tied_earlier(jp, cnt):
            row = jnp.broadcast_to(score_ref[pl.ds(jp, 1), :], score.shape)
            tie = jnp.where(jp < jfull, 1.0, 0.0)
            return cnt + jnp.where(row > score, 1.0, jnp.where(row == score, tie, 0.0))

        score_ref[...] = score
        cnt = lax.fori_loop(0, blk_last + 1, count_above_or_tied_earlier, jnp.zeros(score.shape, F32))
        return jnp.where(valid, jnp.where(cnt < float(n_sel), 1.0, 0.0), 0.0)

    picked = jnp.sum(sel_t, axis=0, keepdims=True)
    wrong = jnp.sum(jnp.where(picked == jnp.minimum(blk_q + 1, n_sel).astype(F32), 0.0, 1.0))
    sel_t = lax.cond(wrong == 0.0, lambda: sel_t, ranked_members)
    bias = ((sel_t.T - 1.0) * MASK_BIAS).astype(BF16)

    n_tiles = blk_last // (SEL_TILE // SEL_BLOCK) + 1
    lane_t = lax.broadcasted_iota(jnp.int32, (1, SEL_TILE), 1)
    for g in range(NSA_KV_HEADS):
        qa_ref[g * rows:(g + 1) * rows, :KV_LANES] = group_q(qrot_ref, g)
        qa_ref[g * rows:(g + 1) * rows, KV_LANES:] = jnp.concatenate([bias[g * nq:(g + 1) * nq]] * hg, axis=0)

    def tile_body(c, carry, causal):
        k0 = pl.multiple_of(c * SEL_TILE, SEL_TILE)
        ka = ksat_ref[0, :, pl.ds(k0, SEL_TILE)]
        if causal:
            bias_d = jnp.where((k0 + lane_t) <= tq, 0.0, NEG_INF)
        new = []
        for g in range(NSA_KV_HEADS):
            m, acc = carry[g]
            s = jnp.dot(qa_ref[g * rows:(g + 1) * rows], ka, preferred_element_type=F32)
            if causal:
                s = add_per_query(s, bias_d)
            m_new = jnp.maximum(m, jnp.max(s, axis=-1, keepdims=True))
            p = jnp.exp2(s - m_new)
            v = vs_ref[0, pl.ds(k0, SEL_TILE), g * KV_LANES:(g + 1) * KV_LANES]
            acc = jnp.exp2(m - m_new) * acc + jnp.dot(p.astype(BF16), v, preferred_element_type=F32)
            new.append((m_new, acc))
        return tuple(new)

    init = tuple((jnp.full((rows, 1), NEG_INF, F32), jnp.zeros((rows, KV_LANES), F32)) for _ in range(NSA_KV_HEADS))
    carry = lax.fori_loop(0, n_tiles - 1, functools.partial(tile_body, causal=False), init)
    carry = tile_body(n_tiles - 1, carry, causal=True)
    o_sel = [normalized(acc) for _, acc in carry]

    gt = gate_ref[0]
    for h in range(hg):
        r = slice(h * nq, (h + 1) * nq)
        parts = []
        for g in range(NSA_KV_HEADS):
            c = (g * hg + h) * 3
            parts.append(gt[:, c:c + 1] * o_cmp[g][r] + gt[:, c + 1:c + 2] * o_sel[g][r] + gt[:, c + 2:c + 3] * o_win[g][r])
        out = parts[0]
        for g in range(1, NSA_KV_HEADS):
            out = jnp.where(own_lanes[g], parts[g], out)
        o_ref[0, :, h * KV_LANES:(h + 1) * KV_LANES] = out.astype(BF16)


def _nsa_call(qraw, qrot, gates, kct, vc, ksat, vs, kwt, vw, ovt):
    batch, seq, _ = qraw.shape
    nq = NSA_QUERIES
    assert nq % V7X_LANES == 0 and seq % nq == 0 and seq >= _window_keys(nq) and seq % SEL_TILE == 0
    step = lambda n: pl.BlockSpec((1, nq, n), lambda b, i: (b, i, 0))
    whole = lambda a: pl.BlockSpec((1,) + a.shape[1:], lambda b, i: (b, 0, 0))
    return pl.pallas_call(
        functools.partial(_nsa_kernel, n_sel=min(N_SEL, seq // SEL_BLOCK)),
        grid=(batch, seq // nq),
        in_specs=[step(Q_LANES), step(Q_LANES), step(V7X_LANES),
                  whole(kct), whole(vc), whole(ksat), whole(vs), whole(kwt), whole(vw),
                  pl.BlockSpec(ovt.shape, lambda b, i: (0, 0))],
        out_specs=step(HEADS_PER_GROUP * KV_LANES),
        out_shape=jax.ShapeDtypeStruct((batch, seq, HEADS_PER_GROUP * KV_LANES), BF16),
        scratch_shapes=[pltpu.VMEM((V7X_LANES, NSA_KV_HEADS * nq), F32),
                        pltpu.VMEM((NSA_KV_HEADS * HEADS_PER_GROUP * nq, KV_LANES + V7X_LANES), BF16)],
        compiler_params=_cparams(("arbitrary", "arbitrary")),
        name="nsa",
    )(qraw, qrot, gates, kct, vc, ksat, vs, kwt, vw, ovt)


def _s5_kernel(u_ref, lag_ref, p_ref, q_ref, lr_ref, li_ref, d_ref, o_ref, m_ref, x_ref, y_ref, z_ref, sp_ref):
    nch = x_ref.shape[0]
    half = z_ref.shape[1] // 2
    lanes = V7X_LANES

    @pl.when(pl.program_id(1) == 0)
    def _():
        for s in range(SSM_CHUNK):
            for t in range(SSM_CHUNK):
                tile = lag_ref[0, t - s] if t >= s else jnp.zeros((lanes, lanes), BF16)
                m_ref[s * lanes:(s + 1) * lanes, t * lanes:(t + 1) * lanes] = tile

    for t in range(SSM_CHUNK):
        x_ref[:, t * lanes:(t + 1) * lanes] = u_ref[pl.ds(t, nch, stride=SSM_CHUNK), :].astype(BF16)
    for c0 in range(0, SSM_CHUNK * lanes, V7X_MXU):
        c1 = c0 + V7X_MXU
        y_ref[:, c0:c1] = jnp.dot(x_ref[:, :c1], m_ref[:c1, c0:c1], preferred_element_type=F32)
    z_ref[...] = jnp.dot(x_ref[...], p_ref[0], preferred_element_type=F32)
    lr, li = lr_ref[0], li_ref[0]

    def scan_body(k, carry):
        sr, si = carry
        sp_ref[pl.ds(k, 1), 0:half] = sr
        sp_ref[pl.ds(k, 1), half:2 * half] = si
        zr = z_ref[pl.ds(k, 1), 0:half]
        zi = z_ref[pl.ds(k, 1), half:2 * half]
        return lr * sr - li * si + zr, lr * si + li * sr + zi

    zero = jnp.zeros((1, half), F32)
    lax.fori_loop(0, nch, scan_body, (zero, zero), unroll=32)
    y = y_ref[...] + jnp.dot(sp_ref[...].astype(BF16), q_ref[0], preferred_element_type=F32)
    d = d_ref[...]
    for t in range(SSM_CHUNK):
        yt = y[:, t * lanes:(t + 1) * lanes] + d * u_ref[pl.ds(t, nch, stride=SSM_CHUNK), :]
        o_ref[pl.ds(t, nch, stride=SSM_CHUNK), :] = _gelu(yt)


def _s5_call(u, lag, p, q, lr, li, dskip, batch, seq):
    nsg = lag.shape[0]
    nch = seq // SSM_CHUNK
    nstate = q.shape[1]
    lw = SSM_CHUNK * V7X_LANES
    slab = lambda a: pl.BlockSpec((1,) + a.shape[1:], lambda g, b: (g,) + (0,) * (a.ndim - 1),
                                  pipeline_mode=pl.Buffered(1))
    return pl.pallas_call(
        _s5_kernel,
        grid=(nsg, batch),
        in_specs=[pl.BlockSpec((seq, V7X_LANES), lambda g, b: (b, g)), slab(lag), slab(p), slab(q),
                  pl.BlockSpec((1, 1, nstate // 2), lambda g, b: (g, 0, 0)),
                  pl.BlockSpec((1, 1, nstate // 2), lambda g, b: (g, 0, 0)),
                  pl.BlockSpec((1, V7X_LANES), lambda g, b: (0, g))],
        out_specs=pl.BlockSpec((seq, V7X_LANES), lambda g, b: (b, g)),
        out_shape=jax.ShapeDtypeStruct(u.shape, F32),
        scratch_shapes=[pltpu.VMEM((lw, lw), BF16), pltpu.VMEM((nch, lw), BF16), pltpu.VMEM((nch, lw), F32),
                        pltpu.VMEM((nch, nstate), F32), pltpu.VMEM((nch, nstate), F32)],
        compiler_params=_cparams(("arbitrary", "arbitrary")),
        name="s5",
    )(u, lag, p, q, lr, li, dskip)


def _merge_kernel(x_ref, g_ref, attn_ref, ssm_ref, wga_ref, wgb_ref, wattn_ref, wval_ref, wgate_ref, wout_ref, o_ref):
    x = x_ref[...]
    hb = _rmsnorm(x, g_ref[...]).astype(BF16)
    dot = functools.partial(jnp.dot, preferred_element_type=F32)
    y_a = dot(attn_ref[...], wattn_ref[...])
    ys = ssm_ref[...].astype(BF16)
    y_b = dot(ys, wval_ref[...]) * jax.nn.sigmoid(dot(ys, wgate_ref[...]))
    merged = jax.nn.sigmoid(dot(hb, wga_ref[...])) * y_a + jax.nn.sigmoid(dot(hb, wgb_ref[...])) * y_b
    o_ref[...] = x + dot(merged.astype(BF16), wout_ref[...])


def _merge_call(x2, g, attn, ssm, wga, wgb, wattn, wval, wgate, wout):
    t, d = x2.shape
    tm = ROW_TILE
    row = lambda i: (i, 0)
    wspec = lambda a: pl.BlockSpec(a.shape, lambda i: (0, 0), pipeline_mode=pl.Buffered(1))
    return pl.pallas_call(
        _merge_kernel,
        grid=(t // tm,),
        in_specs=[pl.BlockSpec((tm, d), row), wspec(g), pl.BlockSpec((tm, attn.shape[1]), row),
                  pl.BlockSpec((tm, ssm.shape[1]), row), wspec(wga), wspec(wgb), wspec(wattn), wspec(wval),
                  wspec(wgate), wspec(wout)],
        out_specs=pl.BlockSpec((tm, d), row),
        out_shape=jax.ShapeDtypeStruct((t, d), F32),
        compiler_params=_cparams(("arbitrary",)),
        name="merge",
    )(x2, g, attn, ssm, wga, wgb, wattn, wval, wgate, wout)


def _mlp_kernel(x_ref, g_ref, wup_ref, wdown_ref, gf_ref, o_ref):
    x = x_ref[...]
    hb = _rmsnorm(x, g_ref[...]).astype(BF16)
    acc = x
    for c in range(wup_ref.shape[1] // FF_CHUNK):
        sl = slice(c * FF_CHUNK, (c + 1) * FF_CHUNK)
        up = jnp.maximum(jnp.dot(hb, wup_ref[:, sl], preferred_element_type=F32), 0.0)
        acc = acc + jnp.dot((up * up).astype(BF16), wdown_ref[sl, :], preferred_element_type=F32)
    o_ref[...] = _rmsnorm(acc, gf_ref[...])


def _mlp_call(x1, g, wup, wdown, gf):
    t, d = x1.shape
    tm = ROW_TILE
    row = lambda i: (i, 0)
    wspec = lambda a: pl.BlockSpec(a.shape, lambda i: (0, 0), pipeline_mode=pl.Buffered(1))
    return pl.pallas_call(
        _mlp_kernel,
        grid=(t // tm,),
        in_specs=[pl.BlockSpec((tm, d), row), wspec(g), wspec(wup), wspec(wdown), wspec(gf)],
        out_specs=pl.BlockSpec((tm, d), row),
        out_shape=jax.ShapeDtypeStruct((t, d), F32),
        compiler_params=_cparams(("arbitrary",)),
        name="mlp",
    )(x1, g, wup, wdown, gf)


def _interleave_heads(wq):
    d = wq.shape[0]
    return wq.reshape(d, NSA_KV_HEADS, HEADS_PER_GROUP, HEAD_DIM).transpose(0, 2, 1, 3).reshape(d, Q_LANES)


def _rope_tables(seq):
    half = ROPE_DIM // 2
    inv = ROPE_THETA ** (-(jnp.arange(half, dtype=F32) * 2.0) / ROPE_DIM)
    ang = jnp.arange(seq, dtype=F32)[:, None] * inv[None, :]
    cos, sin = jnp.cos(ang), jnp.sin(ang)
    rest = HEAD_DIM - ROPE_DIM
    cos_h = jnp.concatenate([cos, cos, jnp.ones((seq, rest), F32)], axis=1)
    slo_h = jnp.concatenate([-sin, jnp.zeros((seq, half + rest), F32)], axis=1)
    shi_h = jnp.concatenate([jnp.zeros((seq, half), F32), sin, jnp.zeros((seq, rest), F32)], axis=1)
    reps = V7X_LANES // HEAD_DIM
    return jnp.tile(cos_h, (1, reps)), jnp.tile(slo_h, (1, reps)), jnp.tile(shi_h, (1, reps))


def _compress_weights(pe, w1, w2):
    eye = jnp.eye(NSA_KV_HEADS, dtype=F32)
    w1e = jnp.einsum('tdj,gk->tgdkj', w1.reshape(CMP_BLOCK, HEAD_DIM, CMP_HIDDEN), eye)
    w1e = w1e.reshape(CMP_BLOCK * KV_LANES, NSA_KV_HEADS * CMP_HIDDEN).astype(BF16)
    w2e = jnp.einsum('jd,gk->gjkd', w2, eye).reshape(NSA_KV_HEADS * CMP_HIDDEN, KV_LANES).astype(BF16)
    pee = jnp.tile(pe, (1, NSA_KV_HEADS)).reshape(1, CMP_BLOCK * KV_LANES)
    halfw = CMP_STRIDE * KV_LANES
    return pee[:, :halfw], pee[:, halfw:], w1e[:halfw], w1e[halfw:], w2e


def _selection_constants(seq):
    nc = seq // CMP_STRIDE - 1
    nb = seq // SEL_BLOCK
    n_np = np.arange(nc)[:, None] * CMP_STRIDE
    j_np = np.arange(nb)[None, :] * SEL_BLOCK
    overlap = ((n_np < j_np + SEL_BLOCK) & (n_np + CMP_BLOCK > j_np)).astype(np.float32)
    ovt = np.zeros((V7X_LANES, nc + 1), np.float32)
    ovt[:nb, :nc] = overlap.T
    return jnp.asarray(ovt, BF16)


def _s5_matrices(lam_re, lam_im, log_step, b_re, b_im, c_re, c_im):
    hp = lax.Precision.HIGHEST
    ng, ns = lam_re.shape
    gc = b_re.shape[-1]
    L = SSM_CHUNK
    step = jnp.exp(log_step)[:, None]
    a, b = lam_re * step, lam_im * step
    k = jnp.arange(L + 1, dtype=F32)[:, None, None]
    mag = jnp.exp(a[None] * k)
    pr, pi = mag * jnp.cos(b[None] * k), mag * jnp.sin(b[None] * k)
    nr, ni = pr[1] - 1.0, pi[1]
    den = lam_re * lam_re + lam_im * lam_im
    cr, ci = (nr * lam_re + ni * lam_im) / den, (ni * lam_re - nr * lam_im) / den
    bbr = cr[..., None] * b_re - ci[..., None] * b_im
    bbi = cr[..., None] * b_im + ci[..., None] * b_re
    cpr = c_re[None] * pr[:, :, None, :] - c_im[None] * pi[:, :, None, :]
    cpi = c_re[None] * pi[:, :, None, :] + c_im[None] * pr[:, :, None, :]
    kk = jnp.einsum('kgcn,gnd->kgcd', jnp.concatenate([cpr[:L], -cpi[:L]], axis=-1),
                    jnp.concatenate([bbr, bbi], axis=1), precision=hp)
    pw_r, pw_i = pr[L - 1 - np.arange(L)], pi[L - 1 - np.arange(L)]
    p_r = pw_r[:, :, :, None] * bbr[None] - pw_i[:, :, :, None] * bbi[None]
    p_i = pw_r[:, :, :, None] * bbi[None] + pw_i[:, :, :, None] * bbr[None]
    p_r, p_i = p_r.transpose(1, 0, 3, 2), p_i.transpose(1, 0, 3, 2)
    q_r = cpr[1:].transpose(1, 3, 0, 2)
    q_i = -cpi[1:].transpose(1, 3, 0, 2)
    sup = SSM_SUPER
    nsg = ng // sup
    lw = L * sup * gc
    bf = lambda a: a.astype(BF16)
    lag_c = bf(kk).reshape(L, nsg, sup, gc, gc).transpose(1, 0, 2, 4, 3).reshape(nsg, L, sup * gc, gc)
    own_lag = np.arange(sup * gc)[:, None] // gc == np.arange(sup * gc)[None, :] // gc
    lag_sg = jnp.where(own_lag, jnp.matmul(lag_c, jnp.asarray(np.tile(np.eye(gc), (1, sup)), BF16)), 0)
    slab_rows = lambda a: bf(a).reshape(nsg, sup, L, gc, -1).transpose(0, 2, 1, 3, 4).reshape(nsg, lw, -1)
    p_c = jnp.concatenate([slab_rows(p_r), slab_rows(p_i)], axis=2)
    rep_p = np.kron(np.eye(2), np.tile(np.eye(ns), (1, sup)))
    own_p = (np.arange(lw)[:, None] // gc) % sup == (np.arange(2 * sup * ns)[None, :] // ns) % sup
    p = jnp.where(own_p, jnp.matmul(p_c, jnp.asarray(rep_p, BF16)), 0)
    q_c = jnp.concatenate([bf(q_r).reshape(nsg, sup * ns, L * gc), bf(q_i).reshape(nsg, sup * ns, L * gc)], axis=1)
    rep_q = np.einsum('ts,cd,b->tcsbd', np.eye(L), np.eye(gc), np.ones(sup)).reshape(L * gc, lw)
    own_q = (np.arange(2 * sup * ns)[:, None] // ns) % sup == (np.arange(lw)[None, :] // gc) % sup
    q = jnp.where(own_q, jnp.matmul(q_c, jnp.asarray(rep_q, BF16)), 0)
    return lag_sg, p, q, pr[L].reshape(nsg, 1, sup * ns), pi[L].reshape(nsg, 1, sup * ns)


def kernel(x, norm_mix_g, w_in, cmp_pe, cmp_k_w1, cmp_k_w2, cmp_v_w1, cmp_v_w2, ssm_lam_re, ssm_lam_im, ssm_log_step, ssm_b_re, ssm_b_im, ssm_c_re, ssm_c_im, ssm_d, w_attn_branch, w_ssm_val, w_ssm_gate, w_out, norm_mlp_g, w_up, w_down, norm_final_g):
    batch, seq, d = x.shape
    depth = w_in.shape[0]
    assert depth == 1, "the final rmsnorm is fused into the single layer's mlp kernel"
    nsa_w = NSA_HEADS * HEAD_DIM
    ssm_w = ssm_d.shape[1]
    o_q, o_kv, o_g = nsa_w, nsa_w + 6 * KV_LANES, nsa_w + 6 * KV_LANES + 3 * NSA_HEADS
    o_u = o_g + ssm_w
    cos, slo, shi = _rope_tables(seq)
    ovt = _selection_constants(seq)
    head_order = np.array([g * HEADS_PER_GROUP + h for h in range(HEADS_PER_GROUP) for g in range(NSA_KV_HEADS)])
    x2 = x.reshape(batch * seq, d)
    for l in range(depth):
        wl = w_in[l]
        wq = _interleave_heads(wl[:, :o_q]).astype(BF16)
        wkv = wl[:, o_q:o_kv].astype(BF16)
        wg = jnp.pad(wl[:, o_kv:o_g], ((0, 0), (0, V7X_LANES - 3 * NSA_HEADS))).astype(BF16)
        wu = wl[:, o_g:o_u].astype(BF16)
        wga = wl[:, o_u:o_u + d].astype(BF16)
        wgb = wl[:, o_u + d:].astype(BF16)
        g_mix = norm_mix_g[l].reshape(1, d)
        qraw, qrot, kc_raw, vc_raw, ksat, vs, kwt, vw, gates, u = _proj_call(
            x2, g_mix, wq, wkv, wg, wu, cos, slo, shi, batch, seq)

        pea, peb, kw1a, kw1b, kw2 = _compress_weights(cmp_pe[l], cmp_k_w1[l], cmp_k_w2[l])
        _, _, vw1a, vw1b, vw2 = _compress_weights(cmp_pe[l], cmp_v_w1[l], cmp_v_w2[l])
        kct, vc = _compress_call(kc_raw, vc_raw, pea, peb, kw1a, kw1b, kw2.T, vw1a, vw1b, vw2, batch, seq)

        b3 = lambda a: a.reshape(batch, seq, a.shape[-1])
        attn = _nsa_call(b3(qraw), b3(qrot), b3(gates), kct, vc, ksat, b3(vs), kwt, b3(vw), ovt)
        attn = attn.reshape(batch * seq, nsa_w)

        lag, pm, qm, lr, li = _s5_matrices(ssm_lam_re[l], ssm_lam_im[l], ssm_log_step[l], ssm_b_re[l], ssm_b_im[l],
                                           ssm_c_re[l], ssm_c_im[l])
        y_ssm = _s5_call(u, lag, pm, qm, lr, li, ssm_d[l].reshape(1, ssm_w), batch, seq)

        wattn = w_attn_branch[l].reshape(NSA_HEADS, HEAD_DIM, d)[head_order].reshape(nsa_w, d).astype(BF16)
        x1 = _merge_call(x2, g_mix, attn, y_ssm, wga, wgb, wattn, w_ssm_val[l].astype(BF16),
                         w_ssm_gate[l].astype(BF16), w_out[l].astype(BF16))
        x2 = _mlp_call(x1, norm_mlp_g[l].reshape(1, d), w_up[l].astype(BF16), w_down[l].astype(BF16),
                       norm_final_g.reshape(1, d))
    return x2.reshape(batch, seq, d)
```

```python
import functools
import math

import jax
import jax.numpy as jnp
import numpy as np
from jax import lax
from jax.experimental import pallas as pl
from jax.experimental.pallas import tpu as pltpu

NSA_HEADS = 8
NSA_KV_HEADS = 2
HEAD_DIM = 64
CMP_BLOCK = 32
CMP_STRIDE = 16
CMP_HIDDEN = 256
SEL_BLOCK = 64
N_SEL = 16
WINDOW = 512
FORCE_BONUS = 1e3
NEG_INF = -1e30
ROPE_THETA = 500000.0
ROPE_DIM = HEAD_DIM // 4
SSM_GROUP = 16
EPS = 1e-6

HEADS_PER_GROUP = NSA_HEADS // NSA_KV_HEADS
KV_LANES = NSA_KV_HEADS * HEAD_DIM
Q_LANES = NSA_HEADS * HEAD_DIM

V7X_LANES = 128
V7X_SUBLANES = 8
V7X_MXU = 256
V7X_VMEM_BYTES = 64 * 1024 * 1024
VMEM_LIMIT = V7X_VMEM_BYTES - 8 * 1024 * 1024

ROW_TILE = 1024
SEL_TILE = 1024
NSA_QUERIES = 256
SSM_CHUNK = 8
SSM_SUPER = V7X_LANES // SSM_GROUP
SSM_CHAINS = 4
FF_CHUNK = 1024

LOG2E = math.log2(math.e)
MASK_BIAS = 1e30

BF16 = jnp.bfloat16
F32 = jnp.float32
_NT = (((1,), (1,)), ((), ()))


def _cparams(semantics):
    return pltpu.CompilerParams(dimension_semantics=semantics, vmem_limit_bytes=VMEM_LIMIT)


def _rmsnorm(x, g):
    return x * lax.rsqrt(jnp.mean(x * x, axis=-1, keepdims=True) + EPS) * g


def _gelu(x):
    return jax.nn.gelu(x)


def _rope_cols(x, cos, sin_lo, sin_hi):
    cols = []
    for c in range(x.shape[1] // V7X_LANES):
        xc = x[:, c * V7X_LANES:(c + 1) * V7X_LANES]
        up = pltpu.roll(xc, V7X_LANES - ROPE_DIM // 2, axis=1)
        dn = pltpu.roll(xc, ROPE_DIM // 2, axis=1)
        cols.append(xc * cos + up * sin_lo + dn * sin_hi)
    return jnp.concatenate(cols, axis=1) if len(cols) > 1 else cols[0]


def _values_with_ones(v):
    lane_head = lax.broadcasted_iota(jnp.int32, v.shape, 1) // HEAD_DIM
    return jnp.concatenate([jnp.where(lane_head == g, v, 1.0) for g in range(NSA_KV_HEADS)], axis=1).astype(BF16)


def _proj_kernel(x_ref, g_ref, wq_ref, wkv_ref, wg_ref, wu_ref, cos_ref, slo_ref, shi_ref,
                 qraw_ref, qrot_ref, kc_ref, vc_ref, ksat_ref, vs_ref, kwt_ref, vw_ref, gate_ref, u_ref, *, seq_tiles):
    hb = _rmsnorm(x_ref[...], g_ref[...]).astype(BF16)
    cos, slo, shi = cos_ref[...], slo_ref[...], shi_ref[...]
    q = jnp.dot(hb, wq_ref[...], preferred_element_type=F32) * (HEAD_DIM ** -0.5 * LOG2E)
    qraw_ref[...] = q.astype(BF16)
    qrot_ref[...] = _rope_cols(q, cos, slo, shi).astype(BF16)
    kv = jnp.dot(hb, wkv_ref[...], preferred_element_type=F32)
    w = KV_LANES
    kc_ref[...] = kv[:, 0 * w:1 * w]
    vc_ref[...] = kv[:, 1 * w:2 * w]
    tm = x_ref.shape[0]
    pos = (pl.program_id(0) % seq_tiles) * tm + lax.broadcasted_iota(jnp.int32, (V7X_LANES, tm), 1)
    onehot = jnp.where(lax.broadcasted_iota(jnp.int32, (V7X_LANES, tm), 0) == pos // SEL_BLOCK, 1.0, 0.0)
    ksat_ref[0, :w, :] = _rope_cols(kv[:, 2 * w:3 * w], cos, slo, shi).T.astype(BF16)
    ksat_ref[0, w:, :] = onehot.astype(BF16)
    vs_ref[...] = _values_with_ones(kv[:, 3 * w:4 * w])
    kwt_ref[0] = _rope_cols(kv[:, 4 * w:5 * w], cos, slo, shi).T.astype(BF16)
    vw_ref[...] = _values_with_ones(kv[:, 5 * w:6 * w])
    gate_ref[...] = jax.nn.sigmoid(jnp.dot(hb, wg_ref[...], preferred_element_type=F32))
    u = jnp.dot(hb, wu_ref[...], preferred_element_type=F32)
    for s in range(u_ref.shape[0]):
        u_ref[s] = u[:, s * V7X_LANES:(s + 1) * V7X_LANES]


def _proj_call(x2, g, wq, wkv, wg, wu, cos, slo, shi, batch, seq):
    t, d = x2.shape
    tm = ROW_TILE
    s_tiles = seq // tm
    row = lambda i: (i, 0)
    pos = lambda i: (i % s_tiles, 0)
    trans = lambda i: (i // s_tiles, 0, i % s_tiles)
    wspec = lambda a: pl.BlockSpec(a.shape, lambda i: (0, 0), pipeline_mode=pl.Buffered(1))
    rows_out = lambda n, dt: (jax.ShapeDtypeStruct((t, n), dt), pl.BlockSpec((tm, n), row))
    trans_out = lambda n: (jax.ShapeDtypeStruct((batch, n, seq), BF16), pl.BlockSpec((1, n, tm), trans))
    slabs_out = lambda n: (jax.ShapeDtypeStruct((n, t, V7X_LANES), F32),
                           pl.BlockSpec((n, tm, V7X_LANES), lambda i: (0, i, 0)))
    outs = [rows_out(Q_LANES, BF16), rows_out(Q_LANES, BF16), rows_out(KV_LANES, F32), rows_out(KV_LANES, F32),
            trans_out(KV_LANES + V7X_LANES), rows_out(NSA_KV_HEADS * KV_LANES, BF16),
            trans_out(KV_LANES), rows_out(NSA_KV_HEADS * KV_LANES, BF16),
            rows_out(V7X_LANES, F32), slabs_out(wu.shape[1] // V7X_LANES)]
    assert seq // SEL_BLOCK <= V7X_LANES, "the selection-block one-hot must fit one lane tile"
    return pl.pallas_call(
        functools.partial(_proj_kernel, seq_tiles=s_tiles),
        grid=(t // tm,),
        in_specs=[pl.BlockSpec((tm, d), row), wspec(g), wspec(wq), wspec(wkv), wspec(wg), wspec(wu),
                  pl.BlockSpec((tm, V7X_LANES), pos), pl.BlockSpec((tm, V7X_LANES), pos),
                  pl.BlockSpec((tm, V7X_LANES), pos)],
        out_specs=[o[1] for o in outs],
        out_shape=[o[0] for o in outs],
        compiler_params=_cparams(("arbitrary",)),
        name="proj",
    )(x2, g, wq, wkv, wg, wu, cos, slo, shi)


def _compress_kernel(k_ref, v_ref, pea_ref, peb_ref, kw1a_ref, kw1b_ref, kw2t_ref, vw1a_ref, vw1b_ref, vw2_ref,
                     kct_ref, vc_ref, ca_ref, cb_ref):
    nch = ca_ref.shape[0]

    def hidden(src_ref, w1a_ref, w1b_ref):
        for t in range(CMP_STRIDE):
            rows = src_ref[pl.ds(t, nch, stride=CMP_STRIDE), :]
            sl = slice(t * KV_LANES, (t + 1) * KV_LANES)
            ca_ref[:, sl] = (rows + pea_ref[:, sl]).astype(BF16)
            cb_ref[:, sl] = (rows + peb_ref[:, sl]).astype(BF16)
        ha = jnp.dot(ca_ref[...], w1a_ref[...], preferred_element_type=F32)
        hb = jnp.dot(cb_ref[...], w1b_ref[...], preferred_element_type=F32)
        return _gelu(ha + pltpu.roll(hb, nch - 1, axis=0)).astype(BF16)

    kct_ref[0] = lax.dot_general(kw2t_ref[...], hidden(k_ref, kw1a_ref, kw1b_ref), _NT,
                                 preferred_element_type=F32).astype(BF16)
    vc_ref[0] = jnp.dot(hidden(v_ref, vw1a_ref, vw1b_ref), vw2_ref[...], preferred_element_type=F32).astype(BF16)


def _compress_call(kc_raw, vc_raw, pea, peb, kw1a, kw1b, kw2, vw1a, vw1b, vw2, batch, seq):
    nch = seq // CMP_STRIDE
    wspec = lambda a: pl.BlockSpec(a.shape, lambda b: (0, 0))
    return pl.pallas_call(
        _compress_kernel,
        grid=(batch,),
        in_specs=[pl.BlockSpec((seq, KV_LANES), lambda b: (b, 0)), pl.BlockSpec((seq, KV_LANES), lambda b: (b, 0)),
                  wspec(pea), wspec(peb), wspec(kw1a), wspec(kw1b), wspec(kw2), wspec(vw1a), wspec(vw1b), wspec(vw2)],
        out_specs=[pl.BlockSpec((1, KV_LANES, nch), lambda b: (b, 0, 0)),
                   pl.BlockSpec((1, nch, KV_LANES), lambda b: (b, 0, 0))],
        out_shape=[jax.ShapeDtypeStruct((batch, KV_LANES, nch), BF16), jax.ShapeDtypeStruct((batch, nch, KV_LANES), BF16)],
        scratch_shapes=[pltpu.VMEM((nch, CMP_STRIDE * KV_LANES), BF16), pltpu.VMEM((nch, CMP_STRIDE * KV_LANES), BF16)],
        compiler_params=_cparams(("arbitrary",)),
        name="compress",
    )(kc_raw, vc_raw, pea, peb, kw1a, kw1b, kw2, vw1a, vw1b, vw2)


def _window_keys(nq):
    return WINDOW + nq


def _nsa_kernel(qraw_ref, qrot_ref, gate_ref, kct_ref, vc_ref, ksat_ref, vs_ref, kwt_ref, vw_ref, ovt_ref,
                o_ref, score_ref, qa_ref, *, n_sel):
    i = pl.program_id(1)
    nq = qraw_ref.shape[1]
    hg = HEADS_PER_GROUP
    rows = hg * nq
    lanes_gq = NSA_KV_HEADS * nq
    nb = score_ref.shape[0]
    nc = kct_ref.shape[2]
    t0 = i * nq
    blk_first = t0 // SEL_BLOCK
    blk_last = blk_first + nq // SEL_BLOCK - 1
    tq = t0 + lax.broadcasted_iota(jnp.int32, (nq, 1), 0)
    tq_rows = t0 + lax.broadcasted_iota(jnp.int32, (rows, 1), 0) % nq

    def add_per_query(s, bias):
        return (s.reshape(hg, nq, s.shape[-1]) + bias[None]).reshape(s.shape)

    own_lanes = [lax.broadcasted_iota(jnp.int32, (nq, KV_LANES), 1) // HEAD_DIM == g for g in range(NSA_KV_HEADS)]

    def group_q(ref, g):
        return jnp.concatenate([jnp.where(own_lanes[g], ref[0, :, h * KV_LANES:(h + 1) * KV_LANES], 0.0).astype(BF16)
                                for h in range(hg)], axis=0)

    def normalized(pv):
        return pv * (1.0 / pltpu.roll(pv, HEAD_DIM, axis=1))

    nwin = _window_keys(nq)
    w0 = pl.multiple_of(jnp.maximum(blk_first - WINDOW // SEL_BLOCK, 0) * SEL_BLOCK, V7X_LANES)
    kp = w0 + lax.broadcasted_iota(jnp.int32, (1, nwin), 1)
    bias_w = jnp.where(kp <= tq, jnp.where(kp > tq - WINDOW, 0.0, NEG_INF), NEG_INF)
    o_win = []
    for g in range(NSA_KV_HEADS):
        s = jnp.dot(group_q(qrot_ref, g), kwt_ref[0, :, pl.ds(w0, nwin)], preferred_element_type=F32)
        s = add_per_query(s, bias_w)
        e = jnp.exp2(s - jnp.max(s, axis=-1, keepdims=True))
        v = vw_ref[0, pl.ds(w0, nwin), g * KV_LANES:(g + 1) * KV_LANES]
        o_win.append(normalized(jnp.dot(e.astype(BF16), v, preferred_element_type=F32)))

    def compressed_branch(width):
        cmp_end = lax.broadcasted_iota(jnp.int32, (1, width), 1) * CMP_STRIDE + (CMP_BLOCK - 1)
        bias_c = jnp.where(cmp_end <= tq, 0.0, NEG_INF)
        any_valid = tq_rows >= CMP_BLOCK - 1
        outs, p_sum = [], []
        for g in range(NSA_KV_HEADS):
            s = jnp.dot(group_q(qraw_ref, g), kct_ref[0, :, :width], preferred_element_type=F32)
            s = add_per_query(s, bias_c)
            e = jnp.exp2(s - jnp.max(s, axis=-1, keepdims=True))
            p = e * jnp.where(any_valid, 1.0 / jnp.sum(e, axis=-1, keepdims=True), 0.0)
            outs.append(jnp.dot(p.astype(BF16), vc_ref[0, :width, :], preferred_element_type=F32))
            ph = p[0:nq]
            for h in range(1, hg):
                ph = ph + p[h * nq:(h + 1) * nq]
            p_sum.append(ph)
        p2 = jnp.concatenate(p_sum, axis=0)
        p_hi = p2.astype(BF16)
        p_lo = (p2 - p_hi.astype(F32)).astype(BF16)
        ovt = ovt_ref[:, :width]
        imp = (lax.dot_general(ovt, p_hi, _NT, preferred_element_type=F32)
               + lax.dot_general(ovt, p_lo, _NT, preferred_element_type=F32))
        return tuple(outs) + (imp,)

    *o_cmp, imp_t = compressed_branch(nc)

    blk_q = (t0 + lax.broadcasted_iota(jnp.int32, (1, lanes_gq), 1) % nq) // SEL_BLOCK
    jfull = lax.broadcasted_iota(jnp.int32, (nb, lanes_gq), 0)
    valid = jfull <= blk_q
    forced = (jfull == 0) | (jfull == blk_q) | (jfull == blk_q - 1)
    score = jnp.where(valid, imp_t + jnp.where(forced, FORCE_BONUS, 0.0), NEG_INF)

    free = jnp.where(valid, jnp.where(forced, NEG_INF, imp_t), NEG_INF)
    n_forced = 1 + jnp.where(blk_q >= 1, 1, 0) + jnp.where(blk_q >= 2, 1, 0)
    rest, cut = free, {}
    for k in range(1, n_sel):
        thr = jnp.max(rest, axis=0, keepdims=True)
        rest = jnp.where(rest >= thr, NEG_INF, rest)
        cut[k] = thr
    thr = cut[n_sel - 1]
    for nf in (2, 3):
        thr = jnp.where(n_forced == nf, cut[n_sel - nf], thr)
    sel_t = jnp.where(valid, jnp.where(forced, 1.0, jnp.where(free >= thr, 1.0, 0.0)), 0.0)

    def ranked_members():
        def count_above_or_tied_earlier(jp, cnt):
            row = jnp.broadcast_to(score_ref[pl.ds(jp, 1), :], score.shape)
            tie = jnp.where(jp < jfull, 1.0, 0.0)
            return cnt + jnp.where(row > score, 1.0, jnp.where(row == score, tie, 0.0))

        score_ref[...] = score
        cnt = lax.fori_loop(0, blk_last + 1, count_above_or_tied_earlier, jnp.zeros(score.shape, F32))
        return jnp.where(valid, jnp.where(cnt < float(n_sel), 1.0, 0.0), 0.0)

    picked = jnp.sum(sel_t, axis=0, keepdims=True)
    wrong = jnp.sum(jnp.where(picked == jnp.minimum(blk_q + 1, n_sel).astype(F32), 0.0, 1.0))
    sel_t = lax.cond(wrong == 0.0, lambda: sel_t, ranked_members)
    bias = ((sel_t.T - 1.0) * MASK_BIAS).astype(BF16)

    n_tiles = blk_last // (SEL_TILE // SEL_BLOCK) + 1
    lane_t = lax.broadcasted_iota(jnp.int32, (1, SEL_TILE), 1)
    for g in range(NSA_KV_HEADS):
        qa_ref[g * rows:(g + 1) * rows, :KV_LANES] = group_q(qrot_ref, g)
        qa_ref[g * rows:(g + 1) * rows, KV_LANES:] = jnp.concatenate([bias[g * nq:(g + 1) * nq]] * hg, axis=0)

    def tile_body(c, carry, causal):
        k0 = pl.multiple_of(c * SEL_TILE, SEL_TILE)
        ka = ksat_ref[0, :, pl.ds(k0, SEL_TILE)]
        if causal:
            bias_d = jnp.where((k0 + lane_t) <= tq, 0.0, NEG_INF)
        new = []
        for g in range(NSA_KV_HEADS):
            m, acc = carry[g]
            s = jnp.dot(qa_ref[g * rows:(g + 1) * rows], ka, preferred_element_type=F32)
            if causal:
                s = add_per_query(s, bias_d)
            m_new = jnp.maximum(m, jnp.max(s, axis=-1, keepdims=True))
            p = jnp.exp2(s - m_new)
            v = vs_ref[0, pl.ds(k0, SEL_TILE), g * KV_LANES:(g + 1) * KV_LANES]
            acc = jnp.exp2(m - m_new) * acc + jnp.dot(p.astype(BF16), v, preferred_element_type=F32)
            new.append((m_new, acc))
        return tuple(new)

    init = tuple((jnp.full((rows, 1), NEG_INF, F32), jnp.zeros((rows, KV_LANES), F32)) for _ in range(NSA_KV_HEADS))
    carry = lax.fori_loop(0, n_tiles - 1, functools.partial(tile_body, causal=False), init)
    carry = tile_body(n_tiles - 1, carry, causal=True)
    o_sel = [normalized(acc) for _, acc in carry]

    gt = gate_ref[0]
    for h in range(hg):
        r = slice(h * nq, (h + 1) * nq)
        parts = []
        for g in range(NSA_KV_HEADS):
            c = (g * hg + h) * 3
            parts.append(gt[:, c:c + 1] * o_cmp[g][r] + gt[:, c + 1:c + 2] * o_sel[g][r] + gt[:, c + 2:c + 3] * o_win[g][r])
        out = parts[0]
        for g in range(1, NSA_KV_HEADS):
            out = jnp.where(own_lanes[g], parts[g], out)
        o_ref[0, :, h * KV_LANES:(h + 1) * KV_LANES] = out.astype(BF16)


def _nsa_call(qraw, qrot, gates, kct, vc, ksat, vs, kwt, vw, ovt):
    batch, seq, _ = qraw.shape
    nq = NSA_QUERIES
    assert nq % V7X_LANES == 0 and seq % nq == 0 and seq >= _window_keys(nq) and seq % SEL_TILE == 0
    step = lambda n: pl.BlockSpec((1, nq, n), lambda b, i: (b, i, 0))
    whole = lambda a: pl.BlockSpec((1,) + a.shape[1:], lambda b, i: (b, 0, 0))
    return pl.pallas_call(
        functools.partial(_nsa_kernel, n_sel=min(N_SEL, seq // SEL_BLOCK)),
        grid=(batch, seq // nq),
        in_specs=[step(Q_LANES), step(Q_LANES), step(V7X_LANES),
                  whole(kct), whole(vc), whole(ksat), whole(vs), whole(kwt), whole(vw),
                  pl.BlockSpec(ovt.shape, lambda b, i: (0, 0))],
        out_specs=step(HEADS_PER_GROUP * KV_LANES),
        out_shape=jax.ShapeDtypeStruct((batch, seq, HEADS_PER_GROUP * KV_LANES), BF16),
        scratch_shapes=[pltpu.VMEM((V7X_LANES, NSA_KV_HEADS * nq), F32),
                        pltpu.VMEM((NSA_KV_HEADS * HEADS_PER_GROUP * nq, KV_LANES + V7X_LANES), BF16)],
        compiler_params=_cparams(("arbitrary", "arbitrary")),
        name="nsa",
    )(qraw, qrot, gates, kct, vc, ksat, vs, kwt, vw, ovt)


def _s5_kernel(u_ref, lag_ref, p_ref, q_ref, lr_ref, li_ref, d_ref, o_ref, m_ref, pw_ref, x_ref, y_ref, z_ref, sp_ref):
    nch = x_ref.shape[0]
    half = z_ref.shape[1] // 2
    seg = nch // SSM_CHAINS
    lanes = V7X_LANES
    re, im = slice(0, half), slice(half, 2 * half)
    lr, li = lr_ref[0], li_ref[0]
    cmul = lambda ar, ai, br, bi: (ar * br - ai * bi, ar * bi + ai * br)

    @pl.when(pl.program_id(1) == 0)
    def _():
        for s in range(SSM_CHUNK):
            for t in range(SSM_CHUNK):
                tile = lag_ref[0, t - s] if t >= s else jnp.zeros((lanes, lanes), BF16)
                m_ref[s * lanes:(s + 1) * lanes, t * lanes:(t + 1) * lanes] = tile

        def power_body(k, w):
            pw_ref[pl.ds(k, 1), re], pw_ref[pl.ds(k, 1), im] = w
            return cmul(lr, li, *w)

        one = (jnp.ones((1, half), F32), jnp.zeros((1, half), F32))
        pw_ref[pl.ds(seg, 1), re], pw_ref[pl.ds(seg, 1), im] = lax.fori_loop(0, seg, power_body, one, unroll=32)

    for t in range(SSM_CHUNK):
        x_ref[:, t * lanes:(t + 1) * lanes] = u_ref[pl.ds(t, nch, stride=SSM_CHUNK), :].astype(BF16)
    z_ref[...] = jnp.dot(x_ref[...], p_ref[0], preferred_element_type=F32)
    for c0 in range(0, SSM_CHUNK * lanes, V7X_MXU):
        c1 = c0 + V7X_MXU
        y_ref[:, c0:c1] = jnp.dot(x_ref[:, :c1], m_ref[:c1, c0:c1], preferred_element_type=F32)

    def scan_body(k, carry):
        out = []
        for j, (sr, si) in enumerate(carry):
            row = pl.ds(j * seg + k, 1)
            sp_ref[row, re], sp_ref[row, im] = sr, si
            tr, ti = cmul(lr, li, sr, si)
            out.append((tr + z_ref[row, re], ti + z_ref[row, im]))
        return tuple(out)

    zero = jnp.zeros((1, half), F32)
    ends = ((zero, zero),) * SSM_CHAINS
    for k in range(seg):
        ends = scan_body(k, ends)
    states = [sp_ref[0:seg, :]]
    entry = ends[0]
    for j in range(1, SSM_CHAINS):
        carried = cmul(pw_ref[0:seg, re], pw_ref[0:seg, im], *entry)
        states.append(sp_ref[j * seg:(j + 1) * seg, :] + jnp.concatenate(carried, axis=1))
        across = cmul(pw_ref[pl.ds(seg, 1), re], pw_ref[pl.ds(seg, 1), im], *entry)
        entry = (ends[j][0] + across[0], ends[j][1] + across[1])
    sp = jnp.concatenate(states, axis=0).astype(BF16)
    y = y_ref[...] + jnp.dot(sp, q_ref[0], preferred_element_type=F32)
    d = d_ref[...]
    for t in range(SSM_CHUNK):
        yt = y[:, t * lanes:(t + 1) * lanes] + d * u_ref[pl.ds(t, nch, stride=SSM_CHUNK), :]
        o_ref[pl.ds(t, nch, stride=SSM_CHUNK), :] = _gelu(yt)


def _s5_call(u, lag, p, q, lr, li, dskip, batch, seq):
    nsg = lag.shape[0]
    nch = seq // SSM_CHUNK
    assert nch % SSM_CHAINS == 0
    nstate = q.shape[1]
    lw = SSM_CHUNK * V7X_LANES
    slab = lambda a: pl.BlockSpec((1,) + a.shape[1:], lambda g, b: (g,) + (0,) * (a.ndim - 1),
                                  pipeline_mode=pl.Buffered(1))
    tokens = pl.BlockSpec((None, seq, V7X_LANES), lambda g, b: (g, b, 0))
    return pl.pallas_call(
        _s5_kernel,
        grid=(nsg, batch),
        in_specs=[tokens, slab(lag), slab(p), slab(q),
                  pl.BlockSpec((1, 1, nstate // 2), lambda g, b: (g, 0, 0)),
                  pl.BlockSpec((1, 1, nstate // 2), lambda g, b: (g, 0, 0)),
                  pl.BlockSpec((1, V7X_LANES), lambda g, b: (0, g))],
        out_specs=tokens,
        out_shape=jax.ShapeDtypeStruct(u.shape, F32),
        scratch_shapes=[pltpu.VMEM((lw, lw), BF16), pltpu.VMEM((nch // SSM_CHAINS + V7X_SUBLANES, nstate), F32),
                        pltpu.VMEM((nch, lw), BF16), pltpu.VMEM((nch, lw), F32),
                        pltpu.VMEM((nch, nstate), F32), pltpu.VMEM((nch, nstate), F32)],
        compiler_params=_cparams(("arbitrary", "arbitrary")),
        name="s5",
    )(u, lag, p, q, lr, li, dskip)


def _merge_kernel(x_ref, g_ref, attn_ref, ssm_ref, wga_ref, wgb_ref, wattn_ref, wval_ref, wgate_ref, wout_ref, o_ref):
    x = x_ref[...]
    hb = _rmsnorm(x, g_ref[...]).astype(BF16)
    dot = functools.partial(jnp.dot, preferred_element_type=F32)
    y_a = dot(attn_ref[...], wattn_ref[...])
    ys = jnp.concatenate([ssm_ref[s] for s in range(ssm_ref.shape[0])], axis=1).astype(BF16)
    y_b = dot(ys, wval_ref[...]) * jax.nn.sigmoid(dot(ys, wgate_ref[...]))
    merged = jax.nn.sigmoid(dot(hb, wga_ref[...])) * y_a + jax.nn.sigmoid(dot(hb, wgb_ref[...])) * y_b
    o_ref[...] = x + dot(merged.astype(BF16), wout_ref[...])


def _merge_call(x2, g, attn, ssm, wga, wgb, wattn, wval, wgate, wout):
    t, d = x2.shape
    tm = ROW_TILE
    row = lambda i: (i, 0)
    wspec = lambda a: pl.BlockSpec(a.shape, lambda i: (0, 0), pipeline_mode=pl.Buffered(1))
    return pl.pallas_call(
        _merge_kernel,
        grid=(t // tm,),
        in_specs=[pl.BlockSpec((tm, d), row), wspec(g), pl.BlockSpec((tm, attn.shape[1]), row),
                  pl.BlockSpec((ssm.shape[0], tm, V7X_LANES), lambda i: (0, i, 0)), wspec(wga), wspec(wgb), wspec(wattn), wspec(wval),
                  wspec(wgate), wspec(wout)],
        out_specs=pl.BlockSpec((tm, d), row),
        out_shape=jax.ShapeDtypeStruct((t, d), F32),
        compiler_params=_cparams(("arbitrary",)),
        name="merge",
    )(x2, g, attn, ssm, wga, wgb, wattn, wval, wgate, wout)


def _mlp_kernel(x_ref, g_ref, wup_ref, wdown_ref, gf_ref, o_ref):
    x = x_ref[...]
    hb = _rmsnorm(x, g_ref[...]).astype(BF16)
    acc = x
    for c in range(wup_ref.shape[1] // FF_CHUNK):
        sl = slice(c * FF_CHUNK, (c + 1) * FF_CHUNK)
        up = jnp.maximum(jnp.dot(hb, wup_ref[:, sl], preferred_element_type=F32), 0.0)
        acc = acc + jnp.dot((up * up).astype(BF16), wdown_ref[sl, :], preferred_element_type=F32)
    o_ref[...] = _rmsnorm(acc, gf_ref[...])


def _mlp_call(x1, g, wup, wdown, gf):
    t, d = x1.shape
    tm = ROW_TILE
    row = lambda i: (i, 0)
    wspec = lambda a: pl.BlockSpec(a.shape, lambda i: (0, 0), pipeline_mode=pl.Buffered(1))
    return pl.pallas_call(
        _mlp_kernel,
        grid=(t // tm,),
        in_specs=[pl.BlockSpec((tm, d), row), wspec(g), wspec(wup), wspec(wdown), wspec(gf)],
        out_specs=pl.BlockSpec((tm, d), row),
        out_shape=jax.ShapeDtypeStruct((t, d), F32),
        compiler_params=_cparams(("arbitrary",)),
        name="mlp",
    )(x1, g, wup, wdown, gf)


def _interleave_heads(wq):
    d = wq.shape[0]
    return wq.reshape(d, NSA_KV_HEADS, HEADS_PER_GROUP, HEAD_DIM).transpose(0, 2, 1, 3).reshape(d, Q_LANES)


def _rope_tables(seq):
    half = ROPE_DIM // 2
    inv = ROPE_THETA ** (-(jnp.arange(half, dtype=F32) * 2.0) / ROPE_DIM)
    ang = jnp.arange(seq, dtype=F32)[:, None] * inv[None, :]
    cos, sin = jnp.cos(ang), jnp.sin(ang)
    rest = HEAD_DIM - ROPE_DIM
    cos_h = jnp.concatenate([cos, cos, jnp.ones((seq, rest), F32)], axis=1)
    slo_h = jnp.concatenate([-sin, jnp.zeros((seq, half + rest), F32)], axis=1)
    shi_h = jnp.concatenate([jnp.zeros((seq, half), F32), sin, jnp.zeros((seq, rest), F32)], axis=1)
    reps = V7X_LANES // HEAD_DIM
    return jnp.tile(cos_h, (1, reps)), jnp.tile(slo_h, (1, reps)), jnp.tile(shi_h, (1, reps))


def _compress_weights(pe, w1, w2):
    eye = jnp.eye(NSA_KV_HEADS, dtype=F32)
    w1e = jnp.einsum('tdj,gk->tgdkj', w1.reshape(CMP_BLOCK, HEAD_DIM, CMP_HIDDEN), eye)
    w1e = w1e.reshape(CMP_BLOCK * KV_LANES, NSA_KV_HEADS * CMP_HIDDEN).astype(BF16)
    w2e = jnp.einsum('jd,gk->gjkd', w2, eye).reshape(NSA_KV_HEADS * CMP_HIDDEN, KV_LANES).astype(BF16)
    pee = jnp.tile(pe, (1, NSA_KV_HEADS)).reshape(1, CMP_BLOCK * KV_LANES)
    halfw = CMP_STRIDE * KV_LANES
    return pee[:, :halfw], pee[:, halfw:], w1e[:halfw], w1e[halfw:], w2e


def _selection_constants(seq):
    nc = seq // CMP_STRIDE - 1
    nb = seq // SEL_BLOCK
    n_np = np.arange(nc)[:, None] * CMP_STRIDE
    j_np = np.arange(nb)[None, :] * SEL_BLOCK
    overlap = ((n_np < j_np + SEL_BLOCK) & (n_np + CMP_BLOCK > j_np)).astype(np.float32)
    ovt = np.zeros((V7X_LANES, nc + 1), np.float32)
    ovt[:nb, :nc] = overlap.T
    return jnp.asarray(ovt, BF16)


def _s5_matrices(lam_re, lam_im, log_step, b_re, b_im, c_re, c_im):
    hp = lax.Precision.HIGHEST
    ng, ns = lam_re.shape
    gc = b_re.shape[-1]
    L = SSM_CHUNK
    step = jnp.exp(log_step)[:, None]
    a, b = lam_re * step, lam_im * step
    k = jnp.arange(L + 1, dtype=F32)[:, None, None]
    mag = jnp.exp(a[None] * k)
    pr, pi = mag * jnp.cos(b[None] * k), mag * jnp.sin(b[None] * k)
    nr, ni = pr[1] - 1.0, pi[1]
    den = lam_re * lam_re + lam_im * lam_im
    cr, ci = (nr * lam_re + ni * lam_im) / den, (ni * lam_re - nr * lam_im) / den
    bbr = cr[..., None] * b_re - ci[..., None] * b_im
    bbi = cr[..., None] * b_im + ci[..., None] * b_re
    cpr = c_re[None] * pr[:, :, None, :] - c_im[None] * pi[:, :, None, :]
    cpi = c_re[None] * pi[:, :, None, :] + c_im[None] * pr[:, :, None, :]
    kk = jnp.einsum('kgcn,gnd->kgcd', jnp.concatenate([cpr[:L], -cpi[:L]], axis=-1),
                    jnp.concatenate([bbr, bbi], axis=1), precision=hp)
    pw_r, pw_i = pr[L - 1 - np.arange(L)], pi[L - 1 - np.arange(L)]
    p_r = pw_r[:, :, :, None] * bbr[None] - pw_i[:, :, :, None] * bbi[None]
    p_i = pw_r[:, :, :, None] * bbi[None] + pw_i[:, :, :, None] * bbr[None]
    p_r, p_i = p_r.transpose(1, 0, 3, 2), p_i.transpose(1, 0, 3, 2)
    q_r = cpr[1:].transpose(1, 3, 0, 2)
    q_i = -cpi[1:].transpose(1, 3, 0, 2)
    sup = SSM_SUPER
    nsg = ng // sup
    lw = L * sup * gc
    bf = lambda a: a.astype(BF16)
    lag_c = bf(kk).reshape(L, nsg, sup, gc, gc).transpose(1, 0, 2, 4, 3).reshape(nsg, L, sup * gc, gc)
    own_lag = np.arange(sup * gc)[:, None] // gc == np.arange(sup * gc)[None, :] // gc
    lag_sg = jnp.where(own_lag, jnp.matmul(lag_c, jnp.asarray(np.tile(np.eye(gc), (1, sup)), BF16)), 0)
    slab_rows = lambda a: bf(a).reshape(nsg, sup, L, gc, -1).transpose(0, 2, 1, 3, 4).reshape(nsg, lw, -1)
    p_c = jnp.concatenate([slab_rows(p_r), slab_rows(p_i)], axis=2)
    rep_p = np.kron(np.eye(2), np.tile(np.eye(ns), (1, sup)))
    own_p = (np.arange(lw)[:, None] // gc) % sup == (np.arange(2 * sup * ns)[None, :] // ns) % sup
    p = jnp.where(own_p, jnp.matmul(p_c, jnp.asarray(rep_p, BF16)), 0)
    q_c = jnp.concatenate([bf(q_r).reshape(nsg, sup * ns, L * gc), bf(q_i).reshape(nsg, sup * ns, L * gc)], axis=1)
    rep_q = np.einsum('ts,cd,b->tcsbd', np.eye(L), np.eye(gc), np.ones(sup)).reshape(L * gc, lw)
    own_q = (np.arange(2 * sup * ns)[:, None] // ns) % sup == (np.arange(lw)[None, :] // gc) % sup
    q = jnp.where(own_q, jnp.matmul(q_c, jnp.asarray(rep_q, BF16)), 0)
    return lag_sg, p, q, pr[L].reshape(nsg, 1, sup * ns), pi[L].reshape(nsg, 1, sup * ns)


def kernel(x, norm_mix_g, w_in, cmp_pe, cmp_k_w1, cmp_k_w2, cmp_v_w1, cmp_v_w2, ssm_lam_re, ssm_lam_im, ssm_log_step, ssm_b_re, ssm_b_im, ssm_c_re, ssm_c_im, ssm_d, w_attn_branch, w_ssm_val, w_ssm_gate, w_out, norm_mlp_g, w_up, w_down, norm_final_g):
    batch, seq, d = x.shape
    depth = w_in.shape[0]
    assert depth == 1, "the final rmsnorm is fused into the single layer's mlp kernel"
    nsa_w = NSA_HEADS * HEAD_DIM
    ssm_w = ssm_d.shape[1]
    o_q, o_kv, o_g = nsa_w, nsa_w + 6 * KV_LANES, nsa_w + 6 * KV_LANES + 3 * NSA_HEADS
    o_u = o_g + ssm_w
    cos, slo, shi = _rope_tables(seq)
    ovt = _selection_constants(seq)
    head_order = np.array([g * HEADS_PER_GROUP + h for h in range(HEADS_PER_GROUP) for g in range(NSA_KV_HEADS)])
    x2 = x.reshape(batch * seq, d)
    for l in range(depth):
        wl = w_in[l]
        wq = _interleave_heads(wl[:, :o_q]).astype(BF16)
        wkv = wl[:, o_q:o_kv].astype(BF16)
        wg = jnp.pad(wl[:, o_kv:o_g], ((0, 0), (0, V7X_LANES - 3 * NSA_HEADS))).astype(BF16)
        wu = wl[:, o_g:o_u].astype(BF16)
        wga = wl[:, o_u:o_u + d].astype(BF16)
        wgb = wl[:, o_u + d:].astype(BF16)
        g_mix = norm_mix_g[l].reshape(1, d)
        qraw, qrot, kc_raw, vc_raw, ksat, vs, kwt, vw, gates, u = _proj_call(
            x2, g_mix, wq, wkv, wg, wu, cos, slo, shi, batch, seq)

        pea, peb, kw1a, kw1b, kw2 = _compress_weights(cmp_pe[l], cmp_k_w1[l], cmp_k_w2[l])
        _, _, vw1a, vw1b, vw2 = _compress_weights(cmp_pe[l], cmp_v_w1[l], cmp_v_w2[l])
        kct, vc = _compress_call(kc_raw, vc_raw, pea, peb, kw1a, kw1b, kw2.T, vw1a, vw1b, vw2, batch, seq)

        b3 = lambda a: a.reshape(batch, seq, a.shape[-1])
        attn = _nsa_call(b3(qraw), b3(qrot), b3(gates), kct, vc, ksat, b3(vs), kwt, b3(vw), ovt)
        attn = attn.reshape(batch * seq, nsa_w)

        lag, pm, qm, lr, li = _s5_matrices(ssm_lam_re[l], ssm_lam_im[l], ssm_log_step[l], ssm_b_re[l], ssm_b_im[l],
                                           ssm_c_re[l], ssm_c_im[l])
        y_ssm = _s5_call(u, lag, pm, qm, lr, li, ssm_d[l].reshape(1, ssm_w), batch, seq)

        wattn = w_attn_branch[l].reshape(NSA_HEADS, HEAD_DIM, d)[head_order].reshape(nsa_w, d).astype(BF16)
        x1 = _merge_call(x2, g_mix, attn, y_ssm, wga, wgb, wattn, w_ssm_val[l].astype(BF16),
                         w_ssm_gate[l].astype(BF16), w_out[l].astype(BF16))
        x2 = _mlp_call(x1, norm_mlp_g[l].reshape(1, d), w_up[l].astype(BF16), w_down[l].astype(BF16),
                       norm_final_g.reshape(1, d))
    return x2.reshape(batch, seq, d)
```

```python
import functools
import math

import jax
import jax.numpy as jnp
import numpy as np
from jax import lax
from jax.experimental import pallas as pl
from jax.experimental.pallas import tpu as pltpu

NSA_HEADS = 8
NSA_KV_HEADS = 2
HEAD_DIM = 64
CMP_BLOCK = 32
CMP_STRIDE = 16
CMP_HIDDEN = 256
SEL_BLOCK = 64
N_SEL = 16
WINDOW = 512
FORCE_BONUS = 1e3
NEG_INF = -1e30
ROPE_THETA = 500000.0
ROPE_DIM = HEAD_DIM // 4
SSM_GROUP = 16
EPS = 1e-6

HEADS_PER_GROUP = NSA_HEADS // NSA_KV_HEADS
KV_LANES = NSA_KV_HEADS * HEAD_DIM
Q_LANES = NSA_HEADS * HEAD_DIM

V7X_LANES = 128
V7X_SUBLANES = 8
V7X_MXU = 256
V7X_VMEM_BYTES = 64 * 1024 * 1024
VMEM_LIMIT = V7X_VMEM_BYTES - 8 * 1024 * 1024

ROW_TILE = 1024
SEL_TILE = 1024
NSA_QUERIES = 256
SSM_CHUNK = 8
SSM_SUPER = V7X_LANES // SSM_GROUP
SSM_CHAINS = 4
FF_CHUNK = 1024

LOG2E = math.log2(math.e)
MASK_BIAS = 1e30

BF16 = jnp.bfloat16
F32 = jnp.float32
_NT = (((1,), (1,)), ((), ()))


def _cparams(semantics):
    return pltpu.CompilerParams(dimension_semantics=semantics, vmem_limit_bytes=VMEM_LIMIT)


def _rmsnorm(x, g):
    return x * lax.rsqrt(jnp.mean(x * x, axis=-1, keepdims=True) + EPS) * g


def _gelu(x):
    return jax.nn.gelu(x)


def _rope_cols(x, cos, sin_lo, sin_hi):
    cols = []
    for c in range(x.shape[1] // V7X_LANES):
        xc = x[:, c * V7X_LANES:(c + 1) * V7X_LANES]
        up = pltpu.roll(xc, V7X_LANES - ROPE_DIM // 2, axis=1)
        dn = pltpu.roll(xc, ROPE_DIM // 2, axis=1)
        cols.append(xc * cos + up * sin_lo + dn * sin_hi)
    return jnp.concatenate(cols, axis=1) if len(cols) > 1 else cols[0]


def _values_with_ones(v):
    lane_head = lax.broadcasted_iota(jnp.int32, v.shape, 1) // HEAD_DIM
    return jnp.concatenate([jnp.where(lane_head == g, v, 1.0) for g in range(NSA_KV_HEADS)], axis=1).astype(BF16)


def _proj_kernel(x_ref, g_ref, wq_ref, wkv_ref, wg_ref, wu_ref, cos_ref, slo_ref, shi_ref,
                 qraw_ref, qrot_ref, kc_ref, vc_ref, ksat_ref, vs_ref, kwt_ref, vw_ref, gate_ref, u_ref, *, seq_tiles):
    hb = _rmsnorm(x_ref[...], g_ref[...]).astype(BF16)
    cos, slo, shi = cos_ref[...], slo_ref[...], shi_ref[...]
    q = jnp.dot(hb, wq_ref[...], preferred_element_type=F32) * (HEAD_DIM ** -0.5 * LOG2E)
    qraw_ref[...] = q.astype(BF16)
    qrot_ref[...] = _rope_cols(q, cos, slo, shi).astype(BF16)
    kv = jnp.dot(hb, wkv_ref[...], preferred_element_type=F32)
    w = KV_LANES
    kc_ref[...] = kv[:, 0 * w:1 * w]
    vc_ref[...] = kv[:, 1 * w:2 * w]
    tm = x_ref.shape[0]
    pos = (pl.program_id(0) % seq_tiles) * tm + lax.broadcasted_iota(jnp.int32, (V7X_LANES, tm), 1)
    onehot = jnp.where(lax.broadcasted_iota(jnp.int32, (V7X_LANES, tm), 0) == pos // SEL_BLOCK, 1.0, 0.0)
    ksat_ref[0, :w, :] = _rope_cols(kv[:, 2 * w:3 * w], cos, slo, shi).T.astype(BF16)
    ksat_ref[0, w:, :] = onehot.astype(BF16)
    vs_ref[...] = _values_with_ones(kv[:, 3 * w:4 * w])
    kwt_ref[0] = _rope_cols(kv[:, 4 * w:5 * w], cos, slo, shi).T.astype(BF16)
    vw_ref[...] = _values_with_ones(kv[:, 5 * w:6 * w])
    gate_ref[...] = jax.nn.sigmoid(jnp.dot(hb, wg_ref[...], preferred_element_type=F32))
    u = jnp.dot(hb, wu_ref[...], preferred_element_type=F32)
    for s in range(u_ref.shape[0]):
        u_ref[s] = u[:, s * V7X_LANES:(s + 1) * V7X_LANES]


def _proj_call(x2, g, wq, wkv, wg, wu, cos, slo, shi, batch, seq):
    t, d = x2.shape
    tm = ROW_TILE
    s_tiles = seq // tm
    row = lambda i: (i, 0)
    pos = lambda i: (i % s_tiles, 0)
    trans = lambda i: (i // s_tiles, 0, i % s_tiles)
    wspec = lambda a: pl.BlockSpec(a.shape, lambda i: (0, 0), pipeline_mode=pl.Buffered(1))
    rows_out = lambda n, dt: (jax.ShapeDtypeStruct((t, n), dt), pl.BlockSpec((tm, n), row))
    trans_out = lambda n: (jax.ShapeDtypeStruct((batch, n, seq), BF16), pl.BlockSpec((1, n, tm), trans))
    slabs_out = lambda n: (jax.ShapeDtypeStruct((n, t, V7X_LANES), F32),
                           pl.BlockSpec((n, tm, V7X_LANES), lambda i: (0, i, 0)))
    outs = [rows_out(Q_LANES, BF16), rows_out(Q_LANES, BF16), rows_out(KV_LANES, F32), rows_out(KV_LANES, F32),
            trans_out(KV_LANES + V7X_LANES), rows_out(NSA_KV_HEADS * KV_LANES, BF16),
            trans_out(KV_LANES), rows_out(NSA_KV_HEADS * KV_LANES, BF16),
            rows_out(V7X_LANES, F32), slabs_out(wu.shape[1] // V7X_LANES)]
    assert seq // SEL_BLOCK <= V7X_LANES, "the selection-block one-hot must fit one lane tile"
    return pl.pallas_call(
        functools.partial(_proj_kernel, seq_tiles=s_tiles),
        grid=(t // tm,),
        in_specs=[pl.BlockSpec((tm, d), row), wspec(g), wspec(wq), wspec(wkv), wspec(wg), wspec(wu),
                  pl.BlockSpec((tm, V7X_LANES), pos), pl.BlockSpec((tm, V7X_LANES), pos),
                  pl.BlockSpec((tm, V7X_LANES), pos)],
        out_specs=[o[1] for o in outs],
        out_shape=[o[0] for o in outs],
        compiler_params=_cparams(("arbitrary",)),
        name="proj",
    )(x2, g, wq, wkv, wg, wu, cos, slo, shi)


def _compress_kernel(k_ref, v_ref, pea_ref, peb_ref, kw1a_ref, kw1b_ref, kw2t_ref, vw1a_ref, vw1b_ref, vw2_ref,
                     kct_ref, vc_ref, ca_ref, cb_ref):
    nch = ca_ref.shape[0]

    def hidden(src_ref, w1a_ref, w1b_ref):
        for t in range(CMP_STRIDE):
            rows = src_ref[pl.ds(t, nch, stride=CMP_STRIDE), :]
            sl = slice(t * KV_LANES, (t + 1) * KV_LANES)
            ca_ref[:, sl] = (rows + pea_ref[:, sl]).astype(BF16)
            cb_ref[:, sl] = (rows + peb_ref[:, sl]).astype(BF16)
        ha = jnp.dot(ca_ref[...], w1a_ref[...], preferred_element_type=F32)
        hb = jnp.dot(cb_ref[...], w1b_ref[...], preferred_element_type=F32)
        return _gelu(ha + pltpu.roll(hb, nch - 1, axis=0)).astype(BF16)

    kct_ref[0] = lax.dot_general(kw2t_ref[...], hidden(k_ref, kw1a_ref, kw1b_ref), _NT,
                                 preferred_element_type=F32).astype(BF16)
    vc_ref[0] = jnp.dot(hidden(v_ref, vw1a_ref, vw1b_ref), vw2_ref[...], preferred_element_type=F32).astype(BF16)


def _compress_call(kc_raw, vc_raw, pea, peb, kw1a, kw1b, kw2, vw1a, vw1b, vw2, batch, seq):
    nch = seq // CMP_STRIDE
    wspec = lambda a: pl.BlockSpec(a.shape, lambda b: (0, 0))
    return pl.pallas_call(
        _compress_kernel,
        grid=(batch,),
        in_specs=[pl.BlockSpec((seq, KV_LANES), lambda b: (b, 0)), pl.BlockSpec((seq, KV_LANES), lambda b: (b, 0)),
                  wspec(pea), wspec(peb), wspec(kw1a), wspec(kw1b), wspec(kw2), wspec(vw1a), wspec(vw1b), wspec(vw2)],
        out_specs=[pl.BlockSpec((1, KV_LANES, nch), lambda b: (b, 0, 0)),
                   pl.BlockSpec((1, nch, KV_LANES), lambda b: (b, 0, 0))],
        out_shape=[jax.ShapeDtypeStruct((batch, KV_LANES, nch), BF16), jax.ShapeDtypeStruct((batch, nch, KV_LANES), BF16)],
        scratch_shapes=[pltpu.VMEM((nch, CMP_STRIDE * KV_LANES), BF16), pltpu.VMEM((nch, CMP_STRIDE * KV_LANES), BF16)],
        compiler_params=_cparams(("arbitrary",)),
        name="compress",
    )(kc_raw, vc_raw, pea, peb, kw1a, kw1b, kw2, vw1a, vw1b, vw2)


def _window_keys(nq):
    return WINDOW + nq


def _nsa_kernel(qraw_ref, qrot_ref, gate_ref, kct_ref, vc_ref, ksat_ref, vs_ref, kwt_ref, vw_ref, ovt_ref,
                o_ref, score_ref, qa_ref, *, n_sel):
    i = pl.program_id(1)
    nq = qraw_ref.shape[1]
    hg = HEADS_PER_GROUP
    rows = hg * nq
    lanes_gq = NSA_KV_HEADS * nq
    nb = score_ref.shape[0]
    nc = kct_ref.shape[2]
    t0 = i * nq
    blk_first = t0 // SEL_BLOCK
    blk_last = blk_first + nq // SEL_BLOCK - 1
    tq = t0 + lax.broadcasted_iota(jnp.int32, (nq, 1), 0)
    tq_rows = t0 + lax.broadcasted_iota(jnp.int32, (rows, 1), 0) % nq

    def add_per_query(s, bias):
        return (s.reshape(hg, nq, s.shape[-1]) + bias[None]).reshape(s.shape)

    own_lanes = [lax.broadcasted_iota(jnp.int32, (nq, KV_LANES), 1) // HEAD_DIM == g for g in range(NSA_KV_HEADS)]

    def group_q(ref, g):
        return jnp.concatenate([jnp.where(own_lanes[g], ref[0, :, h * KV_LANES:(h + 1) * KV_LANES], 0.0).astype(BF16)
                                for h in range(hg)], axis=0)

    def normalized(pv):
        return pv * (1.0 / pltpu.roll(pv, HEAD_DIM, axis=1))

    nwin = _window_keys(nq)
    w0 = pl.multiple_of(jnp.maximum(blk_first - WINDOW // SEL_BLOCK, 0) * SEL_BLOCK, V7X_LANES)
    kp = w0 + lax.broadcasted_iota(jnp.int32, (1, nwin), 1)
    bias_w = jnp.where(kp <= tq, jnp.where(kp > tq - WINDOW, 0.0, NEG_INF), NEG_INF)
    o_win = []
    for g in range(NSA_KV_HEADS):
        s = jnp.dot(group_q(qrot_ref, g), kwt_ref[0, :, pl.ds(w0, nwin)], preferred_element_type=F32)
        s = add_per_query(s, bias_w)
        e = jnp.exp2(s - jnp.max(s, axis=-1, keepdims=True))
        v = vw_ref[0, pl.ds(w0, nwin), g * KV_LANES:(g + 1) * KV_LANES]
        o_win.append(normalized(jnp.dot(e.astype(BF16), v, preferred_element_type=F32)))

    def compressed_branch(width):
        cmp_end = lax.broadcasted_iota(jnp.int32, (1, width), 1) * CMP_STRIDE + (CMP_BLOCK - 1)
        bias_c = jnp.where(cmp_end <= tq, 0.0, NEG_INF)
        any_valid = tq_rows >= CMP_BLOCK - 1
        outs, p_sum = [], []
        for g in range(NSA_KV_HEADS):
            s = jnp.dot(group_q(qraw_ref, g), kct_ref[0, :, :width], preferred_element_type=F32)
            s = add_per_query(s, bias_c)
            e = jnp.exp2(s - jnp.max(s, axis=-1, keepdims=True))
            p = e * jnp.where(any_valid, 1.0 / jnp.sum(e, axis=-1, keepdims=True), 0.0)
            outs.append(jnp.dot(p.astype(BF16), vc_ref[0, :width, :], preferred_element_type=F32))
            ph = p[0:nq]
            for h in range(1, hg):
                ph = ph + p[h * nq:(h + 1) * nq]
            p_sum.append(ph)
        p2 = jnp.concatenate(p_sum, axis=0)
        p_hi = p2.astype(BF16)
        p_lo = (p2 - p_hi.astype(F32)).astype(BF16)
        ovt = ovt_ref[:, :width]
        imp = (lax.dot_general(ovt, p_hi, _NT, preferred_element_type=F32)
               + lax.dot_general(ovt, p_lo, _NT, preferred_element_type=F32))
        return tuple(outs) + (imp,)

    *o_cmp, imp_t = compressed_branch(nc)

    blk_q = (t0 + lax.broadcasted_iota(jnp.int32, (1, lanes_gq), 1) % nq) // SEL_BLOCK
    jfull = lax.broadcasted_iota(jnp.int32, (nb, lanes_gq), 0)
    valid = jfull <= blk_q
    forced = (jfull == 0) | (jfull == blk_q) | (jfull == blk_q - 1)
    score = jnp.where(valid, imp_t + jnp.where(forced, FORCE_BONUS, 0.0), NEG_INF)

    free = jnp.where(valid, jnp.where(forced, NEG_INF, imp_t), NEG_INF)
    n_forced = 1 + jnp.where(blk_q >= 1, 1, 0) + jnp.where(blk_q >= 2, 1, 0)
    rest, cut = free, {}
    for k in range(1, n_sel):
        thr = jnp.max(rest, axis=0, keepdims=True)
        rest = jnp.where(rest >= thr, NEG_INF, rest)
        cut[k] = thr
    thr = cut[n_sel - 1]
    for nf in (2, 3):
        thr = jnp.where(n_forced == nf, cut[n_sel - nf], thr)
    sel_t = jnp.where(valid, jnp.where(forced, 1.0, jnp.where(free >= thr, 1.0, 0.0)), 0.0)

    def ranked_members():
        def count_above_or_tied_earlier(jp, cnt):
            row = jnp.broadcast_to(score_ref[pl.ds(jp, 1), :], score.shape)
            tie = jnp.where(jp < jfull, 1.0, 0.0)
            return cnt + jnp.where(row > score, 1.0, jnp.where(row == score, tie, 0.0))

        score_ref[...] = score
        cnt = lax.fori_loop(0, blk_last + 1, count_above_or_tied_earlier, jnp.zeros(score.shape, F32))
        return jnp.where(valid, jnp.where(cnt < float(n_sel), 1.0, 0.0), 0.0)

    picked = jnp.sum(sel_t, axis=0, keepdims=True)
    wrong = jnp.sum(jnp.where(picked == jnp.minimum(blk_q + 1, n_sel).astype(F32), 0.0, 1.0))
    sel_t = lax.cond(wrong == 0.0, lambda: sel_t, ranked_members)
    bias = ((sel_t.T - 1.0) * MASK_BIAS).astype(BF16)

    n_tiles = blk_last // (SEL_TILE // SEL_BLOCK) + 1
    lane_t = lax.broadcasted_iota(jnp.int32, (1, SEL_TILE), 1)
    for g in range(NSA_KV_HEADS):
        qa_ref[g * rows:(g + 1) * rows, :KV_LANES] = group_q(qrot_ref, g)
        qa_ref[g * rows:(g + 1) * rows, KV_LANES:] = jnp.concatenate([bias[g * nq:(g + 1) * nq]] * hg, axis=0)

    def tile_body(c, carry, causal):
        k0 = pl.multiple_of(c * SEL_TILE, SEL_TILE)
        ka = ksat_ref[0, :, pl.ds(k0, SEL_TILE)]
        if causal:
            bias_d = jnp.where((k0 + lane_t) <= tq, 0.0, NEG_INF)
        new = []
        for g in range(NSA_KV_HEADS):
            m, acc = carry[g]
            s = jnp.dot(qa_ref[g * rows:(g + 1) * rows], ka, preferred_element_type=F32)
            if causal:
                s = add_per_query(s, bias_d)
            m_new = jnp.maximum(m, jnp.max(s, axis=-1, keepdims=True))
            p = jnp.exp2(s - m_new)
            v = vs_ref[0, pl.ds(k0, SEL_TILE), g * KV_LANES:(g + 1) * KV_LANES]
            acc = jnp.exp2(m - m_new) * acc + jnp.dot(p.astype(BF16), v, preferred_element_type=F32)
            new.append((m_new, acc))
        return tuple(new)

    init = tuple((jnp.full((rows, 1), NEG_INF, F32), jnp.zeros((rows, KV_LANES), F32)) for _ in range(NSA_KV_HEADS))
    carry = lax.fori_loop(0, n_tiles - 1, functools.partial(tile_body, causal=False), init)
    carry = tile_body(n_tiles - 1, carry, causal=True)
    o_sel = [normalized(acc) for _, acc in carry]

    gt = gate_ref[0]
    for h in range(hg):
        r = slice(h * nq, (h + 1) * nq)
        parts = []
        for g in range(NSA_KV_HEADS):
            c = (g * hg + h) * 3
            parts.append(gt[:, c:c + 1] * o_cmp[g][r] + gt[:, c + 1:c + 2] * o_sel[g][r] + gt[:, c + 2:c + 3] * o_win[g][r])
        out = parts[0]
        for g in range(1, NSA_KV_HEADS):
            out = jnp.where(own_lanes[g], parts[g], out)
        o_ref[0, :, h * KV_LANES:(h + 1) * KV_LANES] = out.astype(BF16)


def _nsa_call(qraw, qrot, gates, kct, vc, ksat, vs, kwt, vw, ovt):
    batch, seq, _ = qraw.shape
    nq = NSA_QUERIES
    assert nq % V7X_LANES == 0 and seq % nq == 0 and seq >= _window_keys(nq) and seq % SEL_TILE == 0
    step = lambda n: pl.BlockSpec((1, nq, n), lambda b, i: (b, i, 0))
    whole = lambda a: pl.BlockSpec((1,) + a.shape[1:], lambda b, i: (b, 0, 0))
    return pl.pallas_call(
        functools.partial(_nsa_kernel, n_sel=min(N_SEL, seq // SEL_BLOCK)),
        grid=(batch, seq // nq),
        in_specs=[step(Q_LANES), step(Q_LANES), step(V7X_LANES),
                  whole(kct), whole(vc), whole(ksat), whole(vs), whole(kwt), whole(vw),
                  pl.BlockSpec(ovt.shape, lambda b, i: (0, 0))],
        out_specs=step(HEADS_PER_GROUP * KV_LANES),
        out_shape=jax.ShapeDtypeStruct((batch, seq, HEADS_PER_GROUP * KV_LANES), BF16),
        scratch_shapes=[pltpu.VMEM((V7X_LANES, NSA_KV_HEADS * nq), F32),
                        pltpu.VMEM((NSA_KV_HEADS * HEADS_PER_GROUP * nq, KV_LANES + V7X_LANES), BF16)],
        compiler_params=_cparams(("arbitrary", "arbitrary")),
        name="nsa",
    )(qraw, qrot, gates, kct, vc, ksat, vs, kwt, vw, ovt)


def _s5_kernel(u_ref, lag_ref, p_ref, q_ref, lr_ref, li_ref, d_ref, o_ref, m_ref, pw_ref, x_ref, y_ref, z_ref, sp_ref):
    nch = x_ref.shape[0]
    half = z_ref.shape[1] // 2
    seg = nch // SSM_CHAINS
    lanes = V7X_LANES
    re, im = slice(0, half), slice(half, 2 * half)
    lr, li = lr_ref[0], li_ref[0]
    cmul = lambda ar, ai, br, bi: (ar * br - ai * bi, ar * bi + ai * br)

    @pl.when(pl.program_id(1) == 0)
    def _():
        for s in range(SSM_CHUNK):
            for t in range(SSM_CHUNK):
                tile = lag_ref[0, t - s] if t >= s else jnp.zeros((lanes, lanes), BF16)
                m_ref[s * lanes:(s + 1) * lanes, t * lanes:(t + 1) * lanes] = tile

        def power_body(k, w):
            pw_ref[pl.ds(k, 1), re], pw_ref[pl.ds(k, 1), im] = w
            return cmul(lr, li, *w)

        one = (jnp.ones((1, half), F32), jnp.zeros((1, half), F32))
        pw_ref[pl.ds(seg, 1), re], pw_ref[pl.ds(seg, 1), im] = lax.fori_loop(0, seg, power_body, one, unroll=32)

    for t in range(SSM_CHUNK):
        x_ref[:, t * lanes:(t + 1) * lanes] = u_ref[pl.ds(t, nch, stride=SSM_CHUNK), :].astype(BF16)
    z_ref[...] = jnp.dot(x_ref[...], p_ref[0], preferred_element_type=F32)
    for c0 in range(0, SSM_CHUNK * lanes, V7X_MXU):
        c1 = c0 + V7X_MXU
        y_ref[:, c0:c1] = jnp.dot(x_ref[:, :c1], m_ref[:c1, c0:c1], preferred_element_type=F32)

    def scan_body(k, carry):
        out = []
        for j, (sr, si) in enumerate(carry):
            row = pl.ds(j * seg + k, 1)
            sp_ref[row, re], sp_ref[row, im] = sr, si
            tr, ti = cmul(lr, li, sr, si)
            out.append((tr + z_ref[row, re], ti + z_ref[row, im]))
        return tuple(out)

    zero = jnp.zeros((1, half), F32)
    ends = ((zero, zero),) * SSM_CHAINS
    for k in range(seg):
        ends = scan_body(k, ends)
    states = [sp_ref[0:seg, :]]
    entry = ends[0]
    for j in range(1, SSM_CHAINS):
        carried = cmul(pw_ref[0:seg, re], pw_ref[0:seg, im], *entry)
        states.append(sp_ref[j * seg:(j + 1) * seg, :] + jnp.concatenate(carried, axis=1))
        across = cmul(pw_ref[pl.ds(seg, 1), re], pw_ref[pl.ds(seg, 1), im], *entry)
        entry = (ends[j][0] + across[0], ends[j][1] + across[1])
    sp = jnp.concatenate(states, axis=0).astype(BF16)
    y = y_ref[...] + jnp.dot(sp, q_ref[0], preferred_element_type=F32)
    d = d_ref[...]
    for t in range(SSM_CHUNK):
        yt = y[:, t * lanes:(t + 1) * lanes] + d * u_ref[pl.ds(t, nch, stride=SSM_CHUNK), :]
        o_ref[pl.ds(t, nch, stride=SSM_CHUNK), :] = _gelu(yt)


def _s5_call(u, lag, p, q, lr, li, dskip, batch, seq):
    nsg = lag.shape[0]
    nch = seq // SSM_CHUNK
    assert nch % SSM_CHAINS == 0
    nstate = q.shape[1]
    lw = SSM_CHUNK * V7X_LANES
    slab = lambda a: pl.BlockSpec((1,) + a.shape[1:], lambda g, b: (g,) + (0,) * (a.ndim - 1),
                                  pipeline_mode=pl.Buffered(1))
    tokens = pl.BlockSpec((None, seq, V7X_LANES), lambda g, b: (g, b, 0))
    return pl.pallas_call(
        _s5_kernel,
        grid=(nsg, batch),
        in_specs=[tokens, slab(lag), slab(p), slab(q),
                  pl.BlockSpec((1, 1, nstate // 2), lambda g, b: (g, 0, 0)),
                  pl.BlockSpec((1, 1, nstate // 2), lambda g, b: (g, 0, 0)),
                  pl.BlockSpec((1, V7X_LANES), lambda g, b: (0, g))],
        out_specs=tokens,
        out_shape=jax.ShapeDtypeStruct(u.shape, F32),
        scratch_shapes=[pltpu.VMEM((lw, lw), BF16), pltpu.VMEM((nch // SSM_CHAINS + V7X_SUBLANES, nstate), F32),
                        pltpu.VMEM((nch, lw), BF16), pltpu.VMEM((nch, lw), F32),
                        pltpu.VMEM((nch, nstate), F32), pltpu.VMEM((nch, nstate), F32)],
        compiler_params=_cparams(("arbitrary", "arbitrary")),
        name="s5",
    )(u, lag, p, q, lr, li, dskip)


def _merge_kernel(x_ref, g_ref, attn_ref, ssm_ref, wga_ref, wgb_ref, wattn_ref, wval_ref, wgate_ref, wout_ref, o_ref):
    x = x_ref[...]
    hb = _rmsnorm(x, g_ref[...]).astype(BF16)
    dot = functools.partial(jnp.dot, preferred_element_type=F32)
    attn = attn_ref[...]
    ys = jnp.concatenate([ssm_ref[s] for s in range(ssm_ref.shape[0])], axis=1).astype(BF16)
    parts = []
    for c0 in range(0, x.shape[1], V7X_MXU):
        cols = slice(c0, c0 + V7X_MXU)
        y_a = dot(attn, wattn_ref[:, cols])
        y_b = dot(ys, wval_ref[:, cols]) * jax.nn.sigmoid(dot(ys, wgate_ref[:, cols]))
        merged = (jax.nn.sigmoid(dot(hb, wga_ref[:, cols])) * y_a
                  + jax.nn.sigmoid(dot(hb, wgb_ref[:, cols])) * y_b)
        parts.append(merged.astype(BF16))
    o_ref[...] = x + dot(jnp.concatenate(parts, axis=1), wout_ref[...])


def _merge_call(x2, g, attn, ssm, wga, wgb, wattn, wval, wgate, wout):
    t, d = x2.shape
    tm = ROW_TILE
    row = lambda i: (i, 0)
    wspec = lambda a: pl.BlockSpec(a.shape, lambda i: (0, 0), pipeline_mode=pl.Buffered(1))
    return pl.pallas_call(
        _merge_kernel,
        grid=(t // tm,),
        in_specs=[pl.BlockSpec((tm, d), row), wspec(g), pl.BlockSpec((tm, attn.shape[1]), row),
                  pl.BlockSpec((ssm.shape[0], tm, V7X_LANES), lambda i: (0, i, 0)), wspec(wga), wspec(wgb), wspec(wattn), wspec(wval),
                  wspec(wgate), wspec(wout)],
        out_specs=pl.BlockSpec((tm, d), row),
        out_shape=jax.ShapeDtypeStruct((t, d), F32),
        compiler_params=_cparams(("arbitrary",)),
        name="merge",
    )(x2, g, attn, ssm, wga, wgb, wattn, wval, wgate, wout)


def _mlp_kernel(x_ref, g_ref, wup_ref, wdown_ref, gf_ref, o_ref):
    x = x_ref[...]
    hb = _rmsnorm(x, g_ref[...]).astype(BF16)
    acc = x
    for c in range(wup_ref.shape[1] // FF_CHUNK):
        sl = slice(c * FF_CHUNK, (c + 1) * FF_CHUNK)
        up = jnp.maximum(jnp.dot(hb, wup_ref[:, sl], preferred_element_type=F32), 0.0)
        acc = acc + jnp.dot((up * up).astype(BF16), wdown_ref[sl, :], preferred_element_type=F32)
    o_ref[...] = _rmsnorm(acc, gf_ref[...])


def _mlp_call(x1, g, wup, wdown, gf):
    t, d = x1.shape
    tm = ROW_TILE
    row = lambda i: (i, 0)
    wspec = lambda a: pl.BlockSpec(a.shape, lambda i: (0, 0), pipeline_mode=pl.Buffered(1))
    return pl.pallas_call(
        _mlp_kernel,
        grid=(t // tm,),
        in_specs=[pl.BlockSpec((tm, d), row), wspec(g), wspec(wup), wspec(wdown), wspec(gf)],
        out_specs=pl.BlockSpec((tm, d), row),
        out_shape=jax.ShapeDtypeStruct((t, d), F32),
        compiler_params=_cparams(("arbitrary",)),
        name="mlp",
    )(x1, g, wup, wdown, gf)


def _interleave_heads(wq):
    d = wq.shape[0]
    return wq.reshape(d, NSA_KV_HEADS, HEADS_PER_GROUP, HEAD_DIM).transpose(0, 2, 1, 3).reshape(d, Q_LANES)


def _rope_tables(seq):
    half = ROPE_DIM // 2
    inv = ROPE_THETA ** (-(jnp.arange(half, dtype=F32) * 2.0) / ROPE_DIM)
    ang = jnp.arange(seq, dtype=F32)[:, None] * inv[None, :]
    cos, sin = jnp.cos(ang), jnp.sin(ang)
    rest = HEAD_DIM - ROPE_DIM
    cos_h = jnp.concatenate([cos, cos, jnp.ones((seq, rest), F32)], axis=1)
    slo_h = jnp.concatenate([-sin, jnp.zeros((seq, half + rest), F32)], axis=1)
    shi_h = jnp.concatenate([jnp.zeros((seq, half), F32), sin, jnp.zeros((seq, rest), F32)], axis=1)
    reps = V7X_LANES // HEAD_DIM
    return jnp.tile(cos_h, (1, reps)), jnp.tile(slo_h, (1, reps)), jnp.tile(shi_h, (1, reps))


def _compress_weights(pe, w1, w2):
    eye = jnp.eye(NSA_KV_HEADS, dtype=F32)
    w1e = jnp.einsum('tdj,gk->tgdkj', w1.reshape(CMP_BLOCK, HEAD_DIM, CMP_HIDDEN), eye)
    w1e = w1e.reshape(CMP_BLOCK * KV_LANES, NSA_KV_HEADS * CMP_HIDDEN).astype(BF16)
    w2e = jnp.einsum('jd,gk->gjkd', w2, eye).reshape(NSA_KV_HEADS * CMP_HIDDEN, KV_LANES).astype(BF16)
    pee = jnp.tile(pe, (1, NSA_KV_HEADS)).reshape(1, CMP_BLOCK * KV_LANES)
    halfw = CMP_STRIDE * KV_LANES
    return pee[:, :halfw], pee[:, halfw:], w1e[:halfw], w1e[halfw:], w2e


def _selection_constants(seq):
    nc = seq // CMP_STRIDE - 1
    nb = seq // SEL_BLOCK
    n_np = np.arange(nc)[:, None] * CMP_STRIDE
    j_np = np.arange(nb)[None, :] * SEL_BLOCK
    overlap = ((n_np < j_np + SEL_BLOCK) & (n_np + CMP_BLOCK > j_np)).astype(np.float32)
    ovt = np.zeros((V7X_LANES, nc + 1), np.float32)
    ovt[:nb, :nc] = overlap.T
    return jnp.asarray(ovt, BF16)


def _s5_matrices(lam_re, lam_im, log_step, b_re, b_im, c_re, c_im):
    hp = lax.Precision.HIGHEST
    ng, ns = lam_re.shape
    gc = b_re.shape[-1]
    L = SSM_CHUNK
    step = jnp.exp(log_step)[:, None]
    a, b = lam_re * step, lam_im * step
    k = jnp.arange(L + 1, dtype=F32)[:, None, None]
    mag = jnp.exp(a[None] * k)
    pr, pi = mag * jnp.cos(b[None] * k), mag * jnp.sin(b[None] * k)
    nr, ni = pr[1] - 1.0, pi[1]
    den = lam_re * lam_re + lam_im * lam_im
    cr, ci = (nr * lam_re + ni * lam_im) / den, (ni * lam_re - nr * lam_im) / den
    bbr = cr[..., None] * b_re - ci[..., None] * b_im
    bbi = cr[..., None] * b_im + ci[..., None] * b_re
    cpr = c_re[None] * pr[:, :, None, :] - c_im[None] * pi[:, :, None, :]
    cpi = c_re[None] * pi[:, :, None, :] + c_im[None] * pr[:, :, None, :]
    kk = jnp.einsum('kgcn,gnd->kgcd', jnp.concatenate([cpr[:L], -cpi[:L]], axis=-1),
                    jnp.concatenate([bbr, bbi], axis=1), precision=hp)
    pw_r, pw_i = pr[L - 1 - np.arange(L)], pi[L - 1 - np.arange(L)]
    p_r = pw_r[:, :, :, None] * bbr[None] - pw_i[:, :, :, None] * bbi[None]
    p_i = pw_r[:, :, :, None] * bbi[None] + pw_i[:, :, :, None] * bbr[None]
    p_r, p_i = p_r.transpose(1, 0, 3, 2), p_i.transpose(1, 0, 3, 2)
    q_r = cpr[1:].transpose(1, 3, 0, 2)
    q_i = -cpi[1:].transpose(1, 3, 0, 2)
    sup = SSM_SUPER
    nsg = ng // sup
    lw = L * sup * gc
    bf = lambda a: a.astype(BF16)
    lag_c = bf(kk).reshape(L, nsg, sup, gc, gc).transpose(1, 0, 2, 4, 3).reshape(nsg, L, sup * gc, gc)
    own_lag = np.arange(sup * gc)[:, None] // gc == np.arange(sup * gc)[None, :] // gc
    lag_sg = jnp.where(own_lag, jnp.matmul(lag_c, jnp.asarray(np.tile(np.eye(gc), (1, sup)), BF16)), 0)
    slab_rows = lambda a: bf(a).reshape(nsg, sup, L, gc, -1).transpose(0, 2, 1, 3, 4).reshape(nsg, lw, -1)
    p_c = jnp.concatenate([slab_rows(p_r), slab_rows(p_i)], axis=2)
    rep_p = np.kron(np.eye(2), np.tile(np.eye(ns), (1, sup)))
    own_p = (np.arange(lw)[:, None] // gc) % sup == (np.arange(2 * sup * ns)[None, :] // ns) % sup
    p = jnp.where(own_p, jnp.matmul(p_c, jnp.asarray(rep_p, BF16)), 0)
    q_c = jnp.concatenate([bf(q_r).reshape(nsg, sup * ns, L * gc), bf(q_i).reshape(nsg, sup * ns, L * gc)], axis=1)
    rep_q = np.einsum('ts,cd,b->tcsbd', np.eye(L), np.eye(gc), np.ones(sup)).reshape(L * gc, lw)
    own_q = (np.arange(2 * sup * ns)[:, None] // ns) % sup == (np.arange(lw)[None, :] // gc) % sup
    q = jnp.where(own_q, jnp.matmul(q_c, jnp.asarray(rep_q, BF16)), 0)
    return lag_sg, p, q, pr[L].reshape(nsg, 1, sup * ns), pi[L].reshape(nsg, 1, sup * ns)


def kernel(x, norm_mix_g, w_in, cmp_pe, cmp_k_w1, cmp_k_w2, cmp_v_w1, cmp_v_w2, ssm_lam_re, ssm_lam_im, ssm_log_step, ssm_b_re, ssm_b_im, ssm_c_re, ssm_c_im, ssm_d, w_attn_branch, w_ssm_val, w_ssm_gate, w_out, norm_mlp_g, w_up, w_down, norm_final_g):
    batch, seq, d = x.shape
    depth = w_in.shape[0]
    assert depth == 1, "the final rmsnorm is fused into the single layer's mlp kernel"
    nsa_w = NSA_HEADS * HEAD_DIM
    ssm_w = ssm_d.shape[1]
    o_q, o_kv, o_g = nsa_w, nsa_w + 6 * KV_LANES, nsa_w + 6 * KV_LANES + 3 * NSA_HEADS
    o_u = o_g + ssm_w
    cos, slo, shi = _rope_tables(seq)
    ovt = _selection_constants(seq)
    head_order = np.array([g * HEADS_PER_GROUP + h for h in range(HEADS_PER_GROUP) for g in range(NSA_KV_HEADS)])
    x2 = x.reshape(batch * seq, d)
    for l in range(depth):
        wl = w_in[l]
        wq = _interleave_heads(wl[:, :o_q]).astype(BF16)
        wkv = wl[:, o_q:o_kv].astype(BF16)
        wg = jnp.pad(wl[:, o_kv:o_g], ((0, 0), (0, V7X_LANES - 3 * NSA_HEADS))).astype(BF16)
        wu = wl[:, o_g:o_u].astype(BF16)
        wga = wl[:, o_u:o_u + d].astype(BF16)
        wgb = wl[:, o_u + d:].astype(BF16)
        g_mix = norm_mix_g[l].reshape(1, d)
        qraw, qrot, kc_raw, vc_raw, ksat, vs, kwt, vw, gates, u = _proj_call(
            x2, g_mix, wq, wkv, wg, wu, cos, slo, shi, batch, seq)

        pea, peb, kw1a, kw1b, kw2 = _compress_weights(cmp_pe[l], cmp_k_w1[l], cmp_k_w2[l])
        _, _, vw1a, vw1b, vw2 = _compress_weights(cmp_pe[l], cmp_v_w1[l], cmp_v_w2[l])
        kct, vc = _compress_call(kc_raw, vc_raw, pea, peb, kw1a, kw1b, kw2.T, vw1a, vw1b, vw2, batch, seq)

        b3 = lambda a: a.reshape(batch, seq, a.shape[-1])
        attn = _nsa_call(b3(qraw), b3(qrot), b3(gates), kct, vc, ksat, b3(vs), kwt, b3(vw), ovt)
        attn = attn.reshape(batch * seq, nsa_w)

        lag, pm, qm, lr, li = _s5_matrices(ssm_lam_re[l], ssm_lam_im[l], ssm_log_step[l], ssm_b_re[l], ssm_b_im[l],
                                           ssm_c_re[l], ssm_c_im[l])
        y_ssm = _s5_call(u, lag, pm, qm, lr, li, ssm_d[l].reshape(1, ssm_w), batch, seq)

        wattn = w_attn_branch[l].reshape(NSA_HEADS, HEAD_DIM, d)[head_order].reshape(nsa_w, d).astype(BF16)
        x1 = _merge_call(x2, g_mix, attn, y_ssm, wga, wgb, wattn, w_ssm_val[l].astype(BF16),
                         w_ssm_gate[l].astype(BF16), w_out[l].astype(BF16))
        x2 = _mlp_call(x1, norm_mlp_g[l].reshape(1, d), w_up[l].astype(BF16), w_down[l].astype(BF16),
                       norm_final_g.reshape(1, d))
    return x2.reshape(batch, seq, d)
```

```python
import functools
import math

import jax
import jax.numpy as jnp
import numpy as np
from jax import lax
from jax.experimental import pallas as pl
from jax.experimental.pallas import tpu as pltpu

NSA_HEADS = 8
NSA_KV_HEADS = 2
HEAD_DIM = 64
CMP_BLOCK = 32
CMP_STRIDE = 16
CMP_HIDDEN = 256
SEL_BLOCK = 64
N_SEL = 16
WINDOW = 512
FORCE_BONUS = 1e3
NEG_INF = -1e30
ROPE_THETA = 500000.0
ROPE_DIM = HEAD_DIM // 4
SSM_GROUP = 16
EPS = 1e-6

HEADS_PER_GROUP = NSA_HEADS // NSA_KV_HEADS
KV_LANES = NSA_KV_HEADS * HEAD_DIM
Q_LANES = NSA_HEADS * HEAD_DIM

V7X_LANES = 128
V7X_SUBLANES = 8
V7X_MXU = 256
V7X_VMEM_BYTES = 64 * 1024 * 1024
VMEM_LIMIT = V7X_VMEM_BYTES - 8 * 1024 * 1024

ROW_TILE = 1024
SEL_TILE = 1024
NSA_QUERIES = 256
SSM_CHUNK = 8
SSM_SUPER = V7X_LANES // SSM_GROUP
SSM_CHAINS = 4
FF_CHUNK = 1024

LOG2E = math.log2(math.e)
MASK_BIAS = 1e30

BF16 = jnp.bfloat16
F32 = jnp.float32
_NT = (((1,), (1,)), ((), ()))


def _cparams(semantics, vmem_limit=VMEM_LIMIT):
    return pltpu.CompilerParams(dimension_semantics=semantics, vmem_limit_bytes=vmem_limit)


def _rmsnorm(x, g):
    return x * lax.rsqrt(jnp.mean(x * x, axis=-1, keepdims=True) + EPS) * g


def _gelu(x):
    return jax.nn.gelu(x)


def _rope_cols(x, cos, sin_lo, sin_hi):
    cols = []
    for c in range(x.shape[1] // V7X_LANES):
        xc = x[:, c * V7X_LANES:(c + 1) * V7X_LANES]
        up = pltpu.roll(xc, V7X_LANES - ROPE_DIM // 2, axis=1)
        dn = pltpu.roll(xc, ROPE_DIM // 2, axis=1)
        cols.append(xc * cos + up * sin_lo + dn * sin_hi)
    return jnp.concatenate(cols, axis=1) if len(cols) > 1 else cols[0]


def _values_with_ones(v):
    lane_head = lax.broadcasted_iota(jnp.int32, v.shape, 1) // HEAD_DIM
    return jnp.concatenate([jnp.where(lane_head == g, v, 1.0) for g in range(NSA_KV_HEADS)], axis=1).astype(BF16)


def _proj_kernel(x_ref, g_ref, wq_ref, wkv_ref, wg_ref, wu_ref, cos_ref, slo_ref, shi_ref,
                 qraw_ref, qrot_ref, kc_ref, vc_ref, ksat_ref, vs_ref, kwt_ref, vw_ref, gate_ref, u_ref, *, seq_tiles):
    hb = _rmsnorm(x_ref[...], g_ref[...]).astype(BF16)
    cos, slo, shi = cos_ref[...], slo_ref[...], shi_ref[...]
    q = jnp.dot(hb, wq_ref[...], preferred_element_type=F32) * (HEAD_DIM ** -0.5 * LOG2E)
    qraw_ref[...] = q.astype(BF16)
    qrot_ref[...] = _rope_cols(q, cos, slo, shi).astype(BF16)
    kv = jnp.dot(hb, wkv_ref[...], preferred_element_type=F32)
    w = KV_LANES
    kc_ref[...] = kv[:, 0 * w:1 * w]
    vc_ref[...] = kv[:, 1 * w:2 * w]
    tm = x_ref.shape[0]
    pos = (pl.program_id(0) % seq_tiles) * tm + lax.broadcasted_iota(jnp.int32, (V7X_LANES, tm), 1)
    onehot = jnp.where(lax.broadcasted_iota(jnp.int32, (V7X_LANES, tm), 0) == pos // SEL_BLOCK, 1.0, 0.0)
    ksat_ref[0, :w, :] = _rope_cols(kv[:, 2 * w:3 * w], cos, slo, shi).T.astype(BF16)
    ksat_ref[0, w:, :] = onehot.astype(BF16)
    vs_ref[...] = _values_with_ones(kv[:, 3 * w:4 * w])
    kwt_ref[0] = _rope_cols(kv[:, 4 * w:5 * w], cos, slo, shi).T.astype(BF16)
    vw_ref[...] = _values_with_ones(kv[:, 5 * w:6 * w])
    gate_ref[...] = jax.nn.sigmoid(jnp.dot(hb, wg_ref[...], preferred_element_type=F32))
    u = jnp.dot(hb, wu_ref[...], preferred_element_type=F32)
    for s in range(u_ref.shape[0]):
        u_ref[s] = u[:, s * V7X_LANES:(s + 1) * V7X_LANES]


def _proj_call(x2, g, wq, wkv, wg, wu, cos, slo, shi, batch, seq):
    t, d = x2.shape
    tm = ROW_TILE
    s_tiles = seq // tm
    row = lambda i: (i, 0)
    pos = lambda i: (i % s_tiles, 0)
    trans = lambda i: (i // s_tiles, 0, i % s_tiles)
    wspec = lambda a: pl.BlockSpec(a.shape, lambda i: (0, 0), pipeline_mode=pl.Buffered(1))
    rows_out = lambda n, dt: (jax.ShapeDtypeStruct((t, n), dt), pl.BlockSpec((tm, n), row))
    trans_out = lambda n: (jax.ShapeDtypeStruct((batch, n, seq), BF16), pl.BlockSpec((1, n, tm), trans))
    slabs_out = lambda n: (jax.ShapeDtypeStruct((n, t, V7X_LANES), F32),
                           pl.BlockSpec((n, tm, V7X_LANES), lambda i: (0, i, 0)))
    outs = [rows_out(Q_LANES, BF16), rows_out(Q_LANES, BF16), rows_out(KV_LANES, F32), rows_out(KV_LANES, F32),
            trans_out(KV_LANES + V7X_LANES), rows_out(NSA_KV_HEADS * KV_LANES, BF16),
            trans_out(KV_LANES), rows_out(NSA_KV_HEADS * KV_LANES, BF16),
            rows_out(V7X_LANES, F32), slabs_out(wu.shape[1] // V7X_LANES)]
    assert seq // SEL_BLOCK <= V7X_LANES, "the selection-block one-hot must fit one lane tile"
    return pl.pallas_call(
        functools.partial(_proj_kernel, seq_tiles=s_tiles),
        grid=(t // tm,),
        in_specs=[pl.BlockSpec((tm, d), row), wspec(g), wspec(wq), wspec(wkv), wspec(wg), wspec(wu),
                  pl.BlockSpec((tm, V7X_LANES), pos), pl.BlockSpec((tm, V7X_LANES), pos),
                  pl.BlockSpec((tm, V7X_LANES), pos)],
        out_specs=[o[1] for o in outs],
        out_shape=[o[0] for o in outs],
        compiler_params=_cparams(("arbitrary",)),
        name="proj",
    )(x2, g, wq, wkv, wg, wu, cos, slo, shi)


def _compress_kernel(k_ref, v_ref, pea_ref, peb_ref, kw1a_ref, kw1b_ref, kw2t_ref, vw1a_ref, vw1b_ref, vw2_ref,
                     kct_ref, vc_ref, ca_ref, cb_ref):
    nch = ca_ref.shape[0]

    def hidden(src_ref, w1a_ref, w1b_ref):
        for t in range(CMP_STRIDE):
            rows = src_ref[pl.ds(t, nch, stride=CMP_STRIDE), :]
            sl = slice(t * KV_LANES, (t + 1) * KV_LANES)
            ca_ref[:, sl] = (rows + pea_ref[:, sl]).astype(BF16)
            cb_ref[:, sl] = (rows + peb_ref[:, sl]).astype(BF16)
        ha = jnp.dot(ca_ref[...], w1a_ref[...], preferred_element_type=F32)
        hb = jnp.dot(cb_ref[...], w1b_ref[...], preferred_element_type=F32)
        return _gelu(ha + pltpu.roll(hb, nch - 1, axis=0)).astype(BF16)

    kct_ref[0] = lax.dot_general(kw2t_ref[...], hidden(k_ref, kw1a_ref, kw1b_ref), _NT,
                                 preferred_element_type=F32).astype(BF16)
    vc_ref[0] = jnp.dot(hidden(v_ref, vw1a_ref, vw1b_ref), vw2_ref[...], preferred_element_type=F32).astype(BF16)


def _compress_call(kc_raw, vc_raw, pea, peb, kw1a, kw1b, kw2, vw1a, vw1b, vw2, batch, seq):
    nch = seq // CMP_STRIDE
    wspec = lambda a: pl.BlockSpec(a.shape, lambda b: (0, 0))
    return pl.pallas_call(
        _compress_kernel,
        grid=(batch,),
        in_specs=[pl.BlockSpec((seq, KV_LANES), lambda b: (b, 0)), pl.BlockSpec((seq, KV_LANES), lambda b: (b, 0)),
                  wspec(pea), wspec(peb), wspec(kw1a), wspec(kw1b), wspec(kw2), wspec(vw1a), wspec(vw1b), wspec(vw2)],
        out_specs=[pl.BlockSpec((1, KV_LANES, nch), lambda b: (b, 0, 0)),
                   pl.BlockSpec((1, nch, KV_LANES), lambda b: (b, 0, 0))],
        out_shape=[jax.ShapeDtypeStruct((batch, KV_LANES, nch), BF16), jax.ShapeDtypeStruct((batch, nch, KV_LANES), BF16)],
        scratch_shapes=[pltpu.VMEM((nch, CMP_STRIDE * KV_LANES), BF16), pltpu.VMEM((nch, CMP_STRIDE * KV_LANES), BF16)],
        compiler_params=_cparams(("arbitrary",)),
        name="compress",
    )(kc_raw, vc_raw, pea, peb, kw1a, kw1b, kw2, vw1a, vw1b, vw2)


def _window_keys(nq):
    return WINDOW + nq


def _nsa_kernel(qraw_ref, qrot_ref, gate_ref, kct_ref, vc_ref, ksat_ref, vs_ref, kwt_ref, vw_ref, ovt_ref,
                o_ref, score_ref, qa_ref, *, n_sel):
    i = pl.program_id(1)
    nq = qraw_ref.shape[1]
    hg = HEADS_PER_GROUP
    rows = hg * nq
    lanes_gq = NSA_KV_HEADS * nq
    nb = score_ref.shape[0]
    nc = kct_ref.shape[2]
    t0 = i * nq
    blk_first = t0 // SEL_BLOCK
    blk_last = blk_first + nq // SEL_BLOCK - 1
    tq = t0 + lax.broadcasted_iota(jnp.int32, (nq, 1), 0)
    tq_rows = t0 + lax.broadcasted_iota(jnp.int32, (rows, 1), 0) % nq

    def add_per_query(s, bias):
        return (s.reshape(hg, nq, s.shape[-1]) + bias[None]).reshape(s.shape)

    own_lanes = [lax.broadcasted_iota(jnp.int32, (nq, KV_LANES), 1) // HEAD_DIM == g for g in range(NSA_KV_HEADS)]

    def group_q(ref, g):
        return jnp.concatenate([jnp.where(own_lanes[g], ref[0, :, h * KV_LANES:(h + 1) * KV_LANES], 0.0).astype(BF16)
                                for h in range(hg)], axis=0)

    def normalized(pv):
        return pv * (1.0 / pltpu.roll(pv, HEAD_DIM, axis=1))

    nwin = _window_keys(nq)
    w0 = pl.multiple_of(jnp.maximum(blk_first - WINDOW // SEL_BLOCK, 0) * SEL_BLOCK, V7X_LANES)
    kp = w0 + lax.broadcasted_iota(jnp.int32, (1, nwin), 1)
    bias_w = jnp.where(kp <= tq, jnp.where(kp > tq - WINDOW, 0.0, NEG_INF), NEG_INF)
    o_win = []
    for g in range(NSA_KV_HEADS):
        s = jnp.dot(group_q(qrot_ref, g), kwt_ref[0, :, pl.ds(w0, nwin)], preferred_element_type=F32)
        s = add_per_query(s, bias_w)
        e = jnp.exp2(s - jnp.max(s, axis=-1, keepdims=True))
        v = vw_ref[0, pl.ds(w0, nwin), g * KV_LANES:(g + 1) * KV_LANES]
        o_win.append(normalized(jnp.dot(e.astype(BF16), v, preferred_element_type=F32)))

    def compressed_branch(width):
        cmp_end = lax.broadcasted_iota(jnp.int32, (1, width), 1) * CMP_STRIDE + (CMP_BLOCK - 1)
        bias_c = jnp.where(cmp_end <= tq, 0.0, NEG_INF)
        any_valid = tq_rows >= CMP_BLOCK - 1
        outs, p_sum = [], []
        for g in range(NSA_KV_HEADS):
            s = jnp.dot(group_q(qraw_ref, g), kct_ref[0, :, :width], preferred_element_type=F32)
            s = add_per_query(s, bias_c)
            e = jnp.exp2(s - jnp.max(s, axis=-1, keepdims=True))
            p = e * jnp.where(any_valid, 1.0 / jnp.sum(e, axis=-1, keepdims=True), 0.0)
            outs.append(jnp.dot(p.astype(BF16), vc_ref[0, :width, :], preferred_element_type=F32))
            ph = p[0:nq]
            for h in range(1, hg):
                ph = ph + p[h * nq:(h + 1) * nq]
            p_sum.append(ph)
        p2 = jnp.concatenate(p_sum, axis=0)
        p_hi = p2.astype(BF16)
        p_lo = (p2 - p_hi.astype(F32)).astype(BF16)
        ovt = ovt_ref[:, :width]
        imp = (lax.dot_general(ovt, p_hi, _NT, preferred_element_type=F32)
               + lax.dot_general(ovt, p_lo, _NT, preferred_element_type=F32))
        return tuple(outs) + (imp,)

    *o_cmp, imp_t = compressed_branch(nc)

    blk_q = (t0 + lax.broadcasted_iota(jnp.int32, (1, lanes_gq), 1) % nq) // SEL_BLOCK
    jfull = lax.broadcasted_iota(jnp.int32, (nb, lanes_gq), 0)
    valid = jfull <= blk_q
    forced = (jfull == 0) | (jfull == blk_q) | (jfull == blk_q - 1)
    score = jnp.where(valid, imp_t + jnp.where(forced, FORCE_BONUS, 0.0), NEG_INF)

    free = jnp.where(valid, jnp.where(forced, NEG_INF, imp_t), NEG_INF)
    n_forced = 1 + jnp.where(blk_q >= 1, 1, 0) + jnp.where(blk_q >= 2, 1, 0)
    rest, cut = free, {}
    for k in range(1, n_sel):
        thr = jnp.max(rest, axis=0, keepdims=True)
        rest = jnp.where(rest >= thr, NEG_INF, rest)
        cut[k] = thr
    thr = cut[n_sel - 1]
    for nf in (2, 3):
        thr = jnp.where(n_forced == nf, cut[n_sel - nf], thr)
    sel_t = jnp.where(valid, jnp.where(forced, 1.0, jnp.where(free >= thr, 1.0, 0.0)), 0.0)

    def ranked_members():
        def count_above_or_tied_earlier(jp, cnt):
            row = jnp.broadcast_to(score_ref[pl.ds(jp, 1), :], score.shape)
            tie = jnp.where(jp < jfull, 1.0, 0.0)
            return cnt + jnp.where(row > score, 1.0, jnp.where(row == score, tie, 0.0))

        score_ref[...] = score
        cnt = lax.fori_loop(0, blk_last + 1, count_above_or_tied_earlier, jnp.zeros(score.shape, F32))
        return jnp.where(valid, jnp.where(cnt < float(n_sel), 1.0, 0.0), 0.0)

    picked = jnp.sum(sel_t, axis=0, keepdims=True)
    wrong = jnp.sum(jnp.where(picked == jnp.minimum(blk_q + 1, n_sel).astype(F32), 0.0, 1.0))
    sel_t = lax.cond(wrong == 0.0, lambda: sel_t, ranked_members)
    bias = ((sel_t.T - 1.0) * MASK_BIAS).astype(BF16)

    n_tiles = blk_last // (SEL_TILE // SEL_BLOCK) + 1
    lane_t = lax.broadcasted_iota(jnp.int32, (1, SEL_TILE), 1)
    for g in range(NSA_KV_HEADS):
        qa_ref[g * rows:(g + 1) * rows, :KV_LANES] = group_q(qrot_ref, g)
        qa_ref[g * rows:(g + 1) * rows, KV_LANES:] = jnp.concatenate([bias[g * nq:(g + 1) * nq]] * hg, axis=0)

    def tile_body(c, carry, causal):
        k0 = pl.multiple_of(c * SEL_TILE, SEL_TILE)
        ka = ksat_ref[0, :, pl.ds(k0, SEL_TILE)]
        if causal:
            bias_d = jnp.where((k0 + lane_t) <= tq, 0.0, NEG_INF)
        new = []
        for g in range(NSA_KV_HEADS):
            m, acc = carry[g]
            s = jnp.dot(qa_ref[g * rows:(g + 1) * rows], ka, preferred_element_type=F32)
            if causal:
                s = add_per_query(s, bias_d)
            m_new = jnp.maximum(m, jnp.max(s, axis=-1, keepdims=True))
            p = jnp.exp2(s - m_new)
            v = vs_ref[0, pl.ds(k0, SEL_TILE), g * KV_LANES:(g + 1) * KV_LANES]
            acc = jnp.exp2(m - m_new) * acc + jnp.dot(p.astype(BF16), v, preferred_element_type=F32)
            new.append((m_new, acc))
        return tuple(new)

    init = tuple((jnp.full((rows, 1), NEG_INF, F32), jnp.zeros((rows, KV_LANES), F32)) for _ in range(NSA_KV_HEADS))
    carry = lax.fori_loop(0, n_tiles - 1, functools.partial(tile_body, causal=False), init)
    carry = tile_body(n_tiles - 1, carry, causal=True)
    o_sel = [normalized(acc) for _, acc in carry]

    gt = gate_ref[0]
    for h in range(hg):
        r = slice(h * nq, (h + 1) * nq)
        parts = []
        for g in range(NSA_KV_HEADS):
            c = (g * hg + h) * 3
            parts.append(gt[:, c:c + 1] * o_cmp[g][r] + gt[:, c + 1:c + 2] * o_sel[g][r] + gt[:, c + 2:c + 3] * o_win[g][r])
        out = parts[0]
        for g in range(1, NSA_KV_HEADS):
            out = jnp.where(own_lanes[g], parts[g], out)
        o_ref[0, :, h * KV_LANES:(h + 1) * KV_LANES] = out.astype(BF16)


def _nsa_call(qraw, qrot, gates, kct, vc, ksat, vs, kwt, vw, ovt):
    batch, seq, _ = qraw.shape
    nq = NSA_QUERIES
    assert nq % V7X_LANES == 0 and seq % nq == 0 and seq >= _window_keys(nq) and seq % SEL_TILE == 0
    step = lambda n: pl.BlockSpec((1, nq, n), lambda b, i: (b, i, 0))
    whole = lambda a: pl.BlockSpec((1,) + a.shape[1:], lambda b, i: (b, 0, 0))
    return pl.pallas_call(
        functools.partial(_nsa_kernel, n_sel=min(N_SEL, seq // SEL_BLOCK)),
        grid=(batch, seq // nq),
        in_specs=[step(Q_LANES), step(Q_LANES), step(V7X_LANES),
                  whole(kct), whole(vc), whole(ksat), whole(vs), whole(kwt), whole(vw),
                  pl.BlockSpec(ovt.shape, lambda b, i: (0, 0))],
        out_specs=step(HEADS_PER_GROUP * KV_LANES),
        out_shape=jax.ShapeDtypeStruct((batch, seq, HEADS_PER_GROUP * KV_LANES), BF16),
        scratch_shapes=[pltpu.VMEM((V7X_LANES, NSA_KV_HEADS * nq), F32),
                        pltpu.VMEM((NSA_KV_HEADS * HEADS_PER_GROUP * nq, KV_LANES + V7X_LANES), BF16)],
        compiler_params=_cparams(("arbitrary", "arbitrary")),
        name="nsa",
    )(qraw, qrot, gates, kct, vc, ksat, vs, kwt, vw, ovt)


def _s5_kernel(u_ref, lag_ref, p_ref, q_ref, lr_ref, li_ref, d_ref, o_ref, m_ref, pw_ref, x_ref, y_ref, z_ref, sp_ref):
    nch = x_ref.shape[0]
    half = z_ref.shape[1] // 2
    seg = nch // SSM_CHAINS
    lanes = V7X_LANES
    re, im = slice(0, half), slice(half, 2 * half)
    lr, li = lr_ref[0], li_ref[0]
    cmul = lambda ar, ai, br, bi: (ar * br - ai * bi, ar * bi + ai * br)

    @pl.when(pl.program_id(1) == 0)
    def _():
        for s in range(SSM_CHUNK):
            for t in range(SSM_CHUNK):
                tile = lag_ref[0, t - s] if t >= s else jnp.zeros((lanes, lanes), BF16)
                m_ref[s * lanes:(s + 1) * lanes, t * lanes:(t + 1) * lanes] = tile

        def power_body(k, w):
            pw_ref[pl.ds(k, 1), re], pw_ref[pl.ds(k, 1), im] = w
            return cmul(lr, li, *w)

        one = (jnp.ones((1, half), F32), jnp.zeros((1, half), F32))
        pw_ref[pl.ds(seg, 1), re], pw_ref[pl.ds(seg, 1), im] = lax.fori_loop(0, seg, power_body, one, unroll=32)

    for t in range(SSM_CHUNK):
        x_ref[:, t * lanes:(t + 1) * lanes] = u_ref[pl.ds(t, nch, stride=SSM_CHUNK), :].astype(BF16)
    z_ref[...] = jnp.dot(x_ref[...], p_ref[0], preferred_element_type=F32)
    for c0 in range(0, SSM_CHUNK * lanes, V7X_MXU):
        c1 = c0 + V7X_MXU
        y_ref[:, c0:c1] = jnp.dot(x_ref[:, :c1], m_ref[:c1, c0:c1], preferred_element_type=F32)

    def scan_body(k, carry):
        out = []
        for j, (sr, si) in enumerate(carry):
            row = pl.ds(j * seg + k, 1)
            sp_ref[row, re], sp_ref[row, im] = sr, si
            tr, ti = cmul(lr, li, sr, si)
            out.append((tr + z_ref[row, re], ti + z_ref[row, im]))
        return tuple(out)

    zero = jnp.zeros((1, half), F32)
    ends = ((zero, zero),) * SSM_CHAINS
    for k in range(seg):
        ends = scan_body(k, ends)
    states = [sp_ref[0:seg, :]]
    entry = ends[0]
    for j in range(1, SSM_CHAINS):
        carried = cmul(pw_ref[0:seg, re], pw_ref[0:seg, im], *entry)
        states.append(sp_ref[j * seg:(j + 1) * seg, :] + jnp.concatenate(carried, axis=1))
        across = cmul(pw_ref[pl.ds(seg, 1), re], pw_ref[pl.ds(seg, 1), im], *entry)
        entry = (ends[j][0] + across[0], ends[j][1] + across[1])
    sp = jnp.concatenate(states, axis=0).astype(BF16)
    y = y_ref[...] + jnp.dot(sp, q_ref[0], preferred_element_type=F32)
    d = d_ref[...]
    for t in range(SSM_CHUNK):
        yt = y[:, t * lanes:(t + 1) * lanes] + d * u_ref[pl.ds(t, nch, stride=SSM_CHUNK), :]
        o_ref[pl.ds(t, nch, stride=SSM_CHUNK), :] = _gelu(yt)


def _s5_call(u, lag, p, q, lr, li, dskip, batch, seq):
    nsg = lag.shape[0]
    nch = seq // SSM_CHUNK
    assert nch % SSM_CHAINS == 0
    nstate = q.shape[1]
    lw = SSM_CHUNK * V7X_LANES
    slab = lambda a: pl.BlockSpec((1,) + a.shape[1:], lambda g, b: (g,) + (0,) * (a.ndim - 1),
                                  pipeline_mode=pl.Buffered(1))
    tokens = pl.BlockSpec((None, seq, V7X_LANES), lambda g, b: (g, b, 0))
    return pl.pallas_call(
        _s5_kernel,
        grid=(nsg, batch),
        in_specs=[tokens, slab(lag), slab(p), slab(q),
                  pl.BlockSpec((1, 1, nstate // 2), lambda g, b: (g, 0, 0)),
                  pl.BlockSpec((1, 1, nstate // 2), lambda g, b: (g, 0, 0)),
                  pl.BlockSpec((1, V7X_LANES), lambda g, b: (0, g))],
        out_specs=tokens,
        out_shape=jax.ShapeDtypeStruct(u.shape, F32),
        scratch_shapes=[pltpu.VMEM((lw, lw), BF16), pltpu.VMEM((nch // SSM_CHAINS + V7X_SUBLANES, nstate), F32),
                        pltpu.VMEM((nch, lw), BF16), pltpu.VMEM((nch, lw), F32),
                        pltpu.VMEM((nch, nstate), F32), pltpu.VMEM((nch, nstate), F32)],
        compiler_params=_cparams(("arbitrary", "arbitrary")),
        name="s5",
    )(u, lag, p, q, lr, li, dskip)


def _merge_mlp_kernel(x_ref, g_ref, attn_ref, ssm_ref, wga_ref, wgb_ref, wattn_ref, wval_ref, wgate_ref, wout_ref,
                      gm_ref, wup_ref, wdown_ref, gf_ref, o_ref):
    x = x_ref[...]
    hb = _rmsnorm(x, g_ref[...]).astype(BF16)
    dot = functools.partial(jnp.dot, preferred_element_type=F32)
    attn = attn_ref[...]
    ys = jnp.concatenate([ssm_ref[s] for s in range(ssm_ref.shape[0])], axis=1).astype(BF16)
    parts = []
    for c0 in range(0, x.shape[1], V7X_MXU):
        cols = slice(c0, c0 + V7X_MXU)
        y_a = dot(attn, wattn_ref[:, cols])
        y_b = dot(ys, wval_ref[:, cols]) * jax.nn.sigmoid(dot(ys, wgate_ref[:, cols]))
        merged = (jax.nn.sigmoid(dot(hb, wga_ref[:, cols])) * y_a
                  + jax.nn.sigmoid(dot(hb, wgb_ref[:, cols])) * y_b)
        parts.append(merged.astype(BF16))
    x1 = x + dot(jnp.concatenate(parts, axis=1), wout_ref[...])

    hb = _rmsnorm(x1, gm_ref[...]).astype(BF16)
    acc = x1
    for c in range(wup_ref.shape[1] // FF_CHUNK):
        sl = slice(c * FF_CHUNK, (c + 1) * FF_CHUNK)
        up = jnp.maximum(dot(hb, wup_ref[:, sl]), 0.0)
        acc = acc + dot((up * up).astype(BF16), wdown_ref[sl, :])
    o_ref[...] = _rmsnorm(acc, gf_ref[...])


def _merge_mlp_call(x2, g, attn, ssm, wga, wgb, wattn, wval, wgate, wout, gm, wup, wdown, gf):
    t, d = x2.shape
    tm = ROW_TILE
    row = lambda i: (i, 0)
    wspec = lambda a: pl.BlockSpec(a.shape, lambda i: (0, 0), pipeline_mode=pl.Buffered(1))
    return pl.pallas_call(
        _merge_mlp_kernel,
        grid=(t // tm,),
        in_specs=[pl.BlockSpec((tm, d), row), wspec(g), pl.BlockSpec((tm, attn.shape[1]), row),
                  pl.BlockSpec((ssm.shape[0], tm, V7X_LANES), lambda i: (0, i, 0)), wspec(wga), wspec(wgb),
                  wspec(wattn), wspec(wval), wspec(wgate), wspec(wout), wspec(gm), wspec(wup), wspec(wdown), wspec(gf)],
        out_specs=pl.BlockSpec((tm, d), row),
        out_shape=jax.ShapeDtypeStruct((t, d), F32),
        compiler_params=_cparams(("arbitrary",), V7X_VMEM_BYTES - 2 * 1024 * 1024),
        name="merge_mlp",
    )(x2, g, attn, ssm, wga, wgb, wattn, wval, wgate, wout, gm, wup, wdown, gf)


def _interleave_heads(wq):
    d = wq.shape[0]
    return wq.reshape(d, NSA_KV_HEADS, HEADS_PER_GROUP, HEAD_DIM).transpose(0, 2, 1, 3).reshape(d, Q_LANES)


def _rope_tables(seq):
    half = ROPE_DIM // 2
    inv = ROPE_THETA ** (-(jnp.arange(half, dtype=F32) * 2.0) / ROPE_DIM)
    ang = jnp.arange(seq, dtype=F32)[:, None] * inv[None, :]
    cos, sin = jnp.cos(ang), jnp.sin(ang)
    rest = HEAD_DIM - ROPE_DIM
    cos_h = jnp.concatenate([cos, cos, jnp.ones((seq, rest), F32)], axis=1)
    slo_h = jnp.concatenate([-sin, jnp.zeros((seq, half + rest), F32)], axis=1)
    shi_h = jnp.concatenate([jnp.zeros((seq, half), F32), sin, jnp.zeros((seq, rest), F32)], axis=1)
    reps = V7X_LANES // HEAD_DIM
    return jnp.tile(cos_h, (1, reps)), jnp.tile(slo_h, (1, reps)), jnp.tile(shi_h, (1, reps))


def _compress_weights(pe, w1, w2):
    eye = jnp.eye(NSA_KV_HEADS, dtype=F32)
    w1e = jnp.einsum('tdj,gk->tgdkj', w1.reshape(CMP_BLOCK, HEAD_DIM, CMP_HIDDEN), eye)
    w1e = w1e.reshape(CMP_BLOCK * KV_LANES, NSA_KV_HEADS * CMP_HIDDEN).astype(BF16)
    w2e = jnp.einsum('jd,gk->gjkd', w2, eye).reshape(NSA_KV_HEADS * CMP_HIDDEN, KV_LANES).astype(BF16)
    pee = jnp.tile(pe, (1, NSA_KV_HEADS)).reshape(1, CMP_BLOCK * KV_LANES)
    halfw = CMP_STRIDE * KV_LANES
    return pee[:, :halfw], pee[:, halfw:], w1e[:halfw], w1e[halfw:], w2e


def _selection_constants(seq):
    nc = seq // CMP_STRIDE - 1
    nb = seq // SEL_BLOCK
    n_np = np.arange(nc)[:, None] * CMP_STRIDE
    j_np = np.arange(nb)[None, :] * SEL_BLOCK
    overlap = ((n_np < j_np + SEL_BLOCK) & (n_np + CMP_BLOCK > j_np)).astype(np.float32)
    ovt = np.zeros((V7X_LANES, nc + 1), np.float32)
    ovt[:nb, :nc] = overlap.T
    return jnp.asarray(ovt, BF16)


def _s5_matrices(lam_re, lam_im, log_step, b_re, b_im, c_re, c_im):
    hp = lax.Precision.HIGHEST
    ng, ns = lam_re.shape
    gc = b_re.shape[-1]
    L = SSM_CHUNK
    step = jnp.exp(log_step)[:, None]
    a, b = lam_re * step, lam_im * step
    k = jnp.arange(L + 1, dtype=F32)[:, None, None]
    mag = jnp.exp(a[None] * k)
    pr, pi = mag * jnp.cos(b[None] * k), mag * jnp.sin(b[None] * k)
    nr, ni = pr[1] - 1.0, pi[1]
    den = lam_re * lam_re + lam_im * lam_im
    cr, ci = (nr * lam_re + ni * lam_im) / den, (ni * lam_re - nr * lam_im) / den
    bbr = cr[..., None] * b_re - ci[..., None] * b_im
    bbi = cr[..., None] * b_im + ci[..., None] * b_re
    cpr = c_re[None] * pr[:, :, None, :] - c_im[None] * pi[:, :, None, :]
    cpi = c_re[None] * pi[:, :, None, :] + c_im[None] * pr[:, :, None, :]
    kk = jnp.einsum('kgcn,gnd->kgcd', jnp.concatenate([cpr[:L], -cpi[:L]], axis=-1),
                    jnp.concatenate([bbr, bbi], axis=1), precision=hp)
    pw_r, pw_i = pr[L - 1 - np.arange(L)], pi[L - 1 - np.arange(L)]
    p_r = pw_r[:, :, :, None] * bbr[None] - pw_i[:, :, :, None] * bbi[None]
    p_i = pw_r[:, :, :, None] * bbi[None] + pw_i[:, :, :, None] * bbr[None]
    p_r, p_i = p_r.transpose(1, 0, 3, 2), p_i.transpose(1, 0, 3, 2)
    q_r = cpr[1:].transpose(1, 3, 0, 2)
    q_i = -cpi[1:].transpose(1, 3, 0, 2)
    sup = SSM_SUPER
    nsg = ng // sup
    lw = L * sup * gc
    bf = lambda a: a.astype(BF16)
    lag_c = bf(kk).reshape(L, nsg, sup, gc, gc).transpose(1, 0, 2, 4, 3).reshape(nsg, L, sup * gc, gc)
    own_lag = np.arange(sup * gc)[:, None] // gc == np.arange(sup * gc)[None, :] // gc
    lag_sg = jnp.where(own_lag, jnp.matmul(lag_c, jnp.asarray(np.tile(np.eye(gc), (1, sup)), BF16)), 0)
    slab_rows = lambda a: bf(a).reshape(nsg, sup, L, gc, -1).transpose(0, 2, 1, 3, 4).reshape(nsg, lw, -1)
    p_c = jnp.concatenate([slab_rows(p_r), slab_rows(p_i)], axis=2)
    rep_p = np.kron(np.eye(2), np.tile(np.eye(ns), (1, sup)))
    own_p = (np.arange(lw)[:, None] // gc) % sup == (np.arange(2 * sup * ns)[None, :] // ns) % sup
    p = jnp.where(own_p, jnp.matmul(p_c, jnp.asarray(rep_p, BF16)), 0)
    q_c = jnp.concatenate([bf(q_r).reshape(nsg, sup * ns, L * gc), bf(q_i).reshape(nsg, sup * ns, L * gc)], axis=1)
    rep_q = np.einsum('ts,cd,b->tcsbd', np.eye(L), np.eye(gc), np.ones(sup)).reshape(L * gc, lw)
    own_q = (np.arange(2 * sup * ns)[:, None] // ns) % sup == (np.arange(lw)[None, :] // gc) % sup
    q = jnp.where(own_q, jnp.matmul(q_c, jnp.asarray(rep_q, BF16)), 0)
    return lag_sg, p, q, pr[L].reshape(nsg, 1, sup * ns), pi[L].reshape(nsg, 1, sup * ns)


def kernel(x, norm_mix_g, w_in, cmp_pe, cmp_k_w1, cmp_k_w2, cmp_v_w1, cmp_v_w2, ssm_lam_re, ssm_lam_im, ssm_log_step, ssm_b_re, ssm_b_im, ssm_c_re, ssm_c_im, ssm_d, w_attn_branch, w_ssm_val, w_ssm_gate, w_out, norm_mlp_g, w_up, w_down, norm_final_g):
    batch, seq, d = x.shape
    depth = w_in.shape[0]
    assert depth == 1, "the final rmsnorm is fused into the single layer's mlp kernel"
    nsa_w = NSA_HEADS * HEAD_DIM
    ssm_w = ssm_d.shape[1]
    o_q, o_kv, o_g = nsa_w, nsa_w + 6 * KV_LANES, nsa_w + 6 * KV_LANES + 3 * NSA_HEADS
    o_u = o_g + ssm_w
    cos, slo, shi = _rope_tables(seq)
    ovt = _selection_constants(seq)
    head_order = np.array([g * HEADS_PER_GROUP + h for h in range(HEADS_PER_GROUP) for g in range(NSA_KV_HEADS)])
    x2 = x.reshape(batch * seq, d)
    for l in range(depth):
        wl = w_in[l]
        wq = _interleave_heads(wl[:, :o_q]).astype(BF16)
        wkv = wl[:, o_q:o_kv].astype(BF16)
        wg = jnp.pad(wl[:, o_kv:o_g], ((0, 0), (0, V7X_LANES - 3 * NSA_HEADS))).astype(BF16)
        wu = wl[:, o_g:o_u].astype(BF16)
        wga = wl[:, o_u:o_u + d].astype(BF16)
        wgb = wl[:, o_u + d:].astype(BF16)
        g_mix = norm_mix_g[l].reshape(1, d)
        qraw, qrot, kc_raw, vc_raw, ksat, vs, kwt, vw, gates, u = _proj_call(
            x2, g_mix, wq, wkv, wg, wu, cos, slo, shi, batch, seq)

        pea, peb, kw1a, kw1b, kw2 = _compress_weights(cmp_pe[l], cmp_k_w1[l], cmp_k_w2[l])
        _, _, vw1a, vw1b, vw2 = _compress_weights(cmp_pe[l], cmp_v_w1[l], cmp_v_w2[l])
        kct, vc = _compress_call(kc_raw, vc_raw, pea, peb, kw1a, kw1b, kw2.T, vw1a, vw1b, vw2, batch, seq)

        b3 = lambda a: a.reshape(batch, seq, a.shape[-1])
        attn = _nsa_call(b3(qraw), b3(qrot), b3(gates), kct, vc, ksat, b3(vs), kwt, b3(vw), ovt)
        attn = attn.reshape(batch * seq, nsa_w)

        lag, pm, qm, lr, li = _s5_matrices(ssm_lam_re[l], ssm_lam_im[l], ssm_log_step[l], ssm_b_re[l], ssm_b_im[l],
                                           ssm_c_re[l], ssm_c_im[l])
        y_ssm = _s5_call(u, lag, pm, qm, lr, li, ssm_d[l].reshape(1, ssm_w), batch, seq)

        wattn = w_attn_branch[l].reshape(NSA_HEADS, HEAD_DIM, d)[head_order].reshape(nsa_w, d).astype(BF16)
        x2 = _merge_mlp_call(x2, g_mix, attn, y_ssm, wga, wgb, wattn, w_ssm_val[l].astype(BF16),
                             w_ssm_gate[l].astype(BF16), w_out[l].astype(BF16), norm_mlp_g[l].reshape(1, d),
                             w_up[l].astype(BF16), w_down[l].astype(BF16), norm_final_g.reshape(1, d))
    return x2.reshape(batch, seq, d)
```
